```python
import jax, jax.numpy as jnp
from jax import lax
import numpy as np

D_MODEL = 1024
BATCH = 2
SEQ = 8192
DEPTH = 1

GRID_W = 64
CTX_LEN = 256
GLA_HEADS = 4
GLA_DK = D_MODEL // 2
GLA_DV = D_MODEL
GLA_HK = GLA_DK // GLA_HEADS
GLA_HV = GLA_DV // GLA_HEADS
GLA_RANK = 16
GLA_GATE_NORM = 16.0
GLA_CHUNK = 64
CONV_CH = D_MODEL
CONV_K = 3
N_EXPERTS = 16
EC_CAPACITY = 2
D_EXPERT = 2 * D_MODEL
N_MOD = 6
EPS = 1e-6
P_IN = 2 * GLA_DK + 2 * GLA_DV + 2 * GLA_RANK + 3 * CONV_CH + 2 * D_MODEL

kernel_name = "hybrid_gla_shortconv_ecmoe_dit"


def rms_norm(x, g):
    xf = x.astype(jnp.float32)
    y = xf * lax.rsqrt(jnp.mean(xf * xf, axis=-1, keepdims=True) + EPS)
    return (y * g.astype(jnp.float32)).astype(x.dtype)


def adaln(cond, w_ada, b_ada):
    mod = jax.nn.silu(cond) @ w_ada + b_ada
    return [m[:, None, :] for m in jnp.split(mod, N_MOD, axis=-1)]


def modulate(h, shift, scale):
    return h * (1.0 + scale) + shift


def to_heads(t, hd):
    return t.reshape(t.shape[0], t.shape[1], GLA_HEADS, hd).transpose(0, 2, 1, 3)


def project(h, w_in, w_a_up, b_a):
    z = h @ w_in
    offs = np.cumsum([GLA_DK, GLA_DK, GLA_DV, GLA_DV, GLA_RANK, GLA_RANK,
                      CONV_CH, CONV_CH, CONV_CH, D_MODEL])
    q, k, v, g, lr_f, lr_b, u, bg, cg, r_gla, r_conv = jnp.split(z, offs, axis=-1)
    la_f = jax.nn.log_sigmoid(lr_f @ w_a_up[0] + b_a[0]) / GLA_GATE_NORM
    la_b = jax.nn.log_sigmoid(lr_b @ w_a_up[1] + b_a[1]) / GLA_GATE_NORM
    q = to_heads(q * (GLA_HK ** -0.5), GLA_HK)
    return (q, to_heads(k, GLA_HK), to_heads(v, GLA_HV), g,
            to_heads(la_f, GLA_HK), to_heads(la_b, GLA_HK), u, bg, cg, r_gla, r_conv)


def gla_chunked(q, k, v, log_a, s0):
    bsz, h, n, _ = q.shape
    dv = v.shape[-1]
    nc = n // GLA_CHUNK
    rs = lambda t: t.astype(jnp.float32).reshape(bsz, h, nc, GLA_CHUNK, t.shape[-1])
    q, k, v, log_a = rs(q), rs(k), rs(v), rs(log_a)
    b = jnp.cumsum(log_a, axis=-2)
    b_last = b[..., -1:, :]
    q_in = q * jnp.exp(b)
    k_in = k * jnp.exp(-b)
    k_dec = k * jnp.exp(b_last - b)
    mask = jnp.tril(jnp.ones((GLA_CHUNK, GLA_CHUNK), dtype=bool))
    scores = jnp.where(mask, jnp.einsum('bhcld,bhcmd->bhclm', q_in, k_in), 0.0)
    o_intra = jnp.einsum('bhclm,bhcme->bhcle', scores, v)
    u = jnp.einsum('bhcld,bhcle->bhcde', k_dec, v)
    decay = jnp.exp(b_last[..., 0, :])

    def step(s, inp):
        dec_c, u_c = inp
        return dec_c[..., None] * s + u_c, s

    s_fin, s_prev = lax.scan(step, s0.astype(jnp.float32),
                             (jnp.moveaxis(decay, 2, 0), jnp.moveaxis(u, 2, 0)))
    s_prev = jnp.moveaxis(s_prev, 0, 2)
    o = o_intra + jnp.einsum('bhcld,bhcde->bhcle', q_in, s_prev)
    return o.reshape(bsz, h, n, dv), s_fin


def gla_final_state(k, v, log_a):
    k, v, log_a = k.astype(jnp.float32), v.astype(jnp.float32), log_a.astype(jnp.float32)
    b = jnp.cumsum(log_a, axis=2)
    return jnp.einsum('bhnd,bhne->bhde', k * jnp.exp(b[:, :, -1:, :] - b), v)


def centred_conv3(u, w):
    up = jnp.pad(u, [(0, 0)] * (u.ndim - 2) + [(1, 1), (0, 0)])
    return up[..., :-2, :] * w[0] + up[..., 1:-1, :] * w[1] + up[..., 2:, :] * w[2]


def flip_seq(t):
    return jnp.flip(t, axis=2)


def token_mixer(proj, grid_rows, s0_f, s0_b, gla_norm_g, gla_w_o, conv_w, conv_w_out, merge_w_out):
    q, k, v, g, la_f, la_b, u, bg, cg, r_gla, r_conv = proj
    bsz, n, _ = u.shape
    dt = u.dtype
    o_f, s_f = gla_chunked(q, k, v, la_f, s0_f)
    o_b, s_b = gla_chunked(flip_seq(q), flip_seq(k), flip_seq(v), flip_seq(la_b), s0_b)
    o = o_f + flip_seq(o_b)
    o = rms_norm(o, gla_norm_g).transpose(0, 2, 1, 3).reshape(bsz, n, GLA_DV).astype(dt)
    y_gla = (o * jax.nn.silu(g)) @ gla_w_o
    cu = cg * u
    if grid_rows is None:
        conv = centred_conv3(cu, conv_w)
    else:
        conv = centred_conv3(cu.reshape(bsz, grid_rows, GRID_W, CONV_CH), conv_w).reshape(bsz, n, CONV_CH)
    y_conv = (bg * conv) @ conv_w_out
    merged = jax.nn.sigmoid(r_gla) * y_gla + jax.nn.sigmoid(r_conv) * y_conv
    return merged @ merge_w_out, s_f, s_b


def context_states(proj):
    _, k, v, _, la_f, la_b = proj[:6]
    s_f = gla_final_state(k, v, la_f)
    s_b = gla_final_state(flip_seq(k), flip_seq(v), flip_seq(la_b))
    return s_f, s_b


def expert_choice_ffn(h, w_router, w_gate, w_up, w_down):
    bsz, n, d = h.shape
    cap = EC_CAPACITY * n // N_EXPERTS
    aff = jax.nn.softmax((h @ w_router).astype(jnp.float32), axis=-1)
    gates, idx = lax.top_k(jnp.swapaxes(aff, 1, 2), cap)
    xe = jax.vmap(lambda hb, ib: hb[ib])(h, idx)
    hid = jax.nn.silu(jnp.einsum('becd,edf->becf', xe, w_gate)) * jnp.einsum('becd,edf->becf', xe, w_up)
    ye = jnp.einsum('becf,efd->becd', hid, w_down) * gates[..., None].astype(h.dtype)
    return jax.vmap(lambda yb, ib: jnp.zeros((n, d), yb.dtype).at[ib.reshape(-1)].add(yb.reshape(-1, d)))(ye, idx)


def setup_inputs(seed: int = 0) -> dict:
    key = jax.random.key(seed)
    ks = jax.random.split(key, 21)
    D = D_MODEL

    def nrm(k, shape, s):
        return jax.random.normal(k, shape, jnp.float32) * s

    return {
        "x": nrm(ks[0], (BATCH, SEQ, D), 1.0),
        "c": nrm(ks[1], (BATCH, D), 1.0),
        "ctx": nrm(ks[2], (BATCH, CTX_LEN, D), 1.0),
        "c_ctx": nrm(ks[3], (D,), 1.0),
        "w_ada": nrm(ks[4], (DEPTH, D, N_MOD * D), 0.5 * D ** -0.5),
        "b_ada": nrm(ks[5], (DEPTH, N_MOD * D), 0.01),
        "norm1_g": 1.0 + nrm(ks[6], (DEPTH, D), 0.05),
        "norm2_g": 1.0 + nrm(ks[7], (DEPTH, D), 0.05),
        "w_in": nrm(ks[8], (DEPTH, D, P_IN), D ** -0.5),
        "gla_w_a_up": nrm(ks[9], (DEPTH, 2, GLA_RANK, GLA_DK), GLA_RANK ** -0.5),
        "gla_b_a": nrm(ks[10], (DEPTH, 2, GLA_DK), 0.1),
        "gla_norm_g": 1.0 + nrm(ks[11], (DEPTH, GLA_HV), 0.05),
        "gla_w_o": nrm(ks[12], (DEPTH, GLA_DV, D), GLA_DV ** -0.5),
        "conv_w": nrm(ks[13], (DEPTH, CONV_K, CONV_CH), CONV_K ** -0.5),
        "conv_w_out": nrm(ks[14], (DEPTH, CONV_CH, D), CONV_CH ** -0.5),
        "merge_w_out": nrm(ks[15], (DEPTH, D, D), D ** -0.5),
        "router_w": nrm(ks[16], (DEPTH, D, N_EXPERTS), D ** -0.5),
        "exp_w_gate": nrm(ks[17], (DEPTH, N_EXPERTS, D, D_EXPERT), D ** -0.5),
        "exp_w_up": nrm(ks[18], (DEPTH, N_EXPERTS, D, D_EXPERT), D ** -0.5),
        "exp_w_down": nrm(ks[19], (DEPTH, N_EXPERTS, D_EXPERT, D), D_EXPERT ** -0.5),
        "final_g": 1.0 + nrm(ks[20], (D,), 0.05),
    }


def reference(x, c, ctx, c_ctx, w_ada, b_ada, norm1_g, norm2_g, w_in, gla_w_a_up, gla_b_a,
              gla_norm_g, gla_w_o, conv_w, conv_w_out, merge_w_out, router_w,
              exp_w_gate, exp_w_up, exp_w_down, final_g):
    bsz, n, _ = x.shape
    rows = n // GRID_W
    c_ctx_b = jnp.broadcast_to(c_ctx, c.shape)
    for i in range(DEPTH):
        sh1, sc1, gt1, sh2, sc2, gt2 = adaln(c, w_ada[i], b_ada[i])
        csh1, csc1, cgt1, csh2, csc2, cgt2 = adaln(c_ctx_b, w_ada[i], b_ada[i])
        mixer_w = (gla_norm_g[i], gla_w_o[i], conv_w[i], conv_w_out[i], merge_w_out[i])
        hc = modulate(rms_norm(ctx, norm1_g[i]), csh1, csc1)
        pc = project(hc, w_in[i], gla_w_a_up[i], gla_b_a[i])
        if i < DEPTH - 1:
            zero_state = jnp.zeros((bsz, GLA_HEADS, GLA_HK, GLA_HV), jnp.float32)
            yc, s_f, s_b = token_mixer(pc, None, zero_state, zero_state, *mixer_w)
        else:
            s_f, s_b = context_states(pc)
        h = modulate(rms_norm(x, norm1_g[i]), sh1, sc1)
        y, _, _ = token_mixer(project(h, w_in[i], gla_w_a_up[i], gla_b_a[i]), rows, s_f, s_b, *mixer_w)
        x = x + gt1 * y
        h = modulate(rms_norm(x, norm2_g[i]), sh2, sc2)
        x = x + gt2 * expert_choice_ffn(h, router_w[i], exp_w_gate[i], exp_w_up[i], exp_w_down[i])
        if i < DEPTH - 1:
            ctx = ctx + cgt1 * yc
            hc = modulate(rms_norm(ctx, norm2_g[i]), csh2, csc2)
            ctx = ctx + cgt2 * expert_choice_ffn(hc, router_w[i], exp_w_gate[i], exp_w_up[i], exp_w_down[i])
    return rms_norm(x, final_g)
```

```python
import functools

import numpy as np
import jax
import jax.numpy as jnp
from jax import lax
from jax.experimental import pallas as pl
from jax.experimental.pallas import tpu as pltpu

F32 = jnp.float32
BF16 = jnp.bfloat16
I32 = jnp.int32

EPS = 1e-6
N_MOD = 6
GRID_W = 64
GLA_CHUNK = 64
GLA_GATE_NORM = 16.0
EC_CAPACITY = 2

LANES = 128
BF16_ROWS = 16
VMEM_LIMIT = 56 * 1024 * 1024


def _cparams(*sem):
    return pltpu.CompilerParams(dimension_semantics=sem, vmem_limit_bytes=VMEM_LIMIT)


def _dot(a, b):
    return jnp.dot(a, b, preferred_element_type=F32)


def _dot_nt(a, b):
    return lax.dot_general(a, b, (((1,), (1,)), ((), ())), preferred_element_type=F32)


def _dot_tn(a, b):
    return lax.dot_general(a, b, (((0,), (0,)), ((), ())), preferred_element_type=F32)


def _sigmoid(v):
    return 1.0 / (1.0 + jnp.exp(-v))


def _log_sigmoid(v):
    return jnp.minimum(v, 0.0) - jnp.log1p(jnp.exp(-jnp.abs(v)))


def _rms_mod(xv, g, shift, scale):
    ms = jnp.mean(xv * xv, axis=-1, keepdims=True)
    y = xv * lax.rsqrt(ms + EPS) * g
    return y * (1.0 + scale) + shift


def _split_bf16(v):
    hi = v.astype(BF16)
    lo = (v - hi.astype(F32)).astype(BF16)
    return hi, lo


def _adaln_kernel(c_ref, w_ref, b_ref, o_ref):
    cv = c_ref[...]
    s = (cv * _sigmoid(cv)).astype(BF16)
    o_ref[...] = _dot(s, w_ref[...].astype(BF16)) + b_ref[...]


def _adaln(cond, w, b):
    rows, d = cond.shape
    nout = w.shape[1]
    tn = d
    return pl.pallas_call(
        _adaln_kernel,
        grid=(nout // tn,),
        in_specs=[pl.BlockSpec((rows, d), lambda j: (0, 0)),
                  pl.BlockSpec((d, tn), lambda j: (0, j)),
                  pl.BlockSpec((1, tn), lambda j: (0, j))],
        out_specs=pl.BlockSpec((rows, tn), lambda j: (0, j)),
        out_shape=jax.ShapeDtypeStruct((rows, nout), F32),
        compiler_params=_cparams("arbitrary"),
        name="adaln",
    )(cond, w, b.reshape(1, nout))


def _ctx_kernel(heads, ctx_ref, sh_ref, sc_ref, g_ref, wk_ref, wv_ref, wlr_ref,
                wupf_ref, wupb_ref, baf_ref, bab_ref, sf_ref, sb_ref):
    n = ctx_ref.shape[0]
    hc = _rms_mod(ctx_ref[...], g_ref[...], sh_ref[...], sc_ref[...]).astype(BF16)
    k = _dot(hc, wk_ref[...])
    v = _dot(hc, wv_ref[...]).astype(BF16)
    lr = _dot(hc, wlr_ref[...]).astype(BF16)
    hk = k.shape[1] // heads
    hv = v.shape[1] // heads
    row = lax.broadcasted_iota(I32, (n, n), 0)
    col = lax.broadcasted_iota(I32, (n, n), 1)
    for wup_ref, ba_ref, s_ref, tri, last in (
            (wupf_ref, baf_ref, sf_ref, col <= row, n - 1),
            (wupb_ref, bab_ref, sb_ref, col >= row, 0)):
        la = _log_sigmoid(_dot(lr, wup_ref[...]) + ba_ref[...]) * (1.0 / GLA_GATE_NORM)
        hi, lo = _split_bf16(la)
        t = jnp.where(tri, 1.0, 0.0).astype(BF16)
        b = _dot(t, hi) + _dot(t, lo)
        kd = (k * jnp.exp(b[last:last + 1, :] - b)).astype(BF16)
        for h in range(heads):
            s_ref[h] = _dot_tn(v[:, h * hv:(h + 1) * hv], kd[:, h * hk:(h + 1) * hk])


def _ctx_states(ctx, sh, sc, g, w_main, w_lr, wupf, wupb, baf, bab, heads, dk, dv):
    bsz, n, d = ctx.shape
    hk, hv = dk // heads, dv // heads
    full = lambda *shape: pl.BlockSpec(shape, lambda b: (0,) * len(shape))
    st = jax.ShapeDtypeStruct((bsz, heads, hv, hk), F32)
    sspec = pl.BlockSpec((None, heads, hv, hk), lambda b: (b, 0, 0, 0))
    return pl.pallas_call(
        functools.partial(_ctx_kernel, heads),
        grid=(bsz,),
        in_specs=[pl.BlockSpec((None, n, d), lambda b: (b, 0, 0)),
                  full(1, d), full(1, d), full(1, d),
                  pl.BlockSpec((d, dk), lambda b: (0, 1)),
                  pl.BlockSpec((d, dv), lambda b: (0, 2 * dk // dv)),
                  full(d, LANES), full(LANES, dk), full(LANES, dk), full(1, dk), full(1, dk)],
        out_specs=(sspec, sspec),
        out_shape=(st, st),
        compiler_params=_cparams("arbitrary"),
        name="ctx_state",
    )(ctx, sh, sc, g, w_main, w_main, w_lr, wupf, wupb, baf, bab)


def _inproj_kernel(x_ref, sh_ref, sc_ref, g_ref, w_ref, wlr_ref, z_ref, lr_ref, h_scr):
    @pl.when(pl.program_id(2) == 0)
    def _():
        hb = _rms_mod(x_ref[...], g_ref[...], sh_ref[...], sc_ref[...]).astype(BF16)
        h_scr[...] = hb
        lr_ref[...] = _dot(hb, wlr_ref[...])

    z_ref[...] = _dot(h_scr[...], w_ref[...]).astype(BF16)


def _inproj(x, sh, sc, g, w_main, w_lr, tm, tn):
    bsz, n, d = x.shape
    p = w_main.shape[1]
    return pl.pallas_call(
        _inproj_kernel,
        grid=(bsz, n // tm, p // tn),
        in_specs=[pl.BlockSpec((None, tm, d), lambda b, i, j: (b, i, 0)),
                  pl.BlockSpec((None, 1, d), lambda b, i, j: (b, 0, 0)),
                  pl.BlockSpec((None, 1, d), lambda b, i, j: (b, 0, 0)),
                  pl.BlockSpec((1, d), lambda b, i, j: (0, 0)),
                  pl.BlockSpec((d, tn), lambda b, i, j: (0, j)),
                  pl.BlockSpec((d, LANES), lambda b, i, j: (0, 0))],
        out_specs=(pl.BlockSpec((None, tm, tn), lambda b, i, j: (b, i, j)),
                   pl.BlockSpec((None, tm, LANES), lambda b, i, j: (b, i, 0))),
        out_shape=(jax.ShapeDtypeStruct((bsz, n, p), BF16),
                   jax.ShapeDtypeStruct((bsz, n, LANES), F32)),
        scratch_shapes=[pltpu.VMEM((tm, d), BF16)],
        compiler_params=_cparams("arbitrary", "arbitrary", "arbitrary"),
        name="inproj",
    )(x, sh, sc, g, w_main, w_lr)


def _gla_kernel(heads, qf_ref, kf_ref, vf_ref, lrf_ref, qb_ref, kb_ref, vb_ref, lrb_ref,
                wupf_ref, wupb_ref, baf_ref, bab_ref, s0f_ref, s0b_ref,
                of_ref, ob_ref, sf_scr, sb_scr):
    tb, dk = qf_ref.shape
    dv = vf_ref.shape[1]
    hk, hv = dk // heads, dv // heads
    ck = GLA_CHUNK
    nck = tb // ck
    q_scale = hk ** -0.5

    @pl.when(pl.program_id(1) == 0)
    def _():
        sf_scr[...] = s0f_ref[...]
        sb_scr[...] = s0b_ref[...]

    row = lax.broadcasted_iota(I32, (tb, tb), 0)
    col = lax.broadcasted_iota(I32, (tb, tb), 1)
    shift = ck.bit_length() - 1
    same_chunk = (row >> shift) == (col >> shift)
    crow = lax.broadcasted_iota(I32, (ck, ck), 0)
    ccol = lax.broadcasted_iota(I32, (ck, ck), 1)

    dirs = (
        (qf_ref, kf_ref, vf_ref, lrf_ref, wupf_ref, baf_ref, of_ref, sf_scr,
         same_chunk & (col <= row), ccol <= crow, ck - 1, range(nck)),
        (qb_ref, kb_ref, vb_ref, lrb_ref, wupb_ref, bab_ref, ob_ref, sb_scr,
         same_chunk & (col >= row), ccol >= crow, 0, range(nck - 1, -1, -1)),
    )
    for q_ref, k_ref, v_ref, lr_ref, wup_ref, ba_ref, o_ref, s_scr, tri, cmask, last, order in dirs:
        la = _log_sigmoid(_dot(lr_ref[...].astype(BF16), wup_ref[...]) + ba_ref[...])
        la = la * (1.0 / GLA_GATE_NORM)
        hi, lo = _split_bf16(la)
        t = jnp.where(tri, 1.0, 0.0).astype(BF16)
        b_all = _dot(t, hi) + _dot(t, lo)
        for c in order:
            r0 = c * ck
            b = b_all[r0:r0 + ck, :]
            b_last = b[last:last + 1, :]
            qc = q_ref[r0:r0 + ck, :].astype(F32) * q_scale
            kc = k_ref[r0:r0 + ck, :].astype(F32)
            q_in = (qc * jnp.exp(b)).astype(BF16)
            k_in = (kc * jnp.exp(-b)).astype(BF16)
            k_dec = (kc * jnp.exp(b_last - b)).astype(BF16)
            decay = jnp.exp(b_last)
            for h in range(heads):
                qh = q_in[:, h * hk:(h + 1) * hk]
                vh = v_ref[r0:r0 + ck, h * hv:(h + 1) * hv]
                scores = jnp.where(cmask, _dot_nt(qh, k_in[:, h * hk:(h + 1) * hk]), 0.0)
                st = s_scr[h]
                o = _dot(scores.astype(BF16), vh) + _dot_nt(qh, st.astype(BF16))
                o_ref[r0:r0 + ck, h * hv:(h + 1) * hv] = o.astype(BF16)
                ut = _dot_tn(vh, k_dec[:, h * hk:(h + 1) * hk])
                s_scr[h] = st * decay[:, h * hk:(h + 1) * hk] + ut


def _gla(z, lr, wupf, wupb, baf, bab, s0f, s0b, heads, dk, dv, tb):
    bsz, n, _ = z.shape
    nb = n // tb
    hk, hv = dk // heads, dv // heads
    fwd = lambda cb: (lambda b, i: (b, i, cb))
    bwd = lambda cb: (lambda b, i: (b, nb - 1 - i, cb))
    full = lambda *shape: pl.BlockSpec(shape, lambda b, i: (0,) * len(shape))
    sspec = pl.BlockSpec((None, heads, hv, hk), lambda b, i: (b, 0, 0, 0))
    vcb = 2 * dk // dv
    ost = jax.ShapeDtypeStruct((bsz, n, dv), BF16)
    return pl.pallas_call(
        functools.partial(_gla_kernel, heads),
        grid=(bsz, nb),
        in_specs=[pl.BlockSpec((None, tb, dk), fwd(0)), pl.BlockSpec((None, tb, dk), fwd(1)),
                  pl.BlockSpec((None, tb, dv), fwd(vcb)), pl.BlockSpec((None, tb, LANES), fwd(0)),
                  pl.BlockSpec((None, tb, dk), bwd(0)), pl.BlockSpec((None, tb, dk), bwd(1)),
                  pl.BlockSpec((None, tb, dv), bwd(vcb)), pl.BlockSpec((None, tb, LANES), bwd(0)),
                  full(LANES, dk), full(LANES, dk), full(1, dk), full(1, dk), sspec, sspec],
        out_specs=(pl.BlockSpec((None, tb, dv), fwd(0)), pl.BlockSpec((None, tb, dv), bwd(0))),
        out_shape=(ost, ost),
        scratch_shapes=[pltpu.VMEM((heads, hv, hk), F32), pltpu.VMEM((heads, hv, hk), F32)],
        compiler_params=_cparams("arbitrary", "arbitrary"),
        name="gla",
    )(z, z, z, lr, z, z, z, lr, wupf, wupb, baf, bab, s0f, s0b)


def _mixer_kernel(heads, of_ref, ob_ref, g_ref, u_ref, bg_ref, cg_ref, rg_ref, rc_ref, x_ref,
                  gng_ref, cw_ref, wo_ref, wco_ref, wm_ref, gt1_ref, sh2_ref, sc2_ref, n2g_ref,
                  rwt_ref, x1_ref, h2_ref, aff_ref):
    tm, dv = of_ref.shape
    hv = dv // heads
    o = of_ref[...].astype(F32) + ob_ref[...].astype(F32)
    parts = []
    for h in range(heads):
        oh = o[:, h * hv:(h + 1) * hv]
        ms = jnp.mean(oh * oh, axis=-1, keepdims=True)
        parts.append(oh * lax.rsqrt(ms + EPS))
    on = jnp.concatenate(parts, axis=1) * gng_ref[...]
    g = g_ref[...].astype(F32)
    y_gla = _dot((on * (g * _sigmoid(g))).astype(BF16), wo_ref[...])

    cu = cg_ref[...].astype(F32) * u_ref[...].astype(F32)
    gcol = lax.broadcasted_iota(I32, cu.shape, 0) & (GRID_W - 1)
    left = jnp.where(gcol == 0, 0.0, pltpu.roll(cu, 1, 0))
    right = jnp.where(gcol == GRID_W - 1, 0.0, pltpu.roll(cu, tm - 1, 0))
    cw = cw_ref[...]
    conv = left * cw[0:1, :] + cu * cw[1:2, :] + right * cw[2:3, :]
    y_conv = _dot((bg_ref[...].astype(F32) * conv).astype(BF16), wco_ref[...])

    merged = (_sigmoid(rg_ref[...].astype(F32)) * y_gla
              + _sigmoid(rc_ref[...].astype(F32)) * y_conv)
    y = _dot(merged.astype(BF16), wm_ref[...])
    x1 = x_ref[...] + gt1_ref[...] * y
    x1_ref[...] = x1
    h2 = _rms_mod(x1, n2g_ref[...], sh2_ref[...], sc2_ref[...]).astype(BF16)
    h2_ref[...] = h2
    logits = _dot_nt(rwt_ref[...], h2)
    ex = jnp.exp(logits - jnp.max(logits, axis=0, keepdims=True))
    aff_ref[...] = ex / jnp.sum(ex, axis=0, keepdims=True)


def _mixer(o_f, o_b, z, x, gng, conv_w, w_o, w_co, w_m, gt1, sh2, sc2, n2g, rwt, heads, dk, tm):
    bsz, n, d = x.shape
    dv = o_f.shape[2]
    c = conv_w.shape[1]
    e = rwt.shape[0]
    zb = lambda cb: pl.BlockSpec((None, tm, d), lambda b, i: (b, i, cb))
    base = (2 * dk + dv) // d
    tok = lambda w: pl.BlockSpec((None, tm, w), lambda b, i: (b, i, 0))
    full = lambda *shape: pl.BlockSpec(shape, lambda b, i: (0,) * len(shape))
    perb = pl.BlockSpec((None, 1, d), lambda b, i: (b, 0, 0))
    return pl.pallas_call(
        functools.partial(_mixer_kernel, heads),
        grid=(bsz, n // tm),
        in_specs=[tok(dv), tok(dv), zb(base), zb(base + 1), zb(base + 2), zb(base + 3),
                  zb(base + 4), zb(base + 5), tok(d),
                  full(1, dv), full(3, c), full(dv, d), full(c, d), full(d, d),
                  perb, perb, perb, full(1, d), full(e, d)],
        out_specs=(tok(d), tok(d), pl.BlockSpec((None, e, tm), lambda b, i: (b, 0, i))),
        out_shape=(jax.ShapeDtypeStruct((bsz, n, d), F32),
                   jax.ShapeDtypeStruct((bsz, n, d), BF16),
                   jax.ShapeDtypeStruct((bsz, e, n), F32)),
        compiler_params=_cparams("arbitrary", "arbitrary"),
        name="mixer",
    )(o_f, o_b, z, z, z, z, z, z, x, gng, conv_w, w_o, w_co, w_m, gt1, sh2, sc2, n2g, rwt)


def _route_kernel(cap, aff_ref, lmat_ref, pos_ref, base_ref):
    e, n = aff_ref.shape
    nt = n // LANES
    aff = aff_ref[...]

    def count(mask):
        return jnp.sum(jnp.where(mask, 1.0, 0.0), axis=1, keepdims=True)

    def search(k, tbits):
        cand = tbits | jnp.left_shift(jnp.int32(1), 30 - k)
        ok = count(aff >= lax.bitcast_convert_type(cand, F32)) >= cap
        return jnp.where(ok, cand, tbits)

    tbits = lax.fori_loop(0, 31, search, jnp.zeros((e, 1), I32))
    thr = lax.bitcast_convert_type(tbits, F32)
    gt = aff > thr
    eq = aff == thr
    need = cap - count(gt)

    def stack(mask):
        m = jnp.where(mask, 1.0, 0.0)
        return jnp.concatenate([m[:, j * LANES:(j + 1) * LANES] for j in range(nt)], axis=0)

    r = lax.broadcasted_iota(I32, (LANES, LANES), 0)
    cl = lax.broadcasted_iota(I32, (LANES, LANES), 1)
    upper = jnp.where(r <= cl, 1.0, 0.0).astype(BF16)
    ones = jnp.ones((LANES, LANES), BF16)
    lmat = lmat_ref[...]

    def cumsum(ms):
        msb = ms.astype(BF16)
        before = _dot(_dot(lmat, msb).astype(BF16), ones)
        return _dot(msb, upper) + before, before

    eq_s = stack(eq)
    rank_eq, _ = cumsum(eq_s)
    need_s = jnp.concatenate([need] * nt, axis=0)
    sel = jnp.maximum(stack(gt), jnp.where(rank_eq <= need_s, eq_s, 0.0))
    incl, before = cumsum(sel)
    pos_ref[...] = jnp.where(sel > 0.0, incl - 1.0, -1.0).astype(I32)
    base_ref[...] = before.astype(I32)


def _route(aff_t, cap):
    bsz, e, n = aff_t.shape
    nt = n // LANES
    rows = nt * e
    ri = np.arange(rows)
    lmat = ((ri[None, :] % e == ri[:, None] % e) & (ri[None, :] // e < ri[:, None] // e))
    lmat = jnp.asarray(lmat, BF16)
    st = jax.ShapeDtypeStruct((bsz, rows, LANES), I32)
    ospec = pl.BlockSpec((None, rows, LANES), lambda b: (b, 0, 0))
    return pl.pallas_call(
        functools.partial(_route_kernel, cap),
        grid=(bsz,),
        in_specs=[pl.BlockSpec((None, e, n), lambda b: (b, 0, 0)),
                  pl.BlockSpec((rows, rows), lambda b: (0, 0))],
        out_specs=(ospec, ospec),
        out_shape=(st, st),
        compiler_params=_cparams("arbitrary"),
        name="route",
    )(aff_t, lmat)


def _slot_window(base):
    return pl.multiple_of((base // BF16_ROWS) * BF16_ROWS, BF16_ROWS)


def _gather_kernel(n_exp, eg, win, base_ref, h2_ref, pos_ref, xe_hbm, xe_scr, sem):
    b, grp, j = pl.program_id(0), pl.program_id(1), pl.program_id(2)
    nt = pl.num_programs(2)

    @pl.when(j == 0)
    def _():
        xe_scr[...] = jnp.zeros_like(xe_scr)

    h = h2_ref[...]
    slot = lax.broadcasted_iota(I32, (win, LANES), 0)
    starts, onehots = [], []
    for le in range(eg):
        ex = grp * eg + le
        s0 = _slot_window(base_ref[(b * nt + j) * n_exp + ex])
        rel = pos_ref[pl.ds(ex, 1), :] - s0
        starts.append(s0)
        onehots.append(jnp.where(slot == rel, 1.0, 0.0).astype(BF16))
    rows = _dot(jnp.concatenate(onehots, axis=0), h).astype(BF16)
    for le in range(eg):
        dst = xe_scr.at[le, pl.ds(starts[le], win), :]
        dst[...] = dst[...] + rows[le * win:(le + 1) * win, :]

    @pl.when(j == nt - 1)
    def _():
        cp = pltpu.make_async_copy(xe_scr, xe_hbm.at[b, pl.ds(grp * eg, eg)], sem)
        cp.start()
        cp.wait()


def _gather(h2, pos, bases, n_exp, cap, eg, win):
    bsz, n, d = h2.shape
    nt = n // LANES
    rows = cap + win
    return pl.pallas_call(
        functools.partial(_gather_kernel, n_exp, eg, win),
        grid_spec=pltpu.PrefetchScalarGridSpec(
            num_scalar_prefetch=1,
            grid=(bsz, n_exp // eg, nt),
            in_specs=[pl.BlockSpec((None, LANES, d), lambda b, g, j, s: (b, j, 0)),
                      pl.BlockSpec((None, n_exp, LANES), lambda b, g, j, s: (b, j, 0))],
            out_specs=pl.BlockSpec(memory_space=pl.ANY),
            scratch_shapes=[pltpu.VMEM((eg, rows, d), BF16), pltpu.SemaphoreType.DMA(())]),
        out_shape=jax.ShapeDtypeStruct((bsz, n_exp, rows, d), BF16),
        compiler_params=_cparams("arbitrary", "arbitrary", "arbitrary"),
        name="gather",
    )(bases, h2, pos)


def _expert_kernel(cap, xe_ref, wg_ref, wu_ref, wd_ref, ye_ref, acc):
    f = pl.program_id(2)
    xv = xe_ref[0:cap, :]
    hg = _dot(xv, wg_ref[...].astype(BF16))
    hu = _dot(xv, wu_ref[...].astype(BF16))
    hid = (hg * _sigmoid(hg) * hu).astype(BF16)
    part = _dot(hid, wd_ref[...].astype(BF16))

    @pl.when(f == 0)
    def _():
        acc[...] = part

    @pl.when(f > 0)
    def _():
        acc[...] = acc[...] + part

    @pl.when(f == pl.num_programs(2) - 1)
    def _():
        ye_ref[0:cap, :] = acc[...].astype(BF16)
        ye_ref[cap:, :] = jnp.zeros((ye_ref.shape[0] - cap, ye_ref.shape[1]), BF16)


def _experts(xe, w_gate, w_up, w_down, cap, tf):
    bsz, n_exp, rows, d = xe.shape
    df = w_gate.shape[-1]
    return pl.pallas_call(
        functools.partial(_expert_kernel, cap),
        grid=(bsz, n_exp, df // tf),
        in_specs=[pl.BlockSpec((None, None, rows, d), lambda b, e, f: (b, e, 0, 0)),
                  pl.BlockSpec((None, d, tf), lambda b, e, f: (e, 0, f)),
                  pl.BlockSpec((None, d, tf), lambda b, e, f: (e, 0, f)),
                  pl.BlockSpec((None, tf, d), lambda b, e, f: (e, f, 0))],
        out_specs=pl.BlockSpec((None, None, rows, d), lambda b, e, f: (b, e, 0, 0)),
        out_shape=jax.ShapeDtypeStruct((bsz, n_exp, rows, d), BF16),
        scratch_shapes=[pltpu.VMEM((cap, d), F32)],
        compiler_params=_cparams("arbitrary", "arbitrary", "arbitrary"),
        name="experts",
    )(xe, w_gate, w_up, w_down)


def _combine_kernel(n_exp, win, base_ref, pos_ref, aff_ref, x1_ref, gt2_ref, fg_ref, ye_hbm,
                    out_ref, buf, sem):
    b, j = pl.program_id(0), pl.program_id(1)
    nt = pl.num_programs(1)
    step = b * nt + j
    slot = step % 2

    def window_copy(tile, ex, dst_slot):
        s0 = _slot_window(base_ref[tile * n_exp + ex])
        return pltpu.make_async_copy(ye_hbm.at[tile // nt, ex, pl.ds(s0, win), :],
                                     buf.at[dst_slot, ex], sem.at[dst_slot])

    @pl.when(step == 0)
    def _():
        for ex in range(n_exp):
            window_copy(step, ex, slot).start()

    @pl.when(step + 1 < pl.num_programs(0) * nt)
    def _():
        for ex in range(n_exp):
            window_copy(step + 1, ex, 1 - slot).start()

    srow = lax.broadcasted_iota(I32, (win, LANES), 0)
    gts = []
    for ex in range(n_exp):
        rel = pos_ref[pl.ds(ex, 1), :] - _slot_window(base_ref[step * n_exp + ex])
        gts.append(jnp.where(srow == rel, aff_ref[pl.ds(ex, 1), :], 0.0).astype(BF16))
    gt_all = jnp.concatenate(gts, axis=0)

    for ex in range(n_exp):
        window_copy(step, ex, slot).wait()
    ywin = buf[slot].reshape(n_exp * win, buf.shape[-1])
    ffn = _dot_tn(gt_all, ywin)
    x2 = x1_ref[...] + gt2_ref[...] * ffn
    ms = jnp.mean(x2 * x2, axis=-1, keepdims=True)
    out_ref[...] = x2 * lax.rsqrt(ms + EPS) * fg_ref[...]


def _combine(ye, pos, aff_t, bases, x1, gt2, final_g, win):
    bsz, n, d = x1.shape
    n_exp = aff_t.shape[1]
    nt = n // LANES
    return pl.pallas_call(
        functools.partial(_combine_kernel, n_exp, win),
        grid_spec=pltpu.PrefetchScalarGridSpec(
            num_scalar_prefetch=1,
            grid=(bsz, nt),
            in_specs=[pl.BlockSpec((None, n_exp, LANES), lambda b, j, s: (b, j, 0)),
                      pl.BlockSpec((None, n_exp, LANES), lambda b, j, s: (b, 0, j)),
                      pl.BlockSpec((None, LANES, d), lambda b, j, s: (b, j, 0)),
                      pl.BlockSpec((None, 1, d), lambda b, j, s: (b, 0, 0)),
                      pl.BlockSpec((1, d), lambda b, j, s: (0, 0)),
                      pl.BlockSpec(memory_space=pl.ANY)],
            out_specs=pl.BlockSpec((None, LANES, d), lambda b, j, s: (b, j, 0)),
            scratch_shapes=[pltpu.VMEM((2, n_exp, win, d), BF16), pltpu.SemaphoreType.DMA((2,))]),
        out_shape=jax.ShapeDtypeStruct((bsz, n, d), F32),
        compiler_params=_cparams("arbitrary", "arbitrary"),
        name="combine",
    )(bases, pos, aff_t, x1, gt2, final_g, ye)


def _pick(n, pref):
    t = min(n, pref)
    while n % t:
        t //= 2
    return t


def kernel(x, c, ctx, c_ctx, w_ada, b_ada, norm1_g, norm2_g, w_in, gla_w_a_up, gla_b_a,
           gla_norm_g, gla_w_o, conv_w, conv_w_out, merge_w_out, router_w,
           exp_w_gate, exp_w_up, exp_w_down, final_g):
    bsz, n, d = x.shape
    depth = w_ada.shape[0]
    assert depth == 1, "single-layer trunk"
    rank, dk = gla_w_a_up.shape[-2:]
    dv = gla_w_o.shape[1]
    hv = gla_norm_g.shape[-1]
    heads = dv // hv
    cch = conv_w.shape[-1]
    n_exp = router_w.shape[-1]
    cap = EC_CAPACITY * n // n_exp
    assert 2 * dk == d and dv == d and cch == d and 2 * rank <= LANES
    assert n % LANES == 0 and n % GRID_W == 0 and bsz + 1 <= 8 and n_exp % 8 == 0

    cond = jnp.zeros((8, d), F32).at[:bsz].set(c).at[bsz].set(c_ctx)
    mod = _adaln(cond, w_ada[0], b_ada[0]).reshape(8, N_MOD, d)
    sh1, sc1, gt1, sh2, sc2, gt2 = [mod[:bsz, k][:, None, :] for k in range(N_MOD)]
    csh1, csc1 = mod[bsz:bsz + 1, 0], mod[bsz:bsz + 1, 1]

    w = w_in[0].astype(BF16)
    lr0 = 2 * dk + 2 * dv
    w_main = jnp.concatenate([w[:, :lr0], w[:, lr0 + 2 * rank:]], axis=1)
    w_lr = jnp.zeros((d, LANES), BF16).at[:, :2 * rank].set(w[:, lr0:lr0 + 2 * rank])
    wup = gla_w_a_up[0].astype(BF16)
    wupf = jnp.zeros((LANES, dk), BF16).at[:rank].set(wup[0])
    wupb = jnp.zeros((LANES, dk), BF16).at[rank:2 * rank].set(wup[1])
    baf, bab = gla_b_a[0, 0:1], gla_b_a[0, 1:2]
    n1g, n2g = norm1_g[0:1], norm2_g[0:1]

    s0f, s0b = _ctx_states(ctx, csh1, csc1, n1g, w_main, w_lr, wupf, wupb, baf, bab,
                           heads, dk, dv)
    z, lr = _inproj(x, sh1, sc1, n1g, w_main, w_lr, _pick(n, 1024), _pick(w_main.shape[1], 1024))
    o_f, o_b = _gla(z, lr, wupf, wupb, baf, bab, s0f, s0b, heads, dk, dv, _pick(n, 256))
    x1, h2, aff_t = _mixer(
        o_f, o_b, z, x, jnp.tile(gla_norm_g[0:1], (1, heads)), conv_w[0],
        gla_w_o[0].astype(BF16), conv_w_out[0].astype(BF16), merge_w_out[0].astype(BF16),
        gt1, sh2, sc2, n2g, router_w[0].T.astype(BF16), heads, dk, _pick(n, 256))

    pos, before = _route(aff_t, cap)
    bases = before[:, :, 0].reshape(-1)
    win = LANES + BF16_ROWS
    xe = _gather(h2, pos, bases, n_exp, cap, 8, win)
    ye = _experts(xe, exp_w_gate[0], exp_w_up[0], exp_w_down[0], cap,
                  _pick(exp_w_gate.shape[-1], 512))
    return _combine(ye, pos, aff_t, bases, x1, gt2, final_g.reshape(1, d), win)
```

```python
import functools

import numpy as np
import jax
import jax.numpy as jnp
from jax import lax
from jax.experimental import pallas as pl
from jax.experimental.pallas import tpu as pltpu

F32 = jnp.float32
BF16 = jnp.bfloat16
I32 = jnp.int32

EPS = 1e-6
N_MOD = 6
GRID_W = 64
GLA_CHUNK = 64
GLA_GATE_NORM = 16.0
EC_CAPACITY = 2

LANES = 128
BF16_ROWS = 16
VMEM_LIMIT = 56 * 1024 * 1024


def _cparams(*sem):
    return pltpu.CompilerParams(dimension_semantics=sem, vmem_limit_bytes=VMEM_LIMIT)


def _dot(a, b):
    return jnp.dot(a, b, preferred_element_type=F32)


def _dot_nt(a, b):
    return lax.dot_general(a, b, (((1,), (1,)), ((), ())), preferred_element_type=F32)


def _dot_tn(a, b):
    return lax.dot_general(a, b, (((0,), (0,)), ((), ())), preferred_element_type=F32)


def _sigmoid(v):
    return 1.0 / (1.0 + jnp.exp(-v))


def _log_sigmoid(v):
    return jnp.minimum(v, 0.0) - jnp.log1p(jnp.exp(-jnp.abs(v)))


def _rms_mod(xv, g, shift, scale):
    ms = jnp.mean(xv * xv, axis=-1, keepdims=True)
    y = xv * lax.rsqrt(ms + EPS) * g
    return y * (1.0 + scale) + shift


def _split_bf16(v):
    hi = v.astype(BF16)
    lo = (v - hi.astype(F32)).astype(BF16)
    return hi, lo


def _adaln_kernel(c_ref, w_ref, b_ref, o_ref):
    cv = c_ref[...]
    s = (cv * _sigmoid(cv)).astype(BF16)
    o_ref[...] = _dot(s, w_ref[...].astype(BF16)) + b_ref[...]


def _adaln(cond, w, b):
    rows, d = cond.shape
    nout = w.shape[1]
    tn = d
    return pl.pallas_call(
        _adaln_kernel,
        grid=(nout // tn,),
        in_specs=[pl.BlockSpec((rows, d), lambda j: (0, 0)),
                  pl.BlockSpec((d, tn), lambda j: (0, j)),
                  pl.BlockSpec((1, tn), lambda j: (0, j))],
        out_specs=pl.BlockSpec((rows, tn), lambda j: (0, j)),
        out_shape=jax.ShapeDtypeStruct((rows, nout), F32),
        compiler_params=_cparams("arbitrary"),
        name="adaln",
    )(cond, w, b.reshape(1, nout))


def _ctx_kernel(heads, ctx_ref, sh_ref, sc_ref, g_ref, wk_ref, wv_ref, wlr_ref,
                wupf_ref, wupb_ref, baf_ref, bab_ref, sf_ref, sb_ref):
    n = ctx_ref.shape[0]
    hc = _rms_mod(ctx_ref[...], g_ref[...], sh_ref[...], sc_ref[...]).astype(BF16)
    k = _dot(hc, wk_ref[...])
    v = _dot(hc, wv_ref[...]).astype(BF16)
    lr = _dot(hc, wlr_ref[...]).astype(BF16)
    hk = k.shape[1] // heads
    hv = v.shape[1] // heads
    row = lax.broadcasted_iota(I32, (n, n), 0)
    col = lax.broadcasted_iota(I32, (n, n), 1)
    for wup_ref, ba_ref, s_ref, tri, last in (
            (wupf_ref, baf_ref, sf_ref, col <= row, n - 1),
            (wupb_ref, bab_ref, sb_ref, col >= row, 0)):
        la = _log_sigmoid(_dot(lr, wup_ref[...]) + ba_ref[...]) * (1.0 / GLA_GATE_NORM)
        hi, lo = _split_bf16(la)
        t = jnp.where(tri, 1.0, 0.0).astype(BF16)
        b = _dot(t, hi) + _dot(t, lo)
        kd = (k * jnp.exp(b[last:last + 1, :] - b)).astype(BF16)
        for h in range(heads):
            s_ref[h] = _dot_tn(v[:, h * hv:(h + 1) * hv], kd[:, h * hk:(h + 1) * hk])


def _ctx_states(ctx, sh, sc, g, w_main, w_lr, wupf, wupb, baf, bab, heads, dk, dv):
    bsz, n, d = ctx.shape
    hk, hv = dk // heads, dv // heads
    full = lambda *shape: pl.BlockSpec(shape, lambda b: (0,) * len(shape))
    st = jax.ShapeDtypeStruct((bsz, heads, hv, hk), F32)
    sspec = pl.BlockSpec((None, heads, hv, hk), lambda b: (b, 0, 0, 0))
    return pl.pallas_call(
        functools.partial(_ctx_kernel, heads),
        grid=(bsz,),
        in_specs=[pl.BlockSpec((None, n, d), lambda b: (b, 0, 0)),
                  full(1, d), full(1, d), full(1, d),
                  pl.BlockSpec((d, dk), lambda b: (0, 1)),
                  pl.BlockSpec((d, dv), lambda b: (0, 2 * dk // dv)),
                  full(d, LANES), full(LANES, dk), full(LANES, dk), full(1, dk), full(1, dk)],
        out_specs=(sspec, sspec),
        out_shape=(st, st),
        compiler_params=_cparams("arbitrary"),
        name="ctx_state",
    )(ctx, sh, sc, g, w_main, w_main, w_lr, wupf, wupb, baf, bab)


def _inproj_kernel(x_ref, sh_ref, sc_ref, g_ref, w_ref, wlr_ref, z_ref, lr_ref, h_scr):
    @pl.when(pl.program_id(2) == 0)
    def _():
        hb = _rms_mod(x_ref[...], g_ref[...], sh_ref[...], sc_ref[...]).astype(BF16)
        h_scr[...] = hb
        lr_ref[...] = _dot(hb, wlr_ref[...])

    z_ref[...] = _dot(h_scr[...], w_ref[...]).astype(BF16)


def _inproj(x, sh, sc, g, w_main, w_lr, tm, tn):
    bsz, n, d = x.shape
    p = w_main.shape[1]
    return pl.pallas_call(
        _inproj_kernel,
        grid=(bsz, n // tm, p // tn),
        in_specs=[pl.BlockSpec((None, tm, d), lambda b, i, j: (b, i, 0)),
                  pl.BlockSpec((None, 1, d), lambda b, i, j: (b, 0, 0)),
                  pl.BlockSpec((None, 1, d), lambda b, i, j: (b, 0, 0)),
                  pl.BlockSpec((1, d), lambda b, i, j: (0, 0)),
                  pl.BlockSpec((d, tn), lambda b, i, j: (0, j)),
                  pl.BlockSpec((d, LANES), lambda b, i, j: (0, 0))],
        out_specs=(pl.BlockSpec((None, tm, tn), lambda b, i, j: (b, i, j)),
                   pl.BlockSpec((None, tm, LANES), lambda b, i, j: (b, i, 0))),
        out_shape=(jax.ShapeDtypeStruct((bsz, n, p), BF16),
                   jax.ShapeDtypeStruct((bsz, n, LANES), F32)),
        scratch_shapes=[pltpu.VMEM((tm, d), BF16)],
        compiler_params=_cparams("arbitrary", "arbitrary", "arbitrary"),
        name="inproj",
    )(x, sh, sc, g, w_main, w_lr)


def _gla_kernel(heads, qf_ref, kf_ref, vf_ref, lrf_ref, qb_ref, kb_ref, vb_ref, lrb_ref,
                wupf_ref, wupb_ref, baf_ref, bab_ref, s0f_ref, s0b_ref,
                of_ref, ob_ref, sf_scr, sb_scr):
    tb, dk = qf_ref.shape
    dv = vf_ref.shape[1]
    hk, hv = dk // heads, dv // heads
    ck = GLA_CHUNK
    nck = tb // ck
    q_scale = hk ** -0.5

    @pl.when(pl.program_id(1) == 0)
    def _():
        sf_scr[...] = s0f_ref[...]
        sb_scr[...] = s0b_ref[...]

    row = lax.broadcasted_iota(I32, (tb, tb), 0)
    col = lax.broadcasted_iota(I32, (tb, tb), 1)
    shift = ck.bit_length() - 1
    same_chunk = (row >> shift) == (col >> shift)
    crow = lax.broadcasted_iota(I32, (ck, ck), 0)
    ccol = lax.broadcasted_iota(I32, (ck, ck), 1)

    dirs = (
        (qf_ref, kf_ref, vf_ref, lrf_ref, wupf_ref, baf_ref, of_ref, sf_scr,
         same_chunk & (col <= row), ccol <= crow, ck - 1, range(nck)),
        (qb_ref, kb_ref, vb_ref, lrb_ref, wupb_ref, bab_ref, ob_ref, sb_scr,
         same_chunk & (col >= row), ccol >= crow, 0, range(nck - 1, -1, -1)),
    )
    for q_ref, k_ref, v_ref, lr_ref, wup_ref, ba_ref, o_ref, s_scr, tri, cmask, last, order in dirs:
        la = _log_sigmoid(_dot(lr_ref[...].astype(BF16), wup_ref[...]) + ba_ref[...])
        la = la * (1.0 / GLA_GATE_NORM)
        hi, lo = _split_bf16(la)
        t = jnp.where(tri, 1.0, 0.0).astype(BF16)
        b_all = _dot(t, hi) + _dot(t, lo)
        for c in order:
            r0 = c * ck
            b = b_all[r0:r0 + ck, :]
            b_last = b[last:last + 1, :]
            qc = q_ref[r0:r0 + ck, :].astype(F32) * q_scale
            kc = k_ref[r0:r0 + ck, :].astype(F32)
            q_in = (qc * jnp.exp(b)).astype(BF16)
            k_in = (kc * jnp.exp(-b)).astype(BF16)
            k_dec = (kc * jnp.exp(b_last - b)).astype(BF16)
            decay = jnp.exp(b_last)
            for h in range(heads):
                qh = q_in[:, h * hk:(h + 1) * hk]
                vh = v_ref[r0:r0 + ck, h * hv:(h + 1) * hv]
                scores = jnp.where(cmask, _dot_nt(qh, k_in[:, h * hk:(h + 1) * hk]), 0.0)
                st = s_scr[h]
                o = _dot(scores.astype(BF16), vh) + _dot_nt(qh, st.astype(BF16))
                o_ref[r0:r0 + ck, h * hv:(h + 1) * hv] = o.astype(BF16)
                ut = _dot_tn(vh, k_dec[:, h * hk:(h + 1) * hk])
                s_scr[h] = st * decay[:, h * hk:(h + 1) * hk] + ut


def _gla(z, lr, wupf, wupb, baf, bab, s0f, s0b, heads, dk, dv, tb):
    bsz, n, _ = z.shape
    nb = n // tb
    hk, hv = dk // heads, dv // heads
    fwd = lambda cb: (lambda b, i: (b, i, cb))
    bwd = lambda cb: (lambda b, i: (b, nb - 1 - i, cb))
    full = lambda *shape: pl.BlockSpec(shape, lambda b, i: (0,) * len(shape))
    sspec = pl.BlockSpec((None, heads, hv, hk), lambda b, i: (b, 0, 0, 0))
    vcb = 2 * dk // dv
    ost = jax.ShapeDtypeStruct((bsz, n, dv), BF16)
    return pl.pallas_call(
        functools.partial(_gla_kernel, heads),
        grid=(bsz, nb),
        in_specs=[pl.BlockSpec((None, tb, dk), fwd(0)), pl.BlockSpec((None, tb, dk), fwd(1)),
                  pl.BlockSpec((None, tb, dv), fwd(vcb)), pl.BlockSpec((None, tb, LANES), fwd(0)),
                  pl.BlockSpec((None, tb, dk), bwd(0)), pl.BlockSpec((None, tb, dk), bwd(1)),
                  pl.BlockSpec((None, tb, dv), bwd(vcb)), pl.BlockSpec((None, tb, LANES), bwd(0)),
                  full(LANES, dk), full(LANES, dk), full(1, dk), full(1, dk), sspec, sspec],
        out_specs=(pl.BlockSpec((None, tb, dv), fwd(0)), pl.BlockSpec((None, tb, dv), bwd(0))),
        out_shape=(ost, ost),
        scratch_shapes=[pltpu.VMEM((heads, hv, hk), F32), pltpu.VMEM((heads, hv, hk), F32)],
        compiler_params=_cparams("arbitrary", "arbitrary"),
        name="gla",
    )(z, z, z, lr, z, z, z, lr, wupf, wupb, baf, bab, s0f, s0b)


def _mixer_kernel(heads, of_ref, ob_ref, g_ref, u_ref, bg_ref, cg_ref, rg_ref, rc_ref, x_ref,
                  gng_ref, cw_ref, wo_ref, wco_ref, wm_ref, gt1_ref, sh2_ref, sc2_ref, n2g_ref,
                  rwt_ref, x1_ref, h2_ref, aff_ref):
    tm, dv = of_ref.shape
    hv = dv // heads
    o = of_ref[...].astype(F32) + ob_ref[...].astype(F32)
    parts = []
    for h in range(heads):
        oh = o[:, h * hv:(h + 1) * hv]
        ms = jnp.mean(oh * oh, axis=-1, keepdims=True)
        parts.append(oh * lax.rsqrt(ms + EPS))
    on = jnp.concatenate(parts, axis=1) * gng_ref[...]
    g = g_ref[...].astype(F32)
    y_gla = _dot((on * (g * _sigmoid(g))).astype(BF16), wo_ref[...])

    cu = cg_ref[...].astype(F32) * u_ref[...].astype(F32)
    gcol = lax.broadcasted_iota(I32, cu.shape, 0) & (GRID_W - 1)
    left = jnp.where(gcol == 0, 0.0, pltpu.roll(cu, 1, 0))
    right = jnp.where(gcol == GRID_W - 1, 0.0, pltpu.roll(cu, tm - 1, 0))
    cw = cw_ref[...]
    conv = left * cw[0:1, :] + cu * cw[1:2, :] + right * cw[2:3, :]
    y_conv = _dot((bg_ref[...].astype(F32) * conv).astype(BF16), wco_ref[...])

    merged = (_sigmoid(rg_ref[...].astype(F32)) * y_gla
              + _sigmoid(rc_ref[...].astype(F32)) * y_conv)
    y = _dot(merged.astype(BF16), wm_ref[...])
    x1 = x_ref[...] + gt1_ref[...] * y
    x1_ref[...] = x1
    h2 = _rms_mod(x1, n2g_ref[...], sh2_ref[...], sc2_ref[...])
    rpt = h2.shape[1] // LANES
    for a in range(rpt):
        h2_ref[pl.ds(a, tm, stride=rpt), :] = h2[:, a * LANES:(a + 1) * LANES]
    logits = _dot_nt(rwt_ref[...], h2.astype(BF16))
    ex = jnp.exp(logits - jnp.max(logits, axis=0, keepdims=True))
    aff_ref[...] = ex / jnp.sum(ex, axis=0, keepdims=True)


def _mixer(o_f, o_b, z, x, gng, conv_w, w_o, w_co, w_m, gt1, sh2, sc2, n2g, rwt, heads, dk, tm):
    bsz, n, d = x.shape
    dv = o_f.shape[2]
    c = conv_w.shape[1]
    e = rwt.shape[0]
    zb = lambda cb: pl.BlockSpec((None, tm, d), lambda b, i: (b, i, cb))
    base = (2 * dk + dv) // d
    tok = lambda w: pl.BlockSpec((None, tm, w), lambda b, i: (b, i, 0))
    full = lambda *shape: pl.BlockSpec(shape, lambda b, i: (0,) * len(shape))
    perb = pl.BlockSpec((None, 1, d), lambda b, i: (b, 0, 0))
    return pl.pallas_call(
        functools.partial(_mixer_kernel, heads),
        grid=(bsz, n // tm),
        in_specs=[tok(dv), tok(dv), zb(base), zb(base + 1), zb(base + 2), zb(base + 3),
                  zb(base + 4), zb(base + 5), tok(d),
                  full(1, dv), full(3, c), full(dv, d), full(c, d), full(d, d),
                  perb, perb, perb, full(1, d), full(e, d)],
        out_specs=(tok(d), pl.BlockSpec((None, tm * (d // LANES), LANES), lambda b, i: (b, i, 0)),
                   pl.BlockSpec((None, e, tm), lambda b, i: (b, 0, i))),
        out_shape=(jax.ShapeDtypeStruct((bsz, n, d), F32),
                   jax.ShapeDtypeStruct((bsz, n * (d // LANES), LANES), F32),
                   jax.ShapeDtypeStruct((bsz, e, n), F32)),
        compiler_params=_cparams("arbitrary", "arbitrary"),
        name="mixer",
    )(o_f, o_b, z, z, z, z, z, z, x, gng, conv_w, w_o, w_co, w_m, gt1, sh2, sc2, n2g, rwt)


def _route_kernel(cap, aff_ref, lmat_ref, pos_ref, base_ref, idx_ref, loc_scr, bef_scr):
    e, n = aff_ref.shape
    nt = n // LANES
    aff = aff_ref[...]

    def count(mask):
        return jnp.sum(jnp.where(mask, 1.0, 0.0), axis=1, keepdims=True)

    def search(k, tbits):
        cand = tbits | jnp.left_shift(jnp.int32(1), 30 - k)
        ok = count(aff >= lax.bitcast_convert_type(cand, F32)) >= cap
        return jnp.where(ok, cand, tbits)

    tbits = lax.fori_loop(0, 31, search, jnp.zeros((e, 1), I32))
    thr = lax.bitcast_convert_type(tbits, F32)
    gt = aff > thr
    eq = aff == thr
    need = cap - count(gt)

    def stack(mask):
        m = jnp.where(mask, 1.0, 0.0)
        return jnp.concatenate([m[:, j * LANES:(j + 1) * LANES] for j in range(nt)], axis=0)

    r = lax.broadcasted_iota(I32, (LANES, LANES), 0)
    cl = lax.broadcasted_iota(I32, (LANES, LANES), 1)
    upper = jnp.where(r <= cl, 1.0, 0.0).astype(BF16)
    ones = jnp.ones((LANES, LANES), BF16)
    lmat = lmat_ref[...]

    def cumsum(ms):
        msb = ms.astype(BF16)
        before = _dot(_dot(lmat, msb).astype(BF16), ones)
        return _dot(msb, upper), before

    eq_s = stack(eq)
    loc_eq, before_eq = cumsum(eq_s)
    need_s = jnp.concatenate([need] * nt, axis=0)
    sel = jnp.maximum(stack(gt), jnp.where(loc_eq + before_eq <= need_s, eq_s, 0.0))
    loc, before = cumsum(sel)
    pos_ref[...] = jnp.where(sel > 0.0, loc + before - 1.0, -1.0).astype(I32)
    base_ref[...] = before.astype(I32)

    loc_scr[...] = loc
    bef_scr[...] = before
    slot = lax.broadcasted_iota(I32, (1, cap), 1).astype(F32)
    for ex in range(e):
        loc_e = loc_scr[pl.ds(ex, nt, stride=e), :]
        tprev = bef_scr[pl.ds(ex, nt, stride=e), :][:, 0:1]
        tincl = tprev + loc_e[:, LANES - 1:LANES]
        in_tile = jnp.where((tprev <= slot) & (slot < tincl), 1.0, 0.0)
        tile = jnp.sum(jnp.where(tincl <= slot, 1.0, 0.0), axis=0, keepdims=True)
        s_loc = slot - jnp.sum(in_tile * tprev, axis=0, keepdims=True)
        counts = _dot_tn(loc_e.astype(BF16), in_tile.astype(BF16))
        lane = jnp.sum(jnp.where(counts <= s_loc, 1.0, 0.0), axis=0, keepdims=True)
        idx_ref[pl.ds(ex, 1), :] = (tile * LANES + lane).astype(I32)


def _route(aff_t, cap):
    bsz, e, n = aff_t.shape
    nt = n // LANES
    rows = nt * e
    ri = np.arange(rows)
    lmat = ((ri[None, :] % e == ri[:, None] % e) & (ri[None, :] // e < ri[:, None] // e))
    lmat = jnp.asarray(lmat, BF16)
    st = jax.ShapeDtypeStruct((bsz, rows, LANES), I32)
    ospec = pl.BlockSpec((None, rows, LANES), lambda b: (b, 0, 0))
    return pl.pallas_call(
        functools.partial(_route_kernel, cap),
        grid=(bsz,),
        in_specs=[pl.BlockSpec((None, e, n), lambda b: (b, 0, 0)),
                  pl.BlockSpec((rows, rows), lambda b: (0, 0))],
        out_specs=(ospec, ospec, pl.BlockSpec((None, e, cap), lambda b: (b, 0, 0))),
        out_shape=(st, st, jax.ShapeDtypeStruct((bsz, e, cap), I32)),
        scratch_shapes=[pltpu.VMEM((rows, LANES), F32), pltpu.VMEM((rows, LANES), F32)],
        compiler_params=_cparams("arbitrary"),
        name="route",
    )(aff_t, lmat)


def _slot_window(base):
    return pl.multiple_of((base // BF16_ROWS) * BF16_ROWS, BF16_ROWS)


def _expert_kernel(cap, idx_ref, h2_hbm, wg_ref, wu_ref, wd_ref, ye_ref, xbuf, xb, acc, sem):
    e, f = pl.program_id(1), pl.program_id(2)
    n_exp, nf = pl.num_programs(1), pl.num_programs(2)
    last = pl.num_programs(0) * n_exp - 1
    lin = pl.program_id(0) * n_exp + e
    slot = lin % 2
    per_step = cap // nf
    rpt = xb.shape[1] // LANES

    def row_copy(lin_t, part, i, slot_t):
        tok = idx_ref[(lin_t * nf + part) * per_step + i]
        return pltpu.make_async_copy(h2_hbm.at[pl.ds(pl.multiple_of(tok * rpt, rpt), rpt), :],
                                     xbuf.at[slot_t, part, pl.ds(i * rpt, rpt), :],
                                     sem.at[slot_t])

    def wait_rows(slot_t):
        pltpu.make_async_copy(xbuf.at[slot_t], xbuf.at[slot_t], sem.at[slot_t]).wait()

    @pl.when((lin == 0) & (f == 0))
    def _():
        for part in range(nf):
            def first(i, carry):
                row_copy(lin, part, i, slot).start()
                return carry
            lax.fori_loop(0, per_step, first, 0)

    @pl.when(f == 0)
    def _():
        wait_rows(slot)
        for part in range(nf):
            for a in range(rpt):
                xb[part * per_step:(part + 1) * per_step, a * LANES:(a + 1) * LANES] = (
                    xbuf[slot, part, pl.ds(a, per_step, stride=rpt), :].astype(BF16))

    nxt = jnp.minimum(lin + 1, last)
    for i in range(per_step):
        row_copy(nxt, f, i, 1 - slot).start()

    @pl.when((lin == last) & (f == nf - 1))
    def _():
        wait_rows(1 - slot)

    xv = xb[...]
    hg = _dot(xv, wg_ref[...].astype(BF16))
    hu = _dot(xv, wu_ref[...].astype(BF16))
    hid = (hg * _sigmoid(hg) * hu).astype(BF16)
    part = _dot(hid, wd_ref[...].astype(BF16))

    @pl.when(f == 0)
    def _():
        acc[...] = part

    @pl.when(f > 0)
    def _():
        acc[...] = acc[...] + part

    @pl.when(f == nf - 1)
    def _():
        ye_ref[0:cap, :] = acc[...].astype(BF16)
        ye_ref[cap:, :] = jnp.zeros((ye_ref.shape[0] - cap, ye_ref.shape[1]), BF16)


def _experts(h2, idx, w_gate, w_up, w_down, cap, rows, tf):
    n_exp, d, df = w_gate.shape
    bsz = h2.shape[0]
    rpt = d // LANES
    n = h2.shape[1] // rpt
    nf = df // tf
    assert cap % nf == 0
    rows_global = (idx + (jnp.arange(bsz, dtype=I32) * n)[:, None, None]).reshape(-1)
    return pl.pallas_call(
        functools.partial(_expert_kernel, cap),
        grid_spec=pltpu.PrefetchScalarGridSpec(
            num_scalar_prefetch=1,
            grid=(bsz, n_exp, nf),
            in_specs=[pl.BlockSpec(memory_space=pl.ANY),
                      pl.BlockSpec((None, d, tf), lambda b, e, f, s: (e, 0, f)),
                      pl.BlockSpec((None, d, tf), lambda b, e, f, s: (e, 0, f)),
                      pl.BlockSpec((None, tf, d), lambda b, e, f, s: (e, f, 0))],
            out_specs=pl.BlockSpec((None, None, rows, d), lambda b, e, f, s: (b, e, 0, 0)),
            scratch_shapes=[pltpu.VMEM((2, nf, (cap // nf) * rpt, LANES), F32),
                            pltpu.VMEM((cap, d), BF16),
                            pltpu.VMEM((cap, d), F32), pltpu.SemaphoreType.DMA((2,))]),
        out_shape=jax.ShapeDtypeStruct((bsz, n_exp, rows, d), BF16),
        compiler_params=_cparams("arbitrary", "arbitrary", "arbitrary"),
        name="experts",
    )(rows_global, h2.reshape(bsz * n * rpt, LANES), w_gate, w_up, w_down)


def _combine_kernel(n_exp, win, base_ref, pos_ref, aff_ref, x1_ref, gt2_ref, fg_ref, ye_hbm,
                    out_ref, buf, sem):
    b, j = pl.program_id(0), pl.program_id(1)
    nt = pl.num_programs(1)
    step = b * nt + j
    slot = step % 2

    def window_copy(tile, ex, dst_slot):
        s0 = _slot_window(base_ref[tile * n_exp + ex])
        return pltpu.make_async_copy(ye_hbm.at[tile // nt, ex, pl.ds(s0, win), :],
                                     buf.at[dst_slot, ex], sem.at[dst_slot])

    @pl.when(step == 0)
    def _():
        for ex in range(n_exp):
            window_copy(step, ex, slot).start()

    @pl.when(step + 1 < pl.num_programs(0) * nt)
    def _():
        for ex in range(n_exp):
            window_copy(step + 1, ex, 1 - slot).start()

    srow = lax.broadcasted_iota(I32, (win, LANES), 0)
    gts = []
    for ex in range(n_exp):
        rel = pos_ref[pl.ds(ex, 1), :] - _slot_window(base_ref[step * n_exp + ex])
        gts.append(jnp.where(srow == rel, aff_ref[pl.ds(ex, 1), :], 0.0).astype(BF16))
    gt_all = jnp.concatenate(gts, axis=0)

    for ex in range(n_exp):
        window_copy(step, ex, slot).wait()
    ywin = buf[slot].reshape(n_exp * win, buf.shape[-1])
    ffn = _dot_tn(gt_all, ywin)
    x2 = x1_ref[...] + gt2_ref[...] * ffn
    ms = jnp.mean(x2 * x2, axis=-1, keepdims=True)
    out_ref[...] = x2 * lax.rsqrt(ms + EPS) * fg_ref[...]


def _combine(ye, pos, aff_t, bases, x1, gt2, final_g, win):
    bsz, n, d = x1.shape
    n_exp = aff_t.shape[1]
    nt = n // LANES
    return pl.pallas_call(
        functools.partial(_combine_kernel, n_exp, win),
        grid_spec=pltpu.PrefetchScalarGridSpec(
            num_scalar_prefetch=1,
            grid=(bsz, nt),
            in_specs=[pl.BlockSpec((None, n_exp, LANES), lambda b, j, s: (b, j, 0)),
                      pl.BlockSpec((None, n_exp, LANES), lambda b, j, s: (b, 0, j)),
                      pl.BlockSpec((None, LANES, d), lambda b, j, s: (b, j, 0)),
                      pl.BlockSpec((None, 1, d), lambda b, j, s: (b, 0, 0)),
                      pl.BlockSpec((1, d), lambda b, j, s: (0, 0)),
                      pl.BlockSpec(memory_space=pl.ANY)],
            out_specs=pl.BlockSpec((None, LANES, d), lambda b, j, s: (b, j, 0)),
            scratch_shapes=[pltpu.VMEM((2, n_exp, win, d), BF16), pltpu.SemaphoreType.DMA((2,))]),
        out_shape=jax.ShapeDtypeStruct((bsz, n, d), F32),
        compiler_params=_cparams("arbitrary", "arbitrary"),
        name="combine",
    )(bases, pos, aff_t, x1, gt2, final_g, ye)


def _pick(n, pref):
    t = min(n, pref)
    while n % t:
        t //= 2
    return t


def kernel(x, c, ctx, c_ctx, w_ada, b_ada, norm1_g, norm2_g, w_in, gla_w_a_up, gla_b_a,
           gla_norm_g, gla_w_o, conv_w, conv_w_out, merge_w_out, router_w,
           exp_w_gate, exp_w_up, exp_w_down, final_g):
    bsz, n, d = x.shape
    depth = w_ada.shape[0]
    assert depth == 1, "single-layer trunk"
    rank, dk = gla_w_a_up.shape[-2:]
    dv = gla_w_o.shape[1]
    hv = gla_norm_g.shape[-1]
    heads = dv // hv
    cch = conv_w.shape[-1]
    n_exp = router_w.shape[-1]
    cap = EC_CAPACITY * n // n_exp
    assert 2 * dk == d and dv == d and cch == d and 2 * rank <= LANES
    assert n % LANES == 0 and n % GRID_W == 0 and bsz + 1 <= 8 and n_exp % 8 == 0

    cond = jnp.zeros((8, d), F32).at[:bsz].set(c).at[bsz].set(c_ctx)
    mod = _adaln(cond, w_ada[0], b_ada[0]).reshape(8, N_MOD, d)
    sh1, sc1, gt1, sh2, sc2, gt2 = [mod[:bsz, k][:, None, :] for k in range(N_MOD)]
    csh1, csc1 = mod[bsz:bsz + 1, 0], mod[bsz:bsz + 1, 1]

    w = w_in[0].astype(BF16)
    lr0 = 2 * dk + 2 * dv
    w_main = jnp.concatenate([w[:, :lr0], w[:, lr0 + 2 * rank:]], axis=1)
    w_lr = jnp.zeros((d, LANES), BF16).at[:, :2 * rank].set(w[:, lr0:lr0 + 2 * rank])
    wup = gla_w_a_up[0].astype(BF16)
    wupf = jnp.zeros((LANES, dk), BF16).at[:rank].set(wup[0])
    wupb = jnp.zeros((LANES, dk), BF16).at[rank:2 * rank].set(wup[1])
    baf, bab = gla_b_a[0, 0:1], gla_b_a[0, 1:2]
    n1g, n2g = norm1_g[0:1], norm2_g[0:1]

    s0f, s0b = _ctx_states(ctx, csh1, csc1, n1g, w_main, w_lr, wupf, wupb, baf, bab,
                           heads, dk, dv)
    z, lr = _inproj(x, sh1, sc1, n1g, w_main, w_lr, _pick(n, 1024), _pick(w_main.shape[1], 1024))
    o_f, o_b = _gla(z, lr, wupf, wupb, baf, bab, s0f, s0b, heads, dk, dv, _pick(n, 256))
    x1, h2, aff_t = _mixer(
        o_f, o_b, z, x, jnp.tile(gla_norm_g[0:1], (1, heads)), conv_w[0],
        gla_w_o[0].astype(BF16), conv_w_out[0].astype(BF16), merge_w_out[0].astype(BF16),
        gt1, sh2, sc2, n2g, router_w[0].T.astype(BF16), heads, dk, _pick(n, 256))

    pos, before, idx = _route(aff_t, cap)
    bases = before[:, :, 0].reshape(-1)
    win = LANES + BF16_ROWS
    ye = _experts(h2, idx, exp_w_gate[0], exp_w_up[0], exp_w_down[0], cap, cap + win,
                  _pick(exp_w_gate.shape[-1], 512))
    return _combine(ye, pos, aff_t, bases, x1, gt2, final_g.reshape(1, d), win)
```

```python
import functools

import numpy as np
import jax
import jax.numpy as jnp
from jax import lax
from jax.experimental import pallas as pl
from jax.experimental.pallas import tpu as pltpu

F32 = jnp.float32
BF16 = jnp.bfloat16
I32 = jnp.int32

EPS = 1e-6
N_MOD = 6
GRID_W = 64
GLA_CHUNK = 64
GLA_GATE_NORM = 16.0
EC_CAPACITY = 2

LANES = 128
BF16_ROWS = 16
VMEM_LIMIT = 56 * 1024 * 1024


def _cparams(*sem):
    return pltpu.CompilerParams(dimension_semantics=sem, vmem_limit_bytes=VMEM_LIMIT)


def _dot(a, b):
    return jnp.dot(a, b, preferred_element_type=F32)


def _dot_nt(a, b):
    return lax.dot_general(a, b, (((1,), (1,)), ((), ())), preferred_element_type=F32)


def _dot_tn(a, b):
    return lax.dot_general(a, b, (((0,), (0,)), ((), ())), preferred_element_type=F32)


def _sigmoid(v):
    return 1.0 / (1.0 + jnp.exp(-v))


def _log_sigmoid(v):
    return jnp.minimum(v, 0.0) - jnp.log1p(jnp.exp(-jnp.abs(v)))


def _rms_mod(xv, g, shift, scale):
    ms = jnp.mean(xv * xv, axis=-1, keepdims=True)
    y = xv * lax.rsqrt(ms + EPS) * g
    return y * (1.0 + scale) + shift


def _split_bf16(v):
    hi = v.astype(BF16)
    lo = (v - hi.astype(F32)).astype(BF16)
    return hi, lo


def _adaln_kernel(c_ref, w_ref, b_ref, o_ref):
    cv = c_ref[...]
    s = (cv * _sigmoid(cv)).astype(BF16)
    o_ref[...] = _dot(s, w_ref[...].astype(BF16)) + b_ref[...]


def _adaln(cond, w, b):
    rows, d = cond.shape
    nout = w.shape[1]
    tn = d
    return pl.pallas_call(
        _adaln_kernel,
        grid=(nout // tn,),
        in_specs=[pl.BlockSpec((rows, d), lambda j: (0, 0)),
                  pl.BlockSpec((d, tn), lambda j: (0, j)),
                  pl.BlockSpec((1, tn), lambda j: (0, j))],
        out_specs=pl.BlockSpec((rows, tn), lambda j: (0, j)),
        out_shape=jax.ShapeDtypeStruct((rows, nout), F32),
        compiler_params=_cparams("arbitrary"),
        name="adaln",
    )(cond, w, b.reshape(1, nout))


def _ctx_kernel(heads, ctx_ref, sh_ref, sc_ref, g_ref, wk_ref, wv_ref, wlr_ref,
                wupf_ref, wupb_ref, baf_ref, bab_ref, sf_ref, sb_ref):
    n = ctx_ref.shape[0]
    hc = _rms_mod(ctx_ref[...], g_ref[...], sh_ref[...], sc_ref[...]).astype(BF16)
    k = _dot(hc, wk_ref[...])
    v = _dot(hc, wv_ref[...]).astype(BF16)
    lr = _dot(hc, wlr_ref[...]).astype(BF16)
    hk = k.shape[1] // heads
    hv = v.shape[1] // heads
    row = lax.broadcasted_iota(I32, (n, n), 0)
    col = lax.broadcasted_iota(I32, (n, n), 1)
    for wup_ref, ba_ref, s_ref, tri, last in (
            (wupf_ref, baf_ref, sf_ref, col <= row, n - 1),
            (wupb_ref, bab_ref, sb_ref, col >= row, 0)):
        la = _log_sigmoid(_dot(lr, wup_ref[...]) + ba_ref[...]) * (1.0 / GLA_GATE_NORM)
        hi, lo = _split_bf16(la)
        t = jnp.where(tri, 1.0, 0.0).astype(BF16)
        b = _dot(t, hi) + _dot(t, lo)
        kd = (k * jnp.exp(b[last:last + 1, :] - b)).astype(BF16)
        for h in range(heads):
            s_ref[h] = _dot_tn(v[:, h * hv:(h + 1) * hv], kd[:, h * hk:(h + 1) * hk])


def _ctx_states(ctx, sh, sc, g, w_main, w_lr, wupf, wupb, baf, bab, heads, dk, dv):
    bsz, n, d = ctx.shape
    hk, hv = dk // heads, dv // heads
    full = lambda *shape: pl.BlockSpec(shape, lambda b: (0,) * len(shape))
    st = jax.ShapeDtypeStruct((bsz, heads, hv, hk), F32)
    sspec = pl.BlockSpec((None, heads, hv, hk), lambda b: (b, 0, 0, 0))
    return pl.pallas_call(
        functools.partial(_ctx_kernel, heads),
        grid=(bsz,),
        in_specs=[pl.BlockSpec((None, n, d), lambda b: (b, 0, 0)),
                  full(1, d), full(1, d), full(1, d),
                  pl.BlockSpec((d, dk), lambda b: (0, 1)),
                  pl.BlockSpec((d, dv), lambda b: (0, 2 * dk // dv)),
                  full(d, LANES), full(LANES, dk), full(LANES, dk), full(1, dk), full(1, dk)],
        out_specs=(sspec, sspec),
        out_shape=(st, st),
        compiler_params=_cparams("arbitrary"),
        name="ctx_state",
    )(ctx, sh, sc, g, w_main, w_main, w_lr, wupf, wupb, baf, bab)


def _inproj_kernel(x_ref, sh_ref, sc_ref, g_ref, w_ref, wlr_ref, z_ref, lr_ref, h_scr):
    @pl.when(pl.program_id(2) == 0)
    def _():
        hb = _rms_mod(x_ref[...], g_ref[...], sh_ref[...], sc_ref[...]).astype(BF16)
        h_scr[...] = hb
        lr_ref[...] = _dot(hb, wlr_ref[...])

    z_ref[...] = _dot(h_scr[...], w_ref[...]).astype(BF16)


def _inproj(x, sh, sc, g, w_main, w_lr, tm, tn):
    bsz, n, d = x.shape
    p = w_main.shape[1]
    return pl.pallas_call(
        _inproj_kernel,
        grid=(bsz, n // tm, p // tn),
        in_specs=[pl.BlockSpec((None, tm, d), lambda b, i, j: (b, i, 0)),
                  pl.BlockSpec((None, 1, d), lambda b, i, j: (b, 0, 0)),
                  pl.BlockSpec((None, 1, d), lambda b, i, j: (b, 0, 0)),
                  pl.BlockSpec((1, d), lambda b, i, j: (0, 0)),
                  pl.BlockSpec((d, tn), lambda b, i, j: (0, j)),
                  pl.BlockSpec((d, LANES), lambda b, i, j: (0, 0))],
        out_specs=(pl.BlockSpec((None, tm, tn), lambda b, i, j: (b, i, j)),
                   pl.BlockSpec((None, tm, LANES), lambda b, i, j: (b, i, 0))),
        out_shape=(jax.ShapeDtypeStruct((bsz, n, p), BF16),
                   jax.ShapeDtypeStruct((bsz, n, LANES), F32)),
        scratch_shapes=[pltpu.VMEM((tm, d), BF16)],
        compiler_params=_cparams("arbitrary", "arbitrary", "arbitrary"),
        name="inproj",
    )(x, sh, sc, g, w_main, w_lr)


def _gla_kernel(heads, qf_ref, kf_ref, vf_ref, lrf_ref, qb_ref, kb_ref, vb_ref, lrb_ref,
                wupf_ref, wupb_ref, baf_ref, bab_ref, s0f_ref, s0b_ref,
                of_ref, ob_ref, sf_scr, sb_scr):
    tb, dk = qf_ref.shape
    dv = vf_ref.shape[1]
    hk, hv = dk // heads, dv // heads
    ck = GLA_CHUNK
    nck = tb // ck
    q_scale = hk ** -0.5

    @pl.when(pl.program_id(1) == 0)
    def _():
        sf_scr[...] = s0f_ref[...]
        sb_scr[...] = s0b_ref[...]

    row = lax.broadcasted_iota(I32, (tb, tb), 0)
    col = lax.broadcasted_iota(I32, (tb, tb), 1)
    shift = ck.bit_length() - 1
    same_chunk = (row >> shift) == (col >> shift)
    crow = lax.broadcasted_iota(I32, (ck, ck), 0)
    ccol = lax.broadcasted_iota(I32, (ck, ck), 1)

    dirs = (
        (qf_ref, kf_ref, vf_ref, lrf_ref, wupf_ref, baf_ref, of_ref, sf_scr,
         same_chunk & (col <= row), ccol <= crow, ck - 1, range(nck)),
        (qb_ref, kb_ref, vb_ref, lrb_ref, wupb_ref, bab_ref, ob_ref, sb_scr,
         same_chunk & (col >= row), ccol >= crow, 0, range(nck - 1, -1, -1)),
    )
    b_alls = []
    for _, _, _, lr_ref, wup_ref, ba_ref, _, _, tri, _, _, _ in dirs:
        la = _log_sigmoid(_dot(lr_ref[...].astype(BF16), wup_ref[...]) + ba_ref[...])
        la = la * (1.0 / GLA_GATE_NORM)
        hi, lo = _split_bf16(la)
        t = jnp.where(tri, 1.0, 0.0).astype(BF16)
        b_alls.append(_dot(t, hi) + _dot(t, lo))
    work = [[], []]
    for k in range(nck):
        for di, (q_ref, k_ref, v_ref, _, _, _, _, _, _, cmask, last, order) in enumerate(dirs):
            r0 = order[k] * ck
            b = b_alls[di][r0:r0 + ck, :]
            b_last = b[last:last + 1, :]
            qc = q_ref[r0:r0 + ck, :].astype(F32) * q_scale
            kc = k_ref[r0:r0 + ck, :].astype(F32)
            q_in = (qc * jnp.exp(b)).astype(BF16)
            k_in = (kc * jnp.exp(-b)).astype(BF16)
            k_dec = (kc * jnp.exp(b_last - b)).astype(BF16)
            decay = jnp.exp(b_last)
            units = []
            for h in range(heads):
                qh = q_in[:, h * hk:(h + 1) * hk]
                vh = v_ref[r0:r0 + ck, h * hv:(h + 1) * hv]
                scores = jnp.where(cmask, _dot_nt(qh, k_in[:, h * hk:(h + 1) * hk]), 0.0)
                o_intra = _dot(scores.astype(BF16), vh)
                ut = _dot_tn(vh, k_dec[:, h * hk:(h + 1) * hk])
                units.append((qh, o_intra, ut, decay[:, h * hk:(h + 1) * hk]))
            work[di].append((r0, units))
    for k in range(nck):
        for di, (_, _, _, _, _, _, o_ref, s_scr, _, _, _, _) in enumerate(dirs):
            r0, units = work[di][k]
            for h, (qh, o_intra, ut, dec) in enumerate(units):
                st = s_scr[h]
                o = o_intra + _dot_nt(qh, st.astype(BF16))
                o_ref[r0:r0 + ck, h * hv:(h + 1) * hv] = o.astype(BF16)
                s_scr[h] = st * dec + ut


def _gla(z, lr, wupf, wupb, baf, bab, s0f, s0b, heads, dk, dv, tb):
    bsz, n, _ = z.shape
    nb = n // tb
    hk, hv = dk // heads, dv // heads
    fwd = lambda cb: (lambda b, i: (b, i, cb))
    bwd = lambda cb: (lambda b, i: (b, nb - 1 - i, cb))
    full = lambda *shape: pl.BlockSpec(shape, lambda b, i: (0,) * len(shape))
    sspec = pl.BlockSpec((None, heads, hv, hk), lambda b, i: (b, 0, 0, 0))
    vcb = 2 * dk // dv
    ost = jax.ShapeDtypeStruct((bsz, n, dv), BF16)
    return pl.pallas_call(
        functools.partial(_gla_kernel, heads),
        grid=(bsz, nb),
        in_specs=[pl.BlockSpec((None, tb, dk), fwd(0)), pl.BlockSpec((None, tb, dk), fwd(1)),
                  pl.BlockSpec((None, tb, dv), fwd(vcb)), pl.BlockSpec((None, tb, LANES), fwd(0)),
                  pl.BlockSpec((None, tb, dk), bwd(0)), pl.BlockSpec((None, tb, dk), bwd(1)),
                  pl.BlockSpec((None, tb, dv), bwd(vcb)), pl.BlockSpec((None, tb, LANES), bwd(0)),
                  full(LANES, dk), full(LANES, dk), full(1, dk), full(1, dk), sspec, sspec],
        out_specs=(pl.BlockSpec((None, tb, dv), fwd(0)), pl.BlockSpec((None, tb, dv), bwd(0))),
        out_shape=(ost, ost),
        scratch_shapes=[pltpu.VMEM((heads, hv, hk), F32), pltpu.VMEM((heads, hv, hk), F32)],
        compiler_params=_cparams("arbitrary", "arbitrary"),
        name="gla",
    )(z, z, z, lr, z, z, z, lr, wupf, wupb, baf, bab, s0f, s0b)


def _mixer_kernel(heads, of_ref, ob_ref, g_ref, u_ref, bg_ref, cg_ref, rg_ref, rc_ref, x_ref,
                  gng_ref, cw_ref, wo_ref, wco_ref, wm_ref, gt1_ref, sh2_ref, sc2_ref, n2g_ref,
                  rwt_ref, x1_ref, h2_ref, aff_ref):
    tm, dv = of_ref.shape
    hv = dv // heads
    o = of_ref[...].astype(F32) + ob_ref[...].astype(F32)
    parts = []
    for h in range(heads):
        oh = o[:, h * hv:(h + 1) * hv]
        ms = jnp.mean(oh * oh, axis=-1, keepdims=True)
        parts.append(oh * lax.rsqrt(ms + EPS))
    on = jnp.concatenate(parts, axis=1) * gng_ref[...]
    g = g_ref[...].astype(F32)
    y_gla = _dot((on * (g * _sigmoid(g))).astype(BF16), wo_ref[...])

    cu = cg_ref[...].astype(F32) * u_ref[...].astype(F32)
    gcol = lax.broadcasted_iota(I32, cu.shape, 0) & (GRID_W - 1)
    left = jnp.where(gcol == 0, 0.0, pltpu.roll(cu, 1, 0))
    right = jnp.where(gcol == GRID_W - 1, 0.0, pltpu.roll(cu, tm - 1, 0))
    cw = cw_ref[...]
    conv = left * cw[0:1, :] + cu * cw[1:2, :] + right * cw[2:3, :]
    y_conv = _dot((bg_ref[...].astype(F32) * conv).astype(BF16), wco_ref[...])

    merged = (_sigmoid(rg_ref[...].astype(F32)) * y_gla
              + _sigmoid(rc_ref[...].astype(F32)) * y_conv)
    y = _dot(merged.astype(BF16), wm_ref[...])
    x1 = x_ref[...] + gt1_ref[...] * y
    x1_ref[...] = x1
    h2 = _rms_mod(x1, n2g_ref[...], sh2_ref[...], sc2_ref[...])
    rpt = h2.shape[1] // LANES
    for a in range(rpt):
        h2_ref[pl.ds(a, tm, stride=rpt), :] = h2[:, a * LANES:(a + 1) * LANES]
    logits = _dot_nt(rwt_ref[...], h2.astype(BF16))
    ex = jnp.exp(logits - jnp.max(logits, axis=0, keepdims=True))
    aff_ref[...] = ex / jnp.sum(ex, axis=0, keepdims=True)


def _mixer(o_f, o_b, z, x, gng, conv_w, w_o, w_co, w_m, gt1, sh2, sc2, n2g, rwt, heads, dk, tm):
    bsz, n, d = x.shape
    dv = o_f.shape[2]
    c = conv_w.shape[1]
    e = rwt.shape[0]
    zb = lambda cb: pl.BlockSpec((None, tm, d), lambda b, i: (b, i, cb))
    base = (2 * dk + dv) // d
    tok = lambda w: pl.BlockSpec((None, tm, w), lambda b, i: (b, i, 0))
    full = lambda *shape: pl.BlockSpec(shape, lambda b, i: (0,) * len(shape))
    perb = pl.BlockSpec((None, 1, d), lambda b, i: (b, 0, 0))
    return pl.pallas_call(
        functools.partial(_mixer_kernel, heads),
        grid=(bsz, n // tm),
        in_specs=[tok(dv), tok(dv), zb(base), zb(base + 1), zb(base + 2), zb(base + 3),
                  zb(base + 4), zb(base + 5), tok(d),
                  full(1, dv), full(3, c), full(dv, d), full(c, d), full(d, d),
                  perb, perb, perb, full(1, d), full(e, d)],
        out_specs=(tok(d), pl.BlockSpec((None, tm * (d // LANES), LANES), lambda b, i: (b, i, 0)),
                   pl.BlockSpec((None, e, tm), lambda b, i: (b, 0, i))),
        out_shape=(jax.ShapeDtypeStruct((bsz, n, d), F32),
                   jax.ShapeDtypeStruct((bsz, n * (d // LANES), LANES), F32),
                   jax.ShapeDtypeStruct((bsz, e, n), F32)),
        compiler_params=_cparams("arbitrary", "arbitrary"),
        name="mixer",
    )(o_f, o_b, z, z, z, z, z, z, x, gng, conv_w, w_o, w_co, w_m, gt1, sh2, sc2, n2g, rwt)


def _route_kernel(cap, aff_ref, lmat_ref, pos_ref, base_ref, idx_ref, loc_scr, bef_scr):
    e, n = aff_ref.shape
    nt = n // LANES
    aff = aff_ref[...]

    def count(mask):
        return jnp.sum(jnp.where(mask, 1.0, 0.0), axis=1, keepdims=True)

    def search(k, tbits):
        cand = tbits | jnp.left_shift(jnp.int32(1), 30 - k)
        ok = count(aff >= lax.bitcast_convert_type(cand, F32)) >= cap
        return jnp.where(ok, cand, tbits)

    tbits = lax.fori_loop(0, 31, search, jnp.zeros((e, 1), I32))
    thr = lax.bitcast_convert_type(tbits, F32)
    gt = aff > thr
    eq = aff == thr
    need = cap - count(gt)

    def stack(mask):
        m = jnp.where(mask, 1.0, 0.0)
        return jnp.concatenate([m[:, j * LANES:(j + 1) * LANES] for j in range(nt)], axis=0)

    r = lax.broadcasted_iota(I32, (LANES, LANES), 0)
    cl = lax.broadcasted_iota(I32, (LANES, LANES), 1)
    upper = jnp.where(r <= cl, 1.0, 0.0).astype(BF16)
    ones = jnp.ones((LANES, LANES), BF16)
    lmat = lmat_ref[...]

    def cumsum(ms):
        msb = ms.astype(BF16)
        before = _dot(_dot(lmat, msb).astype(BF16), ones)
        return _dot(msb, upper), before

    eq_s = stack(eq)
    loc_eq, before_eq = cumsum(eq_s)
    need_s = jnp.concatenate([need] * nt, axis=0)
    sel = jnp.maximum(stack(gt), jnp.where(loc_eq + before_eq <= need_s, eq_s, 0.0))
    loc, before = cumsum(sel)
    pos_ref[...] = jnp.where(sel > 0.0, loc + before - 1.0, -1.0).astype(I32)
    base_ref[...] = before.astype(I32)

    loc_scr[...] = loc
    bef_scr[...] = before
    slot = lax.broadcasted_iota(I32, (1, cap), 1).astype(F32)
    for ex in range(e):
        loc_e = loc_scr[pl.ds(ex, nt, stride=e), :]
        tprev = bef_scr[pl.ds(ex, nt, stride=e), :][:, 0:1]
        tincl = tprev + loc_e[:, LANES - 1:LANES]
        in_tile = jnp.where((tprev <= slot) & (slot < tincl), 1.0, 0.0)
        tile = jnp.sum(jnp.where(tincl <= slot, 1.0, 0.0), axis=0, keepdims=True)
        s_loc = slot - jnp.sum(in_tile * tprev, axis=0, keepdims=True)
        counts = _dot_tn(loc_e.astype(BF16), in_tile.astype(BF16))
        lane = jnp.sum(jnp.where(counts <= s_loc, 1.0, 0.0), axis=0, keepdims=True)
        idx_ref[pl.ds(ex, 1), :] = (tile * LANES + lane).astype(I32)


def _route(aff_t, cap):
    bsz, e, n = aff_t.shape
    nt = n // LANES
    rows = nt * e
    ri = np.arange(rows)
    lmat = ((ri[None, :] % e == ri[:, None] % e) & (ri[None, :] // e < ri[:, None] // e))
    lmat = jnp.asarray(lmat, BF16)
    st = jax.ShapeDtypeStruct((bsz, rows, LANES), I32)
    ospec = pl.BlockSpec((None, rows, LANES), lambda b: (b, 0, 0))
    return pl.pallas_call(
        functools.partial(_route_kernel, cap),
        grid=(bsz,),
        in_specs=[pl.BlockSpec((None, e, n), lambda b: (b, 0, 0)),
                  pl.BlockSpec((rows, rows), lambda b: (0, 0))],
        out_specs=(ospec, ospec, pl.BlockSpec((None, e, cap), lambda b: (b, 0, 0))),
        out_shape=(st, st, jax.ShapeDtypeStruct((bsz, e, cap), I32)),
        scratch_shapes=[pltpu.VMEM((rows, LANES), F32), pltpu.VMEM((rows, LANES), F32)],
        compiler_params=_cparams("arbitrary"),
        name="route",
    )(aff_t, lmat)


def _slot_window(base):
    return pl.multiple_of((base // BF16_ROWS) * BF16_ROWS, BF16_ROWS)


def _expert_kernel(cap, idx_ref, h2_hbm, wg_ref, wu_ref, wd_ref, ye_ref, xbuf, xb, acc, sem):
    e, f = pl.program_id(1), pl.program_id(2)
    n_exp, nf = pl.num_programs(1), pl.num_programs(2)
    last = pl.num_programs(0) * n_exp - 1
    lin = pl.program_id(0) * n_exp + e
    slot = lin % 2
    per_step = cap // nf
    rpt = xb.shape[1] // LANES

    def row_copy(lin_t, part, i, slot_t):
        tok = idx_ref[(lin_t * nf + part) * per_step + i]
        return pltpu.make_async_copy(h2_hbm.at[pl.ds(pl.multiple_of(tok * rpt, rpt), rpt), :],
                                     xbuf.at[slot_t, part, pl.ds(i * rpt, rpt), :],
                                     sem.at[slot_t])

    def wait_rows(slot_t):
        pltpu.make_async_copy(xbuf.at[slot_t], xbuf.at[slot_t], sem.at[slot_t]).wait()

    @pl.when((lin == 0) & (f == 0))
    def _():
        for part in range(nf):
            def first(i, carry):
                row_copy(lin, part, i, slot).start()
                return carry
            lax.fori_loop(0, per_step, first, 0)

    @pl.when(f == 0)
    def _():
        wait_rows(slot)
        acc[...] = jnp.zeros_like(acc)
        for part in range(nf):
            for a in range(rpt):
                xb[part * per_step:(part + 1) * per_step, a * LANES:(a + 1) * LANES] = (
                    xbuf[slot, part, pl.ds(a, per_step, stride=rpt), :].astype(BF16))

    nxt = jnp.minimum(lin + 1, last)
    for i in range(per_step):
        row_copy(nxt, f, i, 1 - slot).start()

    @pl.when((lin == last) & (f == nf - 1))
    def _():
        wait_rows(1 - slot)

    xv = xb[...]
    hg = _dot(xv, wg_ref[...].astype(BF16))
    hu = _dot(xv, wu_ref[...].astype(BF16))
    hid = (hg * _sigmoid(hg) * hu).astype(BF16)
    acc[...] += _dot(hid, wd_ref[...].astype(BF16))

    @pl.when(f == nf - 1)
    def _():
        ye_ref[0:cap, :] = acc[...].astype(BF16)
        ye_ref[cap:, :] = jnp.zeros((ye_ref.shape[0] - cap, ye_ref.shape[1]), BF16)


def _experts(h2, idx, w_gate, w_up, w_down, cap, rows, tf):
    n_exp, d, df = w_gate.shape
    bsz = h2.shape[0]
    rpt = d // LANES
    n = h2.shape[1] // rpt
    nf = df // tf
    assert cap % nf == 0
    rows_global = (idx + (jnp.arange(bsz, dtype=I32) * n)[:, None, None]).reshape(-1)
    return pl.pallas_call(
        functools.partial(_expert_kernel, cap),
        grid_spec=pltpu.PrefetchScalarGridSpec(
            num_scalar_prefetch=1,
            grid=(bsz, n_exp, nf),
            in_specs=[pl.BlockSpec(memory_space=pl.ANY),
                      pl.BlockSpec((None, d, tf), lambda b, e, f, s: (e, 0, f)),
                      pl.BlockSpec((None, d, tf), lambda b, e, f, s: (e, 0, f)),
                      pl.BlockSpec((None, tf, d), lambda b, e, f, s: (e, f, 0))],
            out_specs=pl.BlockSpec((None, None, rows, d), lambda b, e, f, s: (b, e, 0, 0)),
            scratch_shapes=[pltpu.VMEM((2, nf, (cap // nf) * rpt, LANES), F32),
                            pltpu.VMEM((cap, d), BF16),
                            pltpu.VMEM((cap, d), F32), pltpu.SemaphoreType.DMA((2,))]),
        out_shape=jax.ShapeDtypeStruct((bsz, n_exp, rows, d), BF16),
        compiler_params=_cparams("arbitrary", "arbitrary", "arbitrary"),
        name="experts",
    )(rows_global, h2.reshape(bsz * n * rpt, LANES), w_gate, w_up, w_down)


def _combine_kernel(n_exp, win, base_ref, pos_ref, aff_ref, x1_ref, gt2_ref, fg_ref, ye_hbm,
                    out_ref, buf, sem):
    b, j = pl.program_id(0), pl.program_id(1)
    nt = pl.num_programs(1)
    step = b * nt + j
    slot = step % 2

    def window_copy(tile, ex, dst_slot):
        s0 = _slot_window(base_ref[tile * n_exp + ex])
        return pltpu.make_async_copy(ye_hbm.at[tile // nt, ex, pl.ds(s0, win), :],
                                     buf.at[dst_slot, ex], sem.at[dst_slot])

    @pl.when(step == 0)
    def _():
        for ex in range(n_exp):
            window_copy(step, ex, slot).start()

    @pl.when(step + 1 < pl.num_programs(0) * nt)
    def _():
        for ex in range(n_exp):
            window_copy(step + 1, ex, 1 - slot).start()

    srow = lax.broadcasted_iota(I32, (win, LANES), 0)
    gts = []
    for ex in range(n_exp):
        rel = pos_ref[pl.ds(ex, 1), :] - _slot_window(base_ref[step * n_exp + ex])
        gts.append(jnp.where(srow == rel, aff_ref[pl.ds(ex, 1), :], 0.0).astype(BF16))
    gt_all = jnp.concatenate(gts, axis=0)

    for ex in range(n_exp):
        window_copy(step, ex, slot).wait()
    ywin = buf[slot].reshape(n_exp * win, buf.shape[-1])
    ffn = _dot_tn(gt_all, ywin)
    x2 = x1_ref[...] + gt2_ref[...] * ffn
    ms = jnp.mean(x2 * x2, axis=-1, keepdims=True)
    out_ref[...] = x2 * lax.rsqrt(ms + EPS) * fg_ref[...]


def _combine(ye, pos, aff_t, bases, x1, gt2, final_g, win):
    bsz, n, d = x1.shape
    n_exp = aff_t.shape[1]
    nt = n // LANES
    return pl.pallas_call(
        functools.partial(_combine_kernel, n_exp, win),
        grid_spec=pltpu.PrefetchScalarGridSpec(
            num_scalar_prefetch=1,
            grid=(bsz, nt),
            in_specs=[pl.BlockSpec((None, n_exp, LANES), lambda b, j, s: (b, j, 0)),
                      pl.BlockSpec((None, n_exp, LANES), lambda b, j, s: (b, 0, j)),
                      pl.BlockSpec((None, LANES, d), lambda b, j, s: (b, j, 0)),
                      pl.BlockSpec((None, 1, d), lambda b, j, s: (b, 0, 0)),
                      pl.BlockSpec((1, d), lambda b, j, s: (0, 0)),
                      pl.BlockSpec(memory_space=pl.ANY)],
            out_specs=pl.BlockSpec((None, LANES, d), lambda b, j, s: (b, j, 0)),
            scratch_shapes=[pltpu.VMEM((2, n_exp, win, d), BF16), pltpu.SemaphoreType.DMA((2,))]),
        out_shape=jax.ShapeDtypeStruct((bsz, n, d), F32),
        compiler_params=_cparams("arbitrary", "arbitrary"),
        name="combine",
    )(bases, pos, aff_t, x1, gt2, final_g, ye)


def _pick(n, pref):
    t = min(n, pref)
    while n % t:
        t //= 2
    return t


def kernel(x, c, ctx, c_ctx, w_ada, b_ada, norm1_g, norm2_g, w_in, gla_w_a_up, gla_b_a,
           gla_norm_g, gla_w_o, conv_w, conv_w_out, merge_w_out, router_w,
           exp_w_gate, exp_w_up, exp_w_down, final_g):
    bsz, n, d = x.shape
    depth = w_ada.shape[0]
    assert depth == 1, "single-layer trunk"
    rank, dk = gla_w_a_up.shape[-2:]
    dv = gla_w_o.shape[1]
    hv = gla_norm_g.shape[-1]
    heads = dv // hv
    cch = conv_w.shape[-1]
    n_exp = router_w.shape[-1]
    cap = EC_CAPACITY * n // n_exp
    assert 2 * dk == d and dv == d and cch == d and 2 * rank <= LANES
    assert n % LANES == 0 and n % GRID_W == 0 and bsz + 1 <= 8 and n_exp % 8 == 0

    cond = jnp.zeros((8, d), F32).at[:bsz].set(c).at[bsz].set(c_ctx)
    mod = _adaln(cond, w_ada[0], b_ada[0]).reshape(8, N_MOD, d)
    sh1, sc1, gt1, sh2, sc2, gt2 = [mod[:bsz, k][:, None, :] for k in range(N_MOD)]
    csh1, csc1 = mod[bsz:bsz + 1, 0], mod[bsz:bsz + 1, 1]

    w = w_in[0].astype(BF16)
    lr0 = 2 * dk + 2 * dv
    w_main = jnp.concatenate([w[:, :lr0], w[:, lr0 + 2 * rank:]], axis=1)
    w_lr = jnp.zeros((d, LANES), BF16).at[:, :2 * rank].set(w[:, lr0:lr0 + 2 * rank])
    wup = gla_w_a_up[0].astype(BF16)
    wupf = jnp.zeros((LANES, dk), BF16).at[:rank].set(wup[0])
    wupb = jnp.zeros((LANES, dk), BF16).at[rank:2 * rank].set(wup[1])
    baf, bab = gla_b_a[0, 0:1], gla_b_a[0, 1:2]
    n1g, n2g = norm1_g[0:1], norm2_g[0:1]

    s0f, s0b = _ctx_states(ctx, csh1, csc1, n1g, w_main, w_lr, wupf, wupb, baf, bab,
                           heads, dk, dv)
    z, lr = _inproj(x, sh1, sc1, n1g, w_main, w_lr, _pick(n, 1024), _pick(w_main.shape[1], 2048))
    o_f, o_b = _gla(z, lr, wupf, wupb, baf, bab, s0f, s0b, heads, dk, dv, _pick(n, 512))
    x1, h2, aff_t = _mixer(
        o_f, o_b, z, x, jnp.tile(gla_norm_g[0:1], (1, heads)), conv_w[0],
        gla_w_o[0].astype(BF16), conv_w_out[0].astype(BF16), merge_w_out[0].astype(BF16),
        gt1, sh2, sc2, n2g, router_w[0].T.astype(BF16), heads, dk, _pick(n, 256))

    pos, before, idx = _route(aff_t, cap)
    bases = before[:, :, 0].reshape(-1)
    win = LANES + BF16_ROWS
    ye = _experts(h2, idx, exp_w_gate[0], exp_w_up[0], exp_w_down[0], cap, cap + win,
                  _pick(exp_w_gate.shape[-1], 512))
    return _combine(ye, pos, aff_t, bases, x1, gt2, final_g.reshape(1, d), win)
```

```python
import functools

import numpy as np
import jax
import jax.numpy as jnp
from jax import lax
from jax.experimental import pallas as pl
from jax.experimental.pallas import tpu as pltpu

F32 = jnp.float32
BF16 = jnp.bfloat16
I32 = jnp.int32

EPS = 1e-6
N_MOD = 6
GRID_W = 64
GLA_CHUNK = 64
GLA_GATE_NORM = 16.0
EC_CAPACITY = 2

LANES = 128
BF16_ROWS = 16
VMEM_LIMIT = 56 * 1024 * 1024


def _cparams(*sem):
    return pltpu.CompilerParams(dimension_semantics=sem, vmem_limit_bytes=VMEM_LIMIT)


def _dot(a, b):
    return jnp.dot(a, b, preferred_element_type=F32)


def _dot_nt(a, b):
    return lax.dot_general(a, b, (((1,), (1,)), ((), ())), preferred_element_type=F32)


def _dot_tn(a, b):
    return lax.dot_general(a, b, (((0,), (0,)), ((), ())), preferred_element_type=F32)


def _sigmoid(v):
    return 1.0 / (1.0 + jnp.exp(-v))


def _log_sigmoid(v):
    return jnp.minimum(v, 0.0) - jnp.log1p(jnp.exp(-jnp.abs(v)))


def _rms_mod(xv, g, shift, scale):
    ms = jnp.mean(xv * xv, axis=-1, keepdims=True)
    y = xv * lax.rsqrt(ms + EPS) * g
    return y * (1.0 + scale) + shift


def _split_bf16(v):
    hi = v.astype(BF16)
    lo = (v - hi.astype(F32)).astype(BF16)
    return hi, lo


def _adaln_kernel(c_ref, w_ref, b_ref, o_ref):
    cv = c_ref[...]
    s = (cv * _sigmoid(cv)).astype(BF16)
    o_ref[...] = _dot(s, w_ref[...].astype(BF16)) + b_ref[...]


def _adaln(cond, w, b):
    rows, d = cond.shape
    nout = w.shape[1]
    tn = d
    return pl.pallas_call(
        _adaln_kernel,
        grid=(nout // tn,),
        in_specs=[pl.BlockSpec((rows, d), lambda j: (0, 0)),
                  pl.BlockSpec((d, tn), lambda j: (0, j)),
                  pl.BlockSpec((1, tn), lambda j: (0, j))],
        out_specs=pl.BlockSpec((rows, tn), lambda j: (0, j)),
        out_shape=jax.ShapeDtypeStruct((rows, nout), F32),
        compiler_params=_cparams("arbitrary"),
        name="adaln",
    )(cond, w, b.reshape(1, nout))


def _ctx_kernel(heads, ctx_ref, sh_ref, sc_ref, g_ref, wk_ref, wv_ref, wlr_ref,
                wupf_ref, wupb_ref, baf_ref, bab_ref, sf_ref, sb_ref):
    n = ctx_ref.shape[0]
    hc = _rms_mod(ctx_ref[...], g_ref[...], sh_ref[...], sc_ref[...]).astype(BF16)
    k = _dot(hc, wk_ref[...])
    v = _dot(hc, wv_ref[...]).astype(BF16)
    lr = _dot(hc, wlr_ref[...]).astype(BF16)
    hk = k.shape[1] // heads
    hv = v.shape[1] // heads
    row = lax.broadcasted_iota(I32, (n, n), 0)
    col = lax.broadcasted_iota(I32, (n, n), 1)
    for wup_ref, ba_ref, s_ref, tri, last in (
            (wupf_ref, baf_ref, sf_ref, col <= row, n - 1),
            (wupb_ref, bab_ref, sb_ref, col >= row, 0)):
        la = _log_sigmoid(_dot(lr, wup_ref[...]) + ba_ref[...]) * (1.0 / GLA_GATE_NORM)
        hi, lo = _split_bf16(la)
        t = jnp.where(tri, 1.0, 0.0).astype(BF16)
        b = _dot(t, hi) + _dot(t, lo)
        kd = (k * jnp.exp(b[last:last + 1, :] - b)).astype(BF16)
        for h in range(heads):
            s_ref[h] = _dot_tn(v[:, h * hv:(h + 1) * hv], kd[:, h * hk:(h + 1) * hk])


def _ctx_states(ctx, sh, sc, g, w_main, w_lr, wupf, wupb, baf, bab, heads, dk, dv):
    bsz, n, d = ctx.shape
    hk, hv = dk // heads, dv // heads
    full = lambda *shape: pl.BlockSpec(shape, lambda b: (0,) * len(shape))
    st = jax.ShapeDtypeStruct((bsz, heads, hv, hk), F32)
    sspec = pl.BlockSpec((None, heads, hv, hk), lambda b: (b, 0, 0, 0))
    return pl.pallas_call(
        functools.partial(_ctx_kernel, heads),
        grid=(bsz,),
        in_specs=[pl.BlockSpec((None, n, d), lambda b: (b, 0, 0)),
                  full(1, d), full(1, d), full(1, d),
                  pl.BlockSpec((d, dk), lambda b: (0, 1)),
                  pl.BlockSpec((d, dv), lambda b: (0, 2 * dk // dv)),
                  full(d, LANES), full(LANES, dk), full(LANES, dk), full(1, dk), full(1, dk)],
        out_specs=(sspec, sspec),
        out_shape=(st, st),
        compiler_params=_cparams("arbitrary"),
        name="ctx_state",
    )(ctx, sh, sc, g, w_main, w_main, w_lr, wupf, wupb, baf, bab)


def _inproj_kernel(x_ref, sh_ref, sc_ref, g_ref, w_ref, wlr_ref, z_ref, lr_ref, h_scr):
    @pl.when(pl.program_id(2) == 0)
    def _():
        hb = _rms_mod(x_ref[...], g_ref[...], sh_ref[...], sc_ref[...]).astype(BF16)
        h_scr[...] = hb
        lr_ref[...] = _dot(hb, wlr_ref[...])

    z_ref[...] = _dot(h_scr[...], w_ref[...]).astype(BF16)


def _inproj(x, sh, sc, g, w_main, w_lr, tm, tn):
    bsz, n, d = x.shape
    p = w_main.shape[1]
    return pl.pallas_call(
        _inproj_kernel,
        grid=(bsz, n // tm, p // tn),
        in_specs=[pl.BlockSpec((None, tm, d), lambda b, i, j: (b, i, 0)),
                  pl.BlockSpec((None, 1, d), lambda b, i, j: (b, 0, 0)),
                  pl.BlockSpec((None, 1, d), lambda b, i, j: (b, 0, 0)),
                  pl.BlockSpec((1, d), lambda b, i, j: (0, 0)),
                  pl.BlockSpec((d, tn), lambda b, i, j: (0, j)),
                  pl.BlockSpec((d, LANES), lambda b, i, j: (0, 0))],
        out_specs=(pl.BlockSpec((None, tm, tn), lambda b, i, j: (b, i, j)),
                   pl.BlockSpec((None, tm, LANES), lambda b, i, j: (b, i, 0))),
        out_shape=(jax.ShapeDtypeStruct((bsz, n, p), BF16),
                   jax.ShapeDtypeStruct((bsz, n, LANES), F32)),
        scratch_shapes=[pltpu.VMEM((tm, d), BF16)],
        compiler_params=_cparams("arbitrary", "arbitrary", "arbitrary"),
        name="inproj",
    )(x, sh, sc, g, w_main, w_lr)


def _gla_kernel(heads, qf_ref, kf_ref, vf_ref, lrf_ref, qb_ref, kb_ref, vb_ref, lrb_ref,
                wupf_ref, wupb_ref, baf_ref, bab_ref, s0f_ref, s0b_ref,
                of_ref, ob_ref, sf_scr, sb_scr):
    tb, dk = qf_ref.shape
    dv = vf_ref.shape[1]
    hk, hv = dk // heads, dv // heads
    ck = GLA_CHUNK
    nck = tb // ck
    q_scale = hk ** -0.5

    @pl.when(pl.program_id(1) == 0)
    def _():
        sf_scr[...] = s0f_ref[...]
        sb_scr[...] = s0b_ref[...]

    row = lax.broadcasted_iota(I32, (tb, tb), 0)
    col = lax.broadcasted_iota(I32, (tb, tb), 1)
    shift = ck.bit_length() - 1
    same_chunk = (row >> shift) == (col >> shift)
    crow = lax.broadcasted_iota(I32, (ck, ck), 0)
    ccol = lax.broadcasted_iota(I32, (ck, ck), 1)

    dirs = (
        (qf_ref, kf_ref, vf_ref, lrf_ref, wupf_ref, baf_ref, of_ref, sf_scr,
         same_chunk & (col <= row), ccol <= crow, ck - 1, range(nck)),
        (qb_ref, kb_ref, vb_ref, lrb_ref, wupb_ref, bab_ref, ob_ref, sb_scr,
         same_chunk & (col >= row), ccol >= crow, 0, range(nck - 1, -1, -1)),
    )
    b_alls = []
    for _, _, _, lr_ref, wup_ref, ba_ref, _, _, tri, _, _, _ in dirs:
        la = _log_sigmoid(_dot(lr_ref[...].astype(BF16), wup_ref[...]) + ba_ref[...])
        la = la * (1.0 / GLA_GATE_NORM)
        hi, lo = _split_bf16(la)
        t = jnp.where(tri, 1.0, 0.0).astype(BF16)
        b_alls.append(_dot(t, hi) + _dot(t, lo))
    work = [[], []]
    for k in range(nck):
        for di, (q_ref, k_ref, v_ref, _, _, _, _, _, _, cmask, last, order) in enumerate(dirs):
            r0 = order[k] * ck
            b = b_alls[di][r0:r0 + ck, :]
            b_last = b[last:last + 1, :]
            qc = q_ref[r0:r0 + ck, :].astype(F32) * q_scale
            kc = k_ref[r0:r0 + ck, :].astype(F32)
            q_in = (qc * jnp.exp(b)).astype(BF16)
            k_in = (kc * jnp.exp(-b)).astype(BF16)
            k_dec = (kc * jnp.exp(b_last - b)).astype(BF16)
            decay = jnp.exp(b_last)
            units = []
            for h in range(heads):
                qh = q_in[:, h * hk:(h + 1) * hk]
                vh = v_ref[r0:r0 + ck, h * hv:(h + 1) * hv]
                scores = jnp.where(cmask, _dot_nt(qh, k_in[:, h * hk:(h + 1) * hk]), 0.0)
                o_intra = _dot(scores.astype(BF16), vh)
                ut = _dot_tn(vh, k_dec[:, h * hk:(h + 1) * hk])
                units.append((qh, o_intra, ut, decay[:, h * hk:(h + 1) * hk]))
            work[di].append((r0, units))
    for k in range(nck):
        for di, (_, _, _, _, _, _, o_ref, s_scr, _, _, _, _) in enumerate(dirs):
            r0, units = work[di][k]
            for h, (qh, o_intra, ut, dec) in enumerate(units):
                st = s_scr[h]
                o = o_intra + _dot_nt(qh, st.astype(BF16))
                o_ref[r0:r0 + ck, h * hv:(h + 1) * hv] = o.astype(BF16)
                s_scr[h] = st * dec + ut


def _gla(z, lr, wupf, wupb, baf, bab, s0f, s0b, heads, dk, dv, tb):
    bsz, n, _ = z.shape
    nb = n // tb
    hk, hv = dk // heads, dv // heads
    fwd = lambda cb: (lambda b, i: (b, i, cb))
    bwd = lambda cb: (lambda b, i: (b, nb - 1 - i, cb))
    full = lambda *shape: pl.BlockSpec(shape, lambda b, i: (0,) * len(shape))
    sspec = pl.BlockSpec((None, heads, hv, hk), lambda b, i: (b, 0, 0, 0))
    vcb = 2 * dk // dv
    ost = jax.ShapeDtypeStruct((bsz, n, dv), BF16)
    return pl.pallas_call(
        functools.partial(_gla_kernel, heads),
        grid=(bsz, nb),
        in_specs=[pl.BlockSpec((None, tb, dk), fwd(0)), pl.BlockSpec((None, tb, dk), fwd(1)),
                  pl.BlockSpec((None, tb, dv), fwd(vcb)), pl.BlockSpec((None, tb, LANES), fwd(0)),
                  pl.BlockSpec((None, tb, dk), bwd(0)), pl.BlockSpec((None, tb, dk), bwd(1)),
                  pl.BlockSpec((None, tb, dv), bwd(vcb)), pl.BlockSpec((None, tb, LANES), bwd(0)),
                  full(LANES, dk), full(LANES, dk), full(1, dk), full(1, dk), sspec, sspec],
        out_specs=(pl.BlockSpec((None, tb, dv), fwd(0)), pl.BlockSpec((None, tb, dv), bwd(0))),
        out_shape=(ost, ost),
        scratch_shapes=[pltpu.VMEM((heads, hv, hk), F32), pltpu.VMEM((heads, hv, hk), F32)],
        compiler_params=_cparams("arbitrary", "arbitrary"),
        name="gla",
    )(z, z, z, lr, z, z, z, lr, wupf, wupb, baf, bab, s0f, s0b)


def _mixer_kernel(heads, of_ref, ob_ref, g_ref, u_ref, bg_ref, cg_ref, rg_ref, rc_ref, x_ref,
                  gng_ref, cw_ref, wo_ref, wco_ref, wm_ref, gt1_ref, sh2_ref, sc2_ref, n2g_ref,
                  rwt_ref, x1_ref, h2_ref, aff_ref):
    tm, dv = of_ref.shape
    hv = dv // heads
    o = of_ref[...].astype(F32) + ob_ref[...].astype(F32)
    parts = []
    for h in range(heads):
        oh = o[:, h * hv:(h + 1) * hv]
        ms = jnp.mean(oh * oh, axis=-1, keepdims=True)
        parts.append(oh * lax.rsqrt(ms + EPS))
    on = jnp.concatenate(parts, axis=1) * gng_ref[...]
    g = g_ref[...].astype(F32)
    y_gla = _dot((on * (g * _sigmoid(g))).astype(BF16), wo_ref[...])

    cu = cg_ref[...].astype(F32) * u_ref[...].astype(F32)
    gcol = lax.broadcasted_iota(I32, cu.shape, 0) & (GRID_W - 1)
    left = jnp.where(gcol == 0, 0.0, pltpu.roll(cu, 1, 0))
    right = jnp.where(gcol == GRID_W - 1, 0.0, pltpu.roll(cu, tm - 1, 0))
    cw = cw_ref[...]
    conv = left * cw[0:1, :] + cu * cw[1:2, :] + right * cw[2:3, :]
    y_conv = _dot((bg_ref[...].astype(F32) * conv).astype(BF16), wco_ref[...])

    merged = (_sigmoid(rg_ref[...].astype(F32)) * y_gla
              + _sigmoid(rc_ref[...].astype(F32)) * y_conv)
    y = _dot(merged.astype(BF16), wm_ref[...])
    x1 = x_ref[...] + gt1_ref[...] * y
    x1_ref[...] = x1
    h2 = _rms_mod(x1, n2g_ref[...], sh2_ref[...], sc2_ref[...])
    rpt = h2.shape[1] // LANES
    for a in range(rpt):
        h2_ref[pl.ds(a, tm, stride=rpt), :] = h2[:, a * LANES:(a + 1) * LANES]
    logits = _dot_nt(rwt_ref[...], h2.astype(BF16))
    ex = jnp.exp(logits - jnp.max(logits, axis=0, keepdims=True))
    aff_ref[...] = ex / jnp.sum(ex, axis=0, keepdims=True)


def _mixer(o_f, o_b, z, x, gng, conv_w, w_o, w_co, w_m, gt1, sh2, sc2, n2g, rwt, heads, dk, tm):
    bsz, n, d = x.shape
    dv = o_f.shape[2]
    c = conv_w.shape[1]
    e = rwt.shape[0]
    zb = lambda cb: pl.BlockSpec((None, tm, d), lambda b, i: (b, i, cb))
    base = (2 * dk + dv) // d
    tok = lambda w: pl.BlockSpec((None, tm, w), lambda b, i: (b, i, 0))
    full = lambda *shape: pl.BlockSpec(shape, lambda b, i: (0,) * len(shape))
    perb = pl.BlockSpec((None, 1, d), lambda b, i: (b, 0, 0))
    return pl.pallas_call(
        functools.partial(_mixer_kernel, heads),
        grid=(bsz, n // tm),
        in_specs=[tok(dv), tok(dv), zb(base), zb(base + 1), zb(base + 2), zb(base + 3),
                  zb(base + 4), zb(base + 5), tok(d),
                  full(1, dv), full(3, c), full(dv, d), full(c, d), full(d, d),
                  perb, perb, perb, full(1, d), full(e, d)],
        out_specs=(tok(d), pl.BlockSpec((None, tm * (d // LANES), LANES), lambda b, i: (b, i, 0)),
                   pl.BlockSpec((None, e, tm), lambda b, i: (b, 0, i))),
        out_shape=(jax.ShapeDtypeStruct((bsz, n, d), F32),
                   jax.ShapeDtypeStruct((bsz, n * (d // LANES), LANES), F32),
                   jax.ShapeDtypeStruct((bsz, e, n), F32)),
        compiler_params=_cparams("arbitrary", "arbitrary"),
        name="mixer",
    )(o_f, o_b, z, z, z, z, z, z, x, gng, conv_w, w_o, w_co, w_m, gt1, sh2, sc2, n2g, rwt)


def _route_kernel(cap, aff_ref, lmat_ref, pos_ref, base_ref, idx_ref, loc_scr, bef_scr):
    e, n = aff_ref.shape
    nt = n // LANES
    aff = aff_ref[...]

    def count(mask):
        return jnp.sum(jnp.where(mask, 1.0, 0.0), axis=1, keepdims=True)

    def search(k, tbits):
        cand = tbits | jnp.left_shift(jnp.int32(1), 30 - k)
        ok = count(aff >= lax.bitcast_convert_type(cand, F32)) >= cap
        return jnp.where(ok, cand, tbits)

    tbits = lax.fori_loop(0, 31, search, jnp.zeros((e, 1), I32))
    thr = lax.bitcast_convert_type(tbits, F32)
    gt = aff > thr
    eq = aff == thr
    need = cap - count(gt)

    def stack(mask):
        m = jnp.where(mask, 1.0, 0.0)
        return jnp.concatenate([m[:, j * LANES:(j + 1) * LANES] for j in range(nt)], axis=0)

    r = lax.broadcasted_iota(I32, (LANES, LANES), 0)
    cl = lax.broadcasted_iota(I32, (LANES, LANES), 1)
    upper = jnp.where(r <= cl, 1.0, 0.0).astype(BF16)
    ones = jnp.ones((LANES, LANES), BF16)
    lmat = lmat_ref[...]

    def cumsum(ms):
        msb = ms.astype(BF16)
        before = _dot(_dot(lmat, msb).astype(BF16), ones)
        return _dot(msb, upper), before

    eq_s = stack(eq)
    loc_eq, before_eq = cumsum(eq_s)
    need_s = jnp.concatenate([need] * nt, axis=0)
    sel = jnp.maximum(stack(gt), jnp.where(loc_eq + before_eq <= need_s, eq_s, 0.0))
    loc, before = cumsum(sel)
    pos_ref[...] = jnp.where(sel > 0.0, loc + before - 1.0, -1.0).astype(I32)
    base_ref[...] = before.astype(I32)

    loc_scr[...] = loc
    bef_scr[...] = before
    slot = lax.broadcasted_iota(I32, (1, cap), 1).astype(F32)
    for ex in range(e):
        loc_e = loc_scr[pl.ds(ex, nt, stride=e), :]
        tprev = bef_scr[pl.ds(ex, nt, stride=e), :][:, 0:1]
        tincl = tprev + loc_e[:, LANES - 1:LANES]
        in_tile = jnp.where((tprev <= slot) & (slot < tincl), 1.0, 0.0)
        tile = jnp.sum(jnp.where(tincl <= slot, 1.0, 0.0), axis=0, keepdims=True)
        s_loc = slot - jnp.sum(in_tile * tprev, axis=0, keepdims=True)
        counts = _dot_tn(loc_e.astype(BF16), in_tile.astype(BF16))
        lane = jnp.sum(jnp.where(counts <= s_loc, 1.0, 0.0), axis=0, keepdims=True)
        idx_ref[pl.ds(ex, 1), :] = (tile * LANES + lane).astype(I32)


def _route(aff_t, cap):
    bsz, e, n = aff_t.shape
    nt = n // LANES
    rows = nt * e
    ri = np.arange(rows)
    lmat = ((ri[None, :] % e == ri[:, None] % e) & (ri[None, :] // e < ri[:, None] // e))
    lmat = jnp.asarray(lmat, BF16)
    st = jax.ShapeDtypeStruct((bsz, rows, LANES), I32)
    ospec = pl.BlockSpec((None, rows, LANES), lambda b: (b, 0, 0))
    return pl.pallas_call(
        functools.partial(_route_kernel, cap),
        grid=(bsz,),
        in_specs=[pl.BlockSpec((None, e, n), lambda b: (b, 0, 0)),
                  pl.BlockSpec((rows, rows), lambda b: (0, 0))],
        out_specs=(ospec, ospec, pl.BlockSpec((None, e, cap), lambda b: (b, 0, 0))),
        out_shape=(st, st, jax.ShapeDtypeStruct((bsz, e, cap), I32)),
        scratch_shapes=[pltpu.VMEM((rows, LANES), F32), pltpu.VMEM((rows, LANES), F32)],
        compiler_params=_cparams("arbitrary"),
        name="route",
    )(aff_t, lmat)


def _slot_window(base):
    return pl.multiple_of((base // BF16_ROWS) * BF16_ROWS, BF16_ROWS)


def _expert_kernel(cap, idx_ref, h2_hbm, wg_ref, wu_ref, wd_ref, ye_ref, xbuf, xb, acc, sem):
    e, f = pl.program_id(1), pl.program_id(2)
    n_exp, nf = pl.num_programs(1), pl.num_programs(2)
    last = pl.num_programs(0) * n_exp - 1
    lin = pl.program_id(0) * n_exp + e
    slot = lin % 2
    per_step = cap // nf
    rpt = xb.shape[1] // LANES

    def row_copy(lin_t, part, i, slot_t):
        tok = idx_ref[(lin_t * nf + part) * per_step + i]
        return pltpu.make_async_copy(h2_hbm.at[pl.ds(pl.multiple_of(tok * rpt, rpt), rpt), :],
                                     xbuf.at[slot_t, part, pl.ds(i * rpt, rpt), :],
                                     sem.at[slot_t])

    def wait_rows(slot_t):
        pltpu.make_async_copy(xbuf.at[slot_t], xbuf.at[slot_t], sem.at[slot_t]).wait()

    @pl.when((lin == 0) & (f == 0))
    def _():
        for part in range(nf):
            def first(i, carry):
                row_copy(lin, part, i, slot).start()
                return carry
            lax.fori_loop(0, per_step, first, 0)

    @pl.when(f == 0)
    def _():
        wait_rows(slot)
        acc[...] = jnp.zeros_like(acc)
        for part in range(nf):
            for a in range(rpt):
                xb[part * per_step:(part + 1) * per_step, a * LANES:(a + 1) * LANES] = (
                    xbuf[slot, part, pl.ds(a, per_step, stride=rpt), :].astype(BF16))

    nxt = jnp.minimum(lin + 1, last)
    for i in range(per_step):
        row_copy(nxt, f, i, 1 - slot).start()

    @pl.when((lin == last) & (f == nf - 1))
    def _():
        wait_rows(1 - slot)

    xv = xb[...]
    hg = _dot(xv, wg_ref[...].astype(BF16))
    hu = _dot(xv, wu_ref[...].astype(BF16))
    hid = (hg * _sigmoid(hg) * hu).astype(BF16)
    acc[...] += _dot(hid, wd_ref[...].astype(BF16))

    @pl.when(f == nf - 1)
    def _():
        ye_ref[0:cap, :] = acc[...].astype(BF16)
        ye_ref[cap:, :] = jnp.zeros((ye_ref.shape[0] - cap, ye_ref.shape[1]), BF16)


def _experts(h2, idx, w_gate, w_up, w_down, cap, rows, tf):
    n_exp, d, df = w_gate.shape
    bsz = h2.shape[0]
    rpt = d // LANES
    n = h2.shape[1] // rpt
    nf = df // tf
    assert cap % nf == 0
    rows_global = (idx + (jnp.arange(bsz, dtype=I32) * n)[:, None, None]).reshape(-1)
    return pl.pallas_call(
        functools.partial(_expert_kernel, cap),
        grid_spec=pltpu.PrefetchScalarGridSpec(
            num_scalar_prefetch=1,
            grid=(bsz, n_exp, nf),
            in_specs=[pl.BlockSpec(memory_space=pl.ANY),
                      pl.BlockSpec((None, d, tf), lambda b, e, f, s: (e, 0, f)),
                      pl.BlockSpec((None, d, tf), lambda b, e, f, s: (e, 0, f)),
                      pl.BlockSpec((None, tf, d), lambda b, e, f, s: (e, f, 0))],
            out_specs=pl.BlockSpec((None, None, rows, d), lambda b, e, f, s: (b, e, 0, 0)),
            scratch_shapes=[pltpu.VMEM((2, nf, (cap // nf) * rpt, LANES), F32),
                            pltpu.VMEM((cap, d), BF16),
                            pltpu.VMEM((cap, d), F32), pltpu.SemaphoreType.DMA((2,))]),
        out_shape=jax.ShapeDtypeStruct((bsz, n_exp, rows, d), BF16),
        compiler_params=_cparams("arbitrary", "arbitrary", "arbitrary"),
        name="experts",
    )(rows_global, h2.reshape(bsz * n * rpt, LANES), w_gate, w_up, w_down)


def _combine_windows(tm):
    return tm // 4 + BF16_ROWS, tm + BF16_ROWS


def _combine_kernel(n_exp, nsub, base_ref, ovf_ref, pos_ref, aff_ref, x1_ref, gt2_ref, fg_ref,
                    ye_hbm, out_ref, buf, fbuf, ffn_scr, sem, fsem):
    b, j = pl.program_id(0), pl.program_id(1)
    nt = pl.num_programs(1)
    step = b * nt + j
    slot = step % 2
    wp, wfull = buf.shape[2], fbuf.shape[1]
    d = buf.shape[-1]

    def start_of(tile, ex):
        return _slot_window(base_ref[tile * nsub * n_exp + ex])

    def window_copy(tile, ex, dst_slot):
        return pltpu.make_async_copy(ye_hbm.at[tile // nt, ex, pl.ds(start_of(tile, ex), wp), :],
                                     buf.at[dst_slot, ex], sem.at[dst_slot])

    def full_copy(ex):
        return pltpu.make_async_copy(ye_hbm.at[b, ex, pl.ds(start_of(step, ex), wfull), :],
                                     fbuf.at[ex], fsem)

    @pl.when(step == 0)
    def _():
        for ex in range(n_exp):
            window_copy(step, ex, slot).start()

    @pl.when(step + 1 < pl.num_programs(0) * nt)
    def _():
        for ex in range(n_exp):
            window_copy(step + 1, ex, 1 - slot).start()

    def gates_t(rows):
        srow = lax.broadcasted_iota(I32, (rows, LANES), 0)
        per_exp = []
        for ex in range(n_exp):
            s0 = start_of(step, ex)
            parts = []
            for u in range(nsub):
                rel = pos_ref[pl.ds(u * n_exp + ex, 1), :] - s0
                gate = aff_ref[pl.ds(ex, 1), u * LANES:(u + 1) * LANES]
                parts.append(jnp.where(srow == rel, gate, 0.0).astype(BF16))
            per_exp.append(jnp.concatenate(parts, axis=1))
        return jnp.concatenate(per_exp, axis=0)

    for ex in range(n_exp):
        window_copy(step, ex, slot).wait()

    @pl.when(ovf_ref[step] == 0)
    def _():
        ffn_scr[...] = _dot_tn(gates_t(wp), buf[slot].reshape(n_exp * wp, d))

    @pl.when(ovf_ref[step] != 0)
    def _():
        for ex in range(n_exp):
            full_copy(ex).start()
        for ex in range(n_exp):
            full_copy(ex).wait()
        ffn_scr[...] = _dot_tn(gates_t(wfull), fbuf[...].reshape(n_exp * wfull, d))

    x2 = x1_ref[...] + gt2_ref[...] * ffn_scr[...]
    ms = jnp.mean(x2 * x2, axis=-1, keepdims=True)
    out_ref[...] = x2 * lax.rsqrt(ms + EPS) * fg_ref[...]


def _combine(ye, pos, aff_t, before, x1, gt2, final_g, cap, tm):
    bsz, n, d = x1.shape
    n_exp = aff_t.shape[1]
    nsub = tm // LANES
    nt = n // tm
    wp, wfull = _combine_windows(tm)
    starts = before[:, :, 0].reshape(bsz, n // LANES, n_exp)
    tile_starts = starts[:, ::nsub]
    tile_ends = jnp.concatenate([tile_starts[:, 1:], jnp.full((bsz, 1, n_exp), cap, I32)], axis=1)
    overflow = (tile_starts % BF16_ROWS) + (tile_ends - tile_starts) > wp
    ovf = jnp.any(overflow, axis=-1).astype(I32).reshape(-1)
    return pl.pallas_call(
        functools.partial(_combine_kernel, n_exp, nsub),
        grid_spec=pltpu.PrefetchScalarGridSpec(
            num_scalar_prefetch=2,
            grid=(bsz, nt),
            in_specs=[pl.BlockSpec((None, nsub * n_exp, LANES), lambda b, j, s, o: (b, j, 0)),
                      pl.BlockSpec((None, n_exp, tm), lambda b, j, s, o: (b, 0, j)),
                      pl.BlockSpec((None, tm, d), lambda b, j, s, o: (b, j, 0)),
                      pl.BlockSpec((None, 1, d), lambda b, j, s, o: (b, 0, 0)),
                      pl.BlockSpec((1, d), lambda b, j, s, o: (0, 0)),
                      pl.BlockSpec(memory_space=pl.ANY)],
            out_specs=pl.BlockSpec((None, tm, d), lambda b, j, s, o: (b, j, 0)),
            scratch_shapes=[pltpu.VMEM((2, n_exp, wp, d), BF16),
                            pltpu.VMEM((n_exp, wfull, d), BF16),
                            pltpu.VMEM((tm, d), F32),
                            pltpu.SemaphoreType.DMA((2,)), pltpu.SemaphoreType.DMA(())]),
        out_shape=jax.ShapeDtypeStruct((bsz, n, d), F32),
        compiler_params=_cparams("arbitrary", "arbitrary"),
        name="combine",
    )(starts.reshape(-1), ovf, pos, aff_t, x1, gt2, final_g, ye)


def _pick(n, pref):
    t = min(n, pref)
    while n % t:
        t //= 2
    return t


def kernel(x, c, ctx, c_ctx, w_ada, b_ada, norm1_g, norm2_g, w_in, gla_w_a_up, gla_b_a,
           gla_norm_g, gla_w_o, conv_w, conv_w_out, merge_w_out, router_w,
           exp_w_gate, exp_w_up, exp_w_down, final_g):
    bsz, n, d = x.shape
    depth = w_ada.shape[0]
    assert depth == 1, "single-layer trunk"
    rank, dk = gla_w_a_up.shape[-2:]
    dv = gla_w_o.shape[1]
    hv = gla_norm_g.shape[-1]
    heads = dv // hv
    cch = conv_w.shape[-1]
    n_exp = router_w.shape[-1]
    cap = EC_CAPACITY * n // n_exp
    assert 2 * dk == d and dv == d and cch == d and 2 * rank <= LANES
    assert n % LANES == 0 and n % GRID_W == 0 and bsz + 1 <= 8 and n_exp % 8 == 0

    cond = jnp.zeros((8, d), F32).at[:bsz].set(c).at[bsz].set(c_ctx)
    mod = _adaln(cond, w_ada[0], b_ada[0]).reshape(8, N_MOD, d)
    sh1, sc1, gt1, sh2, sc2, gt2 = [mod[:bsz, k][:, None, :] for k in range(N_MOD)]
    csh1, csc1 = mod[bsz:bsz + 1, 0], mod[bsz:bsz + 1, 1]

    w = w_in[0]
    lr0 = 2 * dk + 2 * dv
    w_main = jnp.concatenate([w[:, :lr0], w[:, lr0 + 2 * rank:]], axis=1).astype(BF16)
    w_lr = jnp.pad(w[:, lr0:lr0 + 2 * rank], ((0, 0), (0, LANES - 2 * rank))).astype(BF16)
    wup = gla_w_a_up[0].astype(BF16)
    wupf = jnp.zeros((LANES, dk), BF16).at[:rank].set(wup[0])
    wupb = jnp.zeros((LANES, dk), BF16).at[rank:2 * rank].set(wup[1])
    baf, bab = gla_b_a[0, 0:1], gla_b_a[0, 1:2]
    n1g, n2g = norm1_g[0:1], norm2_g[0:1]

    s0f, s0b = _ctx_states(ctx, csh1, csc1, n1g, w_main, w_lr, wupf, wupb, baf, bab,
                           heads, dk, dv)
    z, lr = _inproj(x, sh1, sc1, n1g, w_main, w_lr, _pick(n, 1024), _pick(w_main.shape[1], 2048))
    o_f, o_b = _gla(z, lr, wupf, wupb, baf, bab, s0f, s0b, heads, dk, dv, _pick(n, 512))
    x1, h2, aff_t = _mixer(
        o_f, o_b, z, x, jnp.tile(gla_norm_g[0:1], (1, heads)), conv_w[0],
        gla_w_o[0].astype(BF16), conv_w_out[0].astype(BF16), merge_w_out[0].astype(BF16),
        gt1, sh2, sc2, n2g, router_w[0].T.astype(BF16), heads, dk, _pick(n, 256))

    pos, before, idx = _route(aff_t, cap)
    tmc = _pick(n, 256)
    ye = _experts(h2, idx, exp_w_gate[0], exp_w_up[0], exp_w_down[0], cap,
                  cap + _combine_windows(tmc)[1], _pick(exp_w_gate.shape[-1], 512))
    return _combine(ye, pos, aff_t, before, x1, gt2, final_g.reshape(1, d), cap, tmc)
```

```python
import functools

import numpy as np
import jax
import jax.numpy as jnp
from jax import lax
from jax.experimental import pallas as pl
from jax.experimental.pallas import tpu as pltpu

F32 = jnp.float32
BF16 = jnp.bfloat16
I32 = jnp.int32

EPS = 1e-6
N_MOD = 6
GRID_W = 64
GLA_CHUNK = 64
GLA_GATE_NORM = 16.0
GLA_SCAN_LAG = 2
EC_CAPACITY = 2

LANES = 128
BF16_ROWS = 16
VMEM_LIMIT = 56 * 1024 * 1024


def _cparams(*sem):
    return pltpu.CompilerParams(dimension_semantics=sem, vmem_limit_bytes=VMEM_LIMIT)


def _dot(a, b):
    return jnp.dot(a, b, preferred_element_type=F32)


def _dot_nt(a, b):
    return lax.dot_general(a, b, (((1,), (1,)), ((), ())), preferred_element_type=F32)


def _dot_tn(a, b):
    return lax.dot_general(a, b, (((0,), (0,)), ((), ())), preferred_element_type=F32)


def _sigmoid(v):
    return 1.0 / (1.0 + jnp.exp(-v))


def _log_sigmoid(v):
    return jnp.minimum(v, 0.0) - jnp.log1p(jnp.exp(-jnp.abs(v)))


def _rms_mod(xv, g, shift, scale):
    ms = jnp.mean(xv * xv, axis=-1, keepdims=True)
    y = xv * lax.rsqrt(ms + EPS) * g
    return y * (1.0 + scale) + shift


def _split_bf16(v):
    hi = v.astype(BF16)
    lo = (v - hi.astype(F32)).astype(BF16)
    return hi, lo


def _adaln_kernel(c_ref, w_ref, b_ref, o_ref):
    cv = c_ref[...]
    s = (cv * _sigmoid(cv)).astype(BF16)
    o_ref[...] = _dot(s, w_ref[...].astype(BF16)) + b_ref[...]


def _adaln(cond, w, b):
    rows, d = cond.shape
    nout = w.shape[1]
    tn = d
    return pl.pallas_call(
        _adaln_kernel,
        grid=(nout // tn,),
        in_specs=[pl.BlockSpec((rows, d), lambda j: (0, 0)),
                  pl.BlockSpec((d, tn), lambda j: (0, j)),
                  pl.BlockSpec((1, tn), lambda j: (0, j))],
        out_specs=pl.BlockSpec((rows, tn), lambda j: (0, j)),
        out_shape=jax.ShapeDtypeStruct((rows, nout), F32),
        compiler_params=_cparams("arbitrary"),
        name="adaln",
    )(cond, w, b.reshape(1, nout))


def _ctx_kernel(heads, ctx_ref, sh_ref, sc_ref, g_ref, wk_ref, wv_ref, wlr_ref,
                wupf_ref, wupb_ref, baf_ref, bab_ref, sf_ref, sb_ref):
    n = ctx_ref.shape[0]
    hc = _rms_mod(ctx_ref[...], g_ref[...], sh_ref[...], sc_ref[...]).astype(BF16)
    k = _dot(hc, wk_ref[...])
    v = _dot(hc, wv_ref[...]).astype(BF16)
    lr = _dot(hc, wlr_ref[...]).astype(BF16)
    hk = k.shape[1] // heads
    hv = v.shape[1] // heads
    row = lax.broadcasted_iota(I32, (n, n), 0)
    col = lax.broadcasted_iota(I32, (n, n), 1)
    for wup_ref, ba_ref, s_ref, tri, last in (
            (wupf_ref, baf_ref, sf_ref, col <= row, n - 1),
            (wupb_ref, bab_ref, sb_ref, col >= row, 0)):
        la = _log_sigmoid(_dot(lr, wup_ref[...]) + ba_ref[...]) * (1.0 / GLA_GATE_NORM)
        hi, lo = _split_bf16(la)
        t = jnp.where(tri, 1.0, 0.0).astype(BF16)
        b = _dot(t, hi) + _dot(t, lo)
        kd = (k * jnp.exp(b[last:last + 1, :] - b)).astype(BF16)
        for h in range(heads):
            s_ref[h] = _dot_tn(v[:, h * hv:(h + 1) * hv], kd[:, h * hk:(h + 1) * hk])


def _ctx_states(ctx, sh, sc, g, w_main, w_lr, wupf, wupb, baf, bab, heads, dk, dv):
    bsz, n, d = ctx.shape
    hk, hv = dk // heads, dv // heads
    full = lambda *shape: pl.BlockSpec(shape, lambda b: (0,) * len(shape))
    st = jax.ShapeDtypeStruct((bsz, heads, hv, hk), F32)
    sspec = pl.BlockSpec((None, heads, hv, hk), lambda b: (b, 0, 0, 0))
    return pl.pallas_call(
        functools.partial(_ctx_kernel, heads),
        grid=(bsz,),
        in_specs=[pl.BlockSpec((None, n, d), lambda b: (b, 0, 0)),
                  full(1, d), full(1, d), full(1, d),
                  pl.BlockSpec((d, dk), lambda b: (0, 1)),
                  pl.BlockSpec((d, dv), lambda b: (0, 2 * dk // dv)),
                  full(d, LANES), full(LANES, dk), full(LANES, dk), full(1, dk), full(1, dk)],
        out_specs=(sspec, sspec),
        out_shape=(st, st),
        compiler_params=_cparams("arbitrary"),
        name="ctx_state",
    )(ctx, sh, sc, g, w_main, w_main, w_lr, wupf, wupb, baf, bab)


def _inproj_kernel(x_ref, sh_ref, sc_ref, g_ref, w_ref, wlr_ref, z_ref, lr_ref, h_scr):
    j = pl.program_id(2)
    d = x_ref.shape[1]

    @pl.when(j == 0)
    def _():
        hb = _rms_mod(x_ref[...], g_ref[...], sh_ref[...], sc_ref[...]).astype(BF16)
        h_scr[...] = hb
        lr_ref[...] = _dot(hb, wlr_ref[...])
        z_ref[...] = _dot(hb, w_ref[...]).astype(BF16)

    @pl.when(j == 1)
    def _():
        a = _dot(h_scr[...], w_ref[...])
        gv = a[:, :d]
        z_ref[:, :d] = (gv * _sigmoid(gv)).astype(BF16)
        z_ref[:, d:] = a[:, d:].astype(BF16)

    @pl.when(j == 2)
    def _():
        a = _dot(h_scr[...], w_ref[...])
        z_ref[:, :d] = (a[:, d:] * a[:, :d]).astype(BF16)
        z_ref[:, d:] = jnp.zeros((z_ref.shape[0], d), BF16)

    @pl.when(j == 3)
    def _():
        z_ref[...] = _sigmoid(_dot(h_scr[...], w_ref[...])).astype(BF16)


def _inproj(x, sh, sc, g, w_main, w_lr, tm):
    bsz, n, d = x.shape
    p = w_main.shape[1]
    tn = 2 * d
    assert p == 4 * tn
    return pl.pallas_call(
        _inproj_kernel,
        grid=(bsz, n // tm, p // tn),
        in_specs=[pl.BlockSpec((None, tm, d), lambda b, i, j: (b, i, 0)),
                  pl.BlockSpec((None, 1, d), lambda b, i, j: (b, 0, 0)),
                  pl.BlockSpec((None, 1, d), lambda b, i, j: (b, 0, 0)),
                  pl.BlockSpec((1, d), lambda b, i, j: (0, 0)),
                  pl.BlockSpec((d, tn), lambda b, i, j: (0, j)),
                  pl.BlockSpec((d, LANES), lambda b, i, j: (0, 0))],
        out_specs=(pl.BlockSpec((None, tm, tn), lambda b, i, j: (b, i, j)),
                   pl.BlockSpec((None, tm, LANES), lambda b, i, j: (b, i, 0))),
        out_shape=(jax.ShapeDtypeStruct((bsz, n, p), BF16),
                   jax.ShapeDtypeStruct((bsz, n, LANES), F32)),
        scratch_shapes=[pltpu.VMEM((tm, d), BF16)],
        compiler_params=_cparams("arbitrary", "arbitrary", "arbitrary"),
        name="inproj",
    )(x, sh, sc, g, w_main, w_lr)


def _gla_kernel(heads, qf_ref, kf_ref, vf_ref, lrf_ref, qb_ref, kb_ref, vb_ref, lrb_ref,
                wupf_ref, wupb_ref, baf_ref, bab_ref, s0f_ref, s0b_ref,
                of_ref, ob_ref, sf_scr, sb_scr):
    tb, dk = qf_ref.shape
    dv = vf_ref.shape[1]
    hk, hv = dk // heads, dv // heads
    ck = GLA_CHUNK
    nck = tb // ck
    q_scale = hk ** -0.5

    @pl.when(pl.program_id(1) == 0)
    def _():
        sf_scr[...] = s0f_ref[...]
        sb_scr[...] = s0b_ref[...]

    row = lax.broadcasted_iota(I32, (tb, tb), 0)
    col = lax.broadcasted_iota(I32, (tb, tb), 1)
    shift = ck.bit_length() - 1
    same_chunk = (row >> shift) == (col >> shift)
    crow = lax.broadcasted_iota(I32, (ck, ck), 0)
    ccol = lax.broadcasted_iota(I32, (ck, ck), 1)

    dirs = (
        (qf_ref, kf_ref, vf_ref, lrf_ref, wupf_ref, baf_ref, of_ref, sf_scr,
         same_chunk & (col <= row), ccol <= crow, ck - 1, range(nck)),
        (qb_ref, kb_ref, vb_ref, lrb_ref, wupb_ref, bab_ref, ob_ref, sb_scr,
         same_chunk & (col >= row), ccol >= crow, 0, range(nck - 1, -1, -1)),
    )
    b_alls = []
    for _, _, _, lr_ref, wup_ref, ba_ref, _, _, tri, _, _, _ in dirs:
        la = _log_sigmoid(_dot(lr_ref[...].astype(BF16), wup_ref[...]) + ba_ref[...])
        la = la * (1.0 / GLA_GATE_NORM)
        hi, lo = _split_bf16(la)
        t = jnp.where(tri, 1.0, 0.0).astype(BF16)
        b_alls.append(_dot(t, hi) + _dot(t, lo))
    work = [[], []]

    def stage1(k):
        for di, (q_ref, k_ref, v_ref, _, _, _, _, _, _, cmask, last, order) in enumerate(dirs):
            r0 = order[k] * ck
            b = b_alls[di][r0:r0 + ck, :]
            b_last = b[last:last + 1, :]
            qc = q_ref[r0:r0 + ck, :].astype(F32) * q_scale
            kc = k_ref[r0:r0 + ck, :].astype(F32)
            q_in = (qc * jnp.exp(b)).astype(BF16)
            k_in = (kc * jnp.exp(-b)).astype(BF16)
            k_dec = (kc * jnp.exp(b_last - b)).astype(BF16)
            decay = jnp.exp(b_last)
            units = []
            for h in range(heads):
                qh = q_in[:, h * hk:(h + 1) * hk]
                vh = v_ref[r0:r0 + ck, h * hv:(h + 1) * hv]
                scores = jnp.where(cmask, _dot_nt(qh, k_in[:, h * hk:(h + 1) * hk]), 0.0)
                o_intra = _dot(scores.astype(BF16), vh)
                ut = _dot_tn(vh, k_dec[:, h * hk:(h + 1) * hk])
                units.append((qh, o_intra, ut, decay[:, h * hk:(h + 1) * hk]))
            work[di].append((r0, units))
    def scan(k):
        for di, (_, _, _, _, _, _, o_ref, s_scr, _, _, _, _) in enumerate(dirs):
            r0, units = work[di][k]
            for h, (qh, o_intra, ut, dec) in enumerate(units):
                st = s_scr[h]
                o = o_intra + _dot_nt(qh, st.astype(BF16))
                o_ref[r0:r0 + ck, h * hv:(h + 1) * hv] = o.astype(BF16)
                s_scr[h] = st * dec + ut

    for k in range(nck + GLA_SCAN_LAG):
        if k < nck:
            stage1(k)
        if k >= GLA_SCAN_LAG:
            scan(k - GLA_SCAN_LAG)


def _gla(z, lr, wupf, wupb, baf, bab, s0f, s0b, heads, dk, dv, tb):
    bsz, n, _ = z.shape
    nb = n // tb
    hk, hv = dk // heads, dv // heads
    fwd = lambda cb: (lambda b, i: (b, i, cb))
    bwd = lambda cb: (lambda b, i: (b, nb - 1 - i, cb))
    full = lambda *shape: pl.BlockSpec(shape, lambda b, i: (0,) * len(shape))
    sspec = pl.BlockSpec((None, heads, hv, hk), lambda b, i: (b, 0, 0, 0))
    vcb = 2 * dk // dv
    ost = jax.ShapeDtypeStruct((bsz, n, dv), BF16)
    return pl.pallas_call(
        functools.partial(_gla_kernel, heads),
        grid=(bsz, nb),
        in_specs=[pl.BlockSpec((None, tb, dk), fwd(0)), pl.BlockSpec((None, tb, dk), fwd(1)),
                  pl.BlockSpec((None, tb, dv), fwd(vcb)), pl.BlockSpec((None, tb, LANES), fwd(0)),
                  pl.BlockSpec((None, tb, dk), bwd(0)), pl.BlockSpec((None, tb, dk), bwd(1)),
                  pl.BlockSpec((None, tb, dv), bwd(vcb)), pl.BlockSpec((None, tb, LANES), bwd(0)),
                  full(LANES, dk), full(LANES, dk), full(1, dk), full(1, dk), sspec, sspec],
        out_specs=(pl.BlockSpec((None, tb, dv), fwd(0)), pl.BlockSpec((None, tb, dv), bwd(0))),
        out_shape=(ost, ost),
        scratch_shapes=[pltpu.VMEM((heads, hv, hk), F32), pltpu.VMEM((heads, hv, hk), F32)],
        compiler_params=_cparams("arbitrary", "arbitrary"),
        name="gla",
    )(z, z, z, lr, z, z, z, lr, wupf, wupb, baf, bab, s0f, s0b)


def _mixer_kernel(heads, of_ref, ob_ref, sg_ref, bg_ref, cu_ref, srg_ref, src_ref, x_ref,
                  gng_ref, cw_ref, wo_ref, wco_ref, wm_ref, gt1_ref, sh2_ref, sc2_ref, n2g_ref,
                  rwt_ref, x1_ref, h2_ref, aff_ref):
    tm, dv = of_ref.shape
    hv = dv // heads
    o = of_ref[...].astype(F32) + ob_ref[...].astype(F32)
    parts = []
    for h in range(heads):
        oh = o[:, h * hv:(h + 1) * hv]
        ms = jnp.mean(oh * oh, axis=-1, keepdims=True)
        parts.append(oh * lax.rsqrt(ms + EPS))
    on = jnp.concatenate(parts, axis=1) * gng_ref[...]
    y_gla = _dot((on * sg_ref[...].astype(F32)).astype(BF16), wo_ref[...])

    cu = cu_ref[...].astype(F32)
    gcol = lax.broadcasted_iota(I32, cu.shape, 0) & (GRID_W - 1)
    left = jnp.where(gcol == 0, 0.0, pltpu.roll(cu, 1, 0))
    right = jnp.where(gcol == GRID_W - 1, 0.0, pltpu.roll(cu, tm - 1, 0))
    cw = cw_ref[...]
    conv = left * cw[0:1, :] + cu * cw[1:2, :] + right * cw[2:3, :]
    y_conv = _dot((bg_ref[...].astype(F32) * conv).astype(BF16), wco_ref[...])

    merged = srg_ref[...].astype(F32) * y_gla + src_ref[...].astype(F32) * y_conv
    y = _dot(merged.astype(BF16), wm_ref[...])
    x1 = x_ref[...] + gt1_ref[...] * y
    x1_ref[...] = x1
    h2 = _rms_mod(x1, n2g_ref[...], sh2_ref[...], sc2_ref[...])
    rpt = h2.shape[1] // LANES
    for a in range(rpt):
        h2_ref[pl.ds(a, tm, stride=rpt), :] = h2[:, a * LANES:(a + 1) * LANES]
    logits = _dot_nt(rwt_ref[...], h2.astype(BF16))
    ex = jnp.exp(logits - jnp.max(logits, axis=0, keepdims=True))
    aff_ref[...] = ex / jnp.sum(ex, axis=0, keepdims=True)


def _mixer(o_f, o_b, z, x, gng, conv_w, w_o, w_co, w_m, gt1, sh2, sc2, n2g, rwt, heads, dk, tm):
    bsz, n, d = x.shape
    dv = o_f.shape[2]
    c = conv_w.shape[1]
    e = rwt.shape[0]
    zb = lambda cb: pl.BlockSpec((None, tm, d), lambda b, i: (b, i, cb))
    base = (2 * dk + dv) // d
    tok = lambda w: pl.BlockSpec((None, tm, w), lambda b, i: (b, i, 0))
    full = lambda *shape: pl.BlockSpec(shape, lambda b, i: (0,) * len(shape))
    perb = pl.BlockSpec((None, 1, d), lambda b, i: (b, 0, 0))
    return pl.pallas_call(
        functools.partial(_mixer_kernel, heads),
        grid=(bsz, n // tm),
        in_specs=[tok(dv), tok(dv), zb(base), zb(base + 1), zb(base + 2),
                  zb(base + 4), zb(base + 5), tok(d),
                  full(1, dv), full(3, c), full(dv, d), full(c, d), full(d, d),
                  perb, perb, perb, full(1, d), full(e, d)],
        out_specs=(tok(d), pl.BlockSpec((None, tm * (d // LANES), LANES), lambda b, i: (b, i, 0)),
                   pl.BlockSpec((None, e, tm), lambda b, i: (b, 0, i))),
        out_shape=(jax.ShapeDtypeStruct((bsz, n, d), F32),
                   jax.ShapeDtypeStruct((bsz, n * (d // LANES), LANES), F32),
                   jax.ShapeDtypeStruct((bsz, e, n), F32)),
        compiler_params=_cparams("arbitrary", "arbitrary"),
        name="mixer",
    )(o_f, o_b, z, z, z, z, z, x, gng, conv_w, w_o, w_co, w_m, gt1, sh2, sc2, n2g, rwt)


def _route_kernel(cap, aff_ref, lmat_ref, pos_ref, base_ref, idx_ref, loc_scr, bef_scr):
    e, n = aff_ref.shape
    nt = n // LANES
    aff = aff_ref[...]

    def count(mask):
        return jnp.sum(jnp.where(mask, 1.0, 0.0), axis=1, keepdims=True)

    def search(k, tbits):
        cand = tbits | jnp.left_shift(jnp.int32(1), 30 - k)
        ok = count(aff >= lax.bitcast_convert_type(cand, F32)) >= cap
        return jnp.where(ok, cand, tbits)

    tbits = lax.fori_loop(0, 31, search, jnp.zeros((e, 1), I32))
    thr = lax.bitcast_convert_type(tbits, F32)
    gt = aff > thr
    eq = aff == thr
    need = cap - count(gt)

    def stack(mask):
        m = jnp.where(mask, 1.0, 0.0)
        return jnp.concatenate([m[:, j * LANES:(j + 1) * LANES] for j in range(nt)], axis=0)

    r = lax.broadcasted_iota(I32, (LANES, LANES), 0)
    cl = lax.broadcasted_iota(I32, (LANES, LANES), 1)
    upper = jnp.where(r <= cl, 1.0, 0.0).astype(BF16)
    ones = jnp.ones((LANES, LANES), BF16)
    lmat = lmat_ref[...]

    def cumsum(ms):
        msb = ms.astype(BF16)
        before = _dot(_dot(lmat, msb).astype(BF16), ones)
        return _dot(msb, upper), before

    eq_s = stack(eq)
    loc_eq, before_eq = cumsum(eq_s)
    need_s = jnp.concatenate([need] * nt, axis=0)
    sel = jnp.maximum(stack(gt), jnp.where(loc_eq + before_eq <= need_s, eq_s, 0.0))
    loc, before = cumsum(sel)
    pos_ref[...] = jnp.where(sel > 0.0, loc + before - 1.0, -1.0).astype(I32)
    base_ref[...] = before.astype(I32)

    loc_scr[...] = loc
    bef_scr[...] = before
    slot = lax.broadcasted_iota(I32, (1, cap), 1).astype(F32)
    for ex in range(e):
        loc_e = loc_scr[pl.ds(ex, nt, stride=e), :]
        tprev = bef_scr[pl.ds(ex, nt, stride=e), :][:, 0:1]
        tincl = tprev + loc_e[:, LANES - 1:LANES]
        in_tile = jnp.where((tprev <= slot) & (slot < tincl), 1.0, 0.0)
        tile = jnp.sum(jnp.where(tincl <= slot, 1.0, 0.0), axis=0, keepdims=True)
        s_loc = slot - jnp.sum(in_tile * tprev, axis=0, keepdims=True)
        counts = _dot_tn(loc_e.astype(BF16), in_tile.astype(BF16))
        lane = jnp.sum(jnp.where(counts <= s_loc, 1.0, 0.0), axis=0, keepdims=True)
        idx_ref[pl.ds(ex, 1), :] = (tile * LANES + lane).astype(I32)


def _route(aff_t, cap):
    bsz, e, n = aff_t.shape
    nt = n // LANES
    rows = nt * e
    ri = np.arange(rows)
    lmat = ((ri[None, :] % e == ri[:, None] % e) & (ri[None, :] // e < ri[:, None] // e))
    lmat = jnp.asarray(lmat, BF16)
    st = jax.ShapeDtypeStruct((bsz, rows, LANES), I32)
    ospec = pl.BlockSpec((None, rows, LANES), lambda b: (b, 0, 0))
    return pl.pallas_call(
        functools.partial(_route_kernel, cap),
        grid=(bsz,),
        in_specs=[pl.BlockSpec((None, e, n), lambda b: (b, 0, 0)),
                  pl.BlockSpec((rows, rows), lambda b: (0, 0))],
        out_specs=(ospec, ospec, pl.BlockSpec((None, e, cap), lambda b: (b, 0, 0))),
        out_shape=(st, st, jax.ShapeDtypeStruct((bsz, e, cap), I32)),
        scratch_shapes=[pltpu.VMEM((rows, LANES), F32), pltpu.VMEM((rows, LANES), F32)],
        compiler_params=_cparams("arbitrary"),
        name="route",
    )(aff_t, lmat)


def _slot_window(base):
    return pl.multiple_of((base // BF16_ROWS) * BF16_ROWS, BF16_ROWS)


def _expert_kernel(cap, idx_ref, h2_hbm, wg_ref, wu_ref, wd_ref, ye_ref, xbuf, xb, acc, sem):
    e, f = pl.program_id(1), pl.program_id(2)
    n_exp, nf = pl.num_programs(1), pl.num_programs(2)
    last = pl.num_programs(0) * n_exp - 1
    lin = pl.program_id(0) * n_exp + e
    slot = lin % 2
    per_step = cap // nf
    rpt = xb.shape[1] // LANES

    def row_copy(lin_t, part, i, slot_t):
        tok = idx_ref[(lin_t * nf + part) * per_step + i]
        return pltpu.make_async_copy(h2_hbm.at[pl.ds(pl.multiple_of(tok * rpt, rpt), rpt), :],
                                     xbuf.at[slot_t, part, pl.ds(i * rpt, rpt), :],
                                     sem.at[slot_t])

    def wait_rows(slot_t):
        pltpu.make_async_copy(xbuf.at[slot_t], xbuf.at[slot_t], sem.at[slot_t]).wait()

    @pl.when((lin == 0) & (f == 0))
    def _():
        for part in range(nf):
            def first(i, carry):
                row_copy(lin, part, i, slot).start()
                return carry
            lax.fori_loop(0, per_step, first, 0)

    @pl.when(f == 0)
    def _():
        wait_rows(slot)
        acc[...] = jnp.zeros_like(acc)
        for part in range(nf):
            for a in range(rpt):
                xb[part * per_step:(part + 1) * per_step, a * LANES:(a + 1) * LANES] = (
                    xbuf[slot, part, pl.ds(a, per_step, stride=rpt), :].astype(BF16))

    nxt = jnp.minimum(lin + 1, last)
    for i in range(per_step):
        row_copy(nxt, f, i, 1 - slot).start()

    @pl.when((lin == last) & (f == nf - 1))
    def _():
        wait_rows(1 - slot)

    xv = xb[...]
    hg = _dot(xv, wg_ref[...].astype(BF16))
    hu = _dot(xv, wu_ref[...].astype(BF16))
    hid = (hg * _sigmoid(hg) * hu).astype(BF16)
    acc[...] += _dot(hid, wd_ref[...].astype(BF16))

    @pl.when(f == nf - 1)
    def _():
        ye_ref[0:cap, :] = acc[...].astype(BF16)
        ye_ref[cap:, :] = jnp.zeros((ye_ref.shape[0] - cap, ye_ref.shape[1]), BF16)


def _experts(h2, idx, w_gate, w_up, w_down, cap, rows, tf):
    n_exp, d, df = w_gate.shape
    bsz = h2.shape[0]
    rpt = d // LANES
    n = h2.shape[1] // rpt
    nf = df // tf
    assert cap % nf == 0
    rows_global = (idx + (jnp.arange(bsz, dtype=I32) * n)[:, None, None]).reshape(-1)
    return pl.pallas_call(
        functools.partial(_expert_kernel, cap),
        grid_spec=pltpu.PrefetchScalarGridSpec(
            num_scalar_prefetch=1,
            grid=(bsz, n_exp, nf),
            in_specs=[pl.BlockSpec(memory_space=pl.ANY),
                      pl.BlockSpec((None, d, tf), lambda b, e, f, s: (e, 0, f)),
                      pl.BlockSpec((None, d, tf), lambda b, e, f, s: (e, 0, f)),
                      pl.BlockSpec((None, tf, d), lambda b, e, f, s: (e, f, 0))],
            out_specs=pl.BlockSpec((None, None, rows, d), lambda b, e, f, s: (b, e, 0, 0)),
            scratch_shapes=[pltpu.VMEM((2, nf, (cap // nf) * rpt, LANES), F32),
                            pltpu.VMEM((cap, d), BF16),
                            pltpu.VMEM((cap, d), F32), pltpu.SemaphoreType.DMA((2,))]),
        out_shape=jax.ShapeDtypeStruct((bsz, n_exp, rows, d), BF16),
        compiler_params=_cparams("arbitrary", "arbitrary", "arbitrary"),
        name="experts",
    )(rows_global, h2.reshape(bsz * n * rpt, LANES), w_gate, w_up, w_down)


def _combine_windows(tm):
    return tm // 4 + BF16_ROWS, tm + BF16_ROWS


def _combine_kernel(n_exp, nsub, base_ref, ovf_ref, pos_ref, aff_ref, x1_ref, gt2_ref, fg_ref,
                    ye_hbm, out_ref, buf, fbuf, ffn_scr, sem, fsem):
    b, j = pl.program_id(0), pl.program_id(1)
    nt = pl.num_programs(1)
    step = b * nt + j
    slot = step % 2
    wp, wfull = buf.shape[2], fbuf.shape[1]
    d = buf.shape[-1]

    def start_of(tile, ex):
        return _slot_window(base_ref[tile * nsub * n_exp + ex])

    def window_copy(tile, ex, dst_slot):
        return pltpu.make_async_copy(ye_hbm.at[tile // nt, ex, pl.ds(start_of(tile, ex), wp), :],
                                     buf.at[dst_slot, ex], sem.at[dst_slot])

    def full_copy(ex):
        return pltpu.make_async_copy(ye_hbm.at[b, ex, pl.ds(start_of(step, ex), wfull), :],
                                     fbuf.at[ex], fsem)

    @pl.when(step == 0)
    def _():
        for ex in range(n_exp):
            window_copy(step, ex, slot).start()

    @pl.when(step + 1 < pl.num_programs(0) * nt)
    def _():
        for ex in range(n_exp):
            window_copy(step + 1, ex, 1 - slot).start()

    def gates_t(rows):
        srow = lax.broadcasted_iota(I32, (rows, LANES), 0)
        per_exp = []
        for ex in range(n_exp):
            s0 = start_of(step, ex)
            parts = []
            for u in range(nsub):
                rel = pos_ref[pl.ds(u * n_exp + ex, 1), :] - s0
                gate = aff_ref[pl.ds(ex, 1), u * LANES:(u + 1) * LANES]
                parts.append(jnp.where(srow == rel, gate, 0.0).astype(BF16))
            per_exp.append(jnp.concatenate(parts, axis=1))
        return jnp.concatenate(per_exp, axis=0)

    for ex in range(n_exp):
        window_copy(step, ex, slot).wait()

    @pl.when(ovf_ref[step] == 0)
    def _():
        ffn_scr[...] = _dot_tn(gates_t(wp), buf[slot].reshape(n_exp * wp, d))

    @pl.when(ovf_ref[step] != 0)
    def _():
        for ex in range(n_exp):
            full_copy(ex).start()
        for ex in range(n_exp):
            full_copy(ex).wait()
        ffn_scr[...] = _dot_tn(gates_t(wfull), fbuf[...].reshape(n_exp * wfull, d))

    x2 = x1_ref[...] + gt2_ref[...] * ffn_scr[...]
    ms = jnp.mean(x2 * x2, axis=-1, keepdims=True)
    out_ref[...] = x2 * lax.rsqrt(ms + EPS) * fg_ref[...]


def _combine(ye, pos, aff_t, before, x1, gt2, final_g, cap, tm):
    bsz, n, d = x1.shape
    n_exp = aff_t.shape[1]
    nsub = tm // LANES
    nt = n // tm
    wp, wfull = _combine_windows(tm)
    starts = before[:, :, 0].reshape(bsz, n // LANES, n_exp)
    tile_starts = starts[:, ::nsub]
    tile_ends = jnp.concatenate([tile_starts[:, 1:], jnp.full((bsz, 1, n_exp), cap, I32)], axis=1)
    overflow = (tile_starts % BF16_ROWS) + (tile_ends - tile_starts) > wp
    ovf = jnp.any(overflow, axis=-1).astype(I32).reshape(-1)
    return pl.pallas_call(
        functools.partial(_combine_kernel, n_exp, nsub),
        grid_spec=pltpu.PrefetchScalarGridSpec(
            num_scalar_prefetch=2,
            grid=(bsz, nt),
            in_specs=[pl.BlockSpec((None, nsub * n_exp, LANES), lambda b, j, s, o: (b, j, 0)),
                      pl.BlockSpec((None, n_exp, tm), lambda b, j, s, o: (b, 0, j)),
                      pl.BlockSpec((None, tm, d), lambda b, j, s, o: (b, j, 0)),
                      pl.BlockSpec((None, 1, d), lambda b, j, s, o: (b, 0, 0)),
                      pl.BlockSpec((1, d), lambda b, j, s, o: (0, 0)),
                      pl.BlockSpec(memory_space=pl.ANY)],
            out_specs=pl.BlockSpec((None, tm, d), lambda b, j, s, o: (b, j, 0)),
            scratch_shapes=[pltpu.VMEM((2, n_exp, wp, d), BF16),
                            pltpu.VMEM((n_exp, wfull, d), BF16),
                            pltpu.VMEM((tm, d), F32),
                            pltpu.SemaphoreType.DMA((2,)), pltpu.SemaphoreType.DMA(())]),
        out_shape=jax.ShapeDtypeStruct((bsz, n, d), F32),
        compiler_params=_cparams("arbitrary", "arbitrary"),
        name="combine",
    )(starts.reshape(-1), ovf, pos, aff_t, x1, gt2, final_g, ye)


def _pick(n, pref):
    t = min(n, pref)
    while n % t:
        t //= 2
    return t


def kernel(x, c, ctx, c_ctx, w_ada, b_ada, norm1_g, norm2_g, w_in, gla_w_a_up, gla_b_a,
           gla_norm_g, gla_w_o, conv_w, conv_w_out, merge_w_out, router_w,
           exp_w_gate, exp_w_up, exp_w_down, final_g):
    bsz, n, d = x.shape
    depth = w_ada.shape[0]
    assert depth == 1, "single-layer trunk"
    rank, dk = gla_w_a_up.shape[-2:]
    dv = gla_w_o.shape[1]
    hv = gla_norm_g.shape[-1]
    heads = dv // hv
    cch = conv_w.shape[-1]
    n_exp = router_w.shape[-1]
    cap = EC_CAPACITY * n // n_exp
    assert 2 * dk == d and dv == d and cch == d and 2 * rank <= LANES
    assert n % LANES == 0 and n % GRID_W == 0 and bsz + 1 <= 8 and n_exp % 8 == 0

    cond = jnp.zeros((8, d), F32).at[:bsz].set(c).at[bsz].set(c_ctx)
    mod = _adaln(cond, w_ada[0], b_ada[0]).reshape(8, N_MOD, d)
    sh1, sc1, gt1, sh2, sc2, gt2 = [mod[:bsz, k][:, None, :] for k in range(N_MOD)]
    csh1, csc1 = mod[bsz:bsz + 1, 0], mod[bsz:bsz + 1, 1]

    w = w_in[0]
    lr0 = 2 * dk + 2 * dv
    u0 = lr0 + 2 * rank
    w_main = jnp.concatenate(
        [w[:, :lr0], w[:, u0 + cch:u0 + 2 * cch], w[:, u0:u0 + cch], w[:, u0 + 2 * cch:]],
        axis=1).astype(BF16)
    w_lr = jnp.pad(w[:, lr0:lr0 + 2 * rank], ((0, 0), (0, LANES - 2 * rank))).astype(BF16)
    wup = gla_w_a_up[0].astype(BF16)
    wupf = jnp.zeros((LANES, dk), BF16).at[:rank].set(wup[0])
    wupb = jnp.zeros((LANES, dk), BF16).at[rank:2 * rank].set(wup[1])
    baf, bab = gla_b_a[0, 0:1], gla_b_a[0, 1:2]
    n1g, n2g = norm1_g[0:1], norm2_g[0:1]

    s0f, s0b = _ctx_states(ctx, csh1, csc1, n1g, w_main, w_lr, wupf, wupb, baf, bab,
                           heads, dk, dv)
    z, lr = _inproj(x, sh1, sc1, n1g, w_main, w_lr, _pick(n, 1024))
    o_f, o_b = _gla(z, lr, wupf, wupb, baf, bab, s0f, s0b, heads, dk, dv, _pick(n, 512))
    x1, h2, aff_t = _mixer(
        o_f, o_b, z, x, jnp.tile(gla_norm_g[0:1], (1, heads)), conv_w[0],
        gla_w_o[0].astype(BF16), conv_w_out[0].astype(BF16), merge_w_out[0].astype(BF16),
        gt1, sh2, sc2, n2g, router_w[0].T.astype(BF16), heads, dk, _pick(n, 256))

    pos, before, idx = _route(aff_t, cap)
    tmc = _pick(n, 256)
    ye = _experts(h2, idx, exp_w_gate[0], exp_w_up[0], exp_w_down[0], cap,
                  cap + _combine_windows(tmc)[1], _pick(exp_w_gate.shape[-1], 512))
    return _combine(ye, pos, aff_t, before, x1, gt2, final_g.reshape(1, d), cap, tmc)
```

```python
import functools

import numpy as np
import jax
import jax.numpy as jnp
from jax import lax
from jax.experimental import pallas as pl
from jax.experimental.pallas import tpu as pltpu

F32 = jnp.float32
BF16 = jnp.bfloat16
I32 = jnp.int32

EPS = 1e-6
N_MOD = 6
GRID_W = 64
GLA_CHUNK = 64
GLA_GATE_NORM = 16.0
GLA_SCAN_LAG = 2
EC_CAPACITY = 2

LANES = 128
BF16_ROWS = 16
VMEM_LIMIT = 56 * 1024 * 1024


def _cparams(*sem):
    return pltpu.CompilerParams(dimension_semantics=sem, vmem_limit_bytes=VMEM_LIMIT)


def _dot(a, b):
    return jnp.dot(a, b, preferred_element_type=F32)


def _dot_nt(a, b):
    return lax.dot_general(a, b, (((1,), (1,)), ((), ())), preferred_element_type=F32)


def _dot_tn(a, b):
    return lax.dot_general(a, b, (((0,), (0,)), ((), ())), preferred_element_type=F32)


def _sigmoid(v):
    return 1.0 / (1.0 + jnp.exp(-v))


def _log_sigmoid(v):
    return jnp.minimum(v, 0.0) - jnp.log1p(jnp.exp(-jnp.abs(v)))


def _rms_mod(xv, g, shift, scale):
    ms = jnp.mean(xv * xv, axis=-1, keepdims=True)
    y = xv * lax.rsqrt(ms + EPS) * g
    return y * (1.0 + scale) + shift


def _split_bf16(v):
    hi = v.astype(BF16)
    lo = (v - hi.astype(F32)).astype(BF16)
    return hi, lo


def _adaln_kernel(c_ref, w_ref, b_ref, o_ref):
    cv = c_ref[...]
    s = (cv * _sigmoid(cv)).astype(BF16)
    o_ref[...] = _dot(s, w_ref[...].astype(BF16)) + b_ref[...]


def _adaln(cond, w, b):
    rows, d = cond.shape
    nout = w.shape[1]
    tn = d
    return pl.pallas_call(
        _adaln_kernel,
        grid=(nout // tn,),
        in_specs=[pl.BlockSpec((rows, d), lambda j: (0, 0)),
                  pl.BlockSpec((d, tn), lambda j: (0, j)),
                  pl.BlockSpec((1, tn), lambda j: (0, j))],
        out_specs=pl.BlockSpec((rows, tn), lambda j: (0, j)),
        out_shape=jax.ShapeDtypeStruct((rows, nout), F32),
        compiler_params=_cparams("arbitrary"),
        name="adaln",
    )(cond, w, b.reshape(1, nout))


def _wprep_kernel(tiles, wt_hbm, o_ref, buf, sem):
    def copy(t):
        row, cnt = tiles[t]
        return pltpu.make_async_copy(wt_hbm.at[pl.ds(row, cnt), :],
                                     buf.at[t % 2, pl.ds(0, cnt), :], sem.at[t % 2])

    copy(0).start()
    out_row = 0
    for t, (_, cnt) in enumerate(tiles):
        if t + 1 < len(tiles):
            copy(t + 1).start()
        copy(t).wait()
        o_ref[out_row:out_row + cnt, :] = buf[t % 2, 0:cnt, :].astype(BF16)
        out_row += cnt


def _wprep(wt, tiles):
    d = wt.shape[1]
    rows = sum(cnt for _, cnt in tiles)
    return pl.pallas_call(
        functools.partial(_wprep_kernel, tiles),
        in_specs=[pl.BlockSpec(memory_space=pl.ANY)],
        out_specs=pl.BlockSpec((rows, d), lambda: (0, 0)),
        out_shape=jax.ShapeDtypeStruct((rows, d), BF16),
        scratch_shapes=[pltpu.VMEM((2, max(cnt for _, cnt in tiles), d), F32),
                        pltpu.SemaphoreType.DMA((2,))],
        compiler_params=pltpu.CompilerParams(vmem_limit_bytes=VMEM_LIMIT),
        name="wprep",
    )(wt)


def _ctx_kernel(heads, ctx_ref, sh_ref, sc_ref, g_ref, wk_ref, wv_ref, wlr_ref,
                wupf_ref, wupb_ref, baf_ref, bab_ref, sf_ref, sb_ref):
    n = ctx_ref.shape[0]
    hc = _rms_mod(ctx_ref[...], g_ref[...], sh_ref[...], sc_ref[...]).astype(BF16)
    k = _dot_nt(hc, wk_ref[...])
    v = _dot_nt(hc, wv_ref[...]).astype(BF16)
    lr = _dot_nt(hc, wlr_ref[...]).astype(BF16)
    hk = k.shape[1] // heads
    hv = v.shape[1] // heads
    row = lax.broadcasted_iota(I32, (n, n), 0)
    col = lax.broadcasted_iota(I32, (n, n), 1)
    for wup_ref, ba_ref, s_ref, tri, last in (
            (wupf_ref, baf_ref, sf_ref, col <= row, n - 1),
            (wupb_ref, bab_ref, sb_ref, col >= row, 0)):
        la = _log_sigmoid(_dot(lr, wup_ref[...]) + ba_ref[...]) * (1.0 / GLA_GATE_NORM)
        hi, lo = _split_bf16(la)
        t = jnp.where(tri, 1.0, 0.0).astype(BF16)
        b = _dot(t, hi) + _dot(t, lo)
        kd = (k * jnp.exp(b[last:last + 1, :] - b)).astype(BF16)
        for h in range(heads):
            s_ref[h] = _dot_tn(v[:, h * hv:(h + 1) * hv], kd[:, h * hk:(h + 1) * hk])


def _ctx_states(ctx, sh, sc, g, wt, wupf, wupb, baf, bab, heads, dk, dv):
    bsz, n, d = ctx.shape
    lr_blk = (wt.shape[0] - LANES) // LANES
    hk, hv = dk // heads, dv // heads
    full = lambda *shape: pl.BlockSpec(shape, lambda b: (0,) * len(shape))
    st = jax.ShapeDtypeStruct((bsz, heads, hv, hk), F32)
    sspec = pl.BlockSpec((None, heads, hv, hk), lambda b: (b, 0, 0, 0))
    return pl.pallas_call(
        functools.partial(_ctx_kernel, heads),
        grid=(bsz,),
        in_specs=[pl.BlockSpec((None, n, d), lambda b: (b, 0, 0)),
                  full(1, d), full(1, d), full(1, d),
                  pl.BlockSpec((dk, d), lambda b: (1, 0)),
                  pl.BlockSpec((dv, d), lambda b: (2 * dk // dv, 0)),
                  pl.BlockSpec((LANES, d), lambda b: (lr_blk, 0)),
                  full(LANES, dk), full(LANES, dk), full(1, dk), full(1, dk)],
        out_specs=(sspec, sspec),
        out_shape=(st, st),
        compiler_params=_cparams("arbitrary"),
        name="ctx_state",
    )(ctx, sh, sc, g, wt, wt, wt, wupf, wupb, baf, bab)


def _inproj_kernel(x_ref, sh_ref, sc_ref, g_ref, w_ref, wlr_ref, z_ref, lr_ref, h_scr):
    j = pl.program_id(2)
    d = x_ref.shape[1]

    @pl.when(j == 0)
    def _():
        hb = _rms_mod(x_ref[...], g_ref[...], sh_ref[...], sc_ref[...]).astype(BF16)
        h_scr[...] = hb
        lr_ref[...] = _dot_nt(hb, wlr_ref[...])
        z_ref[...] = _dot_nt(hb, w_ref[...]).astype(BF16)

    @pl.when(j == 1)
    def _():
        a = _dot_nt(h_scr[...], w_ref[...])
        gv = a[:, :d]
        z_ref[:, :d] = (gv * _sigmoid(gv)).astype(BF16)
        z_ref[:, d:] = a[:, d:].astype(BF16)

    @pl.when(j == 2)
    def _():
        a = _dot_nt(h_scr[...], w_ref[...])
        z_ref[:, :d] = (a[:, d:] * a[:, :d]).astype(BF16)
        z_ref[:, d:] = jnp.zeros((z_ref.shape[0], d), BF16)

    @pl.when(j == 3)
    def _():
        z_ref[...] = _sigmoid(_dot_nt(h_scr[...], w_ref[...])).astype(BF16)


def _inproj(x, sh, sc, g, wt, tm):
    bsz, n, d = x.shape
    p = wt.shape[0] - LANES
    tn = 2 * d
    assert p == 4 * tn
    return pl.pallas_call(
        _inproj_kernel,
        grid=(bsz, n // tm, p // tn),
        in_specs=[pl.BlockSpec((None, tm, d), lambda b, i, j: (b, i, 0)),
                  pl.BlockSpec((None, 1, d), lambda b, i, j: (b, 0, 0)),
                  pl.BlockSpec((None, 1, d), lambda b, i, j: (b, 0, 0)),
                  pl.BlockSpec((1, d), lambda b, i, j: (0, 0)),
                  pl.BlockSpec((tn, d), lambda b, i, j: (j, 0)),
                  pl.BlockSpec((LANES, d), lambda b, i, j: (p // LANES, 0))],
        out_specs=(pl.BlockSpec((None, tm, tn), lambda b, i, j: (b, i, j)),
                   pl.BlockSpec((None, tm, LANES), lambda b, i, j: (b, i, 0))),
        out_shape=(jax.ShapeDtypeStruct((bsz, n, p), BF16),
                   jax.ShapeDtypeStruct((bsz, n, LANES), F32)),
        scratch_shapes=[pltpu.VMEM((tm, d), BF16)],
        compiler_params=_cparams("arbitrary", "arbitrary", "arbitrary"),
        name="inproj",
    )(x, sh, sc, g, wt, wt)


def _gla_kernel(heads, qf_ref, kf_ref, vf_ref, lrf_ref, qb_ref, kb_ref, vb_ref, lrb_ref,
                wupf_ref, wupb_ref, baf_ref, bab_ref, s0f_ref, s0b_ref,
                of_ref, ob_ref, sf_scr, sb_scr):
    tb, dk = qf_ref.shape
    dv = vf_ref.shape[1]
    hk, hv = dk // heads, dv // heads
    ck = GLA_CHUNK
    nck = tb // ck
    q_scale = hk ** -0.5

    @pl.when(pl.program_id(1) == 0)
    def _():
        sf_scr[...] = s0f_ref[...]
        sb_scr[...] = s0b_ref[...]

    row = lax.broadcasted_iota(I32, (tb, tb), 0)
    col = lax.broadcasted_iota(I32, (tb, tb), 1)
    shift = ck.bit_length() - 1
    same_chunk = (row >> shift) == (col >> shift)
    crow = lax.broadcasted_iota(I32, (ck, ck), 0)
    ccol = lax.broadcasted_iota(I32, (ck, ck), 1)

    dirs = (
        (qf_ref, kf_ref, vf_ref, lrf_ref, wupf_ref, baf_ref, of_ref, sf_scr,
         same_chunk & (col <= row), ccol <= crow, ck - 1, range(nck)),
        (qb_ref, kb_ref, vb_ref, lrb_ref, wupb_ref, bab_ref, ob_ref, sb_scr,
         same_chunk & (col >= row), ccol >= crow, 0, range(nck - 1, -1, -1)),
    )
    b_alls = []
    for _, _, _, lr_ref, wup_ref, ba_ref, _, _, tri, _, _, _ in dirs:
        la = _log_sigmoid(_dot(lr_ref[...].astype(BF16), wup_ref[...]) + ba_ref[...])
        la = la * (1.0 / GLA_GATE_NORM)
        hi, lo = _split_bf16(la)
        t = jnp.where(tri, 1.0, 0.0).astype(BF16)
        b_alls.append(_dot(t, hi) + _dot(t, lo))
    work = [[], []]

    def stage1(k):
        for di, (q_ref, k_ref, v_ref, _, _, _, _, _, _, cmask, last, order) in enumerate(dirs):
            r0 = order[k] * ck
            b = b_alls[di][r0:r0 + ck, :]
            b_last = b[last:last + 1, :]
            qc = q_ref[r0:r0 + ck, :].astype(F32) * q_scale
            kc = k_ref[r0:r0 + ck, :].astype(F32)
            q_in = (qc * jnp.exp(b)).astype(BF16)
            k_in = (kc * jnp.exp(-b)).astype(BF16)
            k_dec = (kc * jnp.exp(b_last - b)).astype(BF16)
            decay = jnp.exp(b_last)
            units = []
            for h in range(heads):
                qh = q_in[:, h * hk:(h + 1) * hk]
                vh = v_ref[r0:r0 + ck, h * hv:(h + 1) * hv]
                scores = jnp.where(cmask, _dot_nt(qh, k_in[:, h * hk:(h + 1) * hk]), 0.0)
                o_intra = _dot(scores.astype(BF16), vh)
                ut = _dot_tn(vh, k_dec[:, h * hk:(h + 1) * hk])
                units.append((qh, o_intra, ut, decay[:, h * hk:(h + 1) * hk]))
            work[di].append((r0, units))
    def scan(k):
        for di, (_, _, _, _, _, _, o_ref, s_scr, _, _, _, _) in enumerate(dirs):
            r0, units = work[di][k]
            for h, (qh, o_intra, ut, dec) in enumerate(units):
                st = s_scr[h]
                o = o_intra + _dot_nt(qh, st.astype(BF16))
                o_ref[r0:r0 + ck, h * hv:(h + 1) * hv] = o.astype(BF16)
                s_scr[h] = st * dec + ut

    for k in range(nck + GLA_SCAN_LAG):
        if k < nck:
            stage1(k)
        if k >= GLA_SCAN_LAG:
            scan(k - GLA_SCAN_LAG)


def _gla(z, lr, wupf, wupb, baf, bab, s0f, s0b, heads, dk, dv, tb):
    bsz, n, _ = z.shape
    nb = n // tb
    hk, hv = dk // heads, dv // heads
    fwd = lambda cb: (lambda b, i: (b, i, cb))
    bwd = lambda cb: (lambda b, i: (b, nb - 1 - i, cb))
    full = lambda *shape: pl.BlockSpec(shape, lambda b, i: (0,) * len(shape))
    sspec = pl.BlockSpec((None, heads, hv, hk), lambda b, i: (b, 0, 0, 0))
    vcb = 2 * dk // dv
    ost = jax.ShapeDtypeStruct((bsz, n, dv), BF16)
    return pl.pallas_call(
        functools.partial(_gla_kernel, heads),
        grid=(bsz, nb),
        in_specs=[pl.BlockSpec((None, tb, dk), fwd(0)), pl.BlockSpec((None, tb, dk), fwd(1)),
                  pl.BlockSpec((None, tb, dv), fwd(vcb)), pl.BlockSpec((None, tb, LANES), fwd(0)),
                  pl.BlockSpec((None, tb, dk), bwd(0)), pl.BlockSpec((None, tb, dk), bwd(1)),
                  pl.BlockSpec((None, tb, dv), bwd(vcb)), pl.BlockSpec((None, tb, LANES), bwd(0)),
                  full(LANES, dk), full(LANES, dk), full(1, dk), full(1, dk), sspec, sspec],
        out_specs=(pl.BlockSpec((None, tb, dv), fwd(0)), pl.BlockSpec((None, tb, dv), bwd(0))),
        out_shape=(ost, ost),
        scratch_shapes=[pltpu.VMEM((heads, hv, hk), F32), pltpu.VMEM((heads, hv, hk), F32)],
        compiler_params=_cparams("arbitrary", "arbitrary"),
        name="gla",
    )(z, z, z, lr, z, z, z, lr, wupf, wupb, baf, bab, s0f, s0b)


def _mixer_kernel(heads, of_ref, ob_ref, sg_ref, bg_ref, cu_ref, srg_ref, src_ref, x_ref,
                  gng_ref, cw_ref, wo_ref, wco_ref, wm_ref, gt1_ref, sh2_ref, sc2_ref, n2g_ref,
                  rwt_ref, x1_ref, h2_ref, aff_ref):
    tm, dv = of_ref.shape
    hv = dv // heads
    o = of_ref[...].astype(F32) + ob_ref[...].astype(F32)
    parts = []
    for h in range(heads):
        oh = o[:, h * hv:(h + 1) * hv]
        ms = jnp.mean(oh * oh, axis=-1, keepdims=True)
        parts.append(oh * lax.rsqrt(ms + EPS))
    on = jnp.concatenate(parts, axis=1) * gng_ref[...]
    y_gla = _dot((on * sg_ref[...].astype(F32)).astype(BF16), wo_ref[...])

    cu = cu_ref[...].astype(F32)
    gcol = lax.broadcasted_iota(I32, cu.shape, 0) & (GRID_W - 1)
    left = jnp.where(gcol == 0, 0.0, pltpu.roll(cu, 1, 0))
    right = jnp.where(gcol == GRID_W - 1, 0.0, pltpu.roll(cu, tm - 1, 0))
    cw = cw_ref[...]
    conv = left * cw[0:1, :] + cu * cw[1:2, :] + right * cw[2:3, :]
    y_conv = _dot((bg_ref[...].astype(F32) * conv).astype(BF16), wco_ref[...])

    merged = srg_ref[...].astype(F32) * y_gla + src_ref[...].astype(F32) * y_conv
    y = _dot(merged.astype(BF16), wm_ref[...])
    x1 = x_ref[...] + gt1_ref[...] * y
    x1_ref[...] = x1
    h2 = _rms_mod(x1, n2g_ref[...], sh2_ref[...], sc2_ref[...])
    rpt = h2.shape[1] // LANES
    for a in range(rpt):
        h2_ref[pl.ds(a, tm, stride=rpt), :] = h2[:, a * LANES:(a + 1) * LANES]
    logits = _dot_nt(rwt_ref[...], h2.astype(BF16))
    ex = jnp.exp(logits - jnp.max(logits, axis=0, keepdims=True))
    aff_ref[...] = ex / jnp.sum(ex, axis=0, keepdims=True)


def _mixer(o_f, o_b, z, x, gng, conv_w, w_o, w_co, w_m, gt1, sh2, sc2, n2g, rwt, heads, dk, tm):
    bsz, n, d = x.shape
    dv = o_f.shape[2]
    c = conv_w.shape[1]
    e = rwt.shape[0]
    zb = lambda cb: pl.BlockSpec((None, tm, d), lambda b, i: (b, i, cb))
    base = (2 * dk + dv) // d
    tok = lambda w: pl.BlockSpec((None, tm, w), lambda b, i: (b, i, 0))
    full = lambda *shape: pl.BlockSpec(shape, lambda b, i: (0,) * len(shape))
    perb = pl.BlockSpec((None, 1, d), lambda b, i: (b, 0, 0))
    return pl.pallas_call(
        functools.partial(_mixer_kernel, heads),
        grid=(bsz, n // tm),
        in_specs=[tok(dv), tok(dv), zb(base), zb(base + 1), zb(base + 2),
                  zb(base + 4), zb(base + 5), tok(d),
                  full(1, dv), full(3, c), full(dv, d), full(c, d), full(d, d),
                  perb, perb, perb, full(1, d), full(e, d)],
        out_specs=(tok(d), pl.BlockSpec((None, tm * (d // LANES), LANES), lambda b, i: (b, i, 0)),
                   pl.BlockSpec((None, e, tm), lambda b, i: (b, 0, i))),
        out_shape=(jax.ShapeDtypeStruct((bsz, n, d), F32),
                   jax.ShapeDtypeStruct((bsz, n * (d // LANES), LANES), F32),
                   jax.ShapeDtypeStruct((bsz, e, n), F32)),
        compiler_params=_cparams("arbitrary", "arbitrary"),
        name="mixer",
    )(o_f, o_b, z, z, z, z, z, x, gng, conv_w, w_o, w_co, w_m, gt1, sh2, sc2, n2g, rwt)


def _route_kernel(cap, aff_ref, lmat_ref, pos_ref, base_ref, idx_ref, loc_scr, bef_scr):
    e, n = aff_ref.shape
    nt = n // LANES
    aff = aff_ref[...]

    def count(mask):
        return jnp.sum(jnp.where(mask, 1.0, 0.0), axis=1, keepdims=True)

    def search(k, tbits):
        cand = tbits | jnp.left_shift(jnp.int32(1), 30 - k)
        ok = count(aff >= lax.bitcast_convert_type(cand, F32)) >= cap
        return jnp.where(ok, cand, tbits)

    tbits = lax.fori_loop(0, 31, search, jnp.zeros((e, 1), I32))
    thr = lax.bitcast_convert_type(tbits, F32)
    gt = aff > thr
    eq = aff == thr
    need = cap - count(gt)

    def stack(mask):
        m = jnp.where(mask, 1.0, 0.0)
        return jnp.concatenate([m[:, j * LANES:(j + 1) * LANES] for j in range(nt)], axis=0)

    r = lax.broadcasted_iota(I32, (LANES, LANES), 0)
    cl = lax.broadcasted_iota(I32, (LANES, LANES), 1)
    upper = jnp.where(r <= cl, 1.0, 0.0).astype(BF16)
    ones = jnp.ones((LANES, LANES), BF16)
    lmat = lmat_ref[...]

    def cumsum(ms):
        msb = ms.astype(BF16)
        before = _dot(_dot(lmat, msb).astype(BF16), ones)
        return _dot(msb, upper), before

    eq_s = stack(eq)
    loc_eq, before_eq = cumsum(eq_s)
    need_s = jnp.concatenate([need] * nt, axis=0)
    sel = jnp.maximum(stack(gt), jnp.where(loc_eq + before_eq <= need_s, eq_s, 0.0))
    loc, before = cumsum(sel)
    pos_ref[...] = jnp.where(sel > 0.0, loc + before - 1.0, -1.0).astype(I32)
    base_ref[...] = before.astype(I32)

    loc_scr[...] = loc
    bef_scr[...] = before
    slot = lax.broadcasted_iota(I32, (1, cap), 1).astype(F32)
    for ex in range(e):
        loc_e = loc_scr[pl.ds(ex, nt, stride=e), :]
        tprev = bef_scr[pl.ds(ex, nt, stride=e), :][:, 0:1]
        tincl = tprev + loc_e[:, LANES - 1:LANES]
        in_tile = jnp.where((tprev <= slot) & (slot < tincl), 1.0, 0.0)
        tile = jnp.sum(jnp.where(tincl <= slot, 1.0, 0.0), axis=0, keepdims=True)
        s_loc = slot - jnp.sum(in_tile * tprev, axis=0, keepdims=True)
        counts = _dot_tn(loc_e.astype(BF16), in_tile.astype(BF16))
        lane = jnp.sum(jnp.where(counts <= s_loc, 1.0, 0.0), axis=0, keepdims=True)
        idx_ref[pl.ds(ex, 1), :] = (tile * LANES + lane).astype(I32)


def _route(aff_t, cap):
    bsz, e, n = aff_t.shape
    nt = n // LANES
    rows = nt * e
    ri = np.arange(rows)
    lmat = ((ri[None, :] % e == ri[:, None] % e) & (ri[None, :] // e < ri[:, None] // e))
    lmat = jnp.asarray(lmat, BF16)
    st = jax.ShapeDtypeStruct((bsz, rows, LANES), I32)
    ospec = pl.BlockSpec((None, rows, LANES), lambda b: (b, 0, 0))
    return pl.pallas_call(
        functools.partial(_route_kernel, cap),
        grid=(bsz,),
        in_specs=[pl.BlockSpec((None, e, n), lambda b: (b, 0, 0)),
                  pl.BlockSpec((rows, rows), lambda b: (0, 0))],
        out_specs=(ospec, ospec, pl.BlockSpec((None, e, cap), lambda b: (b, 0, 0))),
        out_shape=(st, st, jax.ShapeDtypeStruct((bsz, e, cap), I32)),
        scratch_shapes=[pltpu.VMEM((rows, LANES), F32), pltpu.VMEM((rows, LANES), F32)],
        compiler_params=_cparams("arbitrary"),
        name="route",
    )(aff_t, lmat)


def _slot_window(base):
    return pl.multiple_of((base // BF16_ROWS) * BF16_ROWS, BF16_ROWS)


def _expert_kernel(cap, idx_ref, h2_hbm, wg_ref, wu_ref, wd_ref, ye_ref, xbuf, acc, sem):
    e, f = pl.program_id(1), pl.program_id(2)
    n_exp, nf = pl.num_programs(1), pl.num_programs(2)
    last = pl.num_programs(0) * n_exp - 1
    lin = pl.program_id(0) * n_exp + e
    slot = lin % 2
    per_step = cap // nf
    rpt = wg_ref.shape[0] // LANES

    def row_copy(lin_t, part, i, slot_t):
        tok = idx_ref[(lin_t * nf + part) * per_step + i]
        return pltpu.make_async_copy(h2_hbm.at[pl.ds(pl.multiple_of(tok * rpt, rpt), rpt), :],
                                     xbuf.at[slot_t, part, pl.ds(i * rpt, rpt), :],
                                     sem.at[slot_t])

    def wait_rows(slot_t):
        pltpu.make_async_copy(xbuf.at[slot_t], xbuf.at[slot_t], sem.at[slot_t]).wait()

    @pl.when((lin == 0) & (f == 0))
    def _():
        for part in range(nf):
            def first(i, carry):
                row_copy(lin, part, i, slot).start()
                return carry
            lax.fori_loop(0, per_step, first, 0)

    @pl.when(f == 0)
    def _():
        wait_rows(slot)
        acc[...] = jnp.zeros_like(acc)

    nxt = jnp.minimum(lin + 1, last)
    for i in range(per_step):
        row_copy(nxt, f, i, 1 - slot).start()

    @pl.when((lin == last) & (f == nf - 1))
    def _():
        wait_rows(1 - slot)

    xv = jnp.concatenate(
        [jnp.concatenate([xbuf[slot, part, pl.ds(a, per_step, stride=rpt), :].astype(BF16)
                          for a in range(rpt)], axis=1) for part in range(nf)], axis=0)
    hg =_dot(xv, wg_ref[...].astype(BF16))
    hu = _dot(xv, wu_ref[...].astype(BF16))
    hid = (hg * _sigmoid(hg) * hu).astype(BF16)
    acc[...] += _dot(hid, wd_ref[...].astype(BF16))

    @pl.when(f == nf - 1)
    def _():
        ye_ref[0:cap, :] = acc[...].astype(BF16)
        ye_ref[cap:, :] = jnp.zeros((ye_ref.shape[0] - cap, ye_ref.shape[1]), BF16)


def _experts(h2, idx, w_gate, w_up, w_down, cap, rows, tf):
    n_exp, d, df = w_gate.shape
    bsz = h2.shape[0]
    rpt = d // LANES
    n = h2.shape[1] // rpt
    nf = df // tf
    assert cap % nf == 0
    rows_global = (idx + (jnp.arange(bsz, dtype=I32) * n)[:, None, None]).reshape(-1)
    return pl.pallas_call(
        functools.partial(_expert_kernel, cap),
        grid_spec=pltpu.PrefetchScalarGridSpec(
            num_scalar_prefetch=1,
            grid=(bsz, n_exp, nf),
            in_specs=[pl.BlockSpec(memory_space=pl.ANY),
                      pl.BlockSpec((None, d, tf), lambda b, e, f, s: (e, 0, f)),
                      pl.BlockSpec((None, d, tf), lambda b, e, f, s: (e, 0, f)),
                      pl.BlockSpec((None, tf, d), lambda b, e, f, s: (e, f, 0))],
            out_specs=pl.BlockSpec((None, None, rows, d), lambda b, e, f, s: (b, e, 0, 0)),
            scratch_shapes=[pltpu.VMEM((2, nf, (cap // nf) * rpt, LANES), F32),
                            pltpu.VMEM((cap, d), F32), pltpu.SemaphoreType.DMA((2,))]),
        out_shape=jax.ShapeDtypeStruct((bsz, n_exp, rows, d), BF16),
        compiler_params=_cparams("arbitrary", "arbitrary", "arbitrary"),
        name="experts",
    )(rows_global, h2.reshape(bsz * n * rpt, LANES), w_gate, w_up, w_down)


def _combine_windows(tm):
    return tm // 4 + BF16_ROWS, tm + BF16_ROWS


def _combine_kernel(n_exp, nsub, base_ref, ovf_ref, pos_ref, aff_ref, x1_ref, gt2_ref, fg_ref,
                    ye_hbm, out_ref, buf, fbuf, ffn_scr, sem, fsem):
    b, j = pl.program_id(0), pl.program_id(1)
    nt = pl.num_programs(1)
    step = b * nt + j
    slot = step % 2
    wp, wfull = buf.shape[2], fbuf.shape[1]
    d = buf.shape[-1]

    def start_of(tile, ex):
        return _slot_window(base_ref[tile * nsub * n_exp + ex])

    def window_copy(tile, ex, dst_slot):
        return pltpu.make_async_copy(ye_hbm.at[tile // nt, ex, pl.ds(start_of(tile, ex), wp), :],
                                     buf.at[dst_slot, ex], sem.at[dst_slot])

    def full_copy(ex):
        return pltpu.make_async_copy(ye_hbm.at[b, ex, pl.ds(start_of(step, ex), wfull), :],
                                     fbuf.at[ex], fsem)

    @pl.when(step == 0)
    def _():
        for ex in range(n_exp):
            window_copy(step, ex, slot).start()

    @pl.when(step + 1 < pl.num_programs(0) * nt)
    def _():
        for ex in range(n_exp):
            window_copy(step + 1, ex, 1 - slot).start()

    def gates_t(rows):
        srow = lax.broadcasted_iota(I32, (rows, LANES), 0)
        per_exp = []
        for ex in range(n_exp):
            s0 = start_of(step, ex)
            parts = []
            for u in range(nsub):
                rel = pos_ref[pl.ds(u * n_exp + ex, 1), :] - s0
                gate = aff_ref[pl.ds(ex, 1), u * LANES:(u + 1) * LANES]
                parts.append(jnp.where(srow == rel, gate, 0.0).astype(BF16))
            per_exp.append(jnp.concatenate(parts, axis=1))
        return jnp.concatenate(per_exp, axis=0)

    for ex in range(n_exp):
        window_copy(step, ex, slot).wait()

    @pl.when(ovf_ref[step] == 0)
    def _():
        ffn_scr[...] = _dot_tn(gates_t(wp), buf[slot].reshape(n_exp * wp, d))

    @pl.when(ovf_ref[step] != 0)
    def _():
        for ex in range(n_exp):
            full_copy(ex).start()
        for ex in range(n_exp):
            full_copy(ex).wait()
        ffn_scr[...] = _dot_tn(gates_t(wfull), fbuf[...].reshape(n_exp * wfull, d))

    x2 = x1_ref[...] + gt2_ref[...] * ffn_scr[...]
    ms = jnp.mean(x2 * x2, axis=-1, keepdims=True)
    out_ref[...] = x2 * lax.rsqrt(ms + EPS) * fg_ref[...]


def _combine(ye, pos, aff_t, before, x1, gt2, final_g, cap, tm):
    bsz, n, d = x1.shape
    n_exp = aff_t.shape[1]
    nsub = tm // LANES
    nt = n // tm
    wp, wfull = _combine_windows(tm)
    starts = before[:, :, 0].reshape(bsz, n // LANES, n_exp)
    tile_starts = starts[:, ::nsub]
    tile_ends = jnp.concatenate([tile_starts[:, 1:], jnp.full((bsz, 1, n_exp), cap, I32)], axis=1)
    overflow = (tile_starts % BF16_ROWS) + (tile_ends - tile_starts) > wp
    ovf = jnp.any(overflow, axis=-1).astype(I32).reshape(-1)
    return pl.pallas_call(
        functools.partial(_combine_kernel, n_exp, nsub),
        grid_spec=pltpu.PrefetchScalarGridSpec(
            num_scalar_prefetch=2,
            grid=(bsz, nt),
            in_specs=[pl.BlockSpec((None, nsub * n_exp, LANES), lambda b, j, s, o: (b, j, 0)),
                      pl.BlockSpec((None, n_exp, tm), lambda b, j, s, o: (b, 0, j)),
                      pl.BlockSpec((None, tm, d), lambda b, j, s, o: (b, j, 0)),
                      pl.BlockSpec((None, 1, d), lambda b, j, s, o: (b, 0, 0)),
                      pl.BlockSpec((1, d), lambda b, j, s, o: (0, 0)),
                      pl.BlockSpec(memory_space=pl.ANY)],
            out_specs=pl.BlockSpec((None, tm, d), lambda b, j, s, o: (b, j, 0)),
            scratch_shapes=[pltpu.VMEM((2, n_exp, wp, d), BF16),
                            pltpu.VMEM((n_exp, wfull, d), BF16),
                            pltpu.VMEM((tm, d), F32),
                            pltpu.SemaphoreType.DMA((2,)), pltpu.SemaphoreType.DMA(())]),
        out_shape=jax.ShapeDtypeStruct((bsz, n, d), F32),
        compiler_params=_cparams("arbitrary", "arbitrary"),
        name="combine",
    )(starts.reshape(-1), ovf, pos, aff_t, x1, gt2, final_g, ye)


def _pick(n, pref):
    t = min(n, pref)
    while n % t:
        t //= 2
    return t


def kernel(x, c, ctx, c_ctx, w_ada, b_ada, norm1_g, norm2_g, w_in, gla_w_a_up, gla_b_a,
           gla_norm_g, gla_w_o, conv_w, conv_w_out, merge_w_out, router_w,
           exp_w_gate, exp_w_up, exp_w_down, final_g):
    bsz, n, d = x.shape
    depth = w_ada.shape[0]
    assert depth == 1, "single-layer trunk"
    rank, dk = gla_w_a_up.shape[-2:]
    dv = gla_w_o.shape[1]
    hv = gla_norm_g.shape[-1]
    heads = dv // hv
    cch = conv_w.shape[-1]
    n_exp = router_w.shape[-1]
    cap = EC_CAPACITY * n // n_exp
    assert 2 * dk == d and dv == d and cch == d and 2 * rank <= LANES
    assert n % LANES == 0 and n % GRID_W == 0 and bsz + 1 <= 8 and n_exp % 8 == 0

    cond = jnp.zeros((8, d), F32).at[:bsz].set(c).at[bsz].set(c_ctx)
    mod = _adaln(cond, w_ada[0], b_ada[0]).reshape(8, N_MOD, d)
    sh1, sc1, gt1, sh2, sc2, gt2 = [mod[:bsz, k][:, None, :] for k in range(N_MOD)]
    csh1, csc1 = mod[bsz:bsz + 1, 0], mod[bsz:bsz + 1, 1]

    lr0 = 2 * dk + 2 * dv
    u0 = lr0 + 2 * rank
    groups = ((0, lr0), (u0 + cch, cch), (u0, cch), (u0 + 2 * cch, cch + 2 * d), (lr0, LANES))
    tiles = tuple((r0 + o, min(d, cnt - o)) for r0, cnt in groups for o in range(0, cnt, d))
    wt = _wprep(jnp.swapaxes(w_in[0], 0, 1), tiles)
    wup = gla_w_a_up[0].astype(BF16)
    wupf = jnp.zeros((LANES, dk), BF16).at[:rank].set(wup[0])
    wupb = jnp.zeros((LANES, dk), BF16).at[rank:2 * rank].set(wup[1])
    baf, bab = gla_b_a[0, 0:1], gla_b_a[0, 1:2]
    n1g, n2g = norm1_g[0:1], norm2_g[0:1]

    s0f, s0b = _ctx_states(ctx, csh1, csc1, n1g, wt, wupf, wupb, baf, bab, heads, dk, dv)
    z, lr = _inproj(x, sh1, sc1, n1g, wt, _pick(n, 1024))
    o_f, o_b = _gla(z, lr, wupf, wupb, baf, bab, s0f, s0b, heads, dk, dv, _pick(n, 512))
    x1, h2, aff_t = _mixer(
        o_f, o_b, z, x, jnp.tile(gla_norm_g[0:1], (1, heads)), conv_w[0],
        gla_w_o[0].astype(BF16), conv_w_out[0].astype(BF16), merge_w_out[0].astype(BF16),
        gt1, sh2, sc2, n2g, router_w[0].T.astype(BF16), heads, dk, _pick(n, 256))

    pos, before, idx = _route(aff_t, cap)
    tmc = _pick(n, 256)
    ye = _experts(h2, idx, exp_w_gate[0], exp_w_up[0], exp_w_down[0], cap,
                  cap + _combine_windows(tmc)[1], _pick(exp_w_gate.shape[-1], 512))
    return _combine(ye, pos, aff_t, before, x1, gt2, final_g.reshape(1, d), cap, tmc)
```

```python
import functools

import numpy as np
import jax
import jax.numpy as jnp
from jax import lax
from jax.experimental import pallas as pl
from jax.experimental.pallas import tpu as pltpu

F32 = jnp.float32
BF16 = jnp.bfloat16
I32 = jnp.int32

EPS = 1e-6
N_MOD = 6
GRID_W = 64
GLA_CHUNK = 64
GLA_GATE_NORM = 16.0
GLA_SCAN_LAG = 2
EC_CAPACITY = 2

LANES = 128
BF16_ROWS = 16
VMEM_LIMIT = 56 * 1024 * 1024


def _cparams(*sem, **kw):
    return pltpu.CompilerParams(dimension_semantics=sem, vmem_limit_bytes=VMEM_LIMIT, **kw)


def _dot(a, b):
    return jnp.dot(a, b, preferred_element_type=F32)


def _dot_nt(a, b):
    return lax.dot_general(a, b, (((1,), (1,)), ((), ())), preferred_element_type=F32)


def _dot_tn(a, b):
    return lax.dot_general(a, b, (((0,), (0,)), ((), ())), preferred_element_type=F32)


def _sigmoid(v):
    return 1.0 / (1.0 + jnp.exp(-v))


def _log_sigmoid(v):
    return jnp.minimum(v, 0.0) - jnp.log(1.0 + jnp.exp(-jnp.abs(v)))


def _rms_mod(xv, g, shift, scale):
    ms = jnp.mean(xv * xv, axis=-1, keepdims=True)
    y = xv * lax.rsqrt(ms + EPS) * g
    return y * (1.0 + scale) + shift


def _split_bf16(v):
    hi = v.astype(BF16)
    lo = (v - hi.astype(F32)).astype(BF16)
    return hi, lo


def _adaln_kernel(c_ref, w_ref, b_ref, o_ref):
    cv = c_ref[...]
    s = (cv * _sigmoid(cv)).astype(BF16)
    o_ref[...] = _dot(s, w_ref[...].astype(BF16)) + b_ref[...]


def _adaln(cond, w, b):
    rows, d = cond.shape
    nout = w.shape[1]
    tn = d
    return pl.pallas_call(
        _adaln_kernel,
        grid=(nout // tn,),
        in_specs=[pl.BlockSpec((rows, d), lambda j: (0, 0)),
                  pl.BlockSpec((d, tn), lambda j: (0, j)),
                  pl.BlockSpec((1, tn), lambda j: (0, j))],
        out_specs=pl.BlockSpec((rows, tn), lambda j: (0, j)),
        out_shape=jax.ShapeDtypeStruct((rows, nout), F32),
        compiler_params=_cparams("arbitrary"),
        name="adaln",
    )(cond, w, b.reshape(1, nout))


def _wprep_kernel(tiles, wt_hbm, o_ref, buf, sem):
    def copy(t):
        row, cnt = tiles[t]
        return pltpu.make_async_copy(wt_hbm.at[pl.ds(row, cnt), :],
                                     buf.at[t % 2, pl.ds(0, cnt), :], sem.at[t % 2])

    copy(0).start()
    out_row = 0
    for t, (_, cnt) in enumerate(tiles):
        if t + 1 < len(tiles):
            copy(t + 1).start()
        copy(t).wait()
        o_ref[out_row:out_row + cnt, :] = buf[t % 2, 0:cnt, :].astype(BF16)
        out_row += cnt


def _wprep(wt, tiles):
    d = wt.shape[1]
    rows = sum(cnt for _, cnt in tiles)
    return pl.pallas_call(
        functools.partial(_wprep_kernel, tiles),
        in_specs=[pl.BlockSpec(memory_space=pl.ANY)],
        out_specs=pl.BlockSpec((rows, d), lambda: (0, 0)),
        out_shape=jax.ShapeDtypeStruct((rows, d), BF16),
        scratch_shapes=[pltpu.VMEM((2, max(cnt for _, cnt in tiles), d), F32),
                        pltpu.SemaphoreType.DMA((2,))],
        compiler_params=pltpu.CompilerParams(vmem_limit_bytes=VMEM_LIMIT),
        name="wprep",
    )(wt)


def _ctx_kernel(heads, ctx_ref, sh_ref, sc_ref, g_ref, wk_ref, wv_ref, wlr_ref,
                wupf_ref, wupb_ref, baf_ref, bab_ref, sf_ref, sb_ref):
    n = ctx_ref.shape[0]
    hc = _rms_mod(ctx_ref[...], g_ref[...], sh_ref[...], sc_ref[...]).astype(BF16)
    k = _dot_nt(hc, wk_ref[...])
    v = _dot_nt(hc, wv_ref[...]).astype(BF16)
    lr = _dot_nt(hc, wlr_ref[...]).astype(BF16)
    hk = k.shape[1] // heads
    hv = v.shape[1] // heads
    row = lax.broadcasted_iota(I32, (n, n), 0)
    col = lax.broadcasted_iota(I32, (n, n), 1)
    for wup_ref, ba_ref, s_ref, tri, last in (
            (wupf_ref, baf_ref, sf_ref, col <= row, n - 1),
            (wupb_ref, bab_ref, sb_ref, col >= row, 0)):
        la = _log_sigmoid(_dot(lr, wup_ref[...]) + ba_ref[...]) * (1.0 / GLA_GATE_NORM)
        hi, lo = _split_bf16(la)
        t = jnp.where(tri, 1.0, 0.0).astype(BF16)
        b = _dot(t, hi) + _dot(t, lo)
        kd = (k * jnp.exp(b[last:last + 1, :] - b)).astype(BF16)
        for h in range(heads):
            s_ref[h] = _dot_tn(v[:, h * hv:(h + 1) * hv], kd[:, h * hk:(h + 1) * hk])


def _ctx_states(ctx, sh, sc, g, wt, wupf, wupb, baf, bab, heads, dk, dv):
    bsz, n, d = ctx.shape
    lr_blk = (wt.shape[0] - LANES) // LANES
    hk, hv = dk // heads, dv // heads
    full = lambda *shape: pl.BlockSpec(shape, lambda b: (0,) * len(shape))
    st = jax.ShapeDtypeStruct((bsz, heads, hv, hk), F32)
    sspec = pl.BlockSpec((None, heads, hv, hk), lambda b: (b, 0, 0, 0))
    return pl.pallas_call(
        functools.partial(_ctx_kernel, heads),
        grid=(bsz,),
        in_specs=[pl.BlockSpec((None, n, d), lambda b: (b, 0, 0)),
                  full(1, d), full(1, d), full(1, d),
                  pl.BlockSpec((dk, d), lambda b: (1, 0)),
                  pl.BlockSpec((dv, d), lambda b: (2 * dk // dv, 0)),
                  pl.BlockSpec((LANES, d), lambda b: (lr_blk, 0)),
                  full(LANES, dk), full(LANES, dk), full(1, dk), full(1, dk)],
        out_specs=(sspec, sspec),
        out_shape=(st, st),
        compiler_params=_cparams("arbitrary"),
        name="ctx_state",
    )(ctx, sh, sc, g, wt, wt, wt, wupf, wupb, baf, bab)


def _inproj_kernel(x_ref, sh_ref, sc_ref, g_ref, w_ref, wlr_ref, z_ref, lr_ref, h_scr):
    j = pl.program_id(2)
    d = x_ref.shape[1]

    @pl.when(j == 0)
    def _():
        hb = _rms_mod(x_ref[...], g_ref[...], sh_ref[...], sc_ref[...]).astype(BF16)
        h_scr[...] = hb
        lr_ref[...] = _dot_nt(hb, wlr_ref[...])
        z_ref[...] = _dot_nt(hb, w_ref[...]).astype(BF16)

    @pl.when(j == 1)
    def _():
        a = _dot_nt(h_scr[...], w_ref[...])
        gv = a[:, :d]
        z_ref[:, :d] = (gv * _sigmoid(gv)).astype(BF16)
        z_ref[:, d:] = a[:, d:].astype(BF16)

    @pl.when(j == 2)
    def _():
        a = _dot_nt(h_scr[...], w_ref[...])
        z_ref[:, :d] = (a[:, d:] * a[:, :d]).astype(BF16)
        z_ref[:, d:] = jnp.zeros((z_ref.shape[0], d), BF16)

    @pl.when(j == 3)
    def _():
        z_ref[...] = _sigmoid(_dot_nt(h_scr[...], w_ref[...])).astype(BF16)


def _inproj(x, sh, sc, g, wt, tm):
    bsz, n, d = x.shape
    p = wt.shape[0] - LANES
    tn = 2 * d
    assert p == 4 * tn
    return pl.pallas_call(
        _inproj_kernel,
        grid=(bsz, n // tm, p // tn),
        in_specs=[pl.BlockSpec((None, tm, d), lambda b, i, j: (b, i, 0)),
                  pl.BlockSpec((None, 1, d), lambda b, i, j: (b, 0, 0)),
                  pl.BlockSpec((None, 1, d), lambda b, i, j: (b, 0, 0)),
                  pl.BlockSpec((1, d), lambda b, i, j: (0, 0)),
                  pl.BlockSpec((tn, d), lambda b, i, j: (j, 0)),
                  pl.BlockSpec((LANES, d), lambda b, i, j: (p // LANES, 0))],
        out_specs=(pl.BlockSpec((None, tm, tn), lambda b, i, j: (b, i, j)),
                   pl.BlockSpec((None, tm, LANES), lambda b, i, j: (b, i, 0))),
        out_shape=(jax.ShapeDtypeStruct((bsz, n, p), BF16),
                   jax.ShapeDtypeStruct((bsz, n, LANES), F32)),
        scratch_shapes=[pltpu.VMEM((tm, d), BF16)],
        compiler_params=_cparams("arbitrary", "arbitrary", "arbitrary"),
        name="inproj",
    )(x, sh, sc, g, wt, wt)


def _gla_kernel(heads, qf_ref, kf_ref, vf_ref, lrf_ref, qb_ref, kb_ref, vb_ref, lrb_ref,
                wupf_ref, wupb_ref, baf_ref, bab_ref, trif_ref, trib_ref, s0f_ref, s0b_ref,
                of_ref, ob_ref, sf_scr, sb_scr):
    tb, dk = qf_ref.shape
    dv = vf_ref.shape[1]
    hk, hv = dk // heads, dv // heads
    ck = GLA_CHUNK
    nck = tb // ck
    q_scale = hk ** -0.5

    @pl.when(pl.program_id(1) == 0)
    def _():
        sf_scr[...] = s0f_ref[...]
        sb_scr[...] = s0b_ref[...]

    crow = lax.broadcasted_iota(I32, (ck, ck), 0)
    ccol = lax.broadcasted_iota(I32, (ck, ck), 1)

    dirs = (
        (qf_ref, kf_ref, vf_ref, lrf_ref, wupf_ref, baf_ref, of_ref, sf_scr,
         trif_ref, ccol <= crow, ck - 1, range(nck)),
        (qb_ref, kb_ref, vb_ref, lrb_ref, wupb_ref, bab_ref, ob_ref, sb_scr,
         trib_ref, ccol >= crow, 0, range(nck - 1, -1, -1)),
    )
    b_alls = []
    for _, _, _, lr_ref, wup_ref, ba_ref, _, _, tri, _, _, _ in dirs:
        la = _log_sigmoid(_dot(lr_ref[...].astype(BF16), wup_ref[...]) + ba_ref[...])
        la = la * (1.0 / GLA_GATE_NORM)
        hi, lo = _split_bf16(la)
        t = tri[...]
        b_alls.append(_dot(t, hi) + _dot(t, lo))
    work = [[], []]

    def stage1(k):
        for di, (q_ref, k_ref, v_ref, _, _, _, _, _, _, cmask, last, order) in enumerate(dirs):
            r0 = order[k] * ck
            b = b_alls[di][r0:r0 + ck, :]
            b_last = b[last:last + 1, :]
            qc = q_ref[r0:r0 + ck, :].astype(F32) * q_scale
            kc = k_ref[r0:r0 + ck, :].astype(F32)
            q_in = (qc * jnp.exp(b)).astype(BF16)
            k_in = (kc * jnp.exp(-b)).astype(BF16)
            k_dec = (kc * jnp.exp(b_last - b)).astype(BF16)
            decay = jnp.exp(b_last)
            units = []
            for h in range(heads):
                qh = q_in[:, h * hk:(h + 1) * hk]
                vh = v_ref[r0:r0 + ck, h * hv:(h + 1) * hv]
                scores = jnp.where(cmask, _dot_nt(qh, k_in[:, h * hk:(h + 1) * hk]), 0.0)
                o_intra = _dot(scores.astype(BF16), vh)
                ut = _dot_tn(vh, k_dec[:, h * hk:(h + 1) * hk])
                units.append((qh, o_intra, ut, decay[:, h * hk:(h + 1) * hk]))
            work[di].append((r0, units))
    def scan(k):
        for di, (_, _, _, _, _, _, o_ref, s_scr, _, _, _, _) in enumerate(dirs):
            r0, units = work[di][k]
            for h, (qh, o_intra, ut, dec) in enumerate(units):
                st = s_scr[h]
                o = o_intra + _dot_nt(qh, st.astype(BF16))
                o_ref[r0:r0 + ck, h * hv:(h + 1) * hv] = o.astype(BF16)
                s_scr[h] = st * dec + ut

    for k in range(nck + GLA_SCAN_LAG):
        if k < nck:
            stage1(k)
        if k >= GLA_SCAN_LAG:
            scan(k - GLA_SCAN_LAG)


def _gla(z, lr, wupf, wupb, baf, bab, s0f, s0b, heads, dk, dv, tb):
    bsz, n, _ = z.shape
    nb = n // tb
    hk, hv = dk // heads, dv // heads
    fwd = lambda cb: (lambda b, i: (b, i, cb))
    bwd = lambda cb: (lambda b, i: (b, nb - 1 - i, cb))
    full = lambda *shape: pl.BlockSpec(shape, lambda b, i: (0,) * len(shape))
    sspec = pl.BlockSpec((None, heads, hv, hk), lambda b, i: (b, 0, 0, 0))
    vcb = 2 * dk // dv
    ost = jax.ShapeDtypeStruct((bsz, n, dv), BF16)
    ri = np.arange(tb)
    same_chunk = (ri[:, None] // GLA_CHUNK) == (ri[None, :] // GLA_CHUNK)
    trif = jnp.asarray(same_chunk & (ri[None, :] <= ri[:, None]), BF16)
    trib = jnp.asarray(same_chunk & (ri[None, :] >= ri[:, None]), BF16)
    return pl.pallas_call(
        functools.partial(_gla_kernel, heads),
        grid=(bsz, nb),
        in_specs=[pl.BlockSpec((None, tb, dk), fwd(0)), pl.BlockSpec((None, tb, dk), fwd(1)),
                  pl.BlockSpec((None, tb, dv), fwd(vcb)), pl.BlockSpec((None, tb, LANES), fwd(0)),
                  pl.BlockSpec((None, tb, dk), bwd(0)), pl.BlockSpec((None, tb, dk), bwd(1)),
                  pl.BlockSpec((None, tb, dv), bwd(vcb)), pl.BlockSpec((None, tb, LANES), bwd(0)),
                  full(LANES, dk), full(LANES, dk), full(1, dk), full(1, dk),
                  full(tb, tb), full(tb, tb), sspec, sspec],
        out_specs=(pl.BlockSpec((None, tb, dv), fwd(0)), pl.BlockSpec((None, tb, dv), bwd(0))),
        out_shape=(ost, ost),
        scratch_shapes=[pltpu.VMEM((heads, hv, hk), F32), pltpu.VMEM((heads, hv, hk), F32)],
        compiler_params=_cparams("arbitrary", "arbitrary"),
        name="gla",
    )(z, z, z, lr, z, z, z, lr, wupf, wupb, baf, bab, trif, trib, s0f, s0b)


def _mixer_kernel(heads, of_ref, ob_ref, sg_ref, bg_ref, cu_ref, srg_ref, src_ref, x_ref,
                  gng_ref, cw_ref, wo_ref, wco_ref, wm_ref, gt1_ref, sh2_ref, sc2_ref, n2g_ref,
                  rwt_ref, x1_ref, h2_ref, aff_ref):
    tm, dv = of_ref.shape
    hv = dv // heads
    o = of_ref[...].astype(F32) + ob_ref[...].astype(F32)
    parts = []
    for h in range(heads):
        oh = o[:, h * hv:(h + 1) * hv]
        ms = jnp.mean(oh * oh, axis=-1, keepdims=True)
        parts.append(oh * lax.rsqrt(ms + EPS))
    on = jnp.concatenate(parts, axis=1) * gng_ref[...]
    y_gla = _dot((on * sg_ref[...].astype(F32)).astype(BF16), wo_ref[...])

    cu = cu_ref[...].astype(F32)
    gcol = lax.broadcasted_iota(I32, cu.shape, 0) & (GRID_W - 1)
    left = jnp.where(gcol == 0, 0.0, pltpu.roll(cu, 1, 0))
    right = jnp.where(gcol == GRID_W - 1, 0.0, pltpu.roll(cu, tm - 1, 0))
    cw = cw_ref[...]
    conv = left * cw[0:1, :] + cu * cw[1:2, :] + right * cw[2:3, :]
    y_conv = _dot((bg_ref[...].astype(F32) * conv).astype(BF16), wco_ref[...])

    merged = srg_ref[...].astype(F32) * y_gla + src_ref[...].astype(F32) * y_conv
    y = _dot(merged.astype(BF16), wm_ref[...])
    x1 = x_ref[...] + gt1_ref[...] * y
    x1_ref[...] = x1
    h2 = _rms_mod(x1, n2g_ref[...], sh2_ref[...], sc2_ref[...])
    rpt = h2.shape[1] // LANES
    for a in range(rpt):
        h2_ref[pl.ds(a, tm, stride=rpt), :] = h2[:, a * LANES:(a + 1) * LANES]
    logits = _dot_nt(rwt_ref[...], h2.astype(BF16))
    ex = jnp.exp(logits - jnp.max(logits, axis=0, keepdims=True))
    aff_ref[...] = ex / jnp.sum(ex, axis=0, keepdims=True)


def _mixer(o_f, o_b, z, x, gng, conv_w, w_o, w_co, w_m, gt1, sh2, sc2, n2g, rwt, heads, dk, tm):
    bsz, n, d = x.shape
    dv = o_f.shape[2]
    c = conv_w.shape[1]
    e = rwt.shape[0]
    zb = lambda cb: pl.BlockSpec((None, tm, d), lambda b, i: (b, i, cb))
    base = (2 * dk + dv) // d
    tok = lambda w: pl.BlockSpec((None, tm, w), lambda b, i: (b, i, 0))
    full = lambda *shape: pl.BlockSpec(shape, lambda b, i: (0,) * len(shape))
    perb = pl.BlockSpec((None, 1, d), lambda b, i: (b, 0, 0))
    return pl.pallas_call(
        functools.partial(_mixer_kernel, heads),
        grid=(bsz, n // tm),
        in_specs=[tok(dv), tok(dv), zb(base), zb(base + 1), zb(base + 2),
                  zb(base + 4), zb(base + 5), tok(d),
                  full(1, dv), full(3, c), full(dv, d), full(c, d), full(d, d),
                  perb, perb, perb, full(1, d), full(e, d)],
        out_specs=(tok(d), pl.BlockSpec((None, tm * (d // LANES), LANES), lambda b, i: (b, i, 0)),
                   pl.BlockSpec((None, e, tm), lambda b, i: (b, 0, i))),
        out_shape=(jax.ShapeDtypeStruct((bsz, n, d), F32),
                   jax.ShapeDtypeStruct((bsz, n * (d // LANES), LANES), F32),
                   jax.ShapeDtypeStruct((bsz, e, n), F32)),
        compiler_params=_cparams("arbitrary", "arbitrary"),
        name="mixer",
    )(o_f, o_b, z, z, z, z, z, x, gng, conv_w, w_o, w_co, w_m, gt1, sh2, sc2, n2g, rwt)


def _route_kernel(cap, aff_ref, lmat_ref, pos_ref, base_ref, idx_ref, loc_scr, bef_scr):
    e, n = aff_ref.shape
    nt = n // LANES
    aff = aff_ref[...]

    def count(mask):
        return jnp.sum(jnp.where(mask, 1.0, 0.0), axis=1, keepdims=True)

    def search(k, tbits):
        cand = tbits | jnp.left_shift(jnp.int32(1), 30 - k)
        ok = count(aff >= lax.bitcast_convert_type(cand, F32)) >= cap
        return jnp.where(ok, cand, tbits)

    tbits = lax.fori_loop(0, 31, search, jnp.zeros((e, 1), I32))
    thr = lax.bitcast_convert_type(tbits, F32)
    gt = aff > thr
    eq = aff == thr
    need = cap - count(gt)

    def stack(mask):
        m = jnp.where(mask, 1.0, 0.0)
        return jnp.concatenate([m[:, j * LANES:(j + 1) * LANES] for j in range(nt)], axis=0)

    r = lax.broadcasted_iota(I32, (LANES, LANES), 0)
    cl = lax.broadcasted_iota(I32, (LANES, LANES), 1)
    upper = jnp.where(r <= cl, 1.0, 0.0).astype(BF16)
    ones = jnp.ones((LANES, LANES), BF16)
    lmat = lmat_ref[...]

    def cumsum(ms):
        msb = ms.astype(BF16)
        before = _dot(_dot(lmat, msb).astype(BF16), ones)
        return _dot(msb, upper), before

    eq_s = stack(eq)
    loc_eq, before_eq = cumsum(eq_s)
    need_s = jnp.concatenate([need] * nt, axis=0)
    sel = jnp.maximum(stack(gt), jnp.where(loc_eq + before_eq <= need_s, eq_s, 0.0))
    loc, before = cumsum(sel)
    pos_ref[...] = jnp.where(sel > 0.0, loc + before - 1.0, -1.0).astype(I32)
    base_ref[...] = before.astype(I32)

    loc_scr[...] = loc
    bef_scr[...] = before
    slot = lax.broadcasted_iota(I32, (1, cap), 1).astype(F32)
    for ex in range(e):
        loc_e = loc_scr[pl.ds(ex, nt, stride=e), :]
        tprev = bef_scr[pl.ds(ex, nt, stride=e), :][:, 0:1]
        tincl = tprev + loc_e[:, LANES - 1:LANES]
        in_tile = jnp.where((tprev <= slot) & (slot < tincl), 1.0, 0.0)
        tile = jnp.sum(jnp.where(tincl <= slot, 1.0, 0.0), axis=0, keepdims=True)
        s_loc = slot - jnp.sum(in_tile * tprev, axis=0, keepdims=True)
        counts = _dot_tn(loc_e.astype(BF16), in_tile.astype(BF16))
        lane = jnp.sum(jnp.where(counts <= s_loc, 1.0, 0.0), axis=0, keepdims=True)
        idx_ref[pl.ds(ex, 1), :] = (tile * LANES + lane).astype(I32)


def _route(aff_t, cap):
    bsz, e, n = aff_t.shape
    nt = n // LANES
    rows = nt * e
    ri = np.arange(rows)
    lmat = ((ri[None, :] % e == ri[:, None] % e) & (ri[None, :] // e < ri[:, None] // e))
    lmat = jnp.asarray(lmat, BF16)
    st = jax.ShapeDtypeStruct((bsz, rows, LANES), I32)
    ospec = pl.BlockSpec((None, rows, LANES), lambda b: (b, 0, 0))
    return pl.pallas_call(
        functools.partial(_route_kernel, cap),
        grid=(bsz,),
        in_specs=[pl.BlockSpec((None, e, n), lambda b: (b, 0, 0)),
                  pl.BlockSpec((rows, rows), lambda b: (0, 0))],
        out_specs=(ospec, ospec, pl.BlockSpec((None, e, cap), lambda b: (b, 0, 0))),
        out_shape=(st, st, jax.ShapeDtypeStruct((bsz, e, cap), I32)),
        scratch_shapes=[pltpu.VMEM((rows, LANES), F32), pltpu.VMEM((rows, LANES), F32)],
        compiler_params=_cparams("arbitrary"),
        name="route",
    )(aff_t, lmat)


def _slot_window(base):
    return pl.multiple_of((base // BF16_ROWS) * BF16_ROWS, BF16_ROWS)


def _expert_kernel(cap, idx_ref, h2_hbm, wg_ref, wu_ref, wd_ref, ye_ref, xbuf, acc, sem):
    e, f = pl.program_id(1), pl.program_id(2)
    n_exp, nf = pl.num_programs(1), pl.num_programs(2)
    last = pl.num_programs(0) * n_exp - 1
    lin = pl.program_id(0) * n_exp + e
    slot = lin % 2
    per_step = cap // nf
    rpt = wg_ref.shape[0] // LANES

    def row_copy(lin_t, part, i, slot_t):
        tok = idx_ref[(lin_t * nf + part) * per_step + i]
        return pltpu.make_async_copy(h2_hbm.at[pl.ds(pl.multiple_of(tok * rpt, rpt), rpt), :],
                                     xbuf.at[slot_t, part, pl.ds(i * rpt, rpt), :],
                                     sem.at[slot_t])

    def wait_rows(slot_t):
        pltpu.make_async_copy(xbuf.at[slot_t], xbuf.at[slot_t], sem.at[slot_t]).wait()

    @pl.when((lin == 0) & (f == 0))
    def _():
        for part in range(nf):
            def first(i, carry):
                row_copy(lin, part, i, slot).start()
                return carry
            lax.fori_loop(0, per_step, first, 0)

    @pl.when(f == 0)
    def _():
        wait_rows(slot)
        acc[...] = jnp.zeros_like(acc)

    nxt = jnp.minimum(lin + 1, last)
    for i in range(per_step):
        row_copy(nxt, f, i, 1 - slot).start()

    @pl.when((lin == last) & (f == nf - 1))
    def _():
        wait_rows(1 - slot)

    xv = jnp.concatenate(
        [jnp.concatenate([xbuf[slot, part, pl.ds(a, per_step, stride=rpt), :].astype(BF16)
                          for a in range(rpt)], axis=1) for part in range(nf)], axis=0)
    hg = _dot(xv, wg_ref[...].astype(BF16))
    hu = _dot(xv, wu_ref[...].astype(BF16))
    hid = (hg * _sigmoid(hg) * hu).astype(BF16)
    acc[...] += _dot(hid, wd_ref[...].astype(BF16))

    @pl.when(f == nf - 1)
    def _():
        ye_ref[0:cap, :] = acc[...].astype(BF16)
        ye_ref[cap:, :] = jnp.zeros((ye_ref.shape[0] - cap, ye_ref.shape[1]), BF16)


def _experts(h2, idx, w_gate, w_up, w_down, cap, rows, tf):
    n_exp, d, df = w_gate.shape
    bsz = h2.shape[0]
    rpt = d // LANES
    n = h2.shape[1] // rpt
    nf = df // tf
    assert cap % nf == 0
    rows_global = (idx + (jnp.arange(bsz, dtype=I32) * n)[:, None, None]).reshape(-1)
    return pl.pallas_call(
        functools.partial(_expert_kernel, cap),
        grid_spec=pltpu.PrefetchScalarGridSpec(
            num_scalar_prefetch=1,
            grid=(bsz, n_exp, nf),
            in_specs=[pl.BlockSpec(memory_space=pl.ANY),
                      pl.BlockSpec((None, d, tf), lambda b, e, f, s: (e, 0, f)),
                      pl.BlockSpec((None, d, tf), lambda b, e, f, s: (e, 0, f)),
                      pl.BlockSpec((None, tf, d), lambda b, e, f, s: (e, f, 0))],
            out_specs=pl.BlockSpec((None, None, rows, d), lambda b, e, f, s: (b, e, 0, 0)),
            scratch_shapes=[pltpu.VMEM((2, nf, (cap // nf) * rpt, LANES), F32),
                            pltpu.VMEM((cap, d), F32), pltpu.SemaphoreType.DMA((2,))]),
        out_shape=jax.ShapeDtypeStruct((bsz, n_exp, rows, d), BF16),
        compiler_params=_cparams("arbitrary", "arbitrary", "arbitrary"),
        name="experts",
    )(rows_global, h2.reshape(bsz * n * rpt, LANES), w_gate, w_up, w_down)


def _combine_windows(tm):
    return tm // 4 + BF16_ROWS, tm + BF16_ROWS


def _combine_kernel(n_exp, nsub, base_ref, ovf_ref, pos_ref, aff_ref, x1_ref, gt2_ref, fg_ref,
                    ye_hbm, out_ref, buf, fbuf, ffn_scr, sem, fsem):
    b, j = pl.program_id(0), pl.program_id(1)
    nt = pl.num_programs(1)
    step = b * nt + j
    slot = step % 2
    wp, wfull = buf.shape[2], fbuf.shape[1]
    d = buf.shape[-1]

    def start_of(tile, ex):
        return _slot_window(base_ref[tile * nsub * n_exp + ex])

    def window_copy(tile, ex, dst_slot):
        return pltpu.make_async_copy(ye_hbm.at[tile // nt, ex, pl.ds(start_of(tile, ex), wp), :],
                                     buf.at[dst_slot, ex], sem.at[dst_slot])

    def full_copy(ex):
        return pltpu.make_async_copy(ye_hbm.at[b, ex, pl.ds(start_of(step, ex), wfull), :],
                                     fbuf.at[ex], fsem)

    @pl.when(step == 0)
    def _():
        for ex in range(n_exp):
            window_copy(step, ex, slot).start()

    @pl.when(step + 1 < pl.num_programs(0) * nt)
    def _():
        for ex in range(n_exp):
            window_copy(step + 1, ex, 1 - slot).start()

    def gates_t(rows):
        srow = lax.broadcasted_iota(I32, (rows, LANES), 0)
        per_exp = []
        for ex in range(n_exp):
            s0 = start_of(step, ex)
            parts = []
            for u in range(nsub):
                rel = pos_ref[pl.ds(u * n_exp + ex, 1), :] - s0
                gate = aff_ref[pl.ds(ex, 1), u * LANES:(u + 1) * LANES]
                parts.append(jnp.where(srow == rel, gate, 0.0).astype(BF16))
            per_exp.append(jnp.concatenate(parts, axis=1))
        return jnp.concatenate(per_exp, axis=0)

    for ex in range(n_exp):
        window_copy(step, ex, slot).wait()

    @pl.when(ovf_ref[step] == 0)
    def _():
        ffn_scr[...] = _dot_tn(gates_t(wp), buf[slot].reshape(n_exp * wp, d))

    @pl.when(ovf_ref[step] != 0)
    def _():
        for ex in range(n_exp):
            full_copy(ex).start()
        for ex in range(n_exp):
            full_copy(ex).wait()
        ffn_scr[...] = _dot_tn(gates_t(wfull), fbuf[...].reshape(n_exp * wfull, d))

    x2 = x1_ref[...] + gt2_ref[...] * ffn_scr[...]
    ms = jnp.mean(x2 * x2, axis=-1, keepdims=True)
    out_ref[...] = x2 * lax.rsqrt(ms + EPS) * fg_ref[...]


def _combine(ye, pos, aff_t, before, x1, gt2, final_g, cap, tm):
    bsz, n, d = x1.shape
    n_exp = aff_t.shape[1]
    nsub = tm // LANES
    nt = n // tm
    wp, wfull = _combine_windows(tm)
    starts = before[:, :, 0].reshape(bsz, n // LANES, n_exp)
    tile_starts = starts[:, ::nsub]
    tile_ends = jnp.concatenate([tile_starts[:, 1:], jnp.full((bsz, 1, n_exp), cap, I32)], axis=1)
    overflow = (tile_starts % BF16_ROWS) + (tile_ends - tile_starts) > wp
    ovf = jnp.any(overflow, axis=-1).astype(I32).reshape(-1)
    return pl.pallas_call(
        functools.partial(_combine_kernel, n_exp, nsub),
        grid_spec=pltpu.PrefetchScalarGridSpec(
            num_scalar_prefetch=2,
            grid=(bsz, nt),
            in_specs=[pl.BlockSpec((None, nsub * n_exp, LANES), lambda b, j, s, o: (b, j, 0)),
                      pl.BlockSpec((None, n_exp, tm), lambda b, j, s, o: (b, 0, j)),
                      pl.BlockSpec((None, tm, d), lambda b, j, s, o: (b, j, 0)),
                      pl.BlockSpec((None, 1, d), lambda b, j, s, o: (b, 0, 0)),
                      pl.BlockSpec((1, d), lambda b, j, s, o: (0, 0)),
                      pl.BlockSpec(memory_space=pl.ANY)],
            out_specs=pl.BlockSpec((None, tm, d), lambda b, j, s, o: (b, j, 0)),
            scratch_shapes=[pltpu.VMEM((2, n_exp, wp, d), BF16),
                            pltpu.VMEM((n_exp, wfull, d), BF16),
                            pltpu.VMEM((tm, d), F32),
                            pltpu.SemaphoreType.DMA((2,)), pltpu.SemaphoreType.DMA(())]),
        out_shape=jax.ShapeDtypeStruct((bsz, n, d), F32),
        compiler_params=_cparams("arbitrary", "arbitrary"),
        name="combine",
    )(starts.reshape(-1), ovf, pos, aff_t, x1, gt2, final_g, ye)


def _pick(n, pref):
    t = min(n, pref)
    while n % t:
        t //= 2
    return t


def kernel(x, c, ctx, c_ctx, w_ada, b_ada, norm1_g, norm2_g, w_in, gla_w_a_up, gla_b_a,
           gla_norm_g, gla_w_o, conv_w, conv_w_out, merge_w_out, router_w,
           exp_w_gate, exp_w_up, exp_w_down, final_g):
    bsz, n, d = x.shape
    depth = w_ada.shape[0]
    assert depth == 1, "single-layer trunk"
    rank, dk = gla_w_a_up.shape[-2:]
    dv = gla_w_o.shape[1]
    hv = gla_norm_g.shape[-1]
    heads = dv // hv
    cch = conv_w.shape[-1]
    n_exp = router_w.shape[-1]
    cap = EC_CAPACITY * n // n_exp
    assert 2 * dk == d and dv == d and cch == d and 2 * rank <= LANES
    assert n % LANES == 0 and n % GRID_W == 0 and bsz + 1 <= 8 and n_exp % 8 == 0

    cond = jnp.zeros((8, d), F32).at[:bsz].set(c).at[bsz].set(c_ctx)
    mod = _adaln(cond, w_ada[0], b_ada[0]).reshape(8, N_MOD, d)
    sh1, sc1, gt1, sh2, sc2, gt2 = [mod[:bsz, k][:, None, :] for k in range(N_MOD)]
    csh1, csc1 = mod[bsz:bsz + 1, 0], mod[bsz:bsz + 1, 1]

    lr0 = 2 * dk + 2 * dv
    u0 = lr0 + 2 * rank
    groups = ((0, lr0), (u0 + cch, cch), (u0, cch), (u0 + 2 * cch, cch + 2 * d), (lr0, LANES))
    tiles = tuple((r0 + o, min(d, cnt - o)) for r0, cnt in groups for o in range(0, cnt, d))
    wt = _wprep(jnp.swapaxes(w_in[0], 0, 1), tiles)
    wup = gla_w_a_up[0].astype(BF16)
    wupf = jnp.zeros((LANES, dk), BF16).at[:rank].set(wup[0])
    wupb = jnp.zeros((LANES, dk), BF16).at[rank:2 * rank].set(wup[1])
    baf, bab = gla_b_a[0, 0:1], gla_b_a[0, 1:2]
    n1g, n2g = norm1_g[0:1], norm2_g[0:1]

    s0f, s0b = _ctx_states(ctx, csh1, csc1, n1g, wt, wupf, wupb, baf, bab, heads, dk, dv)
    z, lr = _inproj(x, sh1, sc1, n1g, wt, _pick(n, 1024))
    o_f, o_b = _gla(z, lr, wupf, wupb, baf, bab, s0f, s0b, heads, dk, dv, _pick(n, 512))
    x1, h2, aff_t = _mixer(
        o_f, o_b, z, x, jnp.tile(gla_norm_g[0:1], (1, heads)), conv_w[0],
        gla_w_o[0].astype(BF16), conv_w_out[0].astype(BF16), merge_w_out[0].astype(BF16),
        gt1, sh2, sc2, n2g, router_w[0].T.astype(BF16), heads, dk, _pick(n, 256))

    pos, before, idx = _route(aff_t, cap)
    tmc = _pick(n, 256)
    ye = _experts(h2, idx, exp_w_gate[0], exp_w_up[0], exp_w_down[0], cap,
                  cap + _combine_windows(tmc)[1], _pick(exp_w_gate.shape[-1], 512))
    return _combine(ye, pos, aff_t, before, x1, gt2, final_g.reshape(1, d), cap, tmc)
```

```python
import functools

import numpy as np
import jax
import jax.numpy as jnp
from jax import lax
from jax.experimental import pallas as pl
from jax.experimental.pallas import tpu as pltpu

F32 = jnp.float32
BF16 = jnp.bfloat16
I32 = jnp.int32

EPS = 1e-6
N_MOD = 6
GRID_W = 64
GLA_CHUNK = 64
GLA_GATE_NORM = 16.0
GLA_SCAN_LAG = 2
EC_CAPACITY = 2

LANES = 128
MXU_DEPTH = 256
BF16_ROWS = 16
VMEM_LIMIT = 56 * 1024 * 1024


def _cparams(*sem, **kw):
    return pltpu.CompilerParams(dimension_semantics=sem, vmem_limit_bytes=VMEM_LIMIT, **kw)


def _dot(a, b):
    return jnp.dot(a, b, preferred_element_type=F32)


def _dot_nt(a, b):
    return lax.dot_general(a, b, (((1,), (1,)), ((), ())), preferred_element_type=F32)


def _dot_tn(a, b):
    return lax.dot_general(a, b, (((0,), (0,)), ((), ())), preferred_element_type=F32)


def _sigmoid(v):
    return 1.0 / (1.0 + jnp.exp(-v))


def _log_sigmoid(v):
    return jnp.minimum(v, 0.0) - jnp.log(1.0 + jnp.exp(-jnp.abs(v)))


def _rms_mod(xv, g, shift, scale):
    ms = jnp.mean(xv * xv, axis=-1, keepdims=True)
    y = xv * lax.rsqrt(ms + EPS) * g
    return y * (1.0 + scale) + shift


def _split_bf16(v):
    hi = v.astype(BF16)
    lo = (v - hi.astype(F32)).astype(BF16)
    return hi, lo


def _adaln_kernel(c_ref, w_ref, b_ref, o_ref):
    cv = c_ref[...]
    s = (cv * _sigmoid(cv)).astype(BF16)
    o_ref[...] = _dot(s, w_ref[...].astype(BF16)) + b_ref[...]


def _adaln(cond, w, b):
    rows, d = cond.shape
    nout = w.shape[1]
    tn = d
    return pl.pallas_call(
        _adaln_kernel,
        grid=(nout // tn,),
        in_specs=[pl.BlockSpec((rows, d), lambda j: (0, 0)),
                  pl.BlockSpec((d, tn), lambda j: (0, j)),
                  pl.BlockSpec((1, tn), lambda j: (0, j))],
        out_specs=pl.BlockSpec((rows, tn), lambda j: (0, j)),
        out_shape=jax.ShapeDtypeStruct((rows, nout), F32),
        compiler_params=_cparams("arbitrary"),
        name="adaln",
    )(cond, w, b.reshape(1, nout))


def _wprep_kernel(tiles, wt_hbm, o_ref, buf, sem):
    def copy(t):
        row, cnt = tiles[t]
        return pltpu.make_async_copy(wt_hbm.at[pl.ds(row, cnt), :],
                                     buf.at[t % 2, pl.ds(0, cnt), :], sem.at[t % 2])

    copy(0).start()
    out_row = 0
    for t, (_, cnt) in enumerate(tiles):
        if t + 1 < len(tiles):
            copy(t + 1).start()
        copy(t).wait()
        o_ref[out_row:out_row + cnt, :] = buf[t % 2, 0:cnt, :].astype(BF16)
        out_row += cnt


def _wprep(wt, tiles):
    d = wt.shape[1]
    rows = sum(cnt for _, cnt in tiles)
    return pl.pallas_call(
        functools.partial(_wprep_kernel, tiles),
        in_specs=[pl.BlockSpec(memory_space=pl.ANY)],
        out_specs=pl.BlockSpec((rows, d), lambda: (0, 0)),
        out_shape=jax.ShapeDtypeStruct((rows, d), BF16),
        scratch_shapes=[pltpu.VMEM((2, max(cnt for _, cnt in tiles), d), F32),
                        pltpu.SemaphoreType.DMA((2,))],
        compiler_params=pltpu.CompilerParams(vmem_limit_bytes=VMEM_LIMIT),
        name="wprep",
    )(wt)


def _ctx_kernel(heads, ctx_ref, sh_ref, sc_ref, g_ref, wk_ref, wv_ref, wlr_ref,
                wupf_ref, wupb_ref, baf_ref, bab_ref, sf_ref, sb_ref):
    n = ctx_ref.shape[0]
    hc = _rms_mod(ctx_ref[...], g_ref[...], sh_ref[...], sc_ref[...]).astype(BF16)
    k = _dot_nt(hc, wk_ref[...])
    v = _dot_nt(hc, wv_ref[...]).astype(BF16)
    lr = _dot_nt(hc, wlr_ref[...]).astype(BF16)
    hk = k.shape[1] // heads
    hv = v.shape[1] // heads
    row = lax.broadcasted_iota(I32, (n, n), 0)
    col = lax.broadcasted_iota(I32, (n, n), 1)
    for wup_ref, ba_ref, s_ref, tri, last in (
            (wupf_ref, baf_ref, sf_ref, col <= row, n - 1),
            (wupb_ref, bab_ref, sb_ref, col >= row, 0)):
        la = _log_sigmoid(_dot(lr, wup_ref[...]) + ba_ref[...]) * (1.0 / GLA_GATE_NORM)
        hi, lo = _split_bf16(la)
        t = jnp.where(tri, 1.0, 0.0).astype(BF16)
        b = _dot(t, hi) + _dot(t, lo)
        kd = (k * jnp.exp(b[last:last + 1, :] - b)).astype(BF16)
        for h in range(heads):
            s_ref[h] = _dot_tn(v[:, h * hv:(h + 1) * hv], kd[:, h * hk:(h + 1) * hk])


def _ctx_states(ctx, sh, sc, g, wt, wupf, wupb, baf, bab, heads, dk, dv):
    bsz, n, d = ctx.shape
    lr_blk = (wt.shape[0] - LANES) // LANES
    hk, hv = dk // heads, dv // heads
    full = lambda *shape: pl.BlockSpec(shape, lambda b: (0,) * len(shape))
    st = jax.ShapeDtypeStruct((bsz, heads, hv, hk), F32)
    sspec = pl.BlockSpec((None, heads, hv, hk), lambda b: (b, 0, 0, 0))
    return pl.pallas_call(
        functools.partial(_ctx_kernel, heads),
        grid=(bsz,),
        in_specs=[pl.BlockSpec((None, n, d), lambda b: (b, 0, 0)),
                  full(1, d), full(1, d), full(1, d),
                  pl.BlockSpec((dk, d), lambda b: (1, 0)),
                  pl.BlockSpec((dv, d), lambda b: (2 * dk // dv, 0)),
                  pl.BlockSpec((LANES, d), lambda b: (lr_blk, 0)),
                  full(LANES, dk), full(LANES, dk), full(1, dk), full(1, dk)],
        out_specs=(sspec, sspec),
        out_shape=(st, st),
        compiler_params=_cparams("arbitrary"),
        name="ctx_state",
    )(ctx, sh, sc, g, wt, wt, wt, wupf, wupb, baf, bab)


def _inproj_kernel(x_ref, sh_ref, sc_ref, g_ref, w_ref, wlr_ref, z_ref, lr_ref, h_scr):
    j = pl.program_id(2)
    d = x_ref.shape[1]

    @pl.when(j == 0)
    def _():
        hb = _rms_mod(x_ref[...], g_ref[...], sh_ref[...], sc_ref[...]).astype(BF16)
        h_scr[...] = hb
        lr_ref[...] = _dot_nt(hb, wlr_ref[...])
        z_ref[...] = _dot_nt(hb, w_ref[...]).astype(BF16)

    @pl.when(j == 1)
    def _():
        a = _dot_nt(h_scr[...], w_ref[...])
        gv = a[:, :d]
        z_ref[:, :d] = (gv * _sigmoid(gv)).astype(BF16)
        z_ref[:, d:] = a[:, d:].astype(BF16)

    @pl.when(j == 2)
    def _():
        a = _dot_nt(h_scr[...], w_ref[...])
        z_ref[:, :d] = (a[:, d:] * a[:, :d]).astype(BF16)
        z_ref[:, d:] = jnp.zeros((z_ref.shape[0], d), BF16)

    @pl.when(j == 3)
    def _():
        z_ref[...] = _sigmoid(_dot_nt(h_scr[...], w_ref[...])).astype(BF16)


def _inproj(x, sh, sc, g, wt, tm):
    bsz, n, d = x.shape
    p = wt.shape[0] - LANES
    tn = 2 * d
    assert p == 4 * tn
    return pl.pallas_call(
        _inproj_kernel,
        grid=(bsz, n // tm, p // tn),
        in_specs=[pl.BlockSpec((None, tm, d), lambda b, i, j: (b, i, 0)),
                  pl.BlockSpec((None, 1, d), lambda b, i, j: (b, 0, 0)),
                  pl.BlockSpec((None, 1, d), lambda b, i, j: (b, 0, 0)),
                  pl.BlockSpec((1, d), lambda b, i, j: (0, 0)),
                  pl.BlockSpec((tn, d), lambda b, i, j: (j, 0)),
                  pl.BlockSpec((LANES, d), lambda b, i, j: (p // LANES, 0))],
        out_specs=(pl.BlockSpec((None, tm, tn), lambda b, i, j: (b, i, j)),
                   pl.BlockSpec((None, tm, LANES), lambda b, i, j: (b, i, 0))),
        out_shape=(jax.ShapeDtypeStruct((bsz, n, p), BF16),
                   jax.ShapeDtypeStruct((bsz, n, LANES), F32)),
        scratch_shapes=[pltpu.VMEM((tm, d), BF16)],
        compiler_params=_cparams("arbitrary", "arbitrary", "arbitrary"),
        name="inproj",
    )(x, sh, sc, g, wt, wt)


def _gla_kernel(heads, dk, qkvf_ref, lrf_ref, qkvb_ref, lrb_ref,
                wupf_ref, wupb_ref, baf_ref, bab_ref, trif_ref, trib_ref, s0f_ref, s0b_ref,
                of_ref, ob_ref, sf_scr, sb_scr):
    tb = qkvf_ref.shape[0]
    dv = qkvf_ref.shape[1] - 2 * dk
    hk, hv = dk // heads, dv // heads
    ck = GLA_CHUNK
    nck = tb // ck
    q_scale = hk ** -0.5

    @pl.when(pl.program_id(1) == 0)
    def _():
        sf_scr[...] = s0f_ref[...]
        sb_scr[...] = s0b_ref[...]

    crow = lax.broadcasted_iota(I32, (ck, ck), 0)
    ccol = lax.broadcasted_iota(I32, (ck, ck), 1)

    dirs = (
        (qkvf_ref, lrf_ref, wupf_ref, baf_ref, of_ref, sf_scr,
         trif_ref, ccol <= crow, ck - 1, range(nck)),
        (qkvb_ref, lrb_ref, wupb_ref, bab_ref, ob_ref, sb_scr,
         trib_ref, ccol >= crow, 0, range(nck - 1, -1, -1)),
    )
    b_alls = []
    for _, lr_ref, wup_ref, ba_ref, _, _, tri, _, _, _ in dirs:
        la = _log_sigmoid(_dot(lr_ref[...].astype(BF16), wup_ref[...]) + ba_ref[...])
        la = la * (1.0 / GLA_GATE_NORM)
        hi, lo = _split_bf16(la)
        t = tri[...]
        tr = t.shape[0]
        b_alls.append(jnp.concatenate(
            [_dot(t, hi[r:r + tr]) + _dot(t, lo[r:r + tr]) for r in range(0, tb, tr)], axis=0))
    work = [[], []]

    def stage1(k):
        for di, (qkv_ref, _, _, _, _, _, _, cmask, last, order) in enumerate(dirs):
            r0 = order[k] * ck
            b = b_alls[di][r0:r0 + ck, :]
            b_last = b[last:last + 1, :]
            qc = qkv_ref[r0:r0 + ck, 0:dk].astype(F32) * q_scale
            kc = qkv_ref[r0:r0 + ck, dk:2 * dk].astype(F32)
            q_in = (qc * jnp.exp(b)).astype(BF16)
            k_in = (kc * jnp.exp(-b)).astype(BF16)
            k_dec = (kc * jnp.exp(b_last - b)).astype(BF16)
            decay = jnp.exp(b_last)
            units = []
            for h in range(heads):
                qh = q_in[:, h * hk:(h + 1) * hk]
                vh = qkv_ref[r0:r0 + ck, 2 * dk + h * hv:2 * dk + (h + 1) * hv]
                scores = jnp.where(cmask, _dot_nt(qh, k_in[:, h * hk:(h + 1) * hk]), 0.0)
                o_intra = _dot(scores.astype(BF16), vh)
                ut = _dot_tn(vh, k_dec[:, h * hk:(h + 1) * hk])
                units.append((qh, o_intra, ut, decay[:, h * hk:(h + 1) * hk]))
            work[di].append((r0, units))
    def scan(k):
        for di, (_, _, _, _, o_ref, s_scr, _, _, _, _) in enumerate(dirs):
            r0, units = work[di][k]
            for h, (qh, o_intra, ut, dec) in enumerate(units):
                st = s_scr[h]
                o = o_intra + _dot_nt(qh, st.astype(BF16))
                o_ref[r0:r0 + ck, h * hv:(h + 1) * hv] = o.astype(BF16)
                s_scr[h] = st * dec + ut

    for k in range(nck + GLA_SCAN_LAG):
        if k < nck:
            stage1(k)
        if k >= GLA_SCAN_LAG:
            scan(k - GLA_SCAN_LAG)


def _gla(z, lr, wupf, wupb, baf, bab, s0f, s0b, heads, dk, dv, tb):
    bsz, n, _ = z.shape
    nb = n // tb
    hk, hv = dk // heads, dv // heads
    fwd = lambda cb: (lambda b, i: (b, i, cb))
    bwd = lambda cb: (lambda b, i: (b, nb - 1 - i, cb))
    full = lambda *shape: pl.BlockSpec(shape, lambda b, i: (0,) * len(shape))
    sspec = pl.BlockSpec((None, heads, hv, hk), lambda b, i: (b, 0, 0, 0))
    ost = jax.ShapeDtypeStruct((bsz, n, dv), BF16)
    tr = min(tb, MXU_DEPTH)
    assert tb % tr == 0 and tr % GLA_CHUNK == 0
    ri = np.arange(tr)
    same_chunk = (ri[:, None] // GLA_CHUNK) == (ri[None, :] // GLA_CHUNK)
    trif = jnp.asarray(same_chunk & (ri[None, :] <= ri[:, None]), BF16)
    trib = jnp.asarray(same_chunk & (ri[None, :] >= ri[:, None]), BF16)
    qkv = 2 * dk + dv
    return pl.pallas_call(
        functools.partial(_gla_kernel, heads, dk),
        grid=(bsz, nb),
        in_specs=[pl.BlockSpec((None, tb, qkv), fwd(0)), pl.BlockSpec((None, tb, LANES), fwd(0)),
                  pl.BlockSpec((None, tb, qkv), bwd(0)), pl.BlockSpec((None, tb, LANES), bwd(0)),
                  full(LANES, dk), full(LANES, dk), full(1, dk), full(1, dk),
                  full(tr, tr), full(tr, tr), sspec, sspec],
        out_specs=(pl.BlockSpec((None, tb, dv), fwd(0)), pl.BlockSpec((None, tb, dv), bwd(0))),
        out_shape=(ost, ost),
        scratch_shapes=[pltpu.VMEM((heads, hv, hk), F32), pltpu.VMEM((heads, hv, hk), F32)],
        compiler_params=_cparams("arbitrary", "arbitrary"),
        name="gla",
    )(z, lr, z, lr, wupf, wupb, baf, bab, trif, trib, s0f, s0b)


def _mixer_kernel(heads, of_ref, ob_ref, sg_ref, bg_ref, cu_ref, srg_ref, src_ref, x_ref,
                  gng_ref, cw_ref, wo_ref, wco_ref, wm_ref, gt1_ref, sh2_ref, sc2_ref, n2g_ref,
                  rwt_ref, x1_ref, h2_ref, aff_ref):
    tm, dv = of_ref.shape
    hv = dv // heads
    o = of_ref[...].astype(F32) + ob_ref[...].astype(F32)
    parts = []
    for h in range(heads):
        oh = o[:, h * hv:(h + 1) * hv]
        ms = jnp.mean(oh * oh, axis=-1, keepdims=True)
        parts.append(oh * lax.rsqrt(ms + EPS))
    on = jnp.concatenate(parts, axis=1) * gng_ref[...]
    y_gla = _dot((on * sg_ref[...].astype(F32)).astype(BF16), wo_ref[...])

    cu = cu_ref[...].astype(F32)
    gcol = lax.broadcasted_iota(I32, cu.shape, 0) & (GRID_W - 1)
    left = jnp.where(gcol == 0, 0.0, pltpu.roll(cu, 1, 0))
    right = jnp.where(gcol == GRID_W - 1, 0.0, pltpu.roll(cu, tm - 1, 0))
    cw = cw_ref[...]
    conv = left * cw[0:1, :] + cu * cw[1:2, :] + right * cw[2:3, :]
    y_conv = _dot((bg_ref[...].astype(F32) * conv).astype(BF16), wco_ref[...])

    merged = srg_ref[...].astype(F32) * y_gla + src_ref[...].astype(F32) * y_conv
    y = _dot(merged.astype(BF16), wm_ref[...])
    x1 = x_ref[...] + gt1_ref[...] * y
    x1_ref[...] = x1
    h2 = _rms_mod(x1, n2g_ref[...], sh2_ref[...], sc2_ref[...])
    rpt = h2.shape[1] // LANES
    for a in range(rpt):
        h2_ref[pl.ds(a, tm, stride=rpt), :] = h2[:, a * LANES:(a + 1) * LANES]
    logits = _dot_nt(rwt_ref[...], h2.astype(BF16))
    ex = jnp.exp(logits - jnp.max(logits, axis=0, keepdims=True))
    aff_ref[...] = ex / jnp.sum(ex, axis=0, keepdims=True)


def _mixer(o_f, o_b, z, x, gng, conv_w, w_o, w_co, w_m, gt1, sh2, sc2, n2g, rwt, heads, dk, tm):
    bsz, n, d = x.shape
    dv = o_f.shape[2]
    c = conv_w.shape[1]
    e = rwt.shape[0]
    zb = lambda cb: pl.BlockSpec((None, tm, d), lambda b, i: (b, i, cb))
    base = (2 * dk + dv) // d
    tok = lambda w: pl.BlockSpec((None, tm, w), lambda b, i: (b, i, 0))
    full = lambda *shape: pl.BlockSpec(shape, lambda b, i: (0,) * len(shape))
    perb = pl.BlockSpec((None, 1, d), lambda b, i: (b, 0, 0))
    return pl.pallas_call(
        functools.partial(_mixer_kernel, heads),
        grid=(bsz, n // tm),
        in_specs=[tok(dv), tok(dv), zb(base), zb(base + 1), zb(base + 2),
                  zb(base + 4), zb(base + 5), tok(d),
                  full(1, dv), full(3, c), full(dv, d), full(c, d), full(d, d),
                  perb, perb, perb, full(1, d), full(e, d)],
        out_specs=(tok(d), pl.BlockSpec((None, tm * (d // LANES), LANES), lambda b, i: (b, i, 0)),
                   pl.BlockSpec((None, e, tm), lambda b, i: (b, 0, i))),
        out_shape=(jax.ShapeDtypeStruct((bsz, n, d), F32),
                   jax.ShapeDtypeStruct((bsz, n * (d // LANES), LANES), F32),
                   jax.ShapeDtypeStruct((bsz, e, n), F32)),
        compiler_params=_cparams("arbitrary", "arbitrary"),
        name="mixer",
    )(o_f, o_b, z, z, z, z, z, x, gng, conv_w, w_o, w_co, w_m, gt1, sh2, sc2, n2g, rwt)


def _route_kernel(cap, aff_ref, lmat_ref, pos_ref, base_ref, idx_ref, loc_scr, bef_scr):
    e, n = aff_ref.shape
    nt = n // LANES
    aff = aff_ref[...]

    def count(mask):
        return jnp.sum(jnp.where(mask, 1.0, 0.0), axis=1, keepdims=True)

    def search(k, tbits):
        cand = tbits | jnp.left_shift(jnp.int32(1), 30 - k)
        ok = count(aff >= lax.bitcast_convert_type(cand, F32)) >= cap
        return jnp.where(ok, cand, tbits)

    tbits = lax.fori_loop(0, 31, search, jnp.zeros((e, 1), I32))
    thr = lax.bitcast_convert_type(tbits, F32)
    gt = aff > thr
    eq = aff == thr
    need = cap - count(gt)

    def stack(mask):
        m = jnp.where(mask, 1.0, 0.0)
        return jnp.concatenate([m[:, j * LANES:(j + 1) * LANES] for j in range(nt)], axis=0)

    r = lax.broadcasted_iota(I32, (LANES, LANES), 0)
    cl = lax.broadcasted_iota(I32, (LANES, LANES), 1)
    upper = jnp.where(r <= cl, 1.0, 0.0).astype(BF16)
    ones = jnp.ones((LANES, LANES), BF16)
    lmat = lmat_ref[...]

    def cumsum(ms):
        msb = ms.astype(BF16)
        before = _dot(_dot(lmat, msb).astype(BF16), ones)
        return _dot(msb, upper), before

    eq_s = stack(eq)
    loc_eq, before_eq = cumsum(eq_s)
    need_s = jnp.concatenate([need] * nt, axis=0)
    sel = jnp.maximum(stack(gt), jnp.where(loc_eq + before_eq <= need_s, eq_s, 0.0))
    loc, before = cumsum(sel)
    pos_ref[...] = jnp.where(sel > 0.0, loc + before - 1.0, -1.0).astype(I32)
    base_ref[...] = before.astype(I32)

    loc_scr[...] = loc
    bef_scr[...] = before
    slot = lax.broadcasted_iota(I32, (1, cap), 1).astype(F32)
    for ex in range(e):
        loc_e = loc_scr[pl.ds(ex, nt, stride=e), :]
        tprev = bef_scr[pl.ds(ex, nt, stride=e), :][:, 0:1]
        tincl = tprev + loc_e[:, LANES - 1:LANES]
        in_tile = jnp.where((tprev <= slot) & (slot < tincl), 1.0, 0.0)
        tile = jnp.sum(jnp.where(tincl <= slot, 1.0, 0.0), axis=0, keepdims=True)
        s_loc = slot - jnp.sum(in_tile * tprev, axis=0, keepdims=True)
        counts = _dot_tn(loc_e.astype(BF16), in_tile.astype(BF16))
        lane = jnp.sum(jnp.where(counts <= s_loc, 1.0, 0.0), axis=0, keepdims=True)
        idx_ref[pl.ds(ex, 1), :] = (tile * LANES + lane).astype(I32)


def _route(aff_t, cap):
    bsz, e, n = aff_t.shape
    nt = n // LANES
    rows = nt * e
    ri = np.arange(rows)
    lmat = ((ri[None, :] % e == ri[:, None] % e) & (ri[None, :] // e < ri[:, None] // e))
    lmat = jnp.asarray(lmat, BF16)
    st = jax.ShapeDtypeStruct((bsz, rows, LANES), I32)
    ospec = pl.BlockSpec((None, rows, LANES), lambda b: (b, 0, 0))
    return pl.pallas_call(
        functools.partial(_route_kernel, cap),
        grid=(bsz,),
        in_specs=[pl.BlockSpec((None, e, n), lambda b: (b, 0, 0)),
                  pl.BlockSpec((rows, rows), lambda b: (0, 0))],
        out_specs=(ospec, ospec, pl.BlockSpec((None, e, cap), lambda b: (b, 0, 0))),
        out_shape=(st, st, jax.ShapeDtypeStruct((bsz, e, cap), I32)),
        scratch_shapes=[pltpu.VMEM((rows, LANES), F32), pltpu.VMEM((rows, LANES), F32)],
        compiler_params=_cparams("arbitrary"),
        name="route",
    )(aff_t, lmat)


def _slot_window(base):
    return pl.multiple_of((base // BF16_ROWS) * BF16_ROWS, BF16_ROWS)


def _expert_kernel(cap, idx_ref, h2_hbm, wg_ref, wu_ref, wd_ref, ye_ref, xbuf, acc, sem):
    e, f = pl.program_id(1), pl.program_id(2)
    n_exp, nf = pl.num_programs(1), pl.num_programs(2)
    last = pl.num_programs(0) * n_exp - 1
    lin = pl.program_id(0) * n_exp + e
    slot = lin % 2
    per_step = cap // nf
    rpt = wg_ref.shape[0] // LANES

    def row_copy(lin_t, part, i, slot_t):
        tok = idx_ref[(lin_t * nf + part) * per_step + i]
        return pltpu.make_async_copy(h2_hbm.at[pl.ds(pl.multiple_of(tok * rpt, rpt), rpt), :],
                                     xbuf.at[slot_t, part, pl.ds(i * rpt, rpt), :],
                                     sem.at[slot_t])

    def wait_rows(slot_t):
        pltpu.make_async_copy(xbuf.at[slot_t], xbuf.at[slot_t], sem.at[slot_t]).wait()

    @pl.when((lin == 0) & (f == 0))
    def _():
        for part in range(nf):
            def first(i, carry):
                row_copy(lin, part, i, slot).start()
                return carry
            lax.fori_loop(0, per_step, first, 0)

    @pl.when(f == 0)
    def _():
        wait_rows(slot)
        acc[...] = jnp.zeros_like(acc)

    nxt = jnp.minimum(lin + 1, last)
    for i in range(per_step):
        row_copy(nxt, f, i, 1 - slot).start()

    @pl.when((lin == last) & (f == nf - 1))
    def _():
        wait_rows(1 - slot)

    xv = jnp.concatenate(
        [jnp.concatenate([xbuf[slot, part, pl.ds(a, per_step, stride=rpt), :].astype(BF16)
                          for a in range(rpt)], axis=1) for part in range(nf)], axis=0)
    hg = _dot(xv, wg_ref[...].astype(BF16))
    hu = _dot(xv, wu_ref[...].astype(BF16))
    hid = (hg * _sigmoid(hg) * hu).astype(BF16)
    acc[...] += _dot(hid, wd_ref[...].astype(BF16))

    @pl.when(f == nf - 1)
    def _():
        ye_ref[0:cap, :] = acc[...].astype(BF16)
        ye_ref[cap:, :] = jnp.zeros((ye_ref.shape[0] - cap, ye_ref.shape[1]), BF16)


def _experts(h2, idx, w_gate, w_up, w_down, cap, rows, tf):
    n_exp, d, df = w_gate.shape
    bsz = h2.shape[0]
    rpt = d // LANES
    n = h2.shape[1] // rpt
    nf = df // tf
    assert cap % nf == 0
    rows_global = (idx + (jnp.arange(bsz, dtype=I32) * n)[:, None, None]).reshape(-1)
    return pl.pallas_call(
        functools.partial(_expert_kernel, cap),
        grid_spec=pltpu.PrefetchScalarGridSpec(
            num_scalar_prefetch=1,
            grid=(bsz, n_exp, nf),
            in_specs=[pl.BlockSpec(memory_space=pl.ANY),
                      pl.BlockSpec((None, d, tf), lambda b, e, f, s: (e, 0, f)),
                      pl.BlockSpec((None, d, tf), lambda b, e, f, s: (e, 0, f)),
                      pl.BlockSpec((None, tf, d), lambda b, e, f, s: (e, f, 0))],
            out_specs=pl.BlockSpec((None, None, rows, d), lambda b, e, f, s: (b, e, 0, 0)),
            scratch_shapes=[pltpu.VMEM((2, nf, (cap // nf) * rpt, LANES), F32),
                            pltpu.VMEM((cap, d), F32), pltpu.SemaphoreType.DMA((2,))]),
        out_shape=jax.ShapeDtypeStruct((bsz, n_exp, rows, d), BF16),
        compiler_params=_cparams("arbitrary", "arbitrary", "arbitrary"),
        name="experts",
    )(rows_global, h2.reshape(bsz * n * rpt, LANES), w_gate, w_up, w_down)


def _combine_windows(tm):
    return tm // 4 + BF16_ROWS, tm + BF16_ROWS


def _combine_kernel(n_exp, nsub, base_ref, ovf_ref, pos_ref, aff_ref, x1_ref, gt2_ref, fg_ref,
                    ye_hbm, out_ref, buf, fbuf, ffn_scr, sem, fsem):
    b, j = pl.program_id(0), pl.program_id(1)
    nt = pl.num_programs(1)
    step = b * nt + j
    slot = step % 2
    wp, wfull = buf.shape[2], fbuf.shape[1]
    d = buf.shape[-1]

    def start_of(tile, ex):
        return _slot_window(base_ref[tile * nsub * n_exp + ex])

    def window_copy(tile, ex, dst_slot):
        return pltpu.make_async_copy(ye_hbm.at[tile // nt, ex, pl.ds(start_of(tile, ex), wp), :],
                                     buf.at[dst_slot, ex], sem.at[dst_slot])

    def full_copy(ex):
        return pltpu.make_async_copy(ye_hbm.at[b, ex, pl.ds(start_of(step, ex), wfull), :],
                                     fbuf.at[ex], fsem)

    @pl.when(step == 0)
    def _():
        for ex in range(n_exp):
            window_copy(step, ex, slot).start()

    @pl.when(step + 1 < pl.num_programs(0) * nt)
    def _():
        for ex in range(n_exp):
            window_copy(step + 1, ex, 1 - slot).start()

    def gates_t(rows):
        srow = lax.broadcasted_iota(I32, (rows, LANES), 0)
        per_exp = []
        for ex in range(n_exp):
            s0 = start_of(step, ex)
            parts = []
            for u in range(nsub):
                rel = pos_ref[pl.ds(u * n_exp + ex, 1), :] - s0
                gate = aff_ref[pl.ds(ex, 1), u * LANES:(u + 1) * LANES]
                parts.append(jnp.where(srow == rel, gate, 0.0).astype(BF16))
            per_exp.append(jnp.concatenate(parts, axis=1))
        return jnp.concatenate(per_exp, axis=0)

    for ex in range(n_exp):
        window_copy(step, ex, slot).wait()

    @pl.when(ovf_ref[step] == 0)
    def _():
        ffn_scr[...] = _dot_tn(gates_t(wp), buf[slot].reshape(n_exp * wp, d))

    @pl.when(ovf_ref[step] != 0)
    def _():
        for ex in range(n_exp):
            full_copy(ex).start()
        for ex in range(n_exp):
            full_copy(ex).wait()
        ffn_scr[...] = _dot_tn(gates_t(wfull), fbuf[...].reshape(n_exp * wfull, d))

    x2 = x1_ref[...] + gt2_ref[...] * ffn_scr[...]
    ms = jnp.mean(x2 * x2, axis=-1, keepdims=True)
    out_ref[...] = x2 * lax.rsqrt(ms + EPS) * fg_ref[...]


def _combine(ye, pos, aff_t, before, x1, gt2, final_g, cap, tm):
    bsz, n, d = x1.shape
    n_exp = aff_t.shape[1]
    nsub = tm // LANES
    nt = n // tm
    wp, wfull = _combine_windows(tm)
    starts = before[:, :, 0].reshape(bsz, n // LANES, n_exp)
    tile_starts = starts[:, ::nsub]
    tile_ends = jnp.concatenate([tile_starts[:, 1:], jnp.full((bsz, 1, n_exp), cap, I32)], axis=1)
    overflow = (tile_starts % BF16_ROWS) + (tile_ends - tile_starts) > wp
    ovf = jnp.any(overflow, axis=-1).astype(I32).reshape(-1)
    return pl.pallas_call(
        functools.partial(_combine_kernel, n_exp, nsub),
        grid_spec=pltpu.PrefetchScalarGridSpec(
            num_scalar_prefetch=2,
            grid=(bsz, nt),
            in_specs=[pl.BlockSpec((None, nsub * n_exp, LANES), lambda b, j, s, o: (b, j, 0)),
                      pl.BlockSpec((None, n_exp, tm), lambda b, j, s, o: (b, 0, j)),
                      pl.BlockSpec((None, tm, d), lambda b, j, s, o: (b, j, 0)),
                      pl.BlockSpec((None, 1, d), lambda b, j, s, o: (b, 0, 0)),
                      pl.BlockSpec((1, d), lambda b, j, s, o: (0, 0)),
                      pl.BlockSpec(memory_space=pl.ANY)],
            out_specs=pl.BlockSpec((None, tm, d), lambda b, j, s, o: (b, j, 0)),
            scratch_shapes=[pltpu.VMEM((2, n_exp, wp, d), BF16),
                            pltpu.VMEM((n_exp, wfull, d), BF16),
                            pltpu.VMEM((tm, d), F32),
                            pltpu.SemaphoreType.DMA((2,)), pltpu.SemaphoreType.DMA(())]),
        out_shape=jax.ShapeDtypeStruct((bsz, n, d), F32),
        compiler_params=_cparams("arbitrary", "arbitrary"),
        name="combine",
    )(starts.reshape(-1), ovf, pos, aff_t, x1, gt2, final_g, ye)


def _pick(n, pref):
    t = min(n, pref)
    while n % t:
        t //= 2
    return t


def kernel(x, c, ctx, c_ctx, w_ada, b_ada, norm1_g, norm2_g, w_in, gla_w_a_up, gla_b_a,
           gla_norm_g, gla_w_o, conv_w, conv_w_out, merge_w_out, router_w,
           exp_w_gate, exp_w_up, exp_w_down, final_g):
    bsz, n, d = x.shape
    depth = w_ada.shape[0]
    assert depth == 1, "single-layer trunk"
    rank, dk = gla_w_a_up.shape[-2:]
    dv = gla_w_o.shape[1]
    hv = gla_norm_g.shape[-1]
    heads = dv // hv
    cch = conv_w.shape[-1]
    n_exp = router_w.shape[-1]
    cap = EC_CAPACITY * n // n_exp
    assert 2 * dk == d and dv == d and cch == d and 2 * rank <= LANES
    assert n % LANES == 0 and n % GRID_W == 0 and bsz + 1 <= 8 and n_exp % 8 == 0

    cond = jnp.zeros((8, d), F32).at[:bsz].set(c).at[bsz].set(c_ctx)
    mod = _adaln(cond, w_ada[0], b_ada[0]).reshape(8, N_MOD, d)
    sh1, sc1, gt1, sh2, sc2, gt2 = [mod[:bsz, k][:, None, :] for k in range(N_MOD)]
    csh1, csc1 = mod[bsz:bsz + 1, 0], mod[bsz:bsz + 1, 1]

    lr0 = 2 * dk + 2 * dv
    u0 = lr0 + 2 * rank
    groups = ((0, lr0), (u0 + cch, cch), (u0, cch), (u0 + 2 * cch, cch + 2 * d), (lr0, LANES))
    tiles = tuple((r0 + o, min(d, cnt - o)) for r0, cnt in groups for o in range(0, cnt, d))
    wt = _wprep(jnp.swapaxes(w_in[0], 0, 1), tiles)
    wup = gla_w_a_up[0].astype(BF16)
    wupf = jnp.zeros((LANES, dk), BF16).at[:rank].set(wup[0])
    wupb = jnp.zeros((LANES, dk), BF16).at[rank:2 * rank].set(wup[1])
    baf, bab = gla_b_a[0, 0:1], gla_b_a[0, 1:2]
    n1g, n2g = norm1_g[0:1], norm2_g[0:1]

    s0f, s0b = _ctx_states(ctx, csh1, csc1, n1g, wt, wupf, wupb, baf, bab, heads, dk, dv)
    z, lr = _inproj(x, sh1, sc1, n1g, wt, _pick(n, 1024))
    o_f, o_b = _gla(z, lr, wupf, wupb, baf, bab, s0f, s0b, heads, dk, dv, _pick(n, 512))
    x1, h2, aff_t = _mixer(
        o_f, o_b, z, x, jnp.tile(gla_norm_g[0:1], (1, heads)), conv_w[0],
        gla_w_o[0].astype(BF16), conv_w_out[0].astype(BF16), merge_w_out[0].astype(BF16),
        gt1, sh2, sc2, n2g, router_w[0].T.astype(BF16), heads, dk, _pick(n, 512))

    pos, before, idx = _route(aff_t, cap)
    tmc = _pick(n, 256)
    ye = _experts(h2, idx, exp_w_gate[0], exp_w_up[0], exp_w_down[0], cap,
                  cap + _combine_windows(tmc)[1], _pick(exp_w_gate.shape[-1], 512))
    return _combine(ye, pos, aff_t, before, x1, gt2, final_g.reshape(1, d), cap, tmc)
```

```python
import functools

import numpy as np
import jax
import jax.numpy as jnp
from jax import lax
from jax.experimental import pallas as pl
from jax.experimental.pallas import tpu as pltpu

F32 = jnp.float32
BF16 = jnp.bfloat16
I32 = jnp.int32

EPS = 1e-6
N_MOD = 6
GRID_W = 64
GLA_CHUNK = 64
GLA_GATE_NORM = 16.0
GLA_SCAN_LAG = 2
EC_CAPACITY = 2

LANES = 128
MXU_DEPTH = 256
BF16_ROWS = 16
VMEM_LIMIT = 56 * 1024 * 1024


def _cparams(*sem, **kw):
    return pltpu.CompilerParams(dimension_semantics=sem, vmem_limit_bytes=VMEM_LIMIT, **kw)


def _dot(a, b):
    return jnp.dot(a, b, preferred_element_type=F32)


def _dot_nt(a, b):
    return lax.dot_general(a, b, (((1,), (1,)), ((), ())), preferred_element_type=F32)


def _dot_tn(a, b):
    return lax.dot_general(a, b, (((0,), (0,)), ((), ())), preferred_element_type=F32)


def _sigmoid(v):
    return 1.0 / (1.0 + jnp.exp(-v))


def _log_sigmoid(v):
    return jnp.minimum(v, 0.0) - jnp.log(1.0 + jnp.exp(-jnp.abs(v)))


def _rms_mod(xv, g, shift, scale):
    ms = jnp.mean(xv * xv, axis=-1, keepdims=True)
    y = xv * lax.rsqrt(ms + EPS) * g
    return y * (1.0 + scale) + shift


def _split_bf16(v):
    hi = v.astype(BF16)
    lo = (v - hi.astype(F32)).astype(BF16)
    return hi, lo


def _adaln_kernel(c_ref, w_ref, b_ref, o_ref):
    cv = c_ref[...]
    s = (cv * _sigmoid(cv)).astype(BF16)
    o_ref[...] = _dot(s, w_ref[...].astype(BF16)) + b_ref[...]


def _adaln(cond, w, b):
    rows, d = cond.shape
    nout = w.shape[1]
    tn = d
    return pl.pallas_call(
        _adaln_kernel,
        grid=(nout // tn,),
        in_specs=[pl.BlockSpec((rows, d), lambda j: (0, 0)),
                  pl.BlockSpec((d, tn), lambda j: (0, j)),
                  pl.BlockSpec((1, tn), lambda j: (0, j))],
        out_specs=pl.BlockSpec((rows, tn), lambda j: (0, j)),
        out_shape=jax.ShapeDtypeStruct((rows, nout), F32),
        compiler_params=_cparams("arbitrary"),
        name="adaln",
    )(cond, w, b.reshape(1, nout))


def _wprep_kernel(tiles, wt_hbm, o_ref, buf, sem):
    def copy(t):
        row, cnt = tiles[t]
        return pltpu.make_async_copy(wt_hbm.at[pl.ds(row, cnt), :],
                                     buf.at[t % 2, pl.ds(0, cnt), :], sem.at[t % 2])

    copy(0).start()
    out_row = 0
    for t, (_, cnt) in enumerate(tiles):
        if t + 1 < len(tiles):
            copy(t + 1).start()
        copy(t).wait()
        o_ref[out_row:out_row + cnt, :] = buf[t % 2, 0:cnt, :].astype(BF16)
        out_row += cnt


def _wprep(wt, tiles):
    d = wt.shape[1]
    rows = sum(cnt for _, cnt in tiles)
    return pl.pallas_call(
        functools.partial(_wprep_kernel, tiles),
        in_specs=[pl.BlockSpec(memory_space=pl.ANY)],
        out_specs=pl.BlockSpec((rows, d), lambda: (0, 0)),
        out_shape=jax.ShapeDtypeStruct((rows, d), BF16),
        scratch_shapes=[pltpu.VMEM((2, max(cnt for _, cnt in tiles), d), F32),
                        pltpu.SemaphoreType.DMA((2,))],
        compiler_params=pltpu.CompilerParams(vmem_limit_bytes=VMEM_LIMIT),
        name="wprep",
    )(wt)


def _ctx_kernel(heads, ctx_ref, sh_ref, sc_ref, g_ref, wk_ref, wv_ref, wlr_ref,
                wupf_ref, wupb_ref, baf_ref, bab_ref, sf_ref, sb_ref):
    n = ctx_ref.shape[0]
    hc = _rms_mod(ctx_ref[...], g_ref[...], sh_ref[...], sc_ref[...]).astype(BF16)
    k = _dot_nt(hc, wk_ref[...])
    v = _dot_nt(hc, wv_ref[...]).astype(BF16)
    lr = _dot_nt(hc, wlr_ref[...]).astype(BF16)
    hk = k.shape[1] // heads
    hv = v.shape[1] // heads
    row = lax.broadcasted_iota(I32, (n, n), 0)
    col = lax.broadcasted_iota(I32, (n, n), 1)
    for wup_ref, ba_ref, s_ref, tri, last in (
            (wupf_ref, baf_ref, sf_ref, col <= row, n - 1),
            (wupb_ref, bab_ref, sb_ref, col >= row, 0)):
        la = _log_sigmoid(_dot(lr, wup_ref[...]) + ba_ref[...]) * (1.0 / GLA_GATE_NORM)
        hi, lo = _split_bf16(la)
        t = jnp.where(tri, 1.0, 0.0).astype(BF16)
        b = _dot(t, hi) + _dot(t, lo)
        kd = (k * jnp.exp(b[last:last + 1, :] - b)).astype(BF16)
        for h in range(heads):
            s_ref[h] = _dot_tn(v[:, h * hv:(h + 1) * hv], kd[:, h * hk:(h + 1) * hk])


def _ctx_states(ctx, sh, sc, g, wt, wupf, wupb, baf, bab, heads, dk, dv):
    bsz, n, d = ctx.shape
    lr_blk = (wt.shape[0] - LANES) // LANES
    hk, hv = dk // heads, dv // heads
    full = lambda *shape: pl.BlockSpec(shape, lambda b: (0,) * len(shape))
    st = jax.ShapeDtypeStruct((bsz, heads, hv, hk), F32)
    sspec = pl.BlockSpec((None, heads, hv, hk), lambda b: (b, 0, 0, 0))
    return pl.pallas_call(
        functools.partial(_ctx_kernel, heads),
        grid=(bsz,),
        in_specs=[pl.BlockSpec((None, n, d), lambda b: (b, 0, 0)),
                  full(1, d), full(1, d), full(1, d),
                  pl.BlockSpec((dk, d), lambda b: (1, 0)),
                  pl.BlockSpec((dv, d), lambda b: (2 * dk // dv, 0)),
                  pl.BlockSpec((LANES, d), lambda b: (lr_blk, 0)),
                  full(LANES, dk), full(LANES, dk), full(1, dk), full(1, dk)],
        out_specs=(sspec, sspec),
        out_shape=(st, st),
        compiler_params=_cparams("arbitrary"),
        name="ctx_state",
    )(ctx, sh, sc, g, wt, wt, wt, wupf, wupb, baf, bab)


def _inproj_kernel(x_ref, sh_ref, sc_ref, g_ref, w_ref, wlr_ref, z_ref, lr_ref, h_scr):
    j = pl.program_id(2)
    d = x_ref.shape[1]

    @pl.when(j == 0)
    def _():
        hb = _rms_mod(x_ref[...], g_ref[...], sh_ref[...], sc_ref[...]).astype(BF16)
        h_scr[...] = hb
        lr_ref[...] = _dot_nt(hb, wlr_ref[...])
        z_ref[...] = _dot_nt(hb, w_ref[...]).astype(BF16)

    @pl.when(j == 1)
    def _():
        a = _dot_nt(h_scr[...], w_ref[...])
        gv = a[:, :d]
        z_ref[:, :d] = (gv * _sigmoid(gv)).astype(BF16)
        z_ref[:, d:] = a[:, d:].astype(BF16)

    @pl.when(j == 2)
    def _():
        a = _dot_nt(h_scr[...], w_ref[...])
        z_ref[:, :d] = (a[:, d:] * a[:, :d]).astype(BF16)
        z_ref[:, d:] = jnp.zeros((z_ref.shape[0], d), BF16)

    @pl.when(j == 3)
    def _():
        z_ref[...] = _sigmoid(_dot_nt(h_scr[...], w_ref[...])).astype(BF16)


def _inproj(x, sh, sc, g, wt, tm):
    bsz, n, d = x.shape
    p = wt.shape[0] - LANES
    tn = 2 * d
    assert p == 4 * tn
    return pl.pallas_call(
        _inproj_kernel,
        grid=(bsz, n // tm, p // tn),
        in_specs=[pl.BlockSpec((None, tm, d), lambda b, i, j: (b, i, 0)),
                  pl.BlockSpec((None, 1, d), lambda b, i, j: (b, 0, 0)),
                  pl.BlockSpec((None, 1, d), lambda b, i, j: (b, 0, 0)),
                  pl.BlockSpec((1, d), lambda b, i, j: (0, 0)),
                  pl.BlockSpec((tn, d), lambda b, i, j: (j, 0)),
                  pl.BlockSpec((LANES, d), lambda b, i, j: (p // LANES, 0))],
        out_specs=(pl.BlockSpec((None, tm, tn), lambda b, i, j: (b, i, j)),
                   pl.BlockSpec((None, tm, LANES), lambda b, i, j: (b, i, 0))),
        out_shape=(jax.ShapeDtypeStruct((bsz, n, p), BF16),
                   jax.ShapeDtypeStruct((bsz, n, LANES), F32)),
        scratch_shapes=[pltpu.VMEM((tm, d), BF16)],
        compiler_params=_cparams("arbitrary", "arbitrary", "arbitrary"),
        name="inproj",
    )(x, sh, sc, g, wt, wt)


def _gla_kernel(heads, dk, qkvf_ref, lrf_ref, qkvb_ref, lrb_ref,
                wupf_ref, wupb_ref, baf_ref, bab_ref, trif_ref, trib_ref, s0f_ref, s0b_ref,
                of_ref, ob_ref, sf_scr, sb_scr):
    tb = qkvf_ref.shape[0]
    dv = qkvf_ref.shape[1] - 2 * dk
    hk, hv = dk // heads, dv // heads
    ck = GLA_CHUNK
    nck = tb // ck
    q_scale = hk ** -0.5

    @pl.when(pl.program_id(1) == 0)
    def _():
        sf_scr[...] = s0f_ref[...]
        sb_scr[...] = s0b_ref[...]

    crow = lax.broadcasted_iota(I32, (ck, ck), 0)
    ccol = lax.broadcasted_iota(I32, (ck, ck), 1)

    dirs = (
        (qkvf_ref, lrf_ref, wupf_ref, baf_ref, of_ref, sf_scr,
         trif_ref, ccol <= crow, ck - 1, range(nck)),
        (qkvb_ref, lrb_ref, wupb_ref, bab_ref, ob_ref, sb_scr,
         trib_ref, ccol >= crow, 0, range(nck - 1, -1, -1)),
    )
    b_alls = []
    for _, lr_ref, wup_ref, ba_ref, _, _, tri, _, _, _ in dirs:
        la = _log_sigmoid(_dot(lr_ref[...].astype(BF16), wup_ref[...]) + ba_ref[...])
        la = la * (1.0 / GLA_GATE_NORM)
        hi, lo = _split_bf16(la)
        t = tri[...]
        tr = t.shape[0]
        b_alls.append(jnp.concatenate(
            [_dot(t, hi[r:r + tr]) + _dot(t, lo[r:r + tr]) for r in range(0, tb, tr)], axis=0))
    work = [[], []]

    def stage1(k):
        for di, (qkv_ref, _, _, _, _, _, _, cmask, last, order) in enumerate(dirs):
            r0 = order[k] * ck
            b = b_alls[di][r0:r0 + ck, :]
            b_last = b[last:last + 1, :]
            qc = qkv_ref[r0:r0 + ck, 0:dk].astype(F32) * q_scale
            kc = qkv_ref[r0:r0 + ck, dk:2 * dk].astype(F32)
            q_in = (qc * jnp.exp(b)).astype(BF16)
            k_in = (kc * jnp.exp(-b)).astype(BF16)
            k_dec = (kc * jnp.exp(b_last - b)).astype(BF16)
            decay = jnp.exp(b_last)
            units = []
            for h in range(heads):
                qh = q_in[:, h * hk:(h + 1) * hk]
                vh = qkv_ref[r0:r0 + ck, 2 * dk + h * hv:2 * dk + (h + 1) * hv]
                scores = jnp.where(cmask, _dot_nt(qh, k_in[:, h * hk:(h + 1) * hk]), 0.0)
                o_intra = _dot(scores.astype(BF16), vh)
                ut = _dot_tn(vh, k_dec[:, h * hk:(h + 1) * hk])
                units.append((qh, o_intra, ut, decay[:, h * hk:(h + 1) * hk]))
            work[di].append((r0, units))
    def scan(k):
        for di, (_, _, _, _, o_ref, s_scr, _, _, _, _) in enumerate(dirs):
            r0, units = work[di][k]
            for h, (qh, o_intra, ut, dec) in enumerate(units):
                st = s_scr[h]
                o = o_intra + _dot_nt(qh, st.astype(BF16))
                o_ref[r0:r0 + ck, h * hv:(h + 1) * hv] = o.astype(BF16)
                s_scr[h] = st * dec + ut

    for k in range(nck + GLA_SCAN_LAG):
        if k < nck:
            stage1(k)
        if k >= GLA_SCAN_LAG:
            scan(k - GLA_SCAN_LAG)


def _gla(z, lr, wupf, wupb, baf, bab, s0f, s0b, heads, dk, dv, tb):
    bsz, n, _ = z.shape
    nb = n // tb
    hk, hv = dk // heads, dv // heads
    fwd = lambda cb: (lambda b, i: (b, i, cb))
    bwd = lambda cb: (lambda b, i: (b, nb - 1 - i, cb))
    full = lambda *shape: pl.BlockSpec(shape, lambda b, i: (0,) * len(shape))
    sspec = pl.BlockSpec((None, heads, hv, hk), lambda b, i: (b, 0, 0, 0))
    ost = jax.ShapeDtypeStruct((bsz, n, dv), BF16)
    tr = min(tb, MXU_DEPTH)
    assert tb % tr == 0 and tr % GLA_CHUNK == 0
    ri = np.arange(tr)
    same_chunk = (ri[:, None] // GLA_CHUNK) == (ri[None, :] // GLA_CHUNK)
    trif = jnp.asarray(same_chunk & (ri[None, :] <= ri[:, None]), BF16)
    trib = jnp.asarray(same_chunk & (ri[None, :] >= ri[:, None]), BF16)
    qkv = 2 * dk + dv
    return pl.pallas_call(
        functools.partial(_gla_kernel, heads, dk),
        grid=(bsz, nb),
        in_specs=[pl.BlockSpec((None, tb, qkv), fwd(0)), pl.BlockSpec((None, tb, LANES), fwd(0)),
                  pl.BlockSpec((None, tb, qkv), bwd(0)), pl.BlockSpec((None, tb, LANES), bwd(0)),
                  full(LANES, dk), full(LANES, dk), full(1, dk), full(1, dk),
                  full(tr, tr), full(tr, tr), sspec, sspec],
        out_specs=(pl.BlockSpec((None, tb, dv), fwd(0)), pl.BlockSpec((None, tb, dv), bwd(0))),
        out_shape=(ost, ost),
        scratch_shapes=[pltpu.VMEM((heads, hv, hk), F32), pltpu.VMEM((heads, hv, hk), F32)],
        compiler_params=_cparams("arbitrary", "arbitrary"),
        name="gla",
    )(z, lr, z, lr, wupf, wupb, baf, bab, trif, trib, s0f, s0b)


def _mixer_kernel(heads, of_ref, ob_ref, sg_ref, bg_ref, cu_ref, srg_ref, src_ref, x_ref,
                  gng_ref, cw_ref, wo_ref, wco_ref, wm_ref, gt1_ref, sh2_ref, sc2_ref, n2g_ref,
                  rwt_ref, x1_ref, h2_ref, aff_ref):
    tm, dv = of_ref.shape
    hv = dv // heads
    o = of_ref[...].astype(F32) + ob_ref[...].astype(F32)
    parts = []
    for h in range(heads):
        oh = o[:, h * hv:(h + 1) * hv]
        ms = jnp.mean(oh * oh, axis=-1, keepdims=True)
        parts.append(oh * lax.rsqrt(ms + EPS))
    on = jnp.concatenate(parts, axis=1) * gng_ref[...]
    y_gla = _dot((on * sg_ref[...].astype(F32)).astype(BF16), wo_ref[...])

    cu = cu_ref[...].astype(F32)
    gcol = lax.broadcasted_iota(I32, cu.shape, 0) & (GRID_W - 1)
    left = jnp.where(gcol == 0, 0.0, pltpu.roll(cu, 1, 0))
    right = jnp.where(gcol == GRID_W - 1, 0.0, pltpu.roll(cu, tm - 1, 0))
    cw = cw_ref[...]
    conv = left * cw[0:1, :] + cu * cw[1:2, :] + right * cw[2:3, :]
    y_conv = _dot((bg_ref[...].astype(F32) * conv).astype(BF16), wco_ref[...])

    merged = srg_ref[...].astype(F32) * y_gla + src_ref[...].astype(F32) * y_conv
    y = _dot(merged.astype(BF16), wm_ref[...])
    x1 = x_ref[...] + gt1_ref[...] * y
    x1_ref[...] = x1
    h2 = _rms_mod(x1, n2g_ref[...], sh2_ref[...], sc2_ref[...])
    rpt = h2.shape[1] // LANES
    for a in range(rpt):
        h2_ref[pl.ds(a, tm, stride=rpt), :] = h2[:, a * LANES:(a + 1) * LANES]
    logits = _dot_nt(rwt_ref[...], h2.astype(BF16))
    ex = jnp.exp(logits - jnp.max(logits, axis=0, keepdims=True))
    aff_ref[...] = ex / jnp.sum(ex, axis=0, keepdims=True)


def _mixer(o_f, o_b, z, x, gng, conv_w, w_o, w_co, w_m, gt1, sh2, sc2, n2g, rwt, heads, dk, tm):
    bsz, n, d = x.shape
    dv = o_f.shape[2]
    c = conv_w.shape[1]
    e = rwt.shape[0]
    zb = lambda cb: pl.BlockSpec((None, tm, d), lambda b, i: (b, i, cb))
    base = (2 * dk + dv) // d
    tok = lambda w: pl.BlockSpec((None, tm, w), lambda b, i: (b, i, 0))
    full = lambda *shape: pl.BlockSpec(shape, lambda b, i: (0,) * len(shape))
    perb = pl.BlockSpec((None, 1, d), lambda b, i: (b, 0, 0))
    return pl.pallas_call(
        functools.partial(_mixer_kernel, heads),
        grid=(bsz, n // tm),
        in_specs=[tok(dv), tok(dv), zb(base), zb(base + 1), zb(base + 2),
                  zb(base + 4), zb(base + 5), tok(d),
                  full(1, dv), full(3, c), full(dv, d), full(c, d), full(d, d),
                  perb, perb, perb, full(1, d), full(e, d)],
        out_specs=(tok(d), pl.BlockSpec((None, tm * (d // LANES), LANES), lambda b, i: (b, i, 0)),
                   pl.BlockSpec((None, e, tm), lambda b, i: (b, 0, i))),
        out_shape=(jax.ShapeDtypeStruct((bsz, n, d), F32),
                   jax.ShapeDtypeStruct((bsz, n * (d // LANES), LANES), F32),
                   jax.ShapeDtypeStruct((bsz, e, n), F32)),
        compiler_params=_cparams("arbitrary", "arbitrary"),
        name="mixer",
    )(o_f, o_b, z, z, z, z, z, x, gng, conv_w, w_o, w_co, w_m, gt1, sh2, sc2, n2g, rwt)


def _route_kernel(cap, aff_ref, lmat_ref, pos_ref, base_ref, idx_ref, loc_scr, bef_scr):
    e, n = aff_ref.shape
    nt = n // LANES
    aff = aff_ref[...]

    def count(mask):
        return jnp.sum(jnp.where(mask, 1.0, 0.0), axis=1, keepdims=True)

    def search(k, tbits):
        cand = tbits | jnp.left_shift(jnp.int32(1), 30 - k)
        ok = count(aff >= lax.bitcast_convert_type(cand, F32)) >= cap
        return jnp.where(ok, cand, tbits)

    tbits = lax.fori_loop(0, 31, search, jnp.zeros((e, 1), I32))
    thr = lax.bitcast_convert_type(tbits, F32)
    gt = aff > thr
    eq = aff == thr
    need = cap - count(gt)

    def stack(mask):
        m = jnp.where(mask, 1.0, 0.0)
        return jnp.concatenate([m[:, j * LANES:(j + 1) * LANES] for j in range(nt)], axis=0)

    r = lax.broadcasted_iota(I32, (LANES, LANES), 0)
    cl = lax.broadcasted_iota(I32, (LANES, LANES), 1)
    upper = jnp.where(r <= cl, 1.0, 0.0).astype(BF16)
    ones = jnp.ones((LANES, LANES), BF16)
    lmat = lmat_ref[...]

    def cumsum(ms):
        msb = ms.astype(BF16)
        before = _dot(_dot(lmat, msb).astype(BF16), ones)
        return _dot(msb, upper), before

    eq_s = stack(eq)
    loc_eq, before_eq = cumsum(eq_s)
    need_s = jnp.concatenate([need] * nt, axis=0)
    sel = jnp.maximum(stack(gt), jnp.where(loc_eq + before_eq <= need_s, eq_s, 0.0))
    loc, before = cumsum(sel)
    pos_ref[...] = jnp.where(sel > 0.0, loc + before - 1.0, -1.0).astype(I32)
    base_ref[...] = before.astype(I32)

    loc_scr[...] = loc
    bef_scr[...] = before
    slot = lax.broadcasted_iota(I32, (1, cap), 1).astype(F32)
    for ex in range(e):
        loc_e = loc_scr[pl.ds(ex, nt, stride=e), :]
        tprev = bef_scr[pl.ds(ex, nt, stride=e), :][:, 0:1]
        tincl = tprev + loc_e[:, LANES - 1:LANES]
        in_tile = jnp.where((tprev <= slot) & (slot < tincl), 1.0, 0.0)
        tile = jnp.sum(jnp.where(tincl <= slot, 1.0, 0.0), axis=0, keepdims=True)
        s_loc = slot - jnp.sum(in_tile * tprev, axis=0, keepdims=True)
        counts = _dot_tn(loc_e.astype(BF16), in_tile.astype(BF16))
        lane = jnp.sum(jnp.where(counts <= s_loc, 1.0, 0.0), axis=0, keepdims=True)
        idx_ref[pl.ds(ex, 1), :] = (tile * LANES + lane).astype(I32)


def _route(aff_t, cap):
    bsz, e, n = aff_t.shape
    nt = n // LANES
    rows = nt * e
    ri = np.arange(rows)
    lmat = ((ri[None, :] % e == ri[:, None] % e) & (ri[None, :] // e < ri[:, None] // e))
    lmat = jnp.asarray(lmat, BF16)
    st = jax.ShapeDtypeStruct((bsz, rows, LANES), I32)
    ospec = pl.BlockSpec((None, rows, LANES), lambda b: (b, 0, 0))
    return pl.pallas_call(
        functools.partial(_route_kernel, cap),
        grid=(bsz,),
        in_specs=[pl.BlockSpec((None, e, n), lambda b: (b, 0, 0)),
                  pl.BlockSpec((rows, rows), lambda b: (0, 0))],
        out_specs=(ospec, ospec, pl.BlockSpec((None, e, cap), lambda b: (b, 0, 0))),
        out_shape=(st, st, jax.ShapeDtypeStruct((bsz, e, cap), I32)),
        scratch_shapes=[pltpu.VMEM((rows, LANES), F32), pltpu.VMEM((rows, LANES), F32)],
        compiler_params=_cparams("arbitrary"),
        name="route",
    )(aff_t, lmat)


def _slot_window(base):
    return pl.multiple_of((base // BF16_ROWS) * BF16_ROWS, BF16_ROWS)


def _expert_kernel(cap, idx_ref, h2_hbm, wg_ref, wu_ref, wd_ref, ye_ref, xbuf, acc, sem):
    e, f = pl.program_id(1), pl.program_id(2)
    n_exp, nf = pl.num_programs(1), pl.num_programs(2)
    last = pl.num_programs(0) * n_exp - 1
    lin = pl.program_id(0) * n_exp + e
    slot = lin % 2
    per_step = cap // nf
    rpt = wg_ref.shape[0] // LANES

    def row_copy(lin_t, part, i, slot_t):
        tok = idx_ref[(lin_t * nf + part) * per_step + i]
        return pltpu.make_async_copy(h2_hbm.at[pl.ds(pl.multiple_of(tok * rpt, rpt), rpt), :],
                                     xbuf.at[slot_t, part, pl.ds(i * rpt, rpt), :],
                                     sem.at[slot_t])

    def wait_rows(slot_t):
        pltpu.make_async_copy(xbuf.at[slot_t], xbuf.at[slot_t], sem.at[slot_t]).wait()

    @pl.when((lin == 0) & (f == 0))
    def _():
        acc[...] = jnp.zeros_like(acc)
        for part in range(nf):
            def first(i, carry):
                row_copy(lin, part, i, slot).start()
                return carry
            lax.fori_loop(0, per_step, first, 0)

    @pl.when(f == 0)
    def _():
        wait_rows(slot)

    nxt = jnp.minimum(lin + 1, last)
    for i in range(per_step):
        row_copy(nxt, f, i, 1 - slot).start()

    @pl.when((lin == last) & (f == nf - 1))
    def _():
        wait_rows(1 - slot)

    xv = jnp.concatenate(
        [jnp.concatenate([xbuf[slot, part, pl.ds(a, per_step, stride=rpt), :].astype(BF16)
                          for a in range(rpt)], axis=1) for part in range(nf)], axis=0)
    hg = _dot(xv, wg_ref[...].astype(BF16))
    hu = _dot(xv, wu_ref[...].astype(BF16))
    hid = (hg * _sigmoid(hg) * hu).astype(BF16)
    acc[...] = jnp.where(f > 0, acc[...], 0.0) + _dot(hid, wd_ref[...].astype(BF16))

    @pl.when(f == nf - 1)
    def _():
        ye_ref[0:cap, :] = acc[...].astype(BF16)
        ye_ref[cap:, :] = jnp.zeros((ye_ref.shape[0] - cap, ye_ref.shape[1]), BF16)


def _experts(h2, idx, w_gate, w_up, w_down, cap, rows, tf):
    n_exp, d, df = w_gate.shape
    bsz = h2.shape[0]
    rpt = d // LANES
    n = h2.shape[1] // rpt
    nf = df // tf
    assert cap % nf == 0
    rows_global = (idx + (jnp.arange(bsz, dtype=I32) * n)[:, None, None]).reshape(-1)
    return pl.pallas_call(
        functools.partial(_expert_kernel, cap),
        grid_spec=pltpu.PrefetchScalarGridSpec(
            num_scalar_prefetch=1,
            grid=(bsz, n_exp, nf),
            in_specs=[pl.BlockSpec(memory_space=pl.ANY),
                      pl.BlockSpec((None, d, tf), lambda b, e, f, s: (e, 0, f)),
                      pl.BlockSpec((None, d, tf), lambda b, e, f, s: (e, 0, f)),
                      pl.BlockSpec((None, tf, d), lambda b, e, f, s: (e, f, 0))],
            out_specs=pl.BlockSpec((None, None, rows, d), lambda b, e, f, s: (b, e, 0, 0)),
            scratch_shapes=[pltpu.VMEM((2, nf, (cap // nf) * rpt, LANES), F32),
                            pltpu.VMEM((cap, d), F32), pltpu.SemaphoreType.DMA((2,))]),
        out_shape=jax.ShapeDtypeStruct((bsz, n_exp, rows, d), BF16),
        compiler_params=_cparams("arbitrary", "arbitrary", "arbitrary"),
        name="experts",
    )(rows_global, h2.reshape(bsz * n * rpt, LANES), w_gate, w_up, w_down)


def _combine_windows(tm):
    return tm // 4 + BF16_ROWS, tm + BF16_ROWS


def _combine_kernel(n_exp, nsub, base_ref, ovf_ref, pos_ref, aff_ref, x1_ref, gt2_ref, fg_ref,
                    ye_hbm, out_ref, buf, fbuf, ffn_scr, sem, fsem):
    b, j = pl.program_id(0), pl.program_id(1)
    nt = pl.num_programs(1)
    step = b * nt + j
    slot = step % 2
    wp, wfull = buf.shape[2], fbuf.shape[1]
    d = buf.shape[-1]

    def start_of(tile, ex):
        return _slot_window(base_ref[tile * nsub * n_exp + ex])

    def window_copy(tile, ex, dst_slot):
        return pltpu.make_async_copy(ye_hbm.at[tile // nt, ex, pl.ds(start_of(tile, ex), wp), :],
                                     buf.at[dst_slot, ex], sem.at[dst_slot])

    def full_copy(ex):
        return pltpu.make_async_copy(ye_hbm.at[b, ex, pl.ds(start_of(step, ex), wfull), :],
                                     fbuf.at[ex], fsem)

    @pl.when(step == 0)
    def _():
        for ex in range(n_exp):
            window_copy(step, ex, slot).start()

    @pl.when(step + 1 < pl.num_programs(0) * nt)
    def _():
        for ex in range(n_exp):
            window_copy(step + 1, ex, 1 - slot).start()

    def gates_t(rows):
        srow = lax.broadcasted_iota(I32, (rows, LANES), 0)
        per_exp = []
        for ex in range(n_exp):
            s0 = start_of(step, ex)
            parts = []
            for u in range(nsub):
                rel = pos_ref[pl.ds(u * n_exp + ex, 1), :] - s0
                gate = aff_ref[pl.ds(ex, 1), u * LANES:(u + 1) * LANES]
                parts.append(jnp.where(srow == rel, gate, 0.0).astype(BF16))
            per_exp.append(jnp.concatenate(parts, axis=1))
        return jnp.concatenate(per_exp, axis=0)

    for ex in range(n_exp):
        window_copy(step, ex, slot).wait()

    @pl.when(ovf_ref[step] == 0)
    def _():
        ffn_scr[...] = _dot_tn(gates_t(wp), buf[slot].reshape(n_exp * wp, d))

    @pl.when(ovf_ref[step] != 0)
    def _():
        for ex in range(n_exp):
            full_copy(ex).start()
        for ex in range(n_exp):
            full_copy(ex).wait()
        ffn_scr[...] = _dot_tn(gates_t(wfull), fbuf[...].reshape(n_exp * wfull, d))

    x2 = x1_ref[...] + gt2_ref[...] * ffn_scr[...]
    ms = jnp.mean(x2 * x2, axis=-1, keepdims=True)
    out_ref[...] = x2 * lax.rsqrt(ms + EPS) * fg_ref[...]


def _combine(ye, pos, aff_t, before, x1, gt2, final_g, cap, tm):
    bsz, n, d = x1.shape
    n_exp = aff_t.shape[1]
    nsub = tm // LANES
    nt = n // tm
    wp, wfull = _combine_windows(tm)
    starts = before[:, :, 0].reshape(bsz, n // LANES, n_exp)
    tile_starts = starts[:, ::nsub]
    tile_ends = jnp.concatenate([tile_starts[:, 1:], jnp.full((bsz, 1, n_exp), cap, I32)], axis=1)
    overflow = (tile_starts % BF16_ROWS) + (tile_ends - tile_starts) > wp
    ovf = jnp.any(overflow, axis=-1).astype(I32).reshape(-1)
    return pl.pallas_call(
        functools.partial(_combine_kernel, n_exp, nsub),
        grid_spec=pltpu.PrefetchScalarGridSpec(
            num_scalar_prefetch=2,
            grid=(bsz, nt),
            in_specs=[pl.BlockSpec((None, nsub * n_exp, LANES), lambda b, j, s, o: (b, j, 0)),
                      pl.BlockSpec((None, n_exp, tm), lambda b, j, s, o: (b, 0, j)),
                      pl.BlockSpec((None, tm, d), lambda b, j, s, o: (b, j, 0)),
                      pl.BlockSpec((None, 1, d), lambda b, j, s, o: (b, 0, 0)),
                      pl.BlockSpec((1, d), lambda b, j, s, o: (0, 0)),
                      pl.BlockSpec(memory_space=pl.ANY)],
            out_specs=pl.BlockSpec((None, tm, d), lambda b, j, s, o: (b, j, 0)),
            scratch_shapes=[pltpu.VMEM((2, n_exp, wp, d), BF16),
                            pltpu.VMEM((n_exp, wfull, d), BF16),
                            pltpu.VMEM((tm, d), F32),
                            pltpu.SemaphoreType.DMA((2,)), pltpu.SemaphoreType.DMA(())]),
        out_shape=jax.ShapeDtypeStruct((bsz, n, d), F32),
        compiler_params=_cparams("arbitrary", "arbitrary"),
        name="combine",
    )(starts.reshape(-1), ovf, pos, aff_t, x1, gt2, final_g, ye)


def _pick(n, pref):
    t = min(n, pref)
    while n % t:
        t //= 2
    return t


def _tiles(n, d_expert):
    return dict(inproj=_pick(n, 1024), gla=_pick(n, 512), mixer=_pick(n, 512),
                combine=_pick(n, 512), expert_f=_pick(d_expert, 512))


def kernel(x, c, ctx, c_ctx, w_ada, b_ada, norm1_g, norm2_g, w_in, gla_w_a_up, gla_b_a,
           gla_norm_g, gla_w_o, conv_w, conv_w_out, merge_w_out, router_w,
           exp_w_gate, exp_w_up, exp_w_down, final_g):
    bsz, n, d = x.shape
    depth = w_ada.shape[0]
    assert depth == 1, "single-layer trunk"
    rank, dk = gla_w_a_up.shape[-2:]
    dv = gla_w_o.shape[1]
    hv = gla_norm_g.shape[-1]
    heads = dv // hv
    cch = conv_w.shape[-1]
    n_exp = router_w.shape[-1]
    cap = EC_CAPACITY * n // n_exp
    assert 2 * dk == d and dv == d and cch == d and 2 * rank <= LANES
    assert n % LANES == 0 and n % GRID_W == 0 and bsz + 1 <= 8 and n_exp % 8 == 0

    cond = jnp.zeros((8, d), F32).at[:bsz].set(c).at[bsz].set(c_ctx)
    mod = _adaln(cond, w_ada[0], b_ada[0]).reshape(8, N_MOD, d)
    sh1, sc1, gt1, sh2, sc2, gt2 = [mod[:bsz, k][:, None, :] for k in range(N_MOD)]
    csh1, csc1 = mod[bsz:bsz + 1, 0], mod[bsz:bsz + 1, 1]

    lr0 = 2 * dk + 2 * dv
    u0 = lr0 + 2 * rank
    groups = ((0, lr0), (u0 + cch, cch), (u0, cch), (u0 + 2 * cch, cch + 2 * d), (lr0, LANES))
    tiles = tuple((r0 + o, min(d, cnt - o)) for r0, cnt in groups for o in range(0, cnt, d))
    wt = _wprep(jnp.swapaxes(w_in[0], 0, 1), tiles)
    wup = gla_w_a_up[0].astype(BF16)
    wupf = jnp.zeros((LANES, dk), BF16).at[:rank].set(wup[0])
    wupb = jnp.zeros((LANES, dk), BF16).at[rank:2 * rank].set(wup[1])
    baf, bab = gla_b_a[0, 0:1], gla_b_a[0, 1:2]
    n1g, n2g = norm1_g[0:1], norm2_g[0:1]

    t = _tiles(n, exp_w_gate.shape[-1])
    s0f, s0b = _ctx_states(ctx, csh1, csc1, n1g, wt, wupf, wupb, baf, bab, heads, dk, dv)
    z, lr = _inproj(x, sh1, sc1, n1g, wt, t["inproj"])
    o_f, o_b = _gla(z, lr, wupf, wupb, baf, bab, s0f, s0b, heads, dk, dv, t["gla"])
    x1, h2, aff_t = _mixer(
        o_f, o_b, z, x, jnp.tile(gla_norm_g[0:1], (1, heads)), conv_w[0],
        gla_w_o[0].astype(BF16), conv_w_out[0].astype(BF16), merge_w_out[0].astype(BF16),
        gt1, sh2, sc2, n2g, router_w[0].T.astype(BF16), heads, dk, t["mixer"])

    pos, before, idx = _route(aff_t, cap)
    ye = _experts(h2, idx, exp_w_gate[0], exp_w_up[0], exp_w_down[0], cap,
                  cap + _combine_windows(t["combine"])[1], t["expert_f"])
    return _combine(ye, pos, aff_t, before, x1, gt2, final_g.reshape(1, d), cap, t["combine"])
```

```python
import functools

import numpy as np
import jax
import jax.numpy as jnp
from jax import lax
from jax.experimental import pallas as pl
from jax.experimental.pallas import tpu as pltpu

F32 = jnp.float32
BF16 = jnp.bfloat16
I32 = jnp.int32

EPS = 1e-6
N_MOD = 6
GRID_W = 64
GLA_CHUNK = 64
GLA_GATE_NORM = 16.0
GLA_SCAN_LAG = 2
EC_CAPACITY = 2

LANES = 128
MXU_DEPTH = 256
BF16_ROWS = 16
VMEM_LIMIT = 56 * 1024 * 1024


def _cparams(*sem, **kw):
    return pltpu.CompilerParams(dimension_semantics=sem, vmem_limit_bytes=VMEM_LIMIT, **kw)


def _dot(a, b):
    return jnp.dot(a, b, preferred_element_type=F32)


def _dot_nt(a, b):
    return lax.dot_general(a, b, (((1,), (1,)), ((), ())), preferred_element_type=F32)


def _dot_tn(a, b):
    return lax.dot_general(a, b, (((0,), (0,)), ((), ())), preferred_element_type=F32)


def _sigmoid(v):
    return 1.0 / (1.0 + jnp.exp(-v))


def _log_sigmoid(v):
    return jnp.minimum(v, 0.0) - jnp.log(1.0 + jnp.exp(-jnp.abs(v)))


def _rms_mod(xv, g, shift, scale):
    ms = jnp.mean(xv * xv, axis=-1, keepdims=True)
    y = xv * lax.rsqrt(ms + EPS) * g
    return y * (1.0 + scale) + shift


def _split_bf16(v):
    hi = v.astype(BF16)
    lo = (v - hi.astype(F32)).astype(BF16)
    return hi, lo


def _adaln_kernel(c_ref, w_ref, b_ref, o_ref):
    cv = c_ref[...]
    s = (cv * _sigmoid(cv)).astype(BF16)
    o_ref[...] = _dot(s, w_ref[...].astype(BF16)) + b_ref[...]


def _adaln(cond, w, b):
    rows, d = cond.shape
    nout = w.shape[1]
    tn = d
    return pl.pallas_call(
        _adaln_kernel,
        grid=(nout // tn,),
        in_specs=[pl.BlockSpec((rows, d), lambda j: (0, 0)),
                  pl.BlockSpec((d, tn), lambda j: (0, j)),
                  pl.BlockSpec((1, tn), lambda j: (0, j))],
        out_specs=pl.BlockSpec((rows, tn), lambda j: (0, j)),
        out_shape=jax.ShapeDtypeStruct((rows, nout), F32),
        compiler_params=_cparams("arbitrary"),
        name="adaln",
    )(cond, w, b.reshape(1, nout))


def _wprep_kernel(tiles, wt_hbm, o_ref, buf, sem):
    def copy(t):
        row, cnt = tiles[t]
        return pltpu.make_async_copy(wt_hbm.at[pl.ds(row, cnt), :],
                                     buf.at[t % 2, pl.ds(0, cnt), :], sem.at[t % 2])

    copy(0).start()
    out_row = 0
    for t, (_, cnt) in enumerate(tiles):
        if t + 1 < len(tiles):
            copy(t + 1).start()
        copy(t).wait()
        o_ref[out_row:out_row + cnt, :] = buf[t % 2, 0:cnt, :].astype(BF16)
        out_row += cnt


def _wprep(wt, tiles):
    d = wt.shape[1]
    rows = sum(cnt for _, cnt in tiles)
    return pl.pallas_call(
        functools.partial(_wprep_kernel, tiles),
        in_specs=[pl.BlockSpec(memory_space=pl.ANY)],
        out_specs=pl.BlockSpec((rows, d), lambda: (0, 0)),
        out_shape=jax.ShapeDtypeStruct((rows, d), BF16),
        scratch_shapes=[pltpu.VMEM((2, max(cnt for _, cnt in tiles), d), F32),
                        pltpu.SemaphoreType.DMA((2,))],
        compiler_params=pltpu.CompilerParams(vmem_limit_bytes=VMEM_LIMIT),
        name="wprep",
    )(wt)


def _ctx_kernel(heads, ctx_ref, sh_ref, sc_ref, g_ref, wk_ref, wv_ref, wlr_ref,
                wupf_ref, wupb_ref, baf_ref, bab_ref, sf_ref, sb_ref):
    n = ctx_ref.shape[0]
    hc = _rms_mod(ctx_ref[...], g_ref[...], sh_ref[...], sc_ref[...]).astype(BF16)
    k = _dot_nt(hc, wk_ref[...])
    v = _dot_nt(hc, wv_ref[...]).astype(BF16)
    lr = _dot_nt(hc, wlr_ref[...]).astype(BF16)
    hk = k.shape[1] // heads
    hv = v.shape[1] // heads
    row = lax.broadcasted_iota(I32, (n, n), 0)
    col = lax.broadcasted_iota(I32, (n, n), 1)
    for wup_ref, ba_ref, s_ref, tri, last in (
            (wupf_ref, baf_ref, sf_ref, col <= row, n - 1),
            (wupb_ref, bab_ref, sb_ref, col >= row, 0)):
        la = _log_sigmoid(_dot(lr, wup_ref[...]) + ba_ref[...]) * (1.0 / GLA_GATE_NORM)
        hi, lo = _split_bf16(la)
        t = jnp.where(tri, 1.0, 0.0).astype(BF16)
        b = _dot(t, hi) + _dot(t, lo)
        kd = (k * jnp.exp(b[last:last + 1, :] - b)).astype(BF16)
        for h in range(heads):
            s_ref[h] = _dot_tn(v[:, h * hv:(h + 1) * hv], kd[:, h * hk:(h + 1) * hk])


def _ctx_states(ctx, sh, sc, g, wt, wupf, wupb, baf, bab, heads, dk, dv):
    bsz, n, d = ctx.shape
    lr_blk = (wt.shape[0] - LANES) // LANES
    hk, hv = dk // heads, dv // heads
    full = lambda *shape: pl.BlockSpec(shape, lambda b: (0,) * len(shape))
    st = jax.ShapeDtypeStruct((bsz, heads, hv, hk), F32)
    sspec = pl.BlockSpec((None, heads, hv, hk), lambda b: (b, 0, 0, 0))
    return pl.pallas_call(
        functools.partial(_ctx_kernel, heads),
        grid=(bsz,),
        in_specs=[pl.BlockSpec((None, n, d), lambda b: (b, 0, 0)),
                  full(1, d), full(1, d), full(1, d),
                  pl.BlockSpec((dk, d), lambda b: (1, 0)),
                  pl.BlockSpec((dv, d), lambda b: (2 * dk // dv, 0)),
                  pl.BlockSpec((LANES, d), lambda b: (lr_blk, 0)),
                  full(LANES, dk), full(LANES, dk), full(1, dk), full(1, dk)],
        out_specs=(sspec, sspec),
        out_shape=(st, st),
        compiler_params=_cparams("arbitrary"),
        name="ctx_state",
    )(ctx, sh, sc, g, wt, wt, wt, wupf, wupb, baf, bab)


def _inproj_kernel(x_ref, sh_ref, sc_ref, g_ref, w_ref, wlr_ref, z_ref, lr_ref, h_scr):
    j = pl.program_id(2)
    d = x_ref.shape[1]

    @pl.when(j == 0)
    def _():
        hb = _rms_mod(x_ref[...], g_ref[...], sh_ref[...], sc_ref[...]).astype(BF16)
        h_scr[...] = hb
        lr_ref[...] = _dot_nt(hb, wlr_ref[...])
        z_ref[...] = _dot_nt(hb, w_ref[...]).astype(BF16)

    @pl.when(j == 1)
    def _():
        a = _dot_nt(h_scr[...], w_ref[...])
        gv = a[:, :d]
        z_ref[:, :d] = (gv * _sigmoid(gv)).astype(BF16)
        z_ref[:, d:] = a[:, d:].astype(BF16)

    @pl.when(j == 2)
    def _():
        a = _dot_nt(h_scr[...], w_ref[...])
        z_ref[:, :d] = (a[:, d:] * a[:, :d]).astype(BF16)
        z_ref[:, d:] = jnp.zeros((z_ref.shape[0], d), BF16)

    @pl.when(j == 3)
    def _():
        z_ref[...] = _sigmoid(_dot_nt(h_scr[...], w_ref[...])).astype(BF16)


def _inproj(x, sh, sc, g, wt, tm):
    bsz, n, d = x.shape
    p = wt.shape[0] - LANES
    tn = 2 * d
    assert p == 4 * tn
    return pl.pallas_call(
        _inproj_kernel,
        grid=(bsz, n // tm, p // tn),
        in_specs=[pl.BlockSpec((None, tm, d), lambda b, i, j: (b, i, 0)),
                  pl.BlockSpec((None, 1, d), lambda b, i, j: (b, 0, 0)),
                  pl.BlockSpec((None, 1, d), lambda b, i, j: (b, 0, 0)),
                  pl.BlockSpec((1, d), lambda b, i, j: (0, 0)),
                  pl.BlockSpec((tn, d), lambda b, i, j: (j, 0)),
                  pl.BlockSpec((LANES, d), lambda b, i, j: (p // LANES, 0))],
        out_specs=(pl.BlockSpec((None, tm, tn), lambda b, i, j: (b, i, j)),
                   pl.BlockSpec((None, tm, LANES), lambda b, i, j: (b, i, 0))),
        out_shape=(jax.ShapeDtypeStruct((bsz, n, p), BF16),
                   jax.ShapeDtypeStruct((bsz, n, LANES), F32)),
        scratch_shapes=[pltpu.VMEM((tm, d), BF16)],
        compiler_params=_cparams("arbitrary", "arbitrary", "arbitrary"),
        name="inproj",
    )(x, sh, sc, g, wt, wt)


def _gla_kernel(heads, dk, qkvf_ref, lrf_ref, qkvb_ref, lrb_ref,
                wupf_ref, wupb_ref, baf_ref, bab_ref, trif_ref, trib_ref, s0f_ref, s0b_ref,
                of_ref, ob_ref, sf_scr, sb_scr):
    tb = qkvf_ref.shape[0]
    dv = qkvf_ref.shape[1] - 2 * dk
    hk, hv = dk // heads, dv // heads
    ck = GLA_CHUNK
    nck = tb // ck
    q_scale = hk ** -0.5

    @pl.when(pl.program_id(1) == 0)
    def _():
        sf_scr[...] = s0f_ref[...]
        sb_scr[...] = s0b_ref[...]

    crow = lax.broadcasted_iota(I32, (ck, ck), 0)
    ccol = lax.broadcasted_iota(I32, (ck, ck), 1)

    dirs = (
        (qkvf_ref, lrf_ref, wupf_ref, baf_ref, of_ref, sf_scr,
         trif_ref, ccol <= crow, ck - 1, range(nck)),
        (qkvb_ref, lrb_ref, wupb_ref, bab_ref, ob_ref, sb_scr,
         trib_ref, ccol >= crow, 0, range(nck - 1, -1, -1)),
    )
    b_alls = []
    for _, lr_ref, wup_ref, ba_ref, _, _, tri, _, _, _ in dirs:
        la = _log_sigmoid(_dot(lr_ref[...].astype(BF16), wup_ref[...]) + ba_ref[...])
        la = la * (1.0 / GLA_GATE_NORM)
        hi, lo = _split_bf16(la)
        t = tri[...]
        tr = t.shape[0]
        b_alls.append(jnp.concatenate(
            [_dot(t, hi[r:r + tr]) + _dot(t, lo[r:r + tr]) for r in range(0, tb, tr)], axis=0))
    work = [[], []]

    def stage1(k):
        for di, (qkv_ref, _, _, _, _, _, _, cmask, last, order) in enumerate(dirs):
            r0 = order[k] * ck
            b = b_alls[di][r0:r0 + ck, :]
            b_last = b[last:last + 1, :]
            qc = qkv_ref[r0:r0 + ck, 0:dk].astype(F32) * q_scale
            kc = qkv_ref[r0:r0 + ck, dk:2 * dk].astype(F32)
            q_in = (qc * jnp.exp(b)).astype(BF16)
            k_in = (kc * jnp.exp(-b)).astype(BF16)
            k_dec = (kc * jnp.exp(b_last - b)).astype(BF16)
            decay = jnp.exp(b_last)
            units = []
            for h in range(heads):
                qh = q_in[:, h * hk:(h + 1) * hk]
                vh = qkv_ref[r0:r0 + ck, 2 * dk + h * hv:2 * dk + (h + 1) * hv]
                scores = jnp.where(cmask, _dot_nt(qh, k_in[:, h * hk:(h + 1) * hk]), 0.0)
                o_intra = _dot(scores.astype(BF16), vh)
                ut = _dot_tn(vh, k_dec[:, h * hk:(h + 1) * hk])
                units.append((qh, o_intra, ut, decay[:, h * hk:(h + 1) * hk]))
            work[di].append((r0, units))
    def scan(k):
        for di, (_, _, _, _, o_ref, s_scr, _, _, _, _) in enumerate(dirs):
            r0, units = work[di][k]
            for h, (qh, o_intra, ut, dec) in enumerate(units):
                st = s_scr[h]
                o = o_intra + _dot_nt(qh, st.astype(BF16))
                o_ref[r0:r0 + ck, h * hv:(h + 1) * hv] = o.astype(BF16)
                s_scr[h] = st * dec + ut

    for k in range(nck + GLA_SCAN_LAG):
        if k < nck:
            stage1(k)
        if k >= GLA_SCAN_LAG:
            scan(k - GLA_SCAN_LAG)


def _gla(z, lr, wupf, wupb, baf, bab, s0f, s0b, heads, dk, dv, tb):
    bsz, n, _ = z.shape
    nb = n // tb
    hk, hv = dk // heads, dv // heads
    fwd = lambda cb: (lambda b, i: (b, i, cb))
    bwd = lambda cb: (lambda b, i: (b, nb - 1 - i, cb))
    full = lambda *shape: pl.BlockSpec(shape, lambda b, i: (0,) * len(shape))
    sspec = pl.BlockSpec((None, heads, hv, hk), lambda b, i: (b, 0, 0, 0))
    ost = jax.ShapeDtypeStruct((bsz, n, dv), BF16)
    tr = min(tb, MXU_DEPTH)
    assert tb % tr == 0 and tr % GLA_CHUNK == 0
    ri = np.arange(tr)
    same_chunk = (ri[:, None] // GLA_CHUNK) == (ri[None, :] // GLA_CHUNK)
    trif = jnp.asarray(same_chunk & (ri[None, :] <= ri[:, None]), BF16)
    trib = jnp.asarray(same_chunk & (ri[None, :] >= ri[:, None]), BF16)
    qkv = 2 * dk + dv
    return pl.pallas_call(
        functools.partial(_gla_kernel, heads, dk),
        grid=(bsz, nb),
        in_specs=[pl.BlockSpec((None, tb, qkv), fwd(0)), pl.BlockSpec((None, tb, LANES), fwd(0)),
                  pl.BlockSpec((None, tb, qkv), bwd(0)), pl.BlockSpec((None, tb, LANES), bwd(0)),
                  full(LANES, dk), full(LANES, dk), full(1, dk), full(1, dk),
                  full(tr, tr), full(tr, tr), sspec, sspec],
        out_specs=(pl.BlockSpec((None, tb, dv), fwd(0)), pl.BlockSpec((None, tb, dv), bwd(0))),
        out_shape=(ost, ost),
        scratch_shapes=[pltpu.VMEM((heads, hv, hk), F32), pltpu.VMEM((heads, hv, hk), F32)],
        compiler_params=_cparams("arbitrary", "arbitrary"),
        name="gla",
    )(z, lr, z, lr, wupf, wupb, baf, bab, trif, trib, s0f, s0b)


def _mixer_kernel(heads, of_ref, ob_ref, sg_ref, bg_ref, cu_ref, srg_ref, src_ref, x_ref,
                  gng_ref, cw_ref, wo_ref, wco_ref, wm_ref, gt1_ref, sh2_ref, sc2_ref, n2g_ref,
                  rwt_ref, x1_ref, h2_ref, aff_ref):
    tm, dv = of_ref.shape
    hv = dv // heads
    o = of_ref[...].astype(F32) + ob_ref[...].astype(F32)
    parts = []
    for h in range(heads):
        oh = o[:, h * hv:(h + 1) * hv]
        ms = jnp.mean(oh * oh, axis=-1, keepdims=True)
        parts.append(oh * lax.rsqrt(ms + EPS))
    on = jnp.concatenate(parts, axis=1) * gng_ref[...]
    y_gla = _dot((on * sg_ref[...].astype(F32)).astype(BF16), wo_ref[...])

    cu = cu_ref[...].astype(F32)
    gcol = lax.broadcasted_iota(I32, cu.shape, 0) & (GRID_W - 1)
    left = jnp.where(gcol == 0, 0.0, pltpu.roll(cu, 1, 0))
    right = jnp.where(gcol == GRID_W - 1, 0.0, pltpu.roll(cu, tm - 1, 0))
    cw = cw_ref[...]
    conv = left * cw[0:1, :] + cu * cw[1:2, :] + right * cw[2:3, :]
    y_conv = _dot((bg_ref[...].astype(F32) * conv).astype(BF16), wco_ref[...])

    merged = srg_ref[...].astype(F32) * y_gla + src_ref[...].astype(F32) * y_conv
    y = _dot(merged.astype(BF16), wm_ref[...])
    x1 = x_ref[...] + gt1_ref[...] * y
    x1_ref[...] = x1
    h2 = _rms_mod(x1, n2g_ref[...], sh2_ref[...], sc2_ref[...])
    rpt = h2.shape[1] // LANES
    for a in range(rpt):
        h2_ref[pl.ds(a, tm, stride=rpt), :] = h2[:, a * LANES:(a + 1) * LANES]
    logits = _dot_nt(rwt_ref[...], h2.astype(BF16))
    ex = jnp.exp(logits - jnp.max(logits, axis=0, keepdims=True))
    aff_ref[...] = ex / jnp.sum(ex, axis=0, keepdims=True)


def _mixer(o_f, o_b, z, x, gng, conv_w, w_o, w_co, w_m, gt1, sh2, sc2, n2g, rwt, heads, dk, tm):
    bsz, n, d = x.shape
    dv = o_f.shape[2]
    c = conv_w.shape[1]
    e = rwt.shape[0]
    zb = lambda cb: pl.BlockSpec((None, tm, d), lambda b, i: (b, i, cb))
    base = (2 * dk + dv) // d
    tok = lambda w: pl.BlockSpec((None, tm, w), lambda b, i: (b, i, 0))
    full = lambda *shape: pl.BlockSpec(shape, lambda b, i: (0,) * len(shape))
    perb = pl.BlockSpec((None, 1, d), lambda b, i: (b, 0, 0))
    return pl.pallas_call(
        functools.partial(_mixer_kernel, heads),
        grid=(bsz, n // tm),
        in_specs=[tok(dv), tok(dv), zb(base), zb(base + 1), zb(base + 2),
                  zb(base + 4), zb(base + 5), tok(d),
                  full(1, dv), full(3, c), full(dv, d), full(c, d), full(d, d),
                  perb, perb, perb, full(1, d), full(e, d)],
        out_specs=(tok(d), pl.BlockSpec((None, tm * (d // LANES), LANES), lambda b, i: (b, i, 0)),
                   pl.BlockSpec((None, e, tm), lambda b, i: (b, 0, i))),
        out_shape=(jax.ShapeDtypeStruct((bsz, n, d), F32),
                   jax.ShapeDtypeStruct((bsz, n * (d // LANES), LANES), F32),
                   jax.ShapeDtypeStruct((bsz, e, n), F32)),
        compiler_params=_cparams("arbitrary", "arbitrary"),
        name="mixer",
    )(o_f, o_b, z, z, z, z, z, x, gng, conv_w, w_o, w_co, w_m, gt1, sh2, sc2, n2g, rwt)


def _route_kernel(cap, aff_ref, lmat_ref, pos_ref, base_ref, idx_ref, loc_scr, bef_scr):
    e, n = aff_ref.shape
    nt = n // LANES
    aff = aff_ref[...]

    def count(mask):
        return jnp.sum(jnp.where(mask, 1.0, 0.0), axis=1, keepdims=True)

    def search(k, tbits):
        cand = tbits | jnp.left_shift(jnp.int32(1), 30 - k)
        ok = count(aff >= lax.bitcast_convert_type(cand, F32)) >= cap
        return jnp.where(ok, cand, tbits)

    tbits = lax.fori_loop(0, 31, search, jnp.zeros((e, 1), I32))
    thr = lax.bitcast_convert_type(tbits, F32)
    gt = aff > thr
    eq = aff == thr
    need = cap - count(gt)

    def stack(mask):
        m = jnp.where(mask, 1.0, 0.0)
        return jnp.concatenate([m[:, j * LANES:(j + 1) * LANES] for j in range(nt)], axis=0)

    r = lax.broadcasted_iota(I32, (LANES, LANES), 0)
    cl = lax.broadcasted_iota(I32, (LANES, LANES), 1)
    upper = jnp.where(r <= cl, 1.0, 0.0).astype(BF16)
    ones = jnp.ones((LANES, LANES), BF16)
    lmat = lmat_ref[...]

    def cumsum(ms):
        msb = ms.astype(BF16)
        before = _dot(_dot(lmat, msb).astype(BF16), ones)
        return _dot(msb, upper), before

    eq_s = stack(eq)
    loc_eq, before_eq = cumsum(eq_s)
    need_s = jnp.concatenate([need] * nt, axis=0)
    sel = jnp.maximum(stack(gt), jnp.where(loc_eq + before_eq <= need_s, eq_s, 0.0))
    loc, before = cumsum(sel)
    pos_ref[...] = jnp.where(sel > 0.0, loc + before - 1.0, -1.0).astype(I32)
    base_ref[...] = before.astype(I32)

    loc_scr[...] = loc
    bef_scr[...] = before
    slot = lax.broadcasted_iota(I32, (1, cap), 1).astype(F32)
    for ex in range(e):
        loc_e = loc_scr[pl.ds(ex, nt, stride=e), :]
        tprev = bef_scr[pl.ds(ex, nt, stride=e), :][:, 0:1]
        tincl = tprev + loc_e[:, LANES - 1:LANES]
        in_tile = jnp.where((tprev <= slot) & (slot < tincl), 1.0, 0.0)
        tile = jnp.sum(jnp.where(tincl <= slot, 1.0, 0.0), axis=0, keepdims=True)
        s_loc = slot - jnp.sum(in_tile * tprev, axis=0, keepdims=True)
        counts = _dot_tn(loc_e.astype(BF16), in_tile.astype(BF16))
        lane = jnp.sum(jnp.where(counts <= s_loc, 1.0, 0.0), axis=0, keepdims=True)
        idx_ref[pl.ds(ex, 1), :] = (tile * LANES + lane).astype(I32)


def _route(aff_t, cap):
    bsz, e, n = aff_t.shape
    nt = n // LANES
    rows = nt * e
    ri = np.arange(rows)
    lmat = ((ri[None, :] % e == ri[:, None] % e) & (ri[None, :] // e < ri[:, None] // e))
    lmat = jnp.asarray(lmat, BF16)
    st = jax.ShapeDtypeStruct((bsz, rows, LANES), I32)
    ospec = pl.BlockSpec((None, rows, LANES), lambda b: (b, 0, 0))
    return pl.pallas_call(
        functools.partial(_route_kernel, cap),
        grid=(bsz,),
        in_specs=[pl.BlockSpec((None, e, n), lambda b: (b, 0, 0)),
                  pl.BlockSpec((rows, rows), lambda b: (0, 0))],
        out_specs=(ospec, ospec, pl.BlockSpec((None, e, cap), lambda b: (b, 0, 0))),
        out_shape=(st, st, jax.ShapeDtypeStruct((bsz, e, cap), I32)),
        scratch_shapes=[pltpu.VMEM((rows, LANES), F32), pltpu.VMEM((rows, LANES), F32)],
        compiler_params=_cparams("arbitrary"),
        name="route",
    )(aff_t, lmat)


def _slot_window(base):
    return pl.multiple_of((base // BF16_ROWS) * BF16_ROWS, BF16_ROWS)


def _expert_kernel(cap, idx_ref, h2_hbm, wg_ref, wu_ref, wd_ref, ye_ref, xbuf, acc, sem):
    e, f = pl.program_id(1), pl.program_id(2)
    n_exp, nf = pl.num_programs(1), pl.num_programs(2)
    last = pl.num_programs(0) * n_exp - 1
    lin = pl.program_id(0) * n_exp + e
    slot = lin % 2
    per_step = cap // nf
    rpt = wg_ref.shape[0] // LANES

    def row_copy(lin_t, part, i, slot_t):
        tok = idx_ref[(lin_t * nf + part) * per_step + i]
        return pltpu.make_async_copy(h2_hbm.at[pl.ds(pl.multiple_of(tok * rpt, rpt), rpt), :],
                                     xbuf.at[slot_t, part, pl.ds(i * rpt, rpt), :],
                                     sem.at[slot_t])

    def wait_rows(slot_t):
        pltpu.make_async_copy(xbuf.at[slot_t], xbuf.at[slot_t], sem.at[slot_t]).wait()

    @pl.when((lin == 0) & (f == 0))
    def _():
        acc[...] = jnp.zeros_like(acc)
        for part in range(nf):
            def first(i, carry):
                row_copy(lin, part, i, slot).start()
                return carry
            lax.fori_loop(0, per_step, first, 0)

    @pl.when(f == 0)
    def _():
        wait_rows(slot)

    nxt = jnp.minimum(lin + 1, last)
    for i in range(per_step):
        row_copy(nxt, f, i, 1 - slot).start()

    @pl.when((lin == last) & (f == nf - 1))
    def _():
        wait_rows(1 - slot)

    xv = jnp.concatenate(
        [jnp.concatenate([xbuf[slot, part, pl.ds(a, per_step, stride=rpt), :].astype(BF16)
                          for a in range(rpt)], axis=1) for part in range(nf)], axis=0)
    hg = _dot(xv, wg_ref[...].astype(BF16))
    hu = _dot(xv, wu_ref[...].astype(BF16))
    hid = (hg * _sigmoid(hg) * hu).astype(BF16)
    acc[...] = jnp.where(f > 0, acc[...], 0.0) + _dot(hid, wd_ref[...].astype(BF16))

    @pl.when(f == nf - 1)
    def _():
        ye_ref[0:cap, :] = acc[...].astype(BF16)
        ye_ref[cap:, :] = jnp.zeros((ye_ref.shape[0] - cap, ye_ref.shape[1]), BF16)


def _experts(h2, idx, w_gate, w_up, w_down, cap, rows, tf):
    n_exp, d, df = w_gate.shape
    bsz = h2.shape[0]
    rpt = d // LANES
    n = h2.shape[1] // rpt
    nf = df // tf
    assert cap % nf == 0
    rows_global = (idx + (jnp.arange(bsz, dtype=I32) * n)[:, None, None]).reshape(-1)
    return pl.pallas_call(
        functools.partial(_expert_kernel, cap),
        grid_spec=pltpu.PrefetchScalarGridSpec(
            num_scalar_prefetch=1,
            grid=(bsz, n_exp, nf),
            in_specs=[pl.BlockSpec(memory_space=pl.ANY),
                      pl.BlockSpec((None, d, tf), lambda b, e, f, s: (e, 0, f)),
                      pl.BlockSpec((None, d, tf), lambda b, e, f, s: (e, 0, f)),
                      pl.BlockSpec((None, tf, d), lambda b, e, f, s: (e, f, 0))],
            out_specs=pl.BlockSpec((None, None, rows, d), lambda b, e, f, s: (b, e, 0, 0)),
            scratch_shapes=[pltpu.VMEM((2, nf, (cap // nf) * rpt, LANES), F32),
                            pltpu.VMEM((cap, d), F32), pltpu.SemaphoreType.DMA((2,))]),
        out_shape=jax.ShapeDtypeStruct((bsz, n_exp, rows, d), BF16),
        compiler_params=_cparams("arbitrary", "arbitrary", "arbitrary"),
        name="experts",
    )(rows_global, h2.reshape(bsz * n * rpt, LANES), w_gate, w_up, w_down)


def _combine_windows(tm, n_exp):
    mean = tm * EC_CAPACITY / n_exp
    usual = mean + 5.5 * (mean * (1 - EC_CAPACITY / n_exp)) ** 0.5 + BF16_ROWS - 1
    usual = min(-(-int(usual) // BF16_ROWS) * BF16_ROWS, tm + BF16_ROWS)
    return usual, tm + BF16_ROWS


def _combine_kernel(n_exp, nsub, base_ref, ovf_ref, pos_ref, aff_ref, x1_ref, gt2_ref, fg_ref,
                    ye_hbm, out_ref, buf, fbuf, ffn_scr, sem, fsem):
    b, j = pl.program_id(0), pl.program_id(1)
    nt = pl.num_programs(1)
    step = b * nt + j
    slot = step % 2
    wp, wfull = buf.shape[2], fbuf.shape[0]
    d = buf.shape[-1]

    def start_of(tile, ex):
        return _slot_window(base_ref[tile * nsub * n_exp + ex])

    def window_copy(tile, ex, dst_slot):
        return pltpu.make_async_copy(ye_hbm.at[tile // nt, ex, pl.ds(start_of(tile, ex), wp), :],
                                     buf.at[dst_slot, ex], sem.at[dst_slot])

    def full_copy(ex):
        return pltpu.make_async_copy(ye_hbm.at[b, ex, pl.ds(start_of(step, ex), wfull), :],
                                     fbuf, fsem)

    @pl.when(step == 0)
    def _():
        for ex in range(n_exp):
            window_copy(step, ex, slot).start()

    @pl.when(step + 1 < pl.num_programs(0) * nt)
    def _():
        for ex in range(n_exp):
            window_copy(step + 1, ex, 1 - slot).start()

    def gates_t(rows, ex):
        srow = lax.broadcasted_iota(I32, (rows, LANES), 0)
        s0 = start_of(step, ex)
        parts = []
        for u in range(nsub):
            rel = pos_ref[pl.ds(u * n_exp + ex, 1), :] - s0
            gate = aff_ref[pl.ds(ex, 1), u * LANES:(u + 1) * LANES]
            parts.append(jnp.where(srow == rel, gate, 0.0).astype(BF16))
        return jnp.concatenate(parts, axis=1)

    for ex in range(n_exp):
        window_copy(step, ex, slot).wait()

    @pl.when(ovf_ref[step] == 0)
    def _():
        g_all = jnp.concatenate([gates_t(wp, ex) for ex in range(n_exp)], axis=0)
        ffn_scr[...] = _dot_tn(g_all, buf[slot].reshape(n_exp * wp, d))

    @pl.when(ovf_ref[step] != 0)
    def _():
        ffn_scr[...] = jnp.zeros_like(ffn_scr)
        for ex in range(n_exp):
            full_copy(ex).start()
            full_copy(ex).wait()
            ffn_scr[...] += _dot_tn(gates_t(wfull, ex), fbuf[...])

    x2 = x1_ref[...] + gt2_ref[...] * ffn_scr[...]
    ms = jnp.mean(x2 * x2, axis=-1, keepdims=True)
    out_ref[...] = x2 * lax.rsqrt(ms + EPS) * fg_ref[...]


def _combine(ye, pos, aff_t, before, x1, gt2, final_g, cap, tm):
    bsz, n, d = x1.shape
    n_exp = aff_t.shape[1]
    nsub = tm // LANES
    nt = n // tm
    wp, wfull = _combine_windows(tm, n_exp)
    starts = before[:, :, 0].reshape(bsz, n // LANES, n_exp)
    tile_starts = starts[:, ::nsub]
    tile_ends = jnp.concatenate([tile_starts[:, 1:], jnp.full((bsz, 1, n_exp), cap, I32)], axis=1)
    overflow = (tile_starts % BF16_ROWS) + (tile_ends - tile_starts) > wp
    ovf = jnp.any(overflow, axis=-1).astype(I32).reshape(-1)
    return pl.pallas_call(
        functools.partial(_combine_kernel, n_exp, nsub),
        grid_spec=pltpu.PrefetchScalarGridSpec(
            num_scalar_prefetch=2,
            grid=(bsz, nt),
            in_specs=[pl.BlockSpec((None, nsub * n_exp, LANES), lambda b, j, s, o: (b, j, 0)),
                      pl.BlockSpec((None, n_exp, tm), lambda b, j, s, o: (b, 0, j)),
                      pl.BlockSpec((None, tm, d), lambda b, j, s, o: (b, j, 0)),
                      pl.BlockSpec((None, 1, d), lambda b, j, s, o: (b, 0, 0)),
                      pl.BlockSpec((1, d), lambda b, j, s, o: (0, 0)),
                      pl.BlockSpec(memory_space=pl.ANY)],
            out_specs=pl.BlockSpec((None, tm, d), lambda b, j, s, o: (b, j, 0)),
            scratch_shapes=[pltpu.VMEM((2, n_exp, wp, d), BF16),
                            pltpu.VMEM((wfull, d), BF16),
                            pltpu.VMEM((tm, d), F32),
                            pltpu.SemaphoreType.DMA((2,)), pltpu.SemaphoreType.DMA(())]),
        out_shape=jax.ShapeDtypeStruct((bsz, n, d), F32),
        compiler_params=_cparams("arbitrary", "arbitrary"),
        name="combine",
    )(starts.reshape(-1), ovf, pos, aff_t, x1, gt2, final_g, ye)


def _pick(n, pref):
    t = min(n, pref)
    while n % t:
        t //= 2
    return t


def _tiles(n, d_expert):
    return dict(inproj=_pick(n, 1024), gla=_pick(n, 512), mixer=_pick(n, 512),
                combine=_pick(n, 512), expert_f=_pick(d_expert, 512))


def kernel(x, c, ctx, c_ctx, w_ada, b_ada, norm1_g, norm2_g, w_in, gla_w_a_up, gla_b_a,
           gla_norm_g, gla_w_o, conv_w, conv_w_out, merge_w_out, router_w,
           exp_w_gate, exp_w_up, exp_w_down, final_g):
    bsz, n, d = x.shape
    depth = w_ada.shape[0]
    assert depth == 1, "single-layer trunk"
    rank, dk = gla_w_a_up.shape[-2:]
    dv = gla_w_o.shape[1]
    hv = gla_norm_g.shape[-1]
    heads = dv // hv
    cch = conv_w.shape[-1]
    n_exp = router_w.shape[-1]
    cap = EC_CAPACITY * n // n_exp
    assert 2 * dk == d and dv == d and cch == d and 2 * rank <= LANES
    assert n % LANES == 0 and n % GRID_W == 0 and bsz + 1 <= 8 and n_exp % 8 == 0

    cond = jnp.zeros((8, d), F32).at[:bsz].set(c).at[bsz].set(c_ctx)
    mod = _adaln(cond, w_ada[0], b_ada[0]).reshape(8, N_MOD, d)
    sh1, sc1, gt1, sh2, sc2, gt2 = [mod[:bsz, k][:, None, :] for k in range(N_MOD)]
    csh1, csc1 = mod[bsz:bsz + 1, 0], mod[bsz:bsz + 1, 1]

    lr0 = 2 * dk + 2 * dv
    u0 = lr0 + 2 * rank
    groups = ((0, lr0), (u0 + cch, cch), (u0, cch), (u0 + 2 * cch, cch + 2 * d), (lr0, LANES))
    tiles = tuple((r0 + o, min(d, cnt - o)) for r0, cnt in groups for o in range(0, cnt, d))
    wt = _wprep(jnp.swapaxes(w_in[0], 0, 1), tiles)
    wup = gla_w_a_up[0].astype(BF16)
    wupf = jnp.zeros((LANES, dk), BF16).at[:rank].set(wup[0])
    wupb = jnp.zeros((LANES, dk), BF16).at[rank:2 * rank].set(wup[1])
    baf, bab = gla_b_a[0, 0:1], gla_b_a[0, 1:2]
    n1g, n2g = norm1_g[0:1], norm2_g[0:1]

    t = _tiles(n, exp_w_gate.shape[-1])
    s0f, s0b = _ctx_states(ctx, csh1, csc1, n1g, wt, wupf, wupb, baf, bab, heads, dk, dv)
    z, lr = _inproj(x, sh1, sc1, n1g, wt, t["inproj"])
    o_f, o_b = _gla(z, lr, wupf, wupb, baf, bab, s0f, s0b, heads, dk, dv, t["gla"])
    x1, h2, aff_t = _mixer(
        o_f, o_b, z, x, jnp.tile(gla_norm_g[0:1], (1, heads)), conv_w[0],
        gla_w_o[0].astype(BF16), conv_w_out[0].astype(BF16), merge_w_out[0].astype(BF16),
        gt1, sh2, sc2, n2g, router_w[0].T.astype(BF16), heads, dk, t["mixer"])

    pos, before, idx = _route(aff_t, cap)
    ye = _experts(h2, idx, exp_w_gate[0], exp_w_up[0], exp_w_down[0], cap,
                  cap + _combine_windows(t["combine"], n_exp)[1], t["expert_f"])
    return _combine(ye, pos, aff_t, before, x1, gt2, final_g.reshape(1, d), cap, t["combine"])
```

```python
import functools

import numpy as np
import jax
import jax.numpy as jnp
from jax import lax
from jax.experimental import pallas as pl
from jax.experimental.pallas import tpu as pltpu

F32 = jnp.float32
BF16 = jnp.bfloat16
I32 = jnp.int32

EPS = 1e-6
N_MOD = 6
GRID_W = 64
GLA_CHUNK = 64
GLA_GATE_NORM = 16.0
GLA_SCAN_LAG = 0
EC_CAPACITY = 2

LANES = 128
MXU_DEPTH = 256
BF16_ROWS = 16
VMEM_LIMIT = 56 * 1024 * 1024


def _cparams(*sem, **kw):
    return pltpu.CompilerParams(dimension_semantics=sem, vmem_limit_bytes=VMEM_LIMIT, **kw)


def _dot(a, b):
    return jnp.dot(a, b, preferred_element_type=F32)


def _dot_nt(a, b):
    return lax.dot_general(a, b, (((1,), (1,)), ((), ())), preferred_element_type=F32)


def _dot_tn(a, b):
    return lax.dot_general(a, b, (((0,), (0,)), ((), ())), preferred_element_type=F32)


def _sigmoid(v):
    return 1.0 / (1.0 + jnp.exp(-v))


def _log_sigmoid(v):
    return jnp.minimum(v, 0.0) - jnp.log(1.0 + jnp.exp(-jnp.abs(v)))


def _rms_mod(xv, g, shift, scale):
    ms = jnp.mean(xv * xv, axis=-1, keepdims=True)
    y = xv * lax.rsqrt(ms + EPS) * g
    return y * (1.0 + scale) + shift


def _split_bf16(v):
    hi = v.astype(BF16)
    lo = (v - hi.astype(F32)).astype(BF16)
    return hi, lo


def _adaln_kernel(c_ref, w_ref, b_ref, o_ref):
    cv = c_ref[...]
    s = (cv * _sigmoid(cv)).astype(BF16)
    o_ref[...] = _dot(s, w_ref[...].astype(BF16)) + b_ref[...]


def _adaln(cond, w, b):
    rows, d = cond.shape
    nout = w.shape[1]
    tn = d
    return pl.pallas_call(
        _adaln_kernel,
        grid=(nout // tn,),
        in_specs=[pl.BlockSpec((rows, d), lambda j: (0, 0)),
                  pl.BlockSpec((d, tn), lambda j: (0, j)),
                  pl.BlockSpec((1, tn), lambda j: (0, j))],
        out_specs=pl.BlockSpec((rows, tn), lambda j: (0, j)),
        out_shape=jax.ShapeDtypeStruct((rows, nout), F32),
        compiler_params=_cparams("arbitrary"),
        name="adaln",
    )(cond, w, b.reshape(1, nout))


def _wprep_kernel(tiles, wt_hbm, o_ref, buf, sem):
    def copy(t):
        row, cnt = tiles[t]
        return pltpu.make_async_copy(wt_hbm.at[pl.ds(row, cnt), :],
                                     buf.at[t % 2, pl.ds(0, cnt), :], sem.at[t % 2])

    copy(0).start()
    out_row = 0
    for t, (_, cnt) in enumerate(tiles):
        if t + 1 < len(tiles):
            copy(t + 1).start()
        copy(t).wait()
        o_ref[out_row:out_row + cnt, :] = buf[t % 2, 0:cnt, :].astype(BF16)
        out_row += cnt


def _wprep(wt, tiles):
    d = wt.shape[1]
    rows = sum(cnt for _, cnt in tiles)
    return pl.pallas_call(
        functools.partial(_wprep_kernel, tiles),
        in_specs=[pl.BlockSpec(memory_space=pl.ANY)],
        out_specs=pl.BlockSpec((rows, d), lambda: (0, 0)),
        out_shape=jax.ShapeDtypeStruct((rows, d), BF16),
        scratch_shapes=[pltpu.VMEM((2, max(cnt for _, cnt in tiles), d), F32),
                        pltpu.SemaphoreType.DMA((2,))],
        compiler_params=pltpu.CompilerParams(vmem_limit_bytes=VMEM_LIMIT),
        name="wprep",
    )(wt)


def _ctx_kernel(heads, ctx_ref, sh_ref, sc_ref, g_ref, wk_ref, wv_ref, wlr_ref,
                wupf_ref, wupb_ref, baf_ref, bab_ref, sf_ref, sb_ref):
    n = ctx_ref.shape[0]
    hc = _rms_mod(ctx_ref[...], g_ref[...], sh_ref[...], sc_ref[...]).astype(BF16)
    k = _dot_nt(hc, wk_ref[...])
    v = _dot_nt(hc, wv_ref[...]).astype(BF16)
    lr = _dot_nt(hc, wlr_ref[...]).astype(BF16)
    hk = k.shape[1] // heads
    hv = v.shape[1] // heads
    row = lax.broadcasted_iota(I32, (n, n), 0)
    col = lax.broadcasted_iota(I32, (n, n), 1)
    for wup_ref, ba_ref, s_ref, tri, last in (
            (wupf_ref, baf_ref, sf_ref, col <= row, n - 1),
            (wupb_ref, bab_ref, sb_ref, col >= row, 0)):
        la = _log_sigmoid(_dot(lr, wup_ref[...]) + ba_ref[...]) * (1.0 / GLA_GATE_NORM)
        hi, lo = _split_bf16(la)
        t = jnp.where(tri, 1.0, 0.0).astype(BF16)
        b = _dot(t, hi) + _dot(t, lo)
        kd = (k * jnp.exp(b[last:last + 1, :] - b)).astype(BF16)
        for h in range(heads):
            s_ref[h] = _dot_tn(v[:, h * hv:(h + 1) * hv], kd[:, h * hk:(h + 1) * hk])


def _ctx_states(ctx, sh, sc, g, wt, wupf, wupb, baf, bab, heads, dk, dv):
    bsz, n, d = ctx.shape
    lr_blk = (wt.shape[0] - LANES) // LANES
    hk, hv = dk // heads, dv // heads
    full = lambda *shape: pl.BlockSpec(shape, lambda b: (0,) * len(shape))
    st = jax.ShapeDtypeStruct((bsz, heads, hv, hk), F32)
    sspec = pl.BlockSpec((None, heads, hv, hk), lambda b: (b, 0, 0, 0))
    return pl.pallas_call(
        functools.partial(_ctx_kernel, heads),
        grid=(bsz,),
        in_specs=[pl.BlockSpec((None, n, d), lambda b: (b, 0, 0)),
                  full(1, d), full(1, d), full(1, d),
                  pl.BlockSpec((dk, d), lambda b: (1, 0)),
                  pl.BlockSpec((dv, d), lambda b: (2 * dk // dv, 0)),
                  pl.BlockSpec((LANES, d), lambda b: (lr_blk, 0)),
                  full(LANES, dk), full(LANES, dk), full(1, dk), full(1, dk)],
        out_specs=(sspec, sspec),
        out_shape=(st, st),
        compiler_params=_cparams("arbitrary"),
        name="ctx_state",
    )(ctx, sh, sc, g, wt, wt, wt, wupf, wupb, baf, bab)


def _inproj_kernel(x_ref, sh_ref, sc_ref, g_ref, w_ref, wlr_ref, z_ref, lr_ref, h_scr):
    j = pl.program_id(2)
    d = x_ref.shape[1]

    @pl.when(j == 0)
    def _():
        hb = _rms_mod(x_ref[...], g_ref[...], sh_ref[...], sc_ref[...]).astype(BF16)
        h_scr[...] = hb
        lr_ref[...] = _dot_nt(hb, wlr_ref[...])
        z_ref[...] = _dot_nt(hb, w_ref[...]).astype(BF16)

    @pl.when(j == 1)
    def _():
        a = _dot_nt(h_scr[...], w_ref[...])
        gv = a[:, :d]
        z_ref[:, :d] = (gv * _sigmoid(gv)).astype(BF16)
        z_ref[:, d:] = a[:, d:].astype(BF16)

    @pl.when(j == 2)
    def _():
        a = _dot_nt(h_scr[...], w_ref[...])
        z_ref[:, :d] = (a[:, d:] * a[:, :d]).astype(BF16)
        z_ref[:, d:] = jnp.zeros((z_ref.shape[0], d), BF16)

    @pl.when(j == 3)
    def _():
        z_ref[...] = _sigmoid(_dot_nt(h_scr[...], w_ref[...])).astype(BF16)


def _inproj(x, sh, sc, g, wt, tm):
    bsz, n, d = x.shape
    p = wt.shape[0] - LANES
    tn = 2 * d
    assert p == 4 * tn
    return pl.pallas_call(
        _inproj_kernel,
        grid=(bsz, n // tm, p // tn),
        in_specs=[pl.BlockSpec((None, tm, d), lambda b, i, j: (b, i, 0)),
                  pl.BlockSpec((None, 1, d), lambda b, i, j: (b, 0, 0)),
                  pl.BlockSpec((None, 1, d), lambda b, i, j: (b, 0, 0)),
                  pl.BlockSpec((1, d), lambda b, i, j: (0, 0)),
                  pl.BlockSpec((tn, d), lambda b, i, j: (j, 0)),
                  pl.BlockSpec((LANES, d), lambda b, i, j: (p // LANES, 0))],
        out_specs=(pl.BlockSpec((None, tm, tn), lambda b, i, j: (b, i, j)),
                   pl.BlockSpec((None, tm, LANES), lambda b, i, j: (b, i, 0))),
        out_shape=(jax.ShapeDtypeStruct((bsz, n, p), BF16),
                   jax.ShapeDtypeStruct((bsz, n, LANES), F32)),
        scratch_shapes=[pltpu.VMEM((tm, d), BF16)],
        compiler_params=_cparams("arbitrary", "arbitrary", "arbitrary"),
        name="inproj",
    )(x, sh, sc, g, wt, wt)


def _gla_kernel(heads, dk, qkvf_ref, lrf_ref, qkvb_ref, lrb_ref,
                wupf_ref, wupb_ref, baf_ref, bab_ref, trif_ref, trib_ref, s0f_ref, s0b_ref,
                of_ref, ob_ref, sf_scr, sb_scr):
    tb = qkvf_ref.shape[0]
    dv = qkvf_ref.shape[1] - 2 * dk
    hk, hv = dk // heads, dv // heads
    ck = GLA_CHUNK
    nck = tb // ck
    q_scale = hk ** -0.5

    @pl.when(pl.program_id(1) == 0)
    def _():
        sf_scr[...] = s0f_ref[...]
        sb_scr[...] = s0b_ref[...]

    crow = lax.broadcasted_iota(I32, (ck, ck), 0)
    ccol = lax.broadcasted_iota(I32, (ck, ck), 1)

    dirs = (
        (qkvf_ref, lrf_ref, wupf_ref, baf_ref, of_ref, sf_scr,
         trif_ref, ccol <= crow, ck - 1, range(nck)),
        (qkvb_ref, lrb_ref, wupb_ref, bab_ref, ob_ref, sb_scr,
         trib_ref, ccol >= crow, 0, range(nck - 1, -1, -1)),
    )
    b_alls = []
    for _, lr_ref, wup_ref, ba_ref, _, _, tri, _, _, _ in dirs:
        la = _log_sigmoid(_dot(lr_ref[...].astype(BF16), wup_ref[...]) + ba_ref[...])
        la = la * (1.0 / GLA_GATE_NORM)
        hi, lo = _split_bf16(la)
        t = tri[...]
        tr = t.shape[0]
        b_alls.append(jnp.concatenate(
            [_dot(t, hi[r:r + tr]) + _dot(t, lo[r:r + tr]) for r in range(0, tb, tr)], axis=0))
    work = [[], []]

    def stage1(k):
        for di, (qkv_ref, _, _, _, _, _, _, cmask, last, order) in enumerate(dirs):
            r0 = order[k] * ck
            b = b_alls[di][r0:r0 + ck, :]
            b_last = b[last:last + 1, :]
            qc = qkv_ref[r0:r0 + ck, 0:dk].astype(F32) * q_scale
            kc = qkv_ref[r0:r0 + ck, dk:2 * dk].astype(F32)
            q_in = (qc * jnp.exp(b)).astype(BF16)
            k_in = (kc * jnp.exp(-b)).astype(BF16)
            k_dec = (kc * jnp.exp(b_last - b)).astype(BF16)
            decay = jnp.exp(b_last)
            units = []
            for h in range(heads):
                qh = q_in[:, h * hk:(h + 1) * hk]
                vh = qkv_ref[r0:r0 + ck, 2 * dk + h * hv:2 * dk + (h + 1) * hv]
                scores = jnp.where(cmask, _dot_nt(qh, k_in[:, h * hk:(h + 1) * hk]), 0.0)
                o_intra = _dot(scores.astype(BF16), vh)
                ut = _dot_tn(vh, k_dec[:, h * hk:(h + 1) * hk])
                units.append((qh, o_intra, ut, decay[:, h * hk:(h + 1) * hk]))
            work[di].append((r0, units))
    def scan(k):
        for di, (_, _, _, _, o_ref, s_scr, _, _, _, _) in enumerate(dirs):
            r0, units = work[di][k]
            for h, (qh, o_intra, ut, dec) in enumerate(units):
                st = s_scr[h]
                o = o_intra + _dot_nt(qh, st.astype(BF16))
                o_ref[r0:r0 + ck, h * hv:(h + 1) * hv] = o.astype(BF16)
                s_scr[h] = st * dec + ut

    for k in range(nck + GLA_SCAN_LAG):
        if k < nck:
            stage1(k)
        if k >= GLA_SCAN_LAG:
            scan(k - GLA_SCAN_LAG)


def _gla(z, lr, wupf, wupb, baf, bab, s0f, s0b, heads, dk, dv, tb):
    bsz, n, _ = z.shape
    nb = n // tb
    hk, hv = dk // heads, dv // heads
    fwd = lambda cb: (lambda b, i: (b, i, cb))
    bwd = lambda cb: (lambda b, i: (b, nb - 1 - i, cb))
    full = lambda *shape: pl.BlockSpec(shape, lambda b, i: (0,) * len(shape))
    sspec = pl.BlockSpec((None, heads, hv, hk), lambda b, i: (b, 0, 0, 0))
    ost = jax.ShapeDtypeStruct((bsz, n, dv), BF16)
    tr = min(tb, MXU_DEPTH)
    assert tb % tr == 0 and tr % GLA_CHUNK == 0
    ri = np.arange(tr)
    same_chunk = (ri[:, None] // GLA_CHUNK) == (ri[None, :] // GLA_CHUNK)
    trif = jnp.asarray(same_chunk & (ri[None, :] <= ri[:, None]), BF16)
    trib = jnp.asarray(same_chunk & (ri[None, :] >= ri[:, None]), BF16)
    qkv = 2 * dk + dv
    return pl.pallas_call(
        functools.partial(_gla_kernel, heads, dk),
        grid=(bsz, nb),
        in_specs=[pl.BlockSpec((None, tb, qkv), fwd(0)), pl.BlockSpec((None, tb, LANES), fwd(0)),
                  pl.BlockSpec((None, tb, qkv), bwd(0)), pl.BlockSpec((None, tb, LANES), bwd(0)),
                  full(LANES, dk), full(LANES, dk), full(1, dk), full(1, dk),
                  full(tr, tr), full(tr, tr), sspec, sspec],
        out_specs=(pl.BlockSpec((None, tb, dv), fwd(0)), pl.BlockSpec((None, tb, dv), bwd(0))),
        out_shape=(ost, ost),
        scratch_shapes=[pltpu.VMEM((heads, hv, hk), F32), pltpu.VMEM((heads, hv, hk), F32)],
        compiler_params=_cparams("arbitrary", "arbitrary"),
        name="gla",
    )(z, lr, z, lr, wupf, wupb, baf, bab, trif, trib, s0f, s0b)


def _mixer_kernel(heads, of_ref, ob_ref, sg_ref, bg_ref, cu_ref, srg_ref, src_ref, x_ref,
                  gng_ref, cw_ref, wo_ref, wco_ref, wm_ref, gt1_ref, sh2_ref, sc2_ref, n2g_ref,
                  rwt_ref, x1_ref, h2_ref, aff_ref):
    tm, dv = of_ref.shape
    hv = dv // heads
    o = of_ref[...].astype(F32) + ob_ref[...].astype(F32)
    parts = []
    for h in range(heads):
        oh = o[:, h * hv:(h + 1) * hv]
        ms = jnp.mean(oh * oh, axis=-1, keepdims=True)
        parts.append(oh * lax.rsqrt(ms + EPS))
    on = jnp.concatenate(parts, axis=1) * gng_ref[...]
    y_gla = _dot((on * sg_ref[...].astype(F32)).astype(BF16), wo_ref[...])

    cu = cu_ref[...].astype(F32)
    gcol = lax.broadcasted_iota(I32, cu.shape, 0) & (GRID_W - 1)
    left = jnp.where(gcol == 0, 0.0, pltpu.roll(cu, 1, 0))
    right = jnp.where(gcol == GRID_W - 1, 0.0, pltpu.roll(cu, tm - 1, 0))
    cw = cw_ref[...]
    conv = left * cw[0:1, :] + cu * cw[1:2, :] + right * cw[2:3, :]
    y_conv = _dot((bg_ref[...].astype(F32) * conv).astype(BF16), wco_ref[...])

    merged = srg_ref[...].astype(F32) * y_gla + src_ref[...].astype(F32) * y_conv
    y = _dot(merged.astype(BF16), wm_ref[...])
    x1 = x_ref[...] + gt1_ref[...] * y
    x1_ref[...] = x1
    h2 = _rms_mod(x1, n2g_ref[...], sh2_ref[...], sc2_ref[...])
    rpt = h2.shape[1] // LANES
    for a in range(rpt):
        h2_ref[pl.ds(a, tm, stride=rpt), :] = h2[:, a * LANES:(a + 1) * LANES]
    logits = _dot_nt(rwt_ref[...], h2.astype(BF16))
    ex = jnp.exp(logits - jnp.max(logits, axis=0, keepdims=True))
    aff_ref[...] = ex / jnp.sum(ex, axis=0, keepdims=True)


def _mixer(o_f, o_b, z, x, gng, conv_w, w_o, w_co, w_m, gt1, sh2, sc2, n2g, rwt, heads, dk, tm):
    bsz, n, d = x.shape
    dv = o_f.shape[2]
    c = conv_w.shape[1]
    e = rwt.shape[0]
    zb = lambda cb: pl.BlockSpec((None, tm, d), lambda b, i: (b, i, cb))
    base = (2 * dk + dv) // d
    tok = lambda w: pl.BlockSpec((None, tm, w), lambda b, i: (b, i, 0))
    full = lambda *shape: pl.BlockSpec(shape, lambda b, i: (0,) * len(shape))
    perb = pl.BlockSpec((None, 1, d), lambda b, i: (b, 0, 0))
    return pl.pallas_call(
        functools.partial(_mixer_kernel, heads),
        grid=(bsz, n // tm),
        in_specs=[tok(dv), tok(dv), zb(base), zb(base + 1), zb(base + 2),
                  zb(base + 4), zb(base + 5), tok(d),
                  full(1, dv), full(3, c), full(dv, d), full(c, d), full(d, d),
                  perb, perb, perb, full(1, d), full(e, d)],
        out_specs=(tok(d), pl.BlockSpec((None, tm * (d // LANES), LANES), lambda b, i: (b, i, 0)),
                   pl.BlockSpec((None, e, tm), lambda b, i: (b, 0, i))),
        out_shape=(jax.ShapeDtypeStruct((bsz, n, d), F32),
                   jax.ShapeDtypeStruct((bsz, n * (d // LANES), LANES), F32),
                   jax.ShapeDtypeStruct((bsz, e, n), F32)),
        compiler_params=_cparams("arbitrary", "arbitrary"),
        name="mixer",
    )(o_f, o_b, z, z, z, z, z, x, gng, conv_w, w_o, w_co, w_m, gt1, sh2, sc2, n2g, rwt)


def _route_kernel(cap, aff_ref, lmat_ref, pos_ref, base_ref, idx_ref, loc_scr, bef_scr):
    e, n = aff_ref.shape
    nt = n // LANES
    aff = aff_ref[...]

    def count(mask):
        return jnp.sum(jnp.where(mask, 1.0, 0.0), axis=1, keepdims=True)

    def search(k, tbits):
        cand = tbits | jnp.left_shift(jnp.int32(1), 30 - k)
        ok = count(aff >= lax.bitcast_convert_type(cand, F32)) >= cap
        return jnp.where(ok, cand, tbits)

    tbits = lax.fori_loop(0, 31, search, jnp.zeros((e, 1), I32))
    thr = lax.bitcast_convert_type(tbits, F32)
    gt = aff > thr
    eq = aff == thr
    need = cap - count(gt)

    def stack(mask):
        m = jnp.where(mask, 1.0, 0.0)
        return jnp.concatenate([m[:, j * LANES:(j + 1) * LANES] for j in range(nt)], axis=0)

    r = lax.broadcasted_iota(I32, (LANES, LANES), 0)
    cl = lax.broadcasted_iota(I32, (LANES, LANES), 1)
    upper = jnp.where(r <= cl, 1.0, 0.0).astype(BF16)
    ones = jnp.ones((LANES, LANES), BF16)
    lmat = lmat_ref[...]

    def cumsum(ms):
        msb = ms.astype(BF16)
        before = _dot(_dot(lmat, msb).astype(BF16), ones)
        return _dot(msb, upper), before

    eq_s = stack(eq)
    loc_eq, before_eq = cumsum(eq_s)
    need_s = jnp.concatenate([need] * nt, axis=0)
    sel = jnp.maximum(stack(gt), jnp.where(loc_eq + before_eq <= need_s, eq_s, 0.0))
    loc, before = cumsum(sel)
    pos_ref[...] = jnp.where(sel > 0.0, loc + before - 1.0, -1.0).astype(I32)
    base_ref[...] = before.astype(I32)

    loc_scr[...] = loc
    bef_scr[...] = before
    slot = lax.broadcasted_iota(I32, (1, cap), 1).astype(F32)
    for ex in range(e):
        loc_e = loc_scr[pl.ds(ex, nt, stride=e), :]
        tprev = bef_scr[pl.ds(ex, nt, stride=e), :][:, 0:1]
        tincl = tprev + loc_e[:, LANES - 1:LANES]
        in_tile = jnp.where((tprev <= slot) & (slot < tincl), 1.0, 0.0)
        tile = jnp.sum(jnp.where(tincl <= slot, 1.0, 0.0), axis=0, keepdims=True)
        s_loc = slot - jnp.sum(in_tile * tprev, axis=0, keepdims=True)
        counts = _dot_tn(loc_e.astype(BF16), in_tile.astype(BF16))
        lane = jnp.sum(jnp.where(counts <= s_loc, 1.0, 0.0), axis=0, keepdims=True)
        idx_ref[pl.ds(ex, 1), :] = (tile * LANES + lane).astype(I32)


def _route(aff_t, cap):
    bsz, e, n = aff_t.shape
    nt = n // LANES
    rows = nt * e
    ri = np.arange(rows)
    lmat = ((ri[None, :] % e == ri[:, None] % e) & (ri[None, :] // e < ri[:, None] // e))
    lmat = jnp.asarray(lmat, BF16)
    st = jax.ShapeDtypeStruct((bsz, rows, LANES), I32)
    ospec = pl.BlockSpec((None, rows, LANES), lambda b: (b, 0, 0))
    return pl.pallas_call(
        functools.partial(_route_kernel, cap),
        grid=(bsz,),
        in_specs=[pl.BlockSpec((None, e, n), lambda b: (b, 0, 0)),
                  pl.BlockSpec((rows, rows), lambda b: (0, 0))],
        out_specs=(ospec, ospec, pl.BlockSpec((None, e, cap), lambda b: (b, 0, 0))),
        out_shape=(st, st, jax.ShapeDtypeStruct((bsz, e, cap), I32)),
        scratch_shapes=[pltpu.VMEM((rows, LANES), F32), pltpu.VMEM((rows, LANES), F32)],
        compiler_params=_cparams("arbitrary"),
        name="route",
    )(aff_t, lmat)


def _slot_window(base):
    return pl.multiple_of((base // BF16_ROWS) * BF16_ROWS, BF16_ROWS)


def _expert_kernel(cap, idx_ref, h2_hbm, wg_ref, wu_ref, wd_ref, ye_ref, xbuf, acc, sem):
    e, f = pl.program_id(1), pl.program_id(2)
    n_exp, nf = pl.num_programs(1), pl.num_programs(2)
    last = pl.num_programs(0) * n_exp - 1
    lin = pl.program_id(0) * n_exp + e
    slot = lin % 2
    per_step = cap // nf
    rpt = wg_ref.shape[0] // LANES

    def row_copy(lin_t, part, i, slot_t):
        tok = idx_ref[(lin_t * nf + part) * per_step + i]
        return pltpu.make_async_copy(h2_hbm.at[pl.ds(pl.multiple_of(tok * rpt, rpt), rpt), :],
                                     xbuf.at[slot_t, part, pl.ds(i * rpt, rpt), :],
                                     sem.at[slot_t])

    def wait_rows(slot_t):
        pltpu.make_async_copy(xbuf.at[slot_t], xbuf.at[slot_t], sem.at[slot_t]).wait()

    @pl.when((lin == 0) & (f == 0))
    def _():
        acc[...] = jnp.zeros_like(acc)
        for part in range(nf):
            def first(i, carry):
                row_copy(lin, part, i, slot).start()
                return carry
            lax.fori_loop(0, per_step, first, 0)

    @pl.when(f == 0)
    def _():
        wait_rows(slot)

    nxt = jnp.minimum(lin + 1, last)
    for i in range(per_step):
        row_copy(nxt, f, i, 1 - slot).start()

    @pl.when((lin == last) & (f == nf - 1))
    def _():
        wait_rows(1 - slot)

    xv = jnp.concatenate(
        [jnp.concatenate([xbuf[slot, part, pl.ds(a, per_step, stride=rpt), :].astype(BF16)
                          for a in range(rpt)], axis=1) for part in range(nf)], axis=0)
    hg = _dot(xv, wg_ref[...].astype(BF16))
    hu = _dot(xv, wu_ref[...].astype(BF16))
    hid = (hg * _sigmoid(hg) * hu).astype(BF16)
    acc[...] = jnp.where(f > 0, acc[...], 0.0) + _dot(hid, wd_ref[...].astype(BF16))

    @pl.when(f == nf - 1)
    def _():
        ye_ref[0:cap, :] = acc[...].astype(BF16)
        ye_ref[cap:, :] = jnp.zeros((ye_ref.shape[0] - cap, ye_ref.shape[1]), BF16)


def _experts(h2, idx, w_gate, w_up, w_down, cap, rows, tf):
    n_exp, d, df = w_gate.shape
    bsz = h2.shape[0]
    rpt = d // LANES
    n = h2.shape[1] // rpt
    nf = df // tf
    assert cap % nf == 0
    rows_global = (idx + (jnp.arange(bsz, dtype=I32) * n)[:, None, None]).reshape(-1)
    return pl.pallas_call(
        functools.partial(_expert_kernel, cap),
        grid_spec=pltpu.PrefetchScalarGridSpec(
            num_scalar_prefetch=1,
            grid=(bsz, n_exp, nf),
            in_specs=[pl.BlockSpec(memory_space=pl.ANY),
                      pl.BlockSpec((None, d, tf), lambda b, e, f, s: (e, 0, f)),
                      pl.BlockSpec((None, d, tf), lambda b, e, f, s: (e, 0, f)),
                      pl.BlockSpec((None, tf, d), lambda b, e, f, s: (e, f, 0))],
            out_specs=pl.BlockSpec((None, None, rows, d), lambda b, e, f, s: (b, e, 0, 0)),
            scratch_shapes=[pltpu.VMEM((2, nf, (cap // nf) * rpt, LANES), F32),
                            pltpu.VMEM((cap, d), F32), pltpu.SemaphoreType.DMA((2,))]),
        out_shape=jax.ShapeDtypeStruct((bsz, n_exp, rows, d), BF16),
        compiler_params=_cparams("arbitrary", "arbitrary", "arbitrary"),
        name="experts",
    )(rows_global, h2.reshape(bsz * n * rpt, LANES), w_gate, w_up, w_down)


def _combine_windows(tm, n_exp):
    mean = tm * EC_CAPACITY / n_exp
    usual = mean + 5.5 * (mean * (1 - EC_CAPACITY / n_exp)) ** 0.5 + BF16_ROWS - 1
    usual = min(-(-int(usual) // BF16_ROWS) * BF16_ROWS, tm + BF16_ROWS)
    return usual, tm + BF16_ROWS


def _combine_kernel(n_exp, nsub, base_ref, ovf_ref, pos_ref, aff_ref, x1_ref, gt2_ref, fg_ref,
                    ye_hbm, out_ref, buf, fbuf, ffn_scr, sem, fsem):
    b, j = pl.program_id(0), pl.program_id(1)
    nt = pl.num_programs(1)
    step = b * nt + j
    slot = step % 2
    wp, wfull = buf.shape[2], fbuf.shape[0]
    d = buf.shape[-1]

    def start_of(tile, ex):
        return _slot_window(base_ref[tile * nsub * n_exp + ex])

    def window_copy(tile, ex, dst_slot):
        return pltpu.make_async_copy(ye_hbm.at[tile // nt, ex, pl.ds(start_of(tile, ex), wp), :],
                                     buf.at[dst_slot, ex], sem.at[dst_slot])

    def full_copy(ex):
        return pltpu.make_async_copy(ye_hbm.at[b, ex, pl.ds(start_of(step, ex), wfull), :],
                                     fbuf, fsem)

    @pl.when(step == 0)
    def _():
        for ex in range(n_exp):
            window_copy(step, ex, slot).start()

    @pl.when(step + 1 < pl.num_programs(0) * nt)
    def _():
        for ex in range(n_exp):
            window_copy(step + 1, ex, 1 - slot).start()

    def gates_t(rows, ex):
        srow = lax.broadcasted_iota(I32, (rows, LANES), 0)
        s0 = start_of(step, ex)
        parts = []
        for u in range(nsub):
            rel = pos_ref[pl.ds(u * n_exp + ex, 1), :] - s0
            gate = aff_ref[pl.ds(ex, 1), u * LANES:(u + 1) * LANES]
            parts.append(jnp.where(srow == rel, gate, 0.0).astype(BF16))
        return jnp.concatenate(parts, axis=1)

    for ex in range(n_exp):
        window_copy(step, ex, slot).wait()

    @pl.when(ovf_ref[step] == 0)
    def _():
        g_all = jnp.concatenate([gates_t(wp, ex) for ex in range(n_exp)], axis=0)
        ffn_scr[...] = _dot_tn(g_all, buf[slot].reshape(n_exp * wp, d))

    @pl.when(ovf_ref[step] != 0)
    def _():
        ffn_scr[...] = jnp.zeros_like(ffn_scr)
        for ex in range(n_exp):
            full_copy(ex).start()
            full_copy(ex).wait()
            ffn_scr[...] += _dot_tn(gates_t(wfull, ex), fbuf[...])

    x2 = x1_ref[...] + gt2_ref[...] * ffn_scr[...]
    ms = jnp.mean(x2 * x2, axis=-1, keepdims=True)
    out_ref[...] = x2 * lax.rsqrt(ms + EPS) * fg_ref[...]


def _combine(ye, pos, aff_t, before, x1, gt2, final_g, cap, tm):
    bsz, n, d = x1.shape
    n_exp = aff_t.shape[1]
    nsub = tm // LANES
    nt = n // tm
    wp, wfull = _combine_windows(tm, n_exp)
    starts = before[:, :, 0].reshape(bsz, n // LANES, n_exp)
    tile_starts = starts[:, ::nsub]
    tile_ends = jnp.concatenate([tile_starts[:, 1:], jnp.full((bsz, 1, n_exp), cap, I32)], axis=1)
    overflow = (tile_starts % BF16_ROWS) + (tile_ends - tile_starts) > wp
    ovf = jnp.any(overflow, axis=-1).astype(I32).reshape(-1)
    return pl.pallas_call(
        functools.partial(_combine_kernel, n_exp, nsub),
        grid_spec=pltpu.PrefetchScalarGridSpec(
            num_scalar_prefetch=2,
            grid=(bsz, nt),
            in_specs=[pl.BlockSpec((None, nsub * n_exp, LANES), lambda b, j, s, o: (b, j, 0)),
                      pl.BlockSpec((None, n_exp, tm), lambda b, j, s, o: (b, 0, j)),
                      pl.BlockSpec((None, tm, d), lambda b, j, s, o: (b, j, 0)),
                      pl.BlockSpec((None, 1, d), lambda b, j, s, o: (b, 0, 0)),
                      pl.BlockSpec((1, d), lambda b, j, s, o: (0, 0)),
                      pl.BlockSpec(memory_space=pl.ANY)],
            out_specs=pl.BlockSpec((None, tm, d), lambda b, j, s, o: (b, j, 0)),
            scratch_shapes=[pltpu.VMEM((2, n_exp, wp, d), BF16),
                            pltpu.VMEM((wfull, d), BF16),
                            pltpu.VMEM((tm, d), F32),
                            pltpu.SemaphoreType.DMA((2,)), pltpu.SemaphoreType.DMA(())]),
        out_shape=jax.ShapeDtypeStruct((bsz, n, d), F32),
        compiler_params=_cparams("arbitrary", "arbitrary"),
        name="combine",
    )(starts.reshape(-1), ovf, pos, aff_t, x1, gt2, final_g, ye)


def _pick(n, pref):
    t = min(n, pref)
    while n % t:
        t //= 2
    return t


def _tiles(n, d_expert):
    return dict(inproj=_pick(n, 1024), gla=_pick(n, 512), mixer=_pick(n, 512),
                combine=_pick(n, 512), expert_f=_pick(d_expert, 512))


def kernel(x, c, ctx, c_ctx, w_ada, b_ada, norm1_g, norm2_g, w_in, gla_w_a_up, gla_b_a,
           gla_norm_g, gla_w_o, conv_w, conv_w_out, merge_w_out, router_w,
           exp_w_gate, exp_w_up, exp_w_down, final_g):
    bsz, n, d = x.shape
    depth = w_ada.shape[0]
    assert depth == 1, "single-layer trunk"
    rank, dk = gla_w_a_up.shape[-2:]
    dv = gla_w_o.shape[1]
    hv = gla_norm_g.shape[-1]
    heads = dv // hv
    cch = conv_w.shape[-1]
    n_exp = router_w.shape[-1]
    cap = EC_CAPACITY * n // n_exp
    assert 2 * dk == d and dv == d and cch == d and 2 * rank <= LANES
    assert n % LANES == 0 and n % GRID_W == 0 and bsz + 1 <= 8 and n_exp % 8 == 0

    cond = jnp.zeros((8, d), F32).at[:bsz].set(c).at[bsz].set(c_ctx)
    mod = _adaln(cond, w_ada[0], b_ada[0]).reshape(8, N_MOD, d)
    sh1, sc1, gt1, sh2, sc2, gt2 = [mod[:bsz, k][:, None, :] for k in range(N_MOD)]
    csh1, csc1 = mod[bsz:bsz + 1, 0], mod[bsz:bsz + 1, 1]

    lr0 = 2 * dk + 2 * dv
    u0 = lr0 + 2 * rank
    groups = ((0, lr0), (u0 + cch, cch), (u0, cch), (u0 + 2 * cch, cch + 2 * d), (lr0, LANES))
    tiles = tuple((r0 + o, min(d, cnt - o)) for r0, cnt in groups for o in range(0, cnt, d))
    wt = _wprep(jnp.swapaxes(w_in[0], 0, 1), tiles)
    wup = gla_w_a_up[0].astype(BF16)
    wupf = jnp.zeros((LANES, dk), BF16).at[:rank].set(wup[0])
    wupb = jnp.zeros((LANES, dk), BF16).at[rank:2 * rank].set(wup[1])
    baf, bab = gla_b_a[0, 0:1], gla_b_a[0, 1:2]
    n1g, n2g = norm1_g[0:1], norm2_g[0:1]

    t = _tiles(n, exp_w_gate.shape[-1])
    s0f, s0b = _ctx_states(ctx, csh1, csc1, n1g, wt, wupf, wupb, baf, bab, heads, dk, dv)
    z, lr = _inproj(x, sh1, sc1, n1g, wt, t["inproj"])
    o_f, o_b = _gla(z, lr, wupf, wupb, baf, bab, s0f, s0b, heads, dk, dv, t["gla"])
    x1, h2, aff_t = _mixer(
        o_f, o_b, z, x, jnp.tile(gla_norm_g[0:1], (1, heads)), conv_w[0],
        gla_w_o[0].astype(BF16), conv_w_out[0].astype(BF16), merge_w_out[0].astype(BF16),
        gt1, sh2, sc2, n2g, router_w[0].T.astype(BF16), heads, dk, t["mixer"])

    pos, before, idx = _route(aff_t, cap)
    ye = _experts(h2, idx, exp_w_gate[0], exp_w_up[0], exp_w_down[0], cap,
                  cap + _combine_windows(t["combine"], n_exp)[1], t["expert_f"])
    return _combine(ye, pos, aff_t, before, x1, gt2, final_g.reshape(1, d), cap, t["combine"])
```

```python
import functools

import numpy as np
import jax
import jax.numpy as jnp
from jax import lax
from jax.experimental import pallas as pl
from jax.experimental.pallas import tpu as pltpu

F32 = jnp.float32
BF16 = jnp.bfloat16
I32 = jnp.int32

EPS = 1e-6
N_MOD = 6
GRID_W = 64
GLA_CHUNK = 64
GLA_GATE_NORM = 16.0
EC_CAPACITY = 2

LANES = 128
MXU_DEPTH = 256
BF16_ROWS = 16
VMEM_LIMIT = 56 * 1024 * 1024


def _cparams(*sem, **kw):
    return pltpu.CompilerParams(dimension_semantics=sem, vmem_limit_bytes=VMEM_LIMIT, **kw)


def _dot(a, b):
    return jnp.dot(a, b, preferred_element_type=F32)


def _dot_nt(a, b):
    return lax.dot_general(a, b, (((1,), (1,)), ((), ())), preferred_element_type=F32)


def _dot_tn(a, b):
    return lax.dot_general(a, b, (((0,), (0,)), ((), ())), preferred_element_type=F32)


def _sigmoid(v):
    return 1.0 / (1.0 + jnp.exp(-v))


def _log_sigmoid(v):
    return jnp.minimum(v, 0.0) - jnp.log(1.0 + jnp.exp(-jnp.abs(v)))


def _rms_mod(xv, g, shift, scale):
    ms = jnp.mean(xv * xv, axis=-1, keepdims=True)
    y = xv * lax.rsqrt(ms + EPS) * g
    return y * (1.0 + scale) + shift


def _split_bf16(v):
    hi = v.astype(BF16)
    lo = (v - hi.astype(F32)).astype(BF16)
    return hi, lo


def _adaln_kernel(c_ref, w_ref, b_ref, o_ref):
    cv = c_ref[...]
    s = (cv * _sigmoid(cv)).astype(BF16)
    o_ref[...] = _dot(s, w_ref[...].astype(BF16)) + b_ref[...]


def _adaln(cond, w, b):
    rows, d = cond.shape
    nout = w.shape[1]
    tn = d
    return pl.pallas_call(
        _adaln_kernel,
        grid=(nout // tn,),
        in_specs=[pl.BlockSpec((rows, d), lambda j: (0, 0)),
                  pl.BlockSpec((d, tn), lambda j: (0, j)),
                  pl.BlockSpec((1, tn), lambda j: (0, j))],
        out_specs=pl.BlockSpec((rows, tn), lambda j: (0, j)),
        out_shape=jax.ShapeDtypeStruct((rows, nout), F32),
        compiler_params=_cparams("arbitrary"),
        name="adaln",
    )(cond, w, b.reshape(1, nout))


def _wprep_kernel(tiles, wt_hbm, o_ref, buf, sem):
    def copy(t):
        row, cnt = tiles[t]
        return pltpu.make_async_copy(wt_hbm.at[pl.ds(row, cnt), :],
                                     buf.at[t % 2, pl.ds(0, cnt), :], sem.at[t % 2])

    copy(0).start()
    out_row = 0
    for t, (_, cnt) in enumerate(tiles):
        if t + 1 < len(tiles):
            copy(t + 1).start()
        copy(t).wait()
        o_ref[out_row:out_row + cnt, :] = buf[t % 2, 0:cnt, :].astype(BF16)
        out_row += cnt


def _wprep(wt, tiles):
    d = wt.shape[1]
    rows = sum(cnt for _, cnt in tiles)
    return pl.pallas_call(
        functools.partial(_wprep_kernel, tiles),
        in_specs=[pl.BlockSpec(memory_space=pl.ANY)],
        out_specs=pl.BlockSpec((rows, d), lambda: (0, 0)),
        out_shape=jax.ShapeDtypeStruct((rows, d), BF16),
        scratch_shapes=[pltpu.VMEM((2, max(cnt for _, cnt in tiles), d), F32),
                        pltpu.SemaphoreType.DMA((2,))],
        compiler_params=pltpu.CompilerParams(vmem_limit_bytes=VMEM_LIMIT),
        name="wprep",
    )(wt)


def _ctx_kernel(heads, ctx_ref, sh_ref, sc_ref, g_ref, wk_ref, wv_ref, wlr_ref,
                wupf_ref, wupb_ref, baf_ref, bab_ref, sf_ref, sb_ref):
    n = ctx_ref.shape[0]
    hc = _rms_mod(ctx_ref[...], g_ref[...], sh_ref[...], sc_ref[...]).astype(BF16)
    k = _dot_nt(hc, wk_ref[...])
    v = _dot_nt(hc, wv_ref[...]).astype(BF16)
    lr = _dot_nt(hc, wlr_ref[...]).astype(BF16)
    hk = k.shape[1] // heads
    hv = v.shape[1] // heads
    row = lax.broadcasted_iota(I32, (n, n), 0)
    col = lax.broadcasted_iota(I32, (n, n), 1)
    for wup_ref, ba_ref, s_ref, tri, last in (
            (wupf_ref, baf_ref, sf_ref, col <= row, n - 1),
            (wupb_ref, bab_ref, sb_ref, col >= row, 0)):
        la = _log_sigmoid(_dot(lr, wup_ref[...]) + ba_ref[...]) * (1.0 / GLA_GATE_NORM)
        hi, lo = _split_bf16(la)
        t = jnp.where(tri, 1.0, 0.0).astype(BF16)
        b = _dot(t, hi) + _dot(t, lo)
        kd = (k * jnp.exp(b[last:last + 1, :] - b)).astype(BF16)
        for h in range(heads):
            s_ref[h] = _dot_tn(v[:, h * hv:(h + 1) * hv], kd[:, h * hk:(h + 1) * hk])


def _ctx_states(ctx, sh, sc, g, wt, wupf, wupb, baf, bab, heads, dk, dv):
    bsz, n, d = ctx.shape
    lr_blk = (wt.shape[0] - LANES) // LANES
    hk, hv = dk // heads, dv // heads
    full = lambda *shape: pl.BlockSpec(shape, lambda b: (0,) * len(shape))
    st = jax.ShapeDtypeStruct((bsz, heads, hv, hk), F32)
    sspec = pl.BlockSpec((None, heads, hv, hk), lambda b: (b, 0, 0, 0))
    return pl.pallas_call(
        functools.partial(_ctx_kernel, heads),
        grid=(bsz,),
        in_specs=[pl.BlockSpec((None, n, d), lambda b: (b, 0, 0)),
                  full(1, d), full(1, d), full(1, d),
                  pl.BlockSpec((dk, d), lambda b: (1, 0)),
                  pl.BlockSpec((dv, d), lambda b: (2 * dk // dv, 0)),
                  pl.BlockSpec((LANES, d), lambda b: (lr_blk, 0)),
                  full(LANES, dk), full(LANES, dk), full(1, dk), full(1, dk)],
        out_specs=(sspec, sspec),
        out_shape=(st, st),
        compiler_params=_cparams("arbitrary"),
        name="ctx_state",
    )(ctx, sh, sc, g, wt, wt, wt, wupf, wupb, baf, bab)


def _inproj_kernel(x_ref, sh_ref, sc_ref, g_ref, w_ref, wlr_ref, z_ref, lr_ref, h_scr):
    j = pl.program_id(2)
    d = x_ref.shape[1]

    @pl.when(j == 0)
    def _():
        hb = _rms_mod(x_ref[...], g_ref[...], sh_ref[...], sc_ref[...]).astype(BF16)
        h_scr[...] = hb
        lr_ref[...] = _dot_nt(hb, wlr_ref[...])
        z_ref[...] = _dot_nt(hb, w_ref[...]).astype(BF16)

    @pl.when(j == 1)
    def _():
        a = _dot_nt(h_scr[...], w_ref[...])
        gv = a[:, :d]
        z_ref[:, :d] = (gv * _sigmoid(gv)).astype(BF16)
        z_ref[:, d:] = a[:, d:].astype(BF16)

    @pl.when(j == 2)
    def _():
        a = _dot_nt(h_scr[...], w_ref[...])
        z_ref[:, :d] = (a[:, d:] * a[:, :d]).astype(BF16)
        z_ref[:, d:] = jnp.zeros((z_ref.shape[0], d), BF16)

    @pl.when(j == 3)
    def _():
        z_ref[...] = _sigmoid(_dot_nt(h_scr[...], w_ref[...])).astype(BF16)


def _inproj(x, sh, sc, g, wt, tm):
    bsz, n, d = x.shape
    p = wt.shape[0] - LANES
    tn = 2 * d
    assert p == 4 * tn
    return pl.pallas_call(
        _inproj_kernel,
        grid=(bsz, n // tm, p // tn),
        in_specs=[pl.BlockSpec((None, tm, d), lambda b, i, j: (b, i, 0)),
                  pl.BlockSpec((None, 1, d), lambda b, i, j: (b, 0, 0)),
                  pl.BlockSpec((None, 1, d), lambda b, i, j: (b, 0, 0)),
                  pl.BlockSpec((1, d), lambda b, i, j: (0, 0)),
                  pl.BlockSpec((tn, d), lambda b, i, j: (j, 0)),
                  pl.BlockSpec((LANES, d), lambda b, i, j: (p // LANES, 0))],
        out_specs=(pl.BlockSpec((None, tm, tn), lambda b, i, j: (b, i, j)),
                   pl.BlockSpec((None, tm, LANES), lambda b, i, j: (b, i, 0))),
        out_shape=(jax.ShapeDtypeStruct((bsz, n, p), BF16),
                   jax.ShapeDtypeStruct((bsz, n, LANES), F32)),
        scratch_shapes=[pltpu.VMEM((tm, d), BF16)],
        compiler_params=_cparams("arbitrary", "arbitrary", "arbitrary"),
        name="inproj",
    )(x, sh, sc, g, wt, wt)


def _gla_kernel(heads, dk, qkvf_ref, lrf_ref, qkvb_ref, lrb_ref,
                wupf_ref, wupb_ref, baf_ref, bab_ref, trif_ref, trib_ref, s0f_ref, s0b_ref,
                of_ref, ob_ref, sf_scr, sb_scr):
    tb = qkvf_ref.shape[0]
    dv = qkvf_ref.shape[1] - 2 * dk
    hk, hv = dk // heads, dv // heads
    ck = GLA_CHUNK
    nck = tb // ck
    q_scale = hk ** -0.5

    @pl.when(pl.program_id(1) == 0)
    def _():
        sf_scr[...] = s0f_ref[...]
        sb_scr[...] = s0b_ref[...]

    crow = lax.broadcasted_iota(I32, (ck, ck), 0)
    ccol = lax.broadcasted_iota(I32, (ck, ck), 1)

    dirs = (
        (qkvf_ref, lrf_ref, wupf_ref, baf_ref, of_ref, sf_scr,
         trif_ref, ccol <= crow, ck - 1, range(nck)),
        (qkvb_ref, lrb_ref, wupb_ref, bab_ref, ob_ref, sb_scr,
         trib_ref, ccol >= crow, 0, range(nck - 1, -1, -1)),
    )
    b_alls = []
    for _, lr_ref, wup_ref, ba_ref, _, _, tri, _, _, _ in dirs:
        la = _log_sigmoid(_dot(lr_ref[...].astype(BF16), wup_ref[...]) + ba_ref[...])
        la = la * (1.0 / GLA_GATE_NORM)
        hi, lo = _split_bf16(la)
        t = tri[...]
        tr = t.shape[0]
        b_alls.append(jnp.concatenate(
            [_dot(t, hi[r:r + tr]) + _dot(t, lo[r:r + tr]) for r in range(0, tb, tr)], axis=0))
    for k in range(nck):
        for di, (qkv_ref, _, _, _, o_ref, s_scr, _, cmask, last, order) in enumerate(dirs):
            r0 = order[k] * ck
            b = b_alls[di][r0:r0 + ck, :]
            b_last = b[last:last + 1, :]
            qc = qkv_ref[r0:r0 + ck, 0:dk].astype(F32) * q_scale
            kc = qkv_ref[r0:r0 + ck, dk:2 * dk].astype(F32)
            q_in = (qc * jnp.exp(b)).astype(BF16)
            k_in = (kc * jnp.exp(-b)).astype(BF16)
            k_dec = (kc * jnp.exp(b_last - b)).astype(BF16)
            decay = jnp.exp(b_last)
            units = []
            for h in range(heads):
                qh = q_in[:, h * hk:(h + 1) * hk]
                vh = qkv_ref[r0:r0 + ck, 2 * dk + h * hv:2 * dk + (h + 1) * hv]
                scores = jnp.where(cmask, _dot_nt(qh, k_in[:, h * hk:(h + 1) * hk]), 0.0)
                o_intra = _dot(scores.astype(BF16), vh)
                ut = _dot_tn(vh, k_dec[:, h * hk:(h + 1) * hk])
                units.append((qh, o_intra, ut, decay[:, h * hk:(h + 1) * hk]))
            for h, (qh, o_intra, ut, dec) in enumerate(units):
                st = s_scr[h]
                o = o_intra + _dot_nt(qh, st.astype(BF16))
                o_ref[r0:r0 + ck, h * hv:(h + 1) * hv] = o.astype(BF16)
                s_scr[h] = st * dec + ut


def _gla(z, lr, wupf, wupb, baf, bab, s0f, s0b, heads, dk, dv, tb):
    bsz, n, _ = z.shape
    nb = n // tb
    hk, hv = dk // heads, dv // heads
    fwd = lambda cb: (lambda b, i: (b, i, cb))
    bwd = lambda cb: (lambda b, i: (b, nb - 1 - i, cb))
    full = lambda *shape: pl.BlockSpec(shape, lambda b, i: (0,) * len(shape))
    sspec = pl.BlockSpec((None, heads, hv, hk), lambda b, i: (b, 0, 0, 0))
    ost = jax.ShapeDtypeStruct((bsz, n, dv), BF16)
    tr = min(tb, MXU_DEPTH)
    assert tb % tr == 0 and tr % GLA_CHUNK == 0
    ri = np.arange(tr)
    same_chunk = (ri[:, None] // GLA_CHUNK) == (ri[None, :] // GLA_CHUNK)
    trif = jnp.asarray(same_chunk & (ri[None, :] <= ri[:, None]), BF16)
    trib = jnp.asarray(same_chunk & (ri[None, :] >= ri[:, None]), BF16)
    qkv = 2 * dk + dv
    return pl.pallas_call(
        functools.partial(_gla_kernel, heads, dk),
        grid=(bsz, nb),
        in_specs=[pl.BlockSpec((None, tb, qkv), fwd(0)), pl.BlockSpec((None, tb, LANES), fwd(0)),
                  pl.BlockSpec((None, tb, qkv), bwd(0)), pl.BlockSpec((None, tb, LANES), bwd(0)),
                  full(LANES, dk), full(LANES, dk), full(1, dk), full(1, dk),
                  full(tr, tr), full(tr, tr), sspec, sspec],
        out_specs=(pl.BlockSpec((None, tb, dv), fwd(0)), pl.BlockSpec((None, tb, dv), bwd(0))),
        out_shape=(ost, ost),
        scratch_shapes=[pltpu.VMEM((heads, hv, hk), F32), pltpu.VMEM((heads, hv, hk), F32)],
        compiler_params=_cparams("arbitrary", "arbitrary"),
        name="gla",
    )(z, lr, z, lr, wupf, wupb, baf, bab, trif, trib, s0f, s0b)


def _mixer_kernel(heads, of_ref, ob_ref, sg_ref, bg_ref, cu_ref, srg_ref, src_ref, x_ref,
                  gng_ref, cw_ref, wo_ref, wco_ref, wm_ref, gt1_ref, sh2_ref, sc2_ref, n2g_ref,
                  rwt_ref, x1_ref, h2_ref, aff_ref):
    tm, dv = of_ref.shape
    hv = dv // heads
    o = of_ref[...].astype(F32) + ob_ref[...].astype(F32)
    parts = []
    for h in range(heads):
        oh = o[:, h * hv:(h + 1) * hv]
        ms = jnp.mean(oh * oh, axis=-1, keepdims=True)
        parts.append(oh * lax.rsqrt(ms + EPS))
    on = jnp.concatenate(parts, axis=1) * gng_ref[...]
    y_gla = _dot((on * sg_ref[...].astype(F32)).astype(BF16), wo_ref[...])

    cu = cu_ref[...].astype(F32)
    gcol = lax.broadcasted_iota(I32, cu.shape, 0) & (GRID_W - 1)
    left = jnp.where(gcol == 0, 0.0, pltpu.roll(cu, 1, 0))
    right = jnp.where(gcol == GRID_W - 1, 0.0, pltpu.roll(cu, tm - 1, 0))
    cw = cw_ref[...]
    conv = left * cw[0:1, :] + cu * cw[1:2, :] + right * cw[2:3, :]
    y_conv = _dot((bg_ref[...].astype(F32) * conv).astype(BF16), wco_ref[...])

    merged = srg_ref[...].astype(F32) * y_gla + src_ref[...].astype(F32) * y_conv
    y = _dot(merged.astype(BF16), wm_ref[...])
    x1 = x_ref[...] + gt1_ref[...] * y
    x1_ref[...] = x1
    h2 = _rms_mod(x1, n2g_ref[...], sh2_ref[...], sc2_ref[...])
    rpt = h2.shape[1] // LANES
    for a in range(rpt):
        h2_ref[pl.ds(a, tm, stride=rpt), :] = h2[:, a * LANES:(a + 1) * LANES]
    logits = _dot_nt(rwt_ref[...], h2.astype(BF16))
    ex = jnp.exp(logits - jnp.max(logits, axis=0, keepdims=True))
    aff_ref[...] = ex / jnp.sum(ex, axis=0, keepdims=True)


def _mixer(o_f, o_b, z, x, gng, conv_w, w_o, w_co, w_m, gt1, sh2, sc2, n2g, rwt, heads, dk, tm):
    bsz, n, d = x.shape
    dv = o_f.shape[2]
    c = conv_w.shape[1]
    e = rwt.shape[0]
    zb = lambda cb: pl.BlockSpec((None, tm, d), lambda b, i: (b, i, cb))
    base = (2 * dk + dv) // d
    tok = lambda w: pl.BlockSpec((None, tm, w), lambda b, i: (b, i, 0))
    full = lambda *shape: pl.BlockSpec(shape, lambda b, i: (0,) * len(shape))
    perb = pl.BlockSpec((None, 1, d), lambda b, i: (b, 0, 0))
    return pl.pallas_call(
        functools.partial(_mixer_kernel, heads),
        grid=(bsz, n // tm),
        in_specs=[tok(dv), tok(dv), zb(base), zb(base + 1), zb(base + 2),
                  zb(base + 4), zb(base + 5), tok(d),
                  full(1, dv), full(3, c), full(dv, d), full(c, d), full(d, d),
                  perb, perb, perb, full(1, d), full(e, d)],
        out_specs=(tok(d), pl.BlockSpec((None, tm * (d // LANES), LANES), lambda b, i: (b, i, 0)),
                   pl.BlockSpec((None, e, tm), lambda b, i: (b, 0, i))),
        out_shape=(jax.ShapeDtypeStruct((bsz, n, d), F32),
                   jax.ShapeDtypeStruct((bsz, n * (d // LANES), LANES), F32),
                   jax.ShapeDtypeStruct((bsz, e, n), F32)),
        compiler_params=_cparams("arbitrary", "arbitrary"),
        name="mixer",
    )(o_f, o_b, z, z, z, z, z, x, gng, conv_w, w_o, w_co, w_m, gt1, sh2, sc2, n2g, rwt)


def _route_kernel(cap, aff_ref, lmat_ref, pos_ref, base_ref, idx_ref, loc_scr, bef_scr):
    e, n = aff_ref.shape
    nt = n // LANES
    aff = aff_ref[...]

    def count(mask):
        return jnp.sum(jnp.where(mask, 1.0, 0.0), axis=1, keepdims=True)

    def search(k, tbits):
        cand = tbits | jnp.left_shift(jnp.int32(1), 30 - k)
        ok = count(aff >= lax.bitcast_convert_type(cand, F32)) >= cap
        return jnp.where(ok, cand, tbits)

    tbits = lax.fori_loop(0, 31, search, jnp.zeros((e, 1), I32))
    thr = lax.bitcast_convert_type(tbits, F32)
    gt = aff > thr
    eq = aff == thr
    need = cap - count(gt)

    def stack(mask):
        m = jnp.where(mask, 1.0, 0.0)
        return jnp.concatenate([m[:, j * LANES:(j + 1) * LANES] for j in range(nt)], axis=0)

    r = lax.broadcasted_iota(I32, (LANES, LANES), 0)
    cl = lax.broadcasted_iota(I32, (LANES, LANES), 1)
    upper = jnp.where(r <= cl, 1.0, 0.0).astype(BF16)
    ones = jnp.ones((LANES, LANES), BF16)
    lmat = lmat_ref[...]

    def cumsum(ms):
        msb = ms.astype(BF16)
        before = _dot(_dot(lmat, msb).astype(BF16), ones)
        return _dot(msb, upper), before

    eq_s = stack(eq)
    loc_eq, before_eq = cumsum(eq_s)
    need_s = jnp.concatenate([need] * nt, axis=0)
    sel = jnp.maximum(stack(gt), jnp.where(loc_eq + before_eq <= need_s, eq_s, 0.0))
    loc, before = cumsum(sel)
    pos_ref[...] = jnp.where(sel > 0.0, loc + before - 1.0, -1.0).astype(I32)
    base_ref[...] = before.astype(I32)

    loc_scr[...] = loc
    bef_scr[...] = before
    slot = lax.broadcasted_iota(I32, (1, cap), 1).astype(F32)
    for ex in range(e):
        loc_e = loc_scr[pl.ds(ex, nt, stride=e), :]
        tprev = bef_scr[pl.ds(ex, nt, stride=e), :][:, 0:1]
        tincl = tprev + loc_e[:, LANES - 1:LANES]
        in_tile = jnp.where((tprev <= slot) & (slot < tincl), 1.0, 0.0)
        tile = jnp.sum(jnp.where(tincl <= slot, 1.0, 0.0), axis=0, keepdims=True)
        s_loc = slot - jnp.sum(in_tile * tprev, axis=0, keepdims=True)
        counts = _dot_tn(loc_e.astype(BF16), in_tile.astype(BF16))
        lane = jnp.sum(jnp.where(counts <= s_loc, 1.0, 0.0), axis=0, keepdims=True)
        idx_ref[pl.ds(ex, 1), :] = (tile * LANES + lane).astype(I32)


def _route(aff_t, cap):
    bsz, e, n = aff_t.shape
    nt = n // LANES
    rows = nt * e
    ri = np.arange(rows)
    lmat = ((ri[None, :] % e == ri[:, None] % e) & (ri[None, :] // e < ri[:, None] // e))
    lmat = jnp.asarray(lmat, BF16)
    st = jax.ShapeDtypeStruct((bsz, rows, LANES), I32)
    ospec = pl.BlockSpec((None, rows, LANES), lambda b: (b, 0, 0))
    return pl.pallas_call(
        functools.partial(_route_kernel, cap),
        grid=(bsz,),
        in_specs=[pl.BlockSpec((None, e, n), lambda b: (b, 0, 0)),
                  pl.BlockSpec((rows, rows), lambda b: (0, 0))],
        out_specs=(ospec, ospec, pl.BlockSpec((None, e, cap), lambda b: (b, 0, 0))),
        out_shape=(st, st, jax.ShapeDtypeStruct((bsz, e, cap), I32)),
        scratch_shapes=[pltpu.VMEM((rows, LANES), F32), pltpu.VMEM((rows, LANES), F32)],
        compiler_params=_cparams("arbitrary"),
        name="route",
    )(aff_t, lmat)


def _slot_window(base):
    return pl.multiple_of((base // BF16_ROWS) * BF16_ROWS, BF16_ROWS)


def _expert_kernel(cap, idx_ref, h2_hbm, wg_ref, wu_ref, wd_ref, ye_ref, xbuf, acc, sem):
    e, f = pl.program_id(1), pl.program_id(2)
    n_exp, nf = pl.num_programs(1), pl.num_programs(2)
    last = pl.num_programs(0) * n_exp - 1
    lin = pl.program_id(0) * n_exp + e
    slot = lin % 2
    per_step = cap // nf
    rpt = wg_ref.shape[0] // LANES

    def row_copy(lin_t, part, i, slot_t):
        tok = idx_ref[(lin_t * nf + part) * per_step + i]
        return pltpu.make_async_copy(h2_hbm.at[pl.ds(pl.multiple_of(tok * rpt, rpt), rpt), :],
                                     xbuf.at[slot_t, part, pl.ds(i * rpt, rpt), :],
                                     sem.at[slot_t])

    def wait_rows(slot_t):
        pltpu.make_async_copy(xbuf.at[slot_t], xbuf.at[slot_t], sem.at[slot_t]).wait()

    @pl.when((lin == 0) & (f == 0))
    def _():
        acc[...] = jnp.zeros_like(acc)
        for part in range(nf):
            def first(i, carry):
                row_copy(lin, part, i, slot).start()
                return carry
            lax.fori_loop(0, per_step, first, 0)

    @pl.when(f == 0)
    def _():
        wait_rows(slot)

    nxt = jnp.minimum(lin + 1, last)
    for i in range(per_step):
        row_copy(nxt, f, i, 1 - slot).start()

    @pl.when((lin == last) & (f == nf - 1))
    def _():
        wait_rows(1 - slot)

    xv = jnp.concatenate(
        [jnp.concatenate([xbuf[slot, part, pl.ds(a, per_step, stride=rpt), :].astype(BF16)
                          for a in range(rpt)], axis=1) for part in range(nf)], axis=0)
    hg = _dot(xv, wg_ref[...].astype(BF16))
    hu = _dot(xv, wu_ref[...].astype(BF16))
    hid = (hg * _sigmoid(hg) * hu).astype(BF16)
    acc[...] = jnp.where(f > 0, acc[...], 0.0) + _dot(hid, wd_ref[...].astype(BF16))

    @pl.when(f == nf - 1)
    def _():
        ye_ref[0:cap, :] = acc[...].astype(BF16)
        ye_ref[cap:, :] = jnp.zeros((ye_ref.shape[0] - cap, ye_ref.shape[1]), BF16)


def _experts(h2, idx, w_gate, w_up, w_down, cap, rows, tf):
    n_exp, d, df = w_gate.shape
    bsz = h2.shape[0]
    rpt = d // LANES
    n = h2.shape[1] // rpt
    nf = df // tf
    assert cap % nf == 0
    rows_global = (idx + (jnp.arange(bsz, dtype=I32) * n)[:, None, None]).reshape(-1)
    return pl.pallas_call(
        functools.partial(_expert_kernel, cap),
        grid_spec=pltpu.PrefetchScalarGridSpec(
            num_scalar_prefetch=1,
            grid=(bsz, n_exp, nf),
            in_specs=[pl.BlockSpec(memory_space=pl.ANY),
                      pl.BlockSpec((None, d, tf), lambda b, e, f, s: (e, 0, f)),
                      pl.BlockSpec((None, d, tf), lambda b, e, f, s: (e, 0, f)),
                      pl.BlockSpec((None, tf, d), lambda b, e, f, s: (e, f, 0))],
            out_specs=pl.BlockSpec((None, None, rows, d), lambda b, e, f, s: (b, e, 0, 0)),
            scratch_shapes=[pltpu.VMEM((2, nf, (cap // nf) * rpt, LANES), F32),
                            pltpu.VMEM((cap, d), F32), pltpu.SemaphoreType.DMA((2,))]),
        out_shape=jax.ShapeDtypeStruct((bsz, n_exp, rows, d), BF16),
        compiler_params=_cparams("arbitrary", "arbitrary", "arbitrary"),
        name="experts",
    )(rows_global, h2.reshape(bsz * n * rpt, LANES), w_gate, w_up, w_down)


def _combine_windows(tm, n_exp):
    mean = tm * EC_CAPACITY / n_exp
    usual = mean + 5.5 * (mean * (1 - EC_CAPACITY / n_exp)) ** 0.5 + BF16_ROWS - 1
    usual = min(-(-int(usual) // BF16_ROWS) * BF16_ROWS, tm + BF16_ROWS)
    return usual, tm + BF16_ROWS


def _combine_kernel(n_exp, nsub, base_ref, ovf_ref, pos_ref, aff_ref, x1_ref, gt2_ref, fg_ref,
                    ye_hbm, out_ref, buf, fbuf, ffn_scr, sem, fsem):
    b, j = pl.program_id(0), pl.program_id(1)
    nt = pl.num_programs(1)
    step = b * nt + j
    slot = step % 2
    wp, wfull = buf.shape[2], fbuf.shape[0]
    d = buf.shape[-1]

    def start_of(tile, ex):
        return _slot_window(base_ref[tile * nsub * n_exp + ex])

    def window_copy(tile, ex, dst_slot):
        return pltpu.make_async_copy(ye_hbm.at[tile // nt, ex, pl.ds(start_of(tile, ex), wp), :],
                                     buf.at[dst_slot, ex], sem.at[dst_slot])

    def full_copy(ex):
        return pltpu.make_async_copy(ye_hbm.at[b, ex, pl.ds(start_of(step, ex), wfull), :],
                                     fbuf, fsem)

    @pl.when(step == 0)
    def _():
        for ex in range(n_exp):
            window_copy(step, ex, slot).start()

    @pl.when(step + 1 < pl.num_programs(0) * nt)
    def _():
        for ex in range(n_exp):
            window_copy(step + 1, ex, 1 - slot).start()

    def gates_t(rows, ex):
        srow = lax.broadcasted_iota(I32, (rows, LANES), 0)
        s0 = start_of(step, ex)
        parts = []
        for u in range(nsub):
            rel = pos_ref[pl.ds(u * n_exp + ex, 1), :] - s0
            gate = aff_ref[pl.ds(ex, 1), u * LANES:(u + 1) * LANES]
            parts.append(jnp.where(srow == rel, gate, 0.0).astype(BF16))
        return jnp.concatenate(parts, axis=1)

    for ex in range(n_exp):
        window_copy(step, ex, slot).wait()

    @pl.when(ovf_ref[step] == 0)
    def _():
        g_all = jnp.concatenate([gates_t(wp, ex) for ex in range(n_exp)], axis=0)
        ffn_scr[...] = _dot_tn(g_all, buf[slot].reshape(n_exp * wp, d))

    @pl.when(ovf_ref[step] != 0)
    def _():
        ffn_scr[...] = jnp.zeros_like(ffn_scr)
        for ex in range(n_exp):
            full_copy(ex).start()
            full_copy(ex).wait()
            ffn_scr[...] += _dot_tn(gates_t(wfull, ex), fbuf[...])

    x2 = x1_ref[...] + gt2_ref[...] * ffn_scr[...]
    ms = jnp.mean(x2 * x2, axis=-1, keepdims=True)
    out_ref[...] = x2 * lax.rsqrt(ms + EPS) * fg_ref[...]


def _combine(ye, pos, aff_t, before, x1, gt2, final_g, cap, tm):
    bsz, n, d = x1.shape
    n_exp = aff_t.shape[1]
    nsub = tm // LANES
    nt = n // tm
    wp, wfull = _combine_windows(tm, n_exp)
    starts = before[:, :, 0].reshape(bsz, n // LANES, n_exp)
    tile_starts = starts[:, ::nsub]
    tile_ends = jnp.concatenate([tile_starts[:, 1:], jnp.full((bsz, 1, n_exp), cap, I32)], axis=1)
    overflow = (tile_starts % BF16_ROWS) + (tile_ends - tile_starts) > wp
    ovf = jnp.any(overflow, axis=-1).astype(I32).reshape(-1)
    return pl.pallas_call(
        functools.partial(_combine_kernel, n_exp, nsub),
        grid_spec=pltpu.PrefetchScalarGridSpec(
            num_scalar_prefetch=2,
            grid=(bsz, nt),
            in_specs=[pl.BlockSpec((None, nsub * n_exp, LANES), lambda b, j, s, o: (b, j, 0)),
                      pl.BlockSpec((None, n_exp, tm), lambda b, j, s, o: (b, 0, j)),
                      pl.BlockSpec((None, tm, d), lambda b, j, s, o: (b, j, 0)),
                      pl.BlockSpec((None, 1, d), lambda b, j, s, o: (b, 0, 0)),
                      pl.BlockSpec((1, d), lambda b, j, s, o: (0, 0)),
                      pl.BlockSpec(memory_space=pl.ANY)],
            out_specs=pl.BlockSpec((None, tm, d), lambda b, j, s, o: (b, j, 0)),
            scratch_shapes=[pltpu.VMEM((2, n_exp, wp, d), BF16),
                            pltpu.VMEM((wfull, d), BF16),
                            pltpu.VMEM((tm, d), F32),
                            pltpu.SemaphoreType.DMA((2,)), pltpu.SemaphoreType.DMA(())]),
        out_shape=jax.ShapeDtypeStruct((bsz, n, d), F32),
        compiler_params=_cparams("arbitrary", "arbitrary"),
        name="combine",
    )(starts.reshape(-1), ovf, pos, aff_t, x1, gt2, final_g, ye)


def _pick(n, pref):
    t = min(n, pref)
    while n % t:
        t //= 2
    return t


def _tiles(n, d_expert):
    return dict(inproj=_pick(n, 1024), gla=_pick(n, 512), mixer=_pick(n, 512),
                combine=_pick(n, 512), expert_f=_pick(d_expert, 512))


def kernel(x, c, ctx, c_ctx, w_ada, b_ada, norm1_g, norm2_g, w_in, gla_w_a_up, gla_b_a,
           gla_norm_g, gla_w_o, conv_w, conv_w_out, merge_w_out, router_w,
           exp_w_gate, exp_w_up, exp_w_down, final_g):
    bsz, n, d = x.shape
    depth = w_ada.shape[0]
    assert depth == 1, "single-layer trunk"
    rank, dk = gla_w_a_up.shape[-2:]
    dv = gla_w_o.shape[1]
    hv = gla_norm_g.shape[-1]
    heads = dv // hv
    cch = conv_w.shape[-1]
    n_exp = router_w.shape[-1]
    cap = EC_CAPACITY * n // n_exp
    assert 2 * dk == d and dv == d and cch == d and 2 * rank <= LANES
    assert n % LANES == 0 and n % GRID_W == 0 and bsz + 1 <= 8 and n_exp % 8 == 0

    cond = jnp.zeros((8, d), F32).at[:bsz].set(c).at[bsz].set(c_ctx)
    mod = _adaln(cond, w_ada[0], b_ada[0]).reshape(8, N_MOD, d)
    sh1, sc1, gt1, sh2, sc2, gt2 = [mod[:bsz, k][:, None, :] for k in range(N_MOD)]
    csh1, csc1 = mod[bsz:bsz + 1, 0], mod[bsz:bsz + 1, 1]

    lr0 = 2 * dk + 2 * dv
    u0 = lr0 + 2 * rank
    groups = ((0, lr0), (u0 + cch, cch), (u0, cch), (u0 + 2 * cch, cch + 2 * d), (lr0, LANES))
    tiles = tuple((r0 + o, min(d, cnt - o)) for r0, cnt in groups for o in range(0, cnt, d))
    wt = _wprep(jnp.swapaxes(w_in[0], 0, 1), tiles)
    wup = gla_w_a_up[0].astype(BF16)
    wupf = jnp.zeros((LANES, dk), BF16).at[:rank].set(wup[0])
    wupb = jnp.zeros((LANES, dk), BF16).at[rank:2 * rank].set(wup[1])
    baf, bab = gla_b_a[0, 0:1], gla_b_a[0, 1:2]
    n1g, n2g = norm1_g[0:1], norm2_g[0:1]

    t = _tiles(n, exp_w_gate.shape[-1])
    s0f, s0b = _ctx_states(ctx, csh1, csc1, n1g, wt, wupf, wupb, baf, bab, heads, dk, dv)
    z, lr = _inproj(x, sh1, sc1, n1g, wt, t["inproj"])
    o_f, o_b = _gla(z, lr, wupf, wupb, baf, bab, s0f, s0b, heads, dk, dv, t["gla"])
    x1, h2, aff_t = _mixer(
        o_f, o_b, z, x, jnp.tile(gla_norm_g[0:1], (1, heads)), conv_w[0],
        gla_w_o[0].astype(BF16), conv_w_out[0].astype(BF16), merge_w_out[0].astype(BF16),
        gt1, sh2, sc2, n2g, router_w[0].T.astype(BF16), heads, dk, t["mixer"])

    pos, before, idx = _route(aff_t, cap)
    ye = _experts(h2, idx, exp_w_gate[0], exp_w_up[0], exp_w_down[0], cap,
                  cap + _combine_windows(t["combine"], n_exp)[1], t["expert_f"])
    return _combine(ye, pos, aff_t, before, x1, gt2, final_g.reshape(1, d), cap, t["combine"])
```

```python
import functools

import numpy as np
import jax
import jax.numpy as jnp
from jax import lax
from jax.experimental import pallas as pl
from jax.experimental.pallas import tpu as pltpu

F32 = jnp.float32
BF16 = jnp.bfloat16
I32 = jnp.int32

EPS = 1e-6
N_MOD = 6
GRID_W = 64
GLA_CHUNK = 64
GLA_GATE_NORM = 16.0
EC_CAPACITY = 2

LANES = 128
MXU_DEPTH = 256
BF16_ROWS = 16
VMEM_LIMIT = 56 * 1024 * 1024


def _cparams(*sem, **kw):
    return pltpu.CompilerParams(dimension_semantics=sem, vmem_limit_bytes=VMEM_LIMIT, **kw)


def _dot(a, b):
    return jnp.dot(a, b, preferred_element_type=F32)


def _dot_nt(a, b):
    return lax.dot_general(a, b, (((1,), (1,)), ((), ())), preferred_element_type=F32)


def _dot_tn(a, b):
    return lax.dot_general(a, b, (((0,), (0,)), ((), ())), preferred_element_type=F32)


def _sigmoid(v):
    return 1.0 / (1.0 + jnp.exp(-v))


def _log_sigmoid(v):
    return jnp.minimum(v, 0.0) - jnp.log(1.0 + jnp.exp(-jnp.abs(v)))


def _rms_mod(xv, g, shift, scale):
    ms = jnp.mean(xv * xv, axis=-1, keepdims=True)
    y = xv * lax.rsqrt(ms + EPS) * g
    return y * (1.0 + scale) + shift


def _split_bf16(v):
    hi = v.astype(BF16)
    lo = (v - hi.astype(F32)).astype(BF16)
    return hi, lo


def _adaln_kernel(c_ref, w_ref, b_ref, o_ref):
    cv = c_ref[...]
    s = (cv * _sigmoid(cv)).astype(BF16)
    o_ref[...] = _dot(s, w_ref[...].astype(BF16)) + b_ref[...]


def _adaln(cond, w, b):
    rows, d = cond.shape
    nout = w.shape[1]
    tn = d
    return pl.pallas_call(
        _adaln_kernel,
        grid=(nout // tn,),
        in_specs=[pl.BlockSpec((rows, d), lambda j: (0, 0)),
                  pl.BlockSpec((d, tn), lambda j: (0, j)),
                  pl.BlockSpec((1, tn), lambda j: (0, j))],
        out_specs=pl.BlockSpec((rows, tn), lambda j: (0, j)),
        out_shape=jax.ShapeDtypeStruct((rows, nout), F32),
        compiler_params=_cparams("arbitrary"),
        name="adaln",
    )(cond, w, b.reshape(1, nout))


def _wprep_kernel(tiles, wt_hbm, o_ref, buf, sem):
    def copy(t):
        row, cnt = tiles[t]
        return pltpu.make_async_copy(wt_hbm.at[pl.ds(row, cnt), :],
                                     buf.at[t % 2, pl.ds(0, cnt), :], sem.at[t % 2])

    copy(0).start()
    out_row = 0
    for t, (_, cnt) in enumerate(tiles):
        if t + 1 < len(tiles):
            copy(t + 1).start()
        copy(t).wait()
        o_ref[out_row:out_row + cnt, :] = buf[t % 2, 0:cnt, :].astype(BF16)
        out_row += cnt


def _wprep(wt, tiles):
    d = wt.shape[1]
    rows = sum(cnt for _, cnt in tiles)
    return pl.pallas_call(
        functools.partial(_wprep_kernel, tiles),
        in_specs=[pl.BlockSpec(memory_space=pl.ANY)],
        out_specs=pl.BlockSpec((rows, d), lambda: (0, 0)),
        out_shape=jax.ShapeDtypeStruct((rows, d), BF16),
        scratch_shapes=[pltpu.VMEM((2, max(cnt for _, cnt in tiles), d), F32),
                        pltpu.SemaphoreType.DMA((2,))],
        compiler_params=pltpu.CompilerParams(vmem_limit_bytes=VMEM_LIMIT),
        name="wprep",
    )(wt)


def _ctx_kernel(heads, ctx_ref, sh_ref, sc_ref, g_ref, wk_ref, wv_ref, wlr_ref,
                wupf_ref, wupb_ref, baf_ref, bab_ref, sf_ref, sb_ref):
    n = ctx_ref.shape[0]
    hc = _rms_mod(ctx_ref[...], g_ref[...], sh_ref[...], sc_ref[...]).astype(BF16)
    k = _dot_nt(hc, wk_ref[...])
    v = _dot_nt(hc, wv_ref[...]).astype(BF16)
    lr = _dot_nt(hc, wlr_ref[...]).astype(BF16)
    hk = k.shape[1] // heads
    hv = v.shape[1] // heads
    row = lax.broadcasted_iota(I32, (n, n), 0)
    col = lax.broadcasted_iota(I32, (n, n), 1)
    for wup_ref, ba_ref, s_ref, tri, last in (
            (wupf_ref, baf_ref, sf_ref, col <= row, n - 1),
            (wupb_ref, bab_ref, sb_ref, col >= row, 0)):
        la = _log_sigmoid(_dot(lr, wup_ref[...]) + ba_ref[...]) * (1.0 / GLA_GATE_NORM)
        hi, lo = _split_bf16(la)
        t = jnp.where(tri, 1.0, 0.0).astype(BF16)
        b = _dot(t, hi) + _dot(t, lo)
        kd = (k * jnp.exp(b[last:last + 1, :] - b)).astype(BF16)
        for h in range(heads):
            s_ref[h] = _dot_tn(v[:, h * hv:(h + 1) * hv], kd[:, h * hk:(h + 1) * hk])


def _ctx_states(ctx, sh, sc, g, wt, wupf, wupb, baf, bab, heads, dk, dv):
    bsz, n, d = ctx.shape
    lr_blk = (wt.shape[0] - LANES) // LANES
    hk, hv = dk // heads, dv // heads
    full = lambda *shape: pl.BlockSpec(shape, lambda b: (0,) * len(shape))
    st = jax.ShapeDtypeStruct((bsz, heads, hv, hk), F32)
    sspec = pl.BlockSpec((None, heads, hv, hk), lambda b: (b, 0, 0, 0))
    return pl.pallas_call(
        functools.partial(_ctx_kernel, heads),
        grid=(bsz,),
        in_specs=[pl.BlockSpec((None, n, d), lambda b: (b, 0, 0)),
                  full(1, d), full(1, d), full(1, d),
                  pl.BlockSpec((dk, d), lambda b: (1, 0)),
                  pl.BlockSpec((dv, d), lambda b: (2 * dk // dv, 0)),
                  pl.BlockSpec((LANES, d), lambda b: (lr_blk, 0)),
                  full(LANES, dk), full(LANES, dk), full(1, dk), full(1, dk)],
        out_specs=(sspec, sspec),
        out_shape=(st, st),
        compiler_params=_cparams("arbitrary"),
        name="ctx_state",
    )(ctx, sh, sc, g, wt, wt, wt, wupf, wupb, baf, bab)


def _inproj_kernel(x_ref, sh_ref, sc_ref, g_ref, w_ref, wlr_ref, z_ref, lr_ref, h_scr):
    j = pl.program_id(2)
    d = x_ref.shape[1]

    @pl.when(j == 0)
    def _():
        hb = _rms_mod(x_ref[...], g_ref[...], sh_ref[...], sc_ref[...]).astype(BF16)
        h_scr[...] = hb
        lr_ref[...] = _dot_nt(hb, wlr_ref[...])
        z_ref[...] = _dot_nt(hb, w_ref[...]).astype(BF16)

    @pl.when(j == 1)
    def _():
        a = _dot_nt(h_scr[...], w_ref[...])
        gv = a[:, :d]
        z_ref[:, :d] = (gv * _sigmoid(gv)).astype(BF16)
        z_ref[:, d:] = a[:, d:].astype(BF16)

    @pl.when(j == 2)
    def _():
        a = _dot_nt(h_scr[...], w_ref[...])
        z_ref[:, :d] = (a[:, d:] * a[:, :d]).astype(BF16)
        z_ref[:, d:] = jnp.zeros((z_ref.shape[0], d), BF16)

    @pl.when(j == 3)
    def _():
        z_ref[...] = _sigmoid(_dot_nt(h_scr[...], w_ref[...])).astype(BF16)


def _inproj(x, sh, sc, g, wt, tm):
    bsz, n, d = x.shape
    p = wt.shape[0] - LANES
    tn = 2 * d
    assert p == 4 * tn
    return pl.pallas_call(
        _inproj_kernel,
        grid=(bsz, n // tm, p // tn),
        in_specs=[pl.BlockSpec((None, tm, d), lambda b, i, j: (b, i, 0)),
                  pl.BlockSpec((None, 1, d), lambda b, i, j: (b, 0, 0)),
                  pl.BlockSpec((None, 1, d), lambda b, i, j: (b, 0, 0)),
                  pl.BlockSpec((1, d), lambda b, i, j: (0, 0)),
                  pl.BlockSpec((tn, d), lambda b, i, j: (j, 0)),
                  pl.BlockSpec((LANES, d), lambda b, i, j: (p // LANES, 0))],
        out_specs=(pl.BlockSpec((None, tm, tn), lambda b, i, j: (b, i, j)),
                   pl.BlockSpec((None, tm, LANES), lambda b, i, j: (b, i, 0))),
        out_shape=(jax.ShapeDtypeStruct((bsz, n, p), BF16),
                   jax.ShapeDtypeStruct((bsz, n, LANES), F32)),
        scratch_shapes=[pltpu.VMEM((tm, d), BF16)],
        compiler_params=_cparams("arbitrary", "arbitrary", "arbitrary"),
        name="inproj",
    )(x, sh, sc, g, wt, wt)


def _gla_kernel(heads, dk, qkvf_ref, lrf_ref, qkvb_ref, lrb_ref,
                wupf_ref, wupb_ref, baf_ref, bab_ref, trif_ref, trib_ref, s0f_ref, s0b_ref,
                of_ref, ob_ref, sf_scr, sb_scr):
    tb = qkvf_ref.shape[0]
    dv = qkvf_ref.shape[1] - 2 * dk
    hk, hv = dk // heads, dv // heads
    ck = GLA_CHUNK
    nck = tb // ck
    q_scale = hk ** -0.5

    @pl.when(pl.program_id(1) == 0)
    def _():
        sf_scr[...] = s0f_ref[...]
        sb_scr[...] = s0b_ref[...]

    crow = lax.broadcasted_iota(I32, (ck, ck), 0)
    ccol = lax.broadcasted_iota(I32, (ck, ck), 1)

    dirs = (
        (qkvf_ref, lrf_ref, wupf_ref, baf_ref, of_ref, sf_scr,
         trif_ref, ccol <= crow, ck - 1, range(nck)),
        (qkvb_ref, lrb_ref, wupb_ref, bab_ref, ob_ref, sb_scr,
         trib_ref, ccol >= crow, 0, range(nck - 1, -1, -1)),
    )
    b_alls = []
    for _, lr_ref, wup_ref, ba_ref, _, _, tri, _, _, _ in dirs:
        la = _log_sigmoid(_dot(lr_ref[...].astype(BF16), wup_ref[...]) + ba_ref[...])
        la = la * (1.0 / GLA_GATE_NORM)
        hi, lo = _split_bf16(la)
        t = tri[...]
        tr = t.shape[0]
        b_alls.append(jnp.concatenate(
            [_dot(t, hi[r:r + tr]) + _dot(t, lo[r:r + tr]) for r in range(0, tb, tr)], axis=0))
    for k in range(nck):
        work = []
        for di, (qkv_ref, _, _, _, _, _, _, cmask, last, order) in enumerate(dirs):
            r0 = order[k] * ck
            b = b_alls[di][r0:r0 + ck, :]
            b_last = b[last:last + 1, :]
            qc = qkv_ref[r0:r0 + ck, 0:dk].astype(F32) * q_scale
            kc = qkv_ref[r0:r0 + ck, dk:2 * dk].astype(F32)
            q_in = (qc * jnp.exp(b)).astype(BF16)
            k_in = (kc * jnp.exp(-b)).astype(BF16)
            k_dec = (kc * jnp.exp(b_last - b)).astype(BF16)
            decay = jnp.exp(b_last)
            units = []
            for h in range(heads):
                qh = q_in[:, h * hk:(h + 1) * hk]
                vh = qkv_ref[r0:r0 + ck, 2 * dk + h * hv:2 * dk + (h + 1) * hv]
                scores = jnp.where(cmask, _dot_nt(qh, k_in[:, h * hk:(h + 1) * hk]), 0.0)
                o_intra = _dot(scores.astype(BF16), vh)
                ut = _dot_tn(vh, k_dec[:, h * hk:(h + 1) * hk])
                units.append((qh, o_intra, ut, decay[:, h * hk:(h + 1) * hk]))
            work.append((r0, units))
        for (r0, units), (_, _, _, _, o_ref, s_scr, _, _, _, _) in zip(work, dirs):
            for h, (qh, o_intra, ut, dec) in enumerate(units):
                st = s_scr[h]
                o = o_intra + _dot_nt(qh, st.astype(BF16))
                o_ref[r0:r0 + ck, h * hv:(h + 1) * hv] = o.astype(BF16)
                s_scr[h] = st * dec + ut


def _gla(z, lr, wupf, wupb, baf, bab, s0f, s0b, heads, dk, dv, tb):
    bsz, n, _ = z.shape
    nb = n // tb
    hk, hv = dk // heads, dv // heads
    fwd = lambda cb: (lambda b, i: (b, i, cb))
    bwd = lambda cb: (lambda b, i: (b, nb - 1 - i, cb))
    full = lambda *shape: pl.BlockSpec(shape, lambda b, i: (0,) * len(shape))
    sspec = pl.BlockSpec((None, heads, hv, hk), lambda b, i: (b, 0, 0, 0))
    ost = jax.ShapeDtypeStruct((bsz, n, dv), BF16)
    tr = min(tb, MXU_DEPTH)
    assert tb % tr == 0 and tr % GLA_CHUNK == 0
    ri = np.arange(tr)
    same_chunk = (ri[:, None] // GLA_CHUNK) == (ri[None, :] // GLA_CHUNK)
    trif = jnp.asarray(same_chunk & (ri[None, :] <= ri[:, None]), BF16)
    trib = jnp.asarray(same_chunk & (ri[None, :] >= ri[:, None]), BF16)
    qkv = 2 * dk + dv
    return pl.pallas_call(
        functools.partial(_gla_kernel, heads, dk),
        grid=(bsz, nb),
        in_specs=[pl.BlockSpec((None, tb, qkv), fwd(0)), pl.BlockSpec((None, tb, LANES), fwd(0)),
                  pl.BlockSpec((None, tb, qkv), bwd(0)), pl.BlockSpec((None, tb, LANES), bwd(0)),
                  full(LANES, dk), full(LANES, dk), full(1, dk), full(1, dk),
                  full(tr, tr), full(tr, tr), sspec, sspec],
        out_specs=(pl.BlockSpec((None, tb, dv), fwd(0)), pl.BlockSpec((None, tb, dv), bwd(0))),
        out_shape=(ost, ost),
        scratch_shapes=[pltpu.VMEM((heads, hv, hk), F32), pltpu.VMEM((heads, hv, hk), F32)],
        compiler_params=_cparams("arbitrary", "arbitrary"),
        name="gla",
    )(z, lr, z, lr, wupf, wupb, baf, bab, trif, trib, s0f, s0b)


def _mixer_kernel(heads, of_ref, ob_ref, sg_ref, bg_ref, cu_ref, srg_ref, src_ref, x_ref,
                  gng_ref, cw_ref, wo_ref, wco_ref, wm_ref, gt1_ref, sh2_ref, sc2_ref, n2g_ref,
                  rwt_ref, x1_ref, h2_ref, aff_ref):
    tm, dv = of_ref.shape
    hv = dv // heads
    o = of_ref[...].astype(F32) + ob_ref[...].astype(F32)
    parts = []
    for h in range(heads):
        oh = o[:, h * hv:(h + 1) * hv]
        ms = jnp.mean(oh * oh, axis=-1, keepdims=True)
        parts.append(oh * lax.rsqrt(ms + EPS))
    on = jnp.concatenate(parts, axis=1) * gng_ref[...]
    y_gla = _dot((on * sg_ref[...].astype(F32)).astype(BF16), wo_ref[...])

    cu = cu_ref[...].astype(F32)
    gcol = lax.broadcasted_iota(I32, cu.shape, 0) & (GRID_W - 1)
    left = jnp.where(gcol == 0, 0.0, pltpu.roll(cu, 1, 0))
    right = jnp.where(gcol == GRID_W - 1, 0.0, pltpu.roll(cu, tm - 1, 0))
    cw = cw_ref[...]
    conv = left * cw[0:1, :] + cu * cw[1:2, :] + right * cw[2:3, :]
    y_conv = _dot((bg_ref[...].astype(F32) * conv).astype(BF16), wco_ref[...])

    merged = srg_ref[...].astype(F32) * y_gla + src_ref[...].astype(F32) * y_conv
    y = _dot(merged.astype(BF16), wm_ref[...])
    x1 = x_ref[...] + gt1_ref[...] * y
    x1_ref[...] = x1
    h2 = _rms_mod(x1, n2g_ref[...], sh2_ref[...], sc2_ref[...])
    rpt = h2.shape[1] // LANES
    for a in range(rpt):
        h2_ref[pl.ds(a, tm, stride=rpt), :] = h2[:, a * LANES:(a + 1) * LANES]
    logits = _dot_nt(rwt_ref[...], h2.astype(BF16))
    ex = jnp.exp(logits - jnp.max(logits, axis=0, keepdims=True))
    aff_ref[...] = ex / jnp.sum(ex, axis=0, keepdims=True)


def _mixer(o_f, o_b, z, x, gng, conv_w, w_o, w_co, w_m, gt1, sh2, sc2, n2g, rwt, heads, dk, tm):
    bsz, n, d = x.shape
    dv = o_f.shape[2]
    c = conv_w.shape[1]
    e = rwt.shape[0]
    zb = lambda cb: pl.BlockSpec((None, tm, d), lambda b, i: (b, i, cb))
    base = (2 * dk + dv) // d
    tok = lambda w: pl.BlockSpec((None, tm, w), lambda b, i: (b, i, 0))
    full = lambda *shape: pl.BlockSpec(shape, lambda b, i: (0,) * len(shape))
    perb = pl.BlockSpec((None, 1, d), lambda b, i: (b, 0, 0))
    return pl.pallas_call(
        functools.partial(_mixer_kernel, heads),
        grid=(bsz, n // tm),
        in_specs=[tok(dv), tok(dv), zb(base), zb(base + 1), zb(base + 2),
                  zb(base + 4), zb(base + 5), tok(d),
                  full(1, dv), full(3, c), full(dv, d), full(c, d), full(d, d),
                  perb, perb, perb, full(1, d), full(e, d)],
        out_specs=(tok(d), pl.BlockSpec((None, tm * (d // LANES), LANES), lambda b, i: (b, i, 0)),
                   pl.BlockSpec((None, e, tm), lambda b, i: (b, 0, i))),
        out_shape=(jax.ShapeDtypeStruct((bsz, n, d), F32),
                   jax.ShapeDtypeStruct((bsz, n * (d // LANES), LANES), F32),
                   jax.ShapeDtypeStruct((bsz, e, n), F32)),
        compiler_params=_cparams("arbitrary", "arbitrary"),
        name="mixer",
    )(o_f, o_b, z, z, z, z, z, x, gng, conv_w, w_o, w_co, w_m, gt1, sh2, sc2, n2g, rwt)


def _route_kernel(cap, aff_ref, lmat_ref, pos_ref, base_ref, idx_ref, loc_scr, bef_scr):
    e, n = aff_ref.shape
    nt = n // LANES
    aff = aff_ref[...]

    def count(mask):
        return jnp.sum(jnp.where(mask, 1.0, 0.0), axis=1, keepdims=True)

    def search(k, tbits):
        cand = tbits | jnp.left_shift(jnp.int32(1), 30 - k)
        ok = count(aff >= lax.bitcast_convert_type(cand, F32)) >= cap
        return jnp.where(ok, cand, tbits)

    tbits = lax.fori_loop(0, 31, search, jnp.zeros((e, 1), I32))
    thr = lax.bitcast_convert_type(tbits, F32)
    gt = aff > thr
    eq = aff == thr
    need = cap - count(gt)

    def stack(mask):
        m = jnp.where(mask, 1.0, 0.0)
        return jnp.concatenate([m[:, j * LANES:(j + 1) * LANES] for j in range(nt)], axis=0)

    r = lax.broadcasted_iota(I32, (LANES, LANES), 0)
    cl = lax.broadcasted_iota(I32, (LANES, LANES), 1)
    upper = jnp.where(r <= cl, 1.0, 0.0).astype(BF16)
    ones = jnp.ones((LANES, LANES), BF16)
    lmat = lmat_ref[...]

    def cumsum(ms):
        msb = ms.astype(BF16)
        before = _dot(_dot(lmat, msb).astype(BF16), ones)
        return _dot(msb, upper), before

    eq_s = stack(eq)
    loc_eq, before_eq = cumsum(eq_s)
    need_s = jnp.concatenate([need] * nt, axis=0)
    sel = jnp.maximum(stack(gt), jnp.where(loc_eq + before_eq <= need_s, eq_s, 0.0))
    loc, before = cumsum(sel)
    pos_ref[...] = jnp.where(sel > 0.0, loc + before - 1.0, -1.0).astype(I32)
    base_ref[...] = before.astype(I32)

    loc_scr[...] = loc
    bef_scr[...] = before
    slot = lax.broadcasted_iota(I32, (1, cap), 1).astype(F32)
    for ex in range(e):
        loc_e = loc_scr[pl.ds(ex, nt, stride=e), :]
        tprev = bef_scr[pl.ds(ex, nt, stride=e), :][:, 0:1]
        tincl = tprev + loc_e[:, LANES - 1:LANES]
        in_tile = jnp.where((tprev <= slot) & (slot < tincl), 1.0, 0.0)
        tile = jnp.sum(jnp.where(tincl <= slot, 1.0, 0.0), axis=0, keepdims=True)
        s_loc = slot - jnp.sum(in_tile * tprev, axis=0, keepdims=True)
        counts = _dot_tn(loc_e.astype(BF16), in_tile.astype(BF16))
        lane = jnp.sum(jnp.where(counts <= s_loc, 1.0, 0.0), axis=0, keepdims=True)
        idx_ref[pl.ds(ex, 1), :] = (tile * LANES + lane).astype(I32)


def _route(aff_t, cap):
    bsz, e, n = aff_t.shape
    nt = n // LANES
    rows = nt * e
    ri = np.arange(rows)
    lmat = ((ri[None, :] % e == ri[:, None] % e) & (ri[None, :] // e < ri[:, None] // e))
    lmat = jnp.asarray(lmat, BF16)
    st = jax.ShapeDtypeStruct((bsz, rows, LANES), I32)
    ospec = pl.BlockSpec((None, rows, LANES), lambda b: (b, 0, 0))
    return pl.pallas_call(
        functools.partial(_route_kernel, cap),
        grid=(bsz,),
        in_specs=[pl.BlockSpec((None, e, n), lambda b: (b, 0, 0)),
                  pl.BlockSpec((rows, rows), lambda b: (0, 0))],
        out_specs=(ospec, ospec, pl.BlockSpec((None, e, cap), lambda b: (b, 0, 0))),
        out_shape=(st, st, jax.ShapeDtypeStruct((bsz, e, cap), I32)),
        scratch_shapes=[pltpu.VMEM((rows, LANES), F32), pltpu.VMEM((rows, LANES), F32)],
        compiler_params=_cparams("arbitrary"),
        name="route",
    )(aff_t, lmat)


def _slot_window(base):
    return pl.multiple_of((base // BF16_ROWS) * BF16_ROWS, BF16_ROWS)


def _expert_kernel(cap, idx_ref, h2_hbm, wg_ref, wu_ref, wd_ref, ye_ref, xbuf, acc, sem):
    e, f = pl.program_id(1), pl.program_id(2)
    n_exp, nf = pl.num_programs(1), pl.num_programs(2)
    last = pl.num_programs(0) * n_exp - 1
    lin = pl.program_id(0) * n_exp + e
    slot = lin % 2
    per_step = cap // nf
    rpt = wg_ref.shape[0] // LANES

    def row_copy(lin_t, part, i, slot_t):
        tok = idx_ref[(lin_t * nf + part) * per_step + i]
        return pltpu.make_async_copy(h2_hbm.at[pl.ds(pl.multiple_of(tok * rpt, rpt), rpt), :],
                                     xbuf.at[slot_t, part, pl.ds(i * rpt, rpt), :],
                                     sem.at[slot_t])

    def wait_rows(slot_t):
        pltpu.make_async_copy(xbuf.at[slot_t], xbuf.at[slot_t], sem.at[slot_t]).wait()

    @pl.when((lin == 0) & (f == 0))
    def _():
        acc[...] = jnp.zeros_like(acc)
        for part in range(nf):
            def first(i, carry):
                row_copy(lin, part, i, slot).start()
                return carry
            lax.fori_loop(0, per_step, first, 0)

    @pl.when(f == 0)
    def _():
        wait_rows(slot)

    nxt = jnp.minimum(lin + 1, last)
    for i in range(per_step):
        row_copy(nxt, f, i, 1 - slot).start()

    @pl.when((lin == last) & (f == nf - 1))
    def _():
        wait_rows(1 - slot)

    xv = jnp.concatenate(
        [jnp.concatenate([xbuf[slot, part, pl.ds(a, per_step, stride=rpt), :].astype(BF16)
                          for a in range(rpt)], axis=1) for part in range(nf)], axis=0)
    hg = _dot(xv, wg_ref[...].astype(BF16))
    hu = _dot(xv, wu_ref[...].astype(BF16))
    hid = (hg * _sigmoid(hg) * hu).astype(BF16)
    acc[...] = jnp.where(f > 0, acc[...], 0.0) + _dot(hid, wd_ref[...].astype(BF16))

    @pl.when(f == nf - 1)
    def _():
        ye_ref[0:cap, :] = acc[...].astype(BF16)
        ye_ref[cap:, :] = jnp.zeros((ye_ref.shape[0] - cap, ye_ref.shape[1]), BF16)


def _experts(h2, idx, w_gate, w_up, w_down, cap, rows, tf):
    n_exp, d, df = w_gate.shape
    bsz = h2.shape[0]
    rpt = d // LANES
    n = h2.shape[1] // rpt
    nf = df // tf
    assert cap % nf == 0
    rows_global = (idx + (jnp.arange(bsz, dtype=I32) * n)[:, None, None]).reshape(-1)
    return pl.pallas_call(
        functools.partial(_expert_kernel, cap),
        grid_spec=pltpu.PrefetchScalarGridSpec(
            num_scalar_prefetch=1,
            grid=(bsz, n_exp, nf),
            in_specs=[pl.BlockSpec(memory_space=pl.ANY),
                      pl.BlockSpec((None, d, tf), lambda b, e, f, s: (e, 0, f)),
                      pl.BlockSpec((None, d, tf), lambda b, e, f, s: (e, 0, f)),
                      pl.BlockSpec((None, tf, d), lambda b, e, f, s: (e, f, 0))],
            out_specs=pl.BlockSpec((None, None, rows, d), lambda b, e, f, s: (b, e, 0, 0)),
            scratch_shapes=[pltpu.VMEM((2, nf, (cap // nf) * rpt, LANES), F32),
                            pltpu.VMEM((cap, d), F32), pltpu.SemaphoreType.DMA((2,))]),
        out_shape=jax.ShapeDtypeStruct((bsz, n_exp, rows, d), BF16),
        compiler_params=_cparams("arbitrary", "arbitrary", "arbitrary"),
        name="experts",
    )(rows_global, h2.reshape(bsz * n * rpt, LANES), w_gate, w_up, w_down)


def _combine_windows(tm, n_exp):
    mean = tm * EC_CAPACITY / n_exp
    usual = mean + 5.5 * (mean * (1 - EC_CAPACITY / n_exp)) ** 0.5 + BF16_ROWS - 1
    usual = min(-(-int(usual) // BF16_ROWS) * BF16_ROWS, tm + BF16_ROWS)
    return usual, tm + BF16_ROWS


def _combine_kernel(n_exp, nsub, base_ref, ovf_ref, pos_ref, aff_ref, x1_ref, gt2_ref, fg_ref,
                    ye_hbm, out_ref, buf, fbuf, ffn_scr, sem, fsem):
    b, j = pl.program_id(0), pl.program_id(1)
    nt = pl.num_programs(1)
    step = b * nt + j
    slot = step % 2
    wp, wfull = buf.shape[2], fbuf.shape[0]
    d = buf.shape[-1]

    def start_of(tile, ex):
        return _slot_window(base_ref[tile * nsub * n_exp + ex])

    def window_copy(tile, ex, dst_slot):
        return pltpu.make_async_copy(ye_hbm.at[tile // nt, ex, pl.ds(start_of(tile, ex), wp), :],
                                     buf.at[dst_slot, ex], sem.at[dst_slot])

    def full_copy(ex):
        return pltpu.make_async_copy(ye_hbm.at[b, ex, pl.ds(start_of(step, ex), wfull), :],
                                     fbuf, fsem)

    @pl.when(step == 0)
    def _():
        for ex in range(n_exp):
            window_copy(step, ex, slot).start()

    @pl.when(step + 1 < pl.num_programs(0) * nt)
    def _():
        for ex in range(n_exp):
            window_copy(step + 1, ex, 1 - slot).start()

    def gates_t(rows, ex):
        srow = lax.broadcasted_iota(I32, (rows, LANES), 0)
        s0 = start_of(step, ex)
        parts = []
        for u in range(nsub):
            rel = pos_ref[pl.ds(u * n_exp + ex, 1), :] - s0
            gate = aff_ref[pl.ds(ex, 1), u * LANES:(u + 1) * LANES]
            parts.append(jnp.where(srow == rel, gate, 0.0).astype(BF16))
        return jnp.concatenate(parts, axis=1)

    for ex in range(n_exp):
        window_copy(step, ex, slot).wait()

    @pl.when(ovf_ref[step] == 0)
    def _():
        g_all = jnp.concatenate([gates_t(wp, ex) for ex in range(n_exp)], axis=0)
        ffn_scr[...] = _dot_tn(g_all, buf[slot].reshape(n_exp * wp, d))

    @pl.when(ovf_ref[step] != 0)
    def _():
        ffn_scr[...] = jnp.zeros_like(ffn_scr)
        for ex in range(n_exp):
            full_copy(ex).start()
            full_copy(ex).wait()
            ffn_scr[...] += _dot_tn(gates_t(wfull, ex), fbuf[...])

    x2 = x1_ref[...] + gt2_ref[...] * ffn_scr[...]
    ms = jnp.mean(x2 * x2, axis=-1, keepdims=True)
    out_ref[...] = x2 * lax.rsqrt(ms + EPS) * fg_ref[...]


def _combine(ye, pos, aff_t, before, x1, gt2, final_g, cap, tm):
    bsz, n, d = x1.shape
    n_exp = aff_t.shape[1]
    nsub = tm // LANES
    nt = n // tm
    wp, wfull = _combine_windows(tm, n_exp)
    starts = before[:, :, 0].reshape(bsz, n // LANES, n_exp)
    tile_starts = starts[:, ::nsub]
    tile_ends = jnp.concatenate([tile_starts[:, 1:], jnp.full((bsz, 1, n_exp), cap, I32)], axis=1)
    overflow = (tile_starts % BF16_ROWS) + (tile_ends - tile_starts) > wp
    ovf = jnp.any(overflow, axis=-1).astype(I32).reshape(-1)
    return pl.pallas_call(
        functools.partial(_combine_kernel, n_exp, nsub),
        grid_spec=pltpu.PrefetchScalarGridSpec(
            num_scalar_prefetch=2,
            grid=(bsz, nt),
            in_specs=[pl.BlockSpec((None, nsub * n_exp, LANES), lambda b, j, s, o: (b, j, 0)),
                      pl.BlockSpec((None, n_exp, tm), lambda b, j, s, o: (b, 0, j)),
                      pl.BlockSpec((None, tm, d), lambda b, j, s, o: (b, j, 0)),
                      pl.BlockSpec((None, 1, d), lambda b, j, s, o: (b, 0, 0)),
                      pl.BlockSpec((1, d), lambda b, j, s, o: (0, 0)),
                      pl.BlockSpec(memory_space=pl.ANY)],
            out_specs=pl.BlockSpec((None, tm, d), lambda b, j, s, o: (b, j, 0)),
            scratch_shapes=[pltpu.VMEM((2, n_exp, wp, d), BF16),
                            pltpu.VMEM((wfull, d), BF16),
                            pltpu.VMEM((tm, d), F32),
                            pltpu.SemaphoreType.DMA((2,)), pltpu.SemaphoreType.DMA(())]),
        out_shape=jax.ShapeDtypeStruct((bsz, n, d), F32),
        compiler_params=_cparams("arbitrary", "arbitrary"),
        name="combine",
    )(starts.reshape(-1), ovf, pos, aff_t, x1, gt2, final_g, ye)


def _pick(n, pref):
    t = min(n, pref)
    while n % t:
        t //= 2
    return t


def _tiles(n, d_expert):
    return dict(inproj=_pick(n, 1024), gla=_pick(n, 512), mixer=_pick(n, 512),
                combine=_pick(n, 512), expert_f=_pick(d_expert, 512))


def kernel(x, c, ctx, c_ctx, w_ada, b_ada, norm1_g, norm2_g, w_in, gla_w_a_up, gla_b_a,
           gla_norm_g, gla_w_o, conv_w, conv_w_out, merge_w_out, router_w,
           exp_w_gate, exp_w_up, exp_w_down, final_g):
    bsz, n, d = x.shape
    depth = w_ada.shape[0]
    assert depth == 1, "single-layer trunk"
    rank, dk = gla_w_a_up.shape[-2:]
    dv = gla_w_o.shape[1]
    hv = gla_norm_g.shape[-1]
    heads = dv // hv
    cch = conv_w.shape[-1]
    n_exp = router_w.shape[-1]
    cap = EC_CAPACITY * n // n_exp
    assert 2 * dk == d and dv == d and cch == d and 2 * rank <= LANES
    assert n % LANES == 0 and n % GRID_W == 0 and bsz + 1 <= 8 and n_exp % 8 == 0

    cond = jnp.zeros((8, d), F32).at[:bsz].set(c).at[bsz].set(c_ctx)
    mod = _adaln(cond, w_ada[0], b_ada[0]).reshape(8, N_MOD, d)
    sh1, sc1, gt1, sh2, sc2, gt2 = [mod[:bsz, k][:, None, :] for k in range(N_MOD)]
    csh1, csc1 = mod[bsz:bsz + 1, 0], mod[bsz:bsz + 1, 1]

    lr0 = 2 * dk + 2 * dv
    u0 = lr0 + 2 * rank
    groups = ((0, lr0), (u0 + cch, cch), (u0, cch), (u0 + 2 * cch, cch + 2 * d), (lr0, LANES))
    tiles = tuple((r0 + o, min(d, cnt - o)) for r0, cnt in groups for o in range(0, cnt, d))
    wt = _wprep(jnp.swapaxes(w_in[0], 0, 1), tiles)
    wup = gla_w_a_up[0].astype(BF16)
    wupf = jnp.zeros((LANES, dk), BF16).at[:rank].set(wup[0])
    wupb = jnp.zeros((LANES, dk), BF16).at[rank:2 * rank].set(wup[1])
    baf, bab = gla_b_a[0, 0:1], gla_b_a[0, 1:2]
    n1g, n2g = norm1_g[0:1], norm2_g[0:1]

    t = _tiles(n, exp_w_gate.shape[-1])
    s0f, s0b = _ctx_states(ctx, csh1, csc1, n1g, wt, wupf, wupb, baf, bab, heads, dk, dv)
    z, lr = _inproj(x, sh1, sc1, n1g, wt, t["inproj"])
    o_f, o_b = _gla(z, lr, wupf, wupb, baf, bab, s0f, s0b, heads, dk, dv, t["gla"])
    x1, h2, aff_t = _mixer(
        o_f, o_b, z, x, jnp.tile(gla_norm_g[0:1], (1, heads)), conv_w[0],
        gla_w_o[0].astype(BF16), conv_w_out[0].astype(BF16), merge_w_out[0].astype(BF16),
        gt1, sh2, sc2, n2g, router_w[0].T.astype(BF16), heads, dk, t["mixer"])

    pos, before, idx = _route(aff_t, cap)
    ye = _experts(h2, idx, exp_w_gate[0], exp_w_up[0], exp_w_down[0], cap,
                  cap + _combine_windows(t["combine"], n_exp)[1], t["expert_f"])
    return _combine(ye, pos, aff_t, before, x1, gt2, final_g.reshape(1, d), cap, t["combine"])
```

```python
import functools

import numpy as np
import jax
import jax.numpy as jnp
from jax import lax
from jax.experimental import pallas as pl
from jax.experimental.pallas import tpu as pltpu

F32 = jnp.float32
BF16 = jnp.bfloat16
I32 = jnp.int32

EPS = 1e-6
N_MOD = 6
GRID_W = 64
GLA_CHUNK = 64
GLA_GATE_NORM = 16.0
EC_CAPACITY = 2

LANES = 128
MXU_DEPTH = 256
BF16_ROWS = 16
VMEM_LIMIT = 56 * 1024 * 1024


def _cparams(*sem):
    return pltpu.CompilerParams(dimension_semantics=sem, vmem_limit_bytes=VMEM_LIMIT)


def _dot(a, b):
    return jnp.dot(a, b, preferred_element_type=F32)


def _dot_nt(a, b):
    return lax.dot_general(a, b, (((1,), (1,)), ((), ())), preferred_element_type=F32)


def _dot_tn(a, b):
    return lax.dot_general(a, b, (((0,), (0,)), ((), ())), preferred_element_type=F32)


def _sigmoid(v):
    return 1.0 / (1.0 + jnp.exp(-v))


def _log_sigmoid(v):
    return jnp.minimum(v, 0.0) - jnp.log(1.0 + jnp.exp(-jnp.abs(v)))


def _rms_mod(xv, g, shift, scale):
    ms = jnp.mean(xv * xv, axis=-1, keepdims=True)
    y = xv * lax.rsqrt(ms + EPS) * g
    return y * (1.0 + scale) + shift


def _split_bf16(v):
    hi = v.astype(BF16)
    lo = (v - hi.astype(F32)).astype(BF16)
    return hi, lo


def _adaln_kernel(c_ref, w_ref, b_ref, o_ref):
    cv = c_ref[...]
    s = (cv * _sigmoid(cv)).astype(BF16)
    o_ref[...] = _dot(s, w_ref[...].astype(BF16)) + b_ref[...]


def _adaln(cond, w, b):
    rows, d = cond.shape
    nout = w.shape[1]
    tn = _pick(nout, 2 * d)
    return pl.pallas_call(
        _adaln_kernel,
        grid=(nout // tn,),
        in_specs=[pl.BlockSpec((rows, d), lambda j: (0, 0)),
                  pl.BlockSpec((d, tn), lambda j: (0, j)),
                  pl.BlockSpec((1, tn), lambda j: (0, j))],
        out_specs=pl.BlockSpec((rows, tn), lambda j: (0, j)),
        out_shape=jax.ShapeDtypeStruct((rows, nout), F32),
        compiler_params=_cparams("arbitrary"),
        name="adaln",
    )(cond, w, b.reshape(1, nout))


def _wprep_kernel(tiles, wt_hbm, o_ref, buf, sem):
    def copy(t):
        row, cnt = tiles[t]
        return pltpu.make_async_copy(wt_hbm.at[pl.ds(row, cnt), :],
                                     buf.at[t % 2, pl.ds(0, cnt), :], sem.at[t % 2])

    copy(0).start()
    out_row = 0
    for t, (_, cnt) in enumerate(tiles):
        if t + 1 < len(tiles):
            copy(t + 1).start()
        copy(t).wait()
        o_ref[out_row:out_row + cnt, :] = buf[t % 2, 0:cnt, :].astype(BF16)
        out_row += cnt


def _wprep(wt, tiles):
    d = wt.shape[1]
    rows = sum(cnt for _, cnt in tiles)
    return pl.pallas_call(
        functools.partial(_wprep_kernel, tiles),
        in_specs=[pl.BlockSpec(memory_space=pl.ANY)],
        out_specs=pl.BlockSpec((rows, d), lambda: (0, 0)),
        out_shape=jax.ShapeDtypeStruct((rows, d), BF16),
        scratch_shapes=[pltpu.VMEM((2, max(cnt for _, cnt in tiles), d), F32),
                        pltpu.SemaphoreType.DMA((2,))],
        compiler_params=pltpu.CompilerParams(vmem_limit_bytes=VMEM_LIMIT),
        name="wprep",
    )(wt)


def _ctx_kernel(heads, ctx_ref, sh_ref, sc_ref, g_ref, wk_ref, wv_ref, wlr_ref,
                wupf_ref, wupb_ref, baf_ref, bab_ref, sf_ref, sb_ref):
    n = ctx_ref.shape[0]
    hc = _rms_mod(ctx_ref[...], g_ref[...], sh_ref[...], sc_ref[...]).astype(BF16)
    k = _dot_nt(hc, wk_ref[...])
    v = _dot_nt(hc, wv_ref[...]).astype(BF16)
    lr = _dot_nt(hc, wlr_ref[...]).astype(BF16)
    hk = k.shape[1] // heads
    hv = v.shape[1] // heads
    row = lax.broadcasted_iota(I32, (n, n), 0)
    col = lax.broadcasted_iota(I32, (n, n), 1)
    for wup_ref, ba_ref, s_ref, tri, last in (
            (wupf_ref, baf_ref, sf_ref, col <= row, n - 1),
            (wupb_ref, bab_ref, sb_ref, col >= row, 0)):
        la = _log_sigmoid(_dot(lr, wup_ref[...]) + ba_ref[...]) * (1.0 / GLA_GATE_NORM)
        hi, lo = _split_bf16(la)
        t = jnp.where(tri, 1.0, 0.0).astype(BF16)
        b = _dot(t, hi) + _dot(t, lo)
        kd = (k * jnp.exp(b[last:last + 1, :] - b)).astype(BF16)
        for h in range(heads):
            s_ref[h] = _dot_tn(v[:, h * hv:(h + 1) * hv], kd[:, h * hk:(h + 1) * hk])


def _ctx_states(ctx, sh, sc, g, wt, wupf, wupb, baf, bab, heads, dk, dv):
    bsz, n, d = ctx.shape
    lr_blk = (wt.shape[0] - LANES) // LANES
    hk, hv = dk // heads, dv // heads
    full = lambda *shape: pl.BlockSpec(shape, lambda b: (0,) * len(shape))
    st = jax.ShapeDtypeStruct((bsz, heads, hv, hk), F32)
    sspec = pl.BlockSpec((None, heads, hv, hk), lambda b: (b, 0, 0, 0))
    return pl.pallas_call(
        functools.partial(_ctx_kernel, heads),
        grid=(bsz,),
        in_specs=[pl.BlockSpec((None, n, d), lambda b: (b, 0, 0)),
                  full(1, d), full(1, d), full(1, d),
                  pl.BlockSpec((dk, d), lambda b: (1, 0)),
                  pl.BlockSpec((dv, d), lambda b: (2 * dk // dv, 0)),
                  pl.BlockSpec((LANES, d), lambda b: (lr_blk, 0)),
                  full(LANES, dk), full(LANES, dk), full(1, dk), full(1, dk)],
        out_specs=(sspec, sspec),
        out_shape=(st, st),
        compiler_params=_cparams("arbitrary"),
        name="ctx_state",
    )(ctx, sh, sc, g, wt, wt, wt, wupf, wupb, baf, bab)


def _inproj_kernel(x_ref, sh_ref, sc_ref, g_ref, w_ref, wlr_ref, z_ref, lr_ref, h_scr):
    j = pl.program_id(2)
    d = x_ref.shape[1]

    @pl.when(j == 0)
    def _():
        hb = _rms_mod(x_ref[...], g_ref[...], sh_ref[...], sc_ref[...]).astype(BF16)
        h_scr[...] = hb
        lr_ref[...] = _dot_nt(hb, wlr_ref[...])
        z_ref[...] = _dot_nt(hb, w_ref[...]).astype(BF16)

    @pl.when(j == 1)
    def _():
        a = _dot_nt(h_scr[...], w_ref[...])
        gv = a[:, :d]
        z_ref[:, :d] = (gv * _sigmoid(gv)).astype(BF16)
        z_ref[:, d:] = a[:, d:].astype(BF16)

    @pl.when(j == 2)
    def _():
        a = _dot_nt(h_scr[...], w_ref[...])
        z_ref[:, :d] = (a[:, d:] * a[:, :d]).astype(BF16)
        z_ref[:, d:] = jnp.zeros((z_ref.shape[0], d), BF16)

    @pl.when(j == 3)
    def _():
        z_ref[...] = _sigmoid(_dot_nt(h_scr[...], w_ref[...])).astype(BF16)


def _inproj(x, sh, sc, g, wt, tm):
    bsz, n, d = x.shape
    p = wt.shape[0] - LANES
    tn = 2 * d
    assert p == 4 * tn
    return pl.pallas_call(
        _inproj_kernel,
        grid=(bsz, n // tm, p // tn),
        in_specs=[pl.BlockSpec((None, tm, d), lambda b, i, j: (b, i, 0)),
                  pl.BlockSpec((None, 1, d), lambda b, i, j: (b, 0, 0)),
                  pl.BlockSpec((None, 1, d), lambda b, i, j: (b, 0, 0)),
                  pl.BlockSpec((1, d), lambda b, i, j: (0, 0)),
                  pl.BlockSpec((tn, d), lambda b, i, j: (j, 0)),
                  pl.BlockSpec((LANES, d), lambda b, i, j: (p // LANES, 0))],
        out_specs=(pl.BlockSpec((None, tm, tn), lambda b, i, j: (b, i, j)),
                   pl.BlockSpec((None, tm, LANES), lambda b, i, j: (b, i, 0))),
        out_shape=(jax.ShapeDtypeStruct((bsz, n, p), BF16),
                   jax.ShapeDtypeStruct((bsz, n, LANES), F32)),
        scratch_shapes=[pltpu.VMEM((tm, d), BF16)],
        compiler_params=_cparams("arbitrary", "arbitrary", "arbitrary"),
        name="inproj",
    )(x, sh, sc, g, wt, wt)


def _gla_kernel(heads, dk, qkvf_ref, lrf_ref, qkvb_ref, lrb_ref,
                wupf_ref, wupb_ref, baf_ref, bab_ref, trif_ref, trib_ref, s0f_ref, s0b_ref,
                of_ref, ob_ref, sf_scr, sb_scr):
    tb = qkvf_ref.shape[0]
    dv = qkvf_ref.shape[1] - 2 * dk
    hk, hv = dk // heads, dv // heads
    ck = GLA_CHUNK
    nck = tb // ck
    q_scale = hk ** -0.5

    @pl.when(pl.program_id(1) == 0)
    def _():
        sf_scr[...] = s0f_ref[...]
        sb_scr[...] = s0b_ref[...]

    crow = lax.broadcasted_iota(I32, (ck, ck), 0)
    ccol = lax.broadcasted_iota(I32, (ck, ck), 1)

    dirs = (
        (qkvf_ref, lrf_ref, wupf_ref, baf_ref, of_ref, sf_scr,
         trif_ref, ccol <= crow, ck - 1, range(nck)),
        (qkvb_ref, lrb_ref, wupb_ref, bab_ref, ob_ref, sb_scr,
         trib_ref, ccol >= crow, 0, range(nck - 1, -1, -1)),
    )
    b_alls = []
    for _, lr_ref, wup_ref, ba_ref, _, _, tri, _, _, _ in dirs:
        la = _log_sigmoid(_dot(lr_ref[...].astype(BF16), wup_ref[...]) + ba_ref[...])
        la = la * (1.0 / GLA_GATE_NORM)
        hi, lo = _split_bf16(la)
        t = tri[...]
        tr = t.shape[0]
        b_alls.append(jnp.concatenate(
            [_dot(t, hi[r:r + tr]) + _dot(t, lo[r:r + tr]) for r in range(0, tb, tr)], axis=0))
    for k in range(nck):
        work = []
        for di, (qkv_ref, _, _, _, _, _, _, cmask, last, order) in enumerate(dirs):
            r0 = order[k] * ck
            b = b_alls[di][r0:r0 + ck, :]
            b_last = b[last:last + 1, :]
            qc = qkv_ref[r0:r0 + ck, 0:dk].astype(F32) * q_scale
            kc = qkv_ref[r0:r0 + ck, dk:2 * dk].astype(F32)
            q_in = (qc * jnp.exp(b)).astype(BF16)
            k_in = (kc * jnp.exp(-b)).astype(BF16)
            k_dec = (kc * jnp.exp(b_last - b)).astype(BF16)
            decay = jnp.exp(b_last)
            units = []
            for h in range(heads):
                qh = q_in[:, h * hk:(h + 1) * hk]
                vh = qkv_ref[r0:r0 + ck, 2 * dk + h * hv:2 * dk + (h + 1) * hv]
                scores = jnp.where(cmask, _dot_nt(qh, k_in[:, h * hk:(h + 1) * hk]), 0.0)
                o_intra = _dot(scores.astype(BF16), vh)
                ut = _dot_tn(vh, k_dec[:, h * hk:(h + 1) * hk])
                units.append((qh, o_intra, ut, decay[:, h * hk:(h + 1) * hk]))
            work.append((r0, units))
        for (r0, units), (_, _, _, _, o_ref, s_scr, _, _, _, _) in zip(work, dirs):
            for h, (qh, o_intra, ut, dec) in enumerate(units):
                st = s_scr[h]
                o = o_intra + _dot_nt(qh, st.astype(BF16))
                o_ref[r0:r0 + ck, h * hv:(h + 1) * hv] = o.astype(BF16)
                s_scr[h] = st * dec + ut


def _gla(z, lr, wupf, wupb, baf, bab, s0f, s0b, heads, dk, dv, tb):
    bsz, n, _ = z.shape
    nb = n // tb
    hk, hv = dk // heads, dv // heads
    fwd = lambda cb: (lambda b, i: (b, i, cb))
    bwd = lambda cb: (lambda b, i: (b, nb - 1 - i, cb))
    full = lambda *shape: pl.BlockSpec(shape, lambda b, i: (0,) * len(shape))
    sspec = pl.BlockSpec((None, heads, hv, hk), lambda b, i: (b, 0, 0, 0))
    ost = jax.ShapeDtypeStruct((bsz, n, dv), BF16)
    tr = min(tb, MXU_DEPTH)
    assert tb % tr == 0 and tr % GLA_CHUNK == 0
    ri = np.arange(tr)
    same_chunk = (ri[:, None] // GLA_CHUNK) == (ri[None, :] // GLA_CHUNK)
    trif = jnp.asarray(same_chunk & (ri[None, :] <= ri[:, None]), BF16)
    trib = jnp.asarray(same_chunk & (ri[None, :] >= ri[:, None]), BF16)
    qkv = 2 * dk + dv
    return pl.pallas_call(
        functools.partial(_gla_kernel, heads, dk),
        grid=(bsz, nb),
        in_specs=[pl.BlockSpec((None, tb, qkv), fwd(0)), pl.BlockSpec((None, tb, LANES), fwd(0)),
                  pl.BlockSpec((None, tb, qkv), bwd(0)), pl.BlockSpec((None, tb, LANES), bwd(0)),
                  full(LANES, dk), full(LANES, dk), full(1, dk), full(1, dk),
                  full(tr, tr), full(tr, tr), sspec, sspec],
        out_specs=(pl.BlockSpec((None, tb, dv), fwd(0)), pl.BlockSpec((None, tb, dv), bwd(0))),
        out_shape=(ost, ost),
        scratch_shapes=[pltpu.VMEM((heads, hv, hk), F32), pltpu.VMEM((heads, hv, hk), F32)],
        compiler_params=_cparams("arbitrary", "arbitrary"),
        name="gla",
    )(z, lr, z, lr, wupf, wupb, baf, bab, trif, trib, s0f, s0b)


def _mixer_kernel(heads, of_ref, ob_ref, sg_ref, bg_ref, cu_ref, srg_ref, src_ref, x_ref,
                  gng_ref, cw_ref, wo_ref, wco_ref, wm_ref, gt1_ref, sh2_ref, sc2_ref, n2g_ref,
                  rwt_ref, x1_ref, h2_ref, aff_ref):
    tm, dv = of_ref.shape
    hv = dv // heads
    o = of_ref[...].astype(F32) + ob_ref[...].astype(F32)
    parts = []
    for h in range(heads):
        oh = o[:, h * hv:(h + 1) * hv]
        ms = jnp.mean(oh * oh, axis=-1, keepdims=True)
        parts.append(oh * lax.rsqrt(ms + EPS))
    on = jnp.concatenate(parts, axis=1) * gng_ref[...]
    y_gla = _dot((on * sg_ref[...].astype(F32)).astype(BF16), wo_ref[...])

    cu = cu_ref[...].astype(F32)
    gcol = lax.broadcasted_iota(I32, cu.shape, 0) & (GRID_W - 1)
    left = jnp.where(gcol == 0, 0.0, pltpu.roll(cu, 1, 0))
    right = jnp.where(gcol == GRID_W - 1, 0.0, pltpu.roll(cu, tm - 1, 0))
    cw = cw_ref[...]
    conv = left * cw[0:1, :] + cu * cw[1:2, :] + right * cw[2:3, :]
    y_conv = _dot((bg_ref[...].astype(F32) * conv).astype(BF16), wco_ref[...])

    merged = srg_ref[...].astype(F32) * y_gla + src_ref[...].astype(F32) * y_conv
    y = _dot(merged.astype(BF16), wm_ref[...])
    x1 = x_ref[...] + gt1_ref[...] * y
    x1_ref[...] = x1
    h2 = _rms_mod(x1, n2g_ref[...], sh2_ref[...], sc2_ref[...])
    rpt = h2.shape[1] // LANES
    for a in range(rpt):
        h2_ref[pl.ds(a, tm, stride=rpt), :] = h2[:, a * LANES:(a + 1) * LANES]
    logits = _dot_nt(rwt_ref[...], h2.astype(BF16))
    ex = jnp.exp(logits - jnp.max(logits, axis=0, keepdims=True))
    aff_ref[...] = ex / jnp.sum(ex, axis=0, keepdims=True)


def _mixer(o_f, o_b, z, x, gng, conv_w, w_o, w_co, w_m, gt1, sh2, sc2, n2g, rwt, heads, dk, tm):
    bsz, n, d = x.shape
    dv = o_f.shape[2]
    c = conv_w.shape[1]
    e = rwt.shape[0]
    zb = lambda cb: pl.BlockSpec((None, tm, d), lambda b, i: (b, i, cb))
    base = (2 * dk + dv) // d
    tok = lambda w: pl.BlockSpec((None, tm, w), lambda b, i: (b, i, 0))
    full = lambda *shape: pl.BlockSpec(shape, lambda b, i: (0,) * len(shape))
    perb = pl.BlockSpec((None, 1, d), lambda b, i: (b, 0, 0))
    return pl.pallas_call(
        functools.partial(_mixer_kernel, heads),
        grid=(bsz, n // tm),
        in_specs=[tok(dv), tok(dv), zb(base), zb(base + 1), zb(base + 2),
                  zb(base + 4), zb(base + 5), tok(d),
                  full(1, dv), full(3, c), full(dv, d), full(c, d), full(d, d),
                  perb, perb, perb, full(1, d), full(e, d)],
        out_specs=(tok(d), pl.BlockSpec((None, tm * (d // LANES), LANES), lambda b, i: (b, i, 0)),
                   pl.BlockSpec((None, e, tm), lambda b, i: (b, 0, i))),
        out_shape=(jax.ShapeDtypeStruct((bsz, n, d), F32),
                   jax.ShapeDtypeStruct((bsz, n * (d // LANES), LANES), F32),
                   jax.ShapeDtypeStruct((bsz, e, n), F32)),
        compiler_params=_cparams("arbitrary", "arbitrary"),
        name="mixer",
    )(o_f, o_b, z, z, z, z, z, x, gng, conv_w, w_o, w_co, w_m, gt1, sh2, sc2, n2g, rwt)


def _route_kernel(cap, aff_ref, lmat_ref, pos_ref, base_ref, idx_ref, loc_scr, bef_scr):
    e, n = aff_ref.shape
    nt = n // LANES
    aff = aff_ref[...]

    def count(mask):
        return jnp.sum(jnp.where(mask, 1.0, 0.0), axis=1, keepdims=True)

    def search(k, tbits):
        cand = tbits | jnp.left_shift(jnp.int32(1), 30 - k)
        ok = count(aff >= lax.bitcast_convert_type(cand, F32)) >= cap
        return jnp.where(ok, cand, tbits)

    tbits = lax.fori_loop(0, 31, search, jnp.zeros((e, 1), I32))
    thr = lax.bitcast_convert_type(tbits, F32)
    gt = aff > thr
    eq = aff == thr
    need = cap - count(gt)

    def stack(mask):
        m = jnp.where(mask, 1.0, 0.0)
        return jnp.concatenate([m[:, j * LANES:(j + 1) * LANES] for j in range(nt)], axis=0)

    r = lax.broadcasted_iota(I32, (LANES, LANES), 0)
    cl = lax.broadcasted_iota(I32, (LANES, LANES), 1)
    upper = jnp.where(r <= cl, 1.0, 0.0).astype(BF16)
    ones = jnp.ones((LANES, LANES), BF16)
    lmat = lmat_ref[...]

    def cumsum(ms):
        msb = ms.astype(BF16)
        before = _dot(_dot(lmat, msb).astype(BF16), ones)
        return _dot(msb, upper), before

    eq_s = stack(eq)
    loc_eq, before_eq = cumsum(eq_s)
    need_s = jnp.concatenate([need] * nt, axis=0)
    sel = jnp.maximum(stack(gt), jnp.where(loc_eq + before_eq <= need_s, eq_s, 0.0))
    loc, before = cumsum(sel)
    pos_ref[...] = jnp.where(sel > 0.0, loc + before - 1.0, -1.0).astype(I32)
    base_ref[...] = before.astype(I32)

    loc_scr[...] = loc
    bef_scr[...] = before
    slot = lax.broadcasted_iota(I32, (1, cap), 1).astype(F32)
    for ex in range(e):
        loc_e = loc_scr[pl.ds(ex, nt, stride=e), :]
        tprev = bef_scr[pl.ds(ex, nt, stride=e), :][:, 0:1]
        tincl = tprev + loc_e[:, LANES - 1:LANES]
        in_tile = jnp.where((tprev <= slot) & (slot < tincl), 1.0, 0.0)
        tile = jnp.sum(jnp.where(tincl <= slot, 1.0, 0.0), axis=0, keepdims=True)
        s_loc = slot - jnp.sum(in_tile * tprev, axis=0, keepdims=True)
        counts = _dot_tn(loc_e.astype(BF16), in_tile.astype(BF16))
        lane = jnp.sum(jnp.where(counts <= s_loc, 1.0, 0.0), axis=0, keepdims=True)
        idx_ref[pl.ds(ex, 1), :] = (tile * LANES + lane).astype(I32)


def _route(aff_t, cap):
    bsz, e, n = aff_t.shape
    nt = n // LANES
    rows = nt * e
    ri = np.arange(rows)
    lmat = ((ri[None, :] % e == ri[:, None] % e) & (ri[None, :] // e < ri[:, None] // e))
    lmat = jnp.asarray(lmat, BF16)
    st = jax.ShapeDtypeStruct((bsz, rows, LANES), I32)
    ospec = pl.BlockSpec((None, rows, LANES), lambda b: (b, 0, 0))
    return pl.pallas_call(
        functools.partial(_route_kernel, cap),
        grid=(bsz,),
        in_specs=[pl.BlockSpec((None, e, n), lambda b: (b, 0, 0)),
                  pl.BlockSpec((rows, rows), lambda b: (0, 0))],
        out_specs=(ospec, ospec, pl.BlockSpec((None, e, cap), lambda b: (b, 0, 0))),
        out_shape=(st, st, jax.ShapeDtypeStruct((bsz, e, cap), I32)),
        scratch_shapes=[pltpu.VMEM((rows, LANES), F32), pltpu.VMEM((rows, LANES), F32)],
        compiler_params=_cparams("arbitrary"),
        name="route",
    )(aff_t, lmat)


def _slot_window(base):
    return pl.multiple_of((base // BF16_ROWS) * BF16_ROWS, BF16_ROWS)


def _expert_kernel(cap, idx_ref, h2_hbm, wg_ref, wu_ref, wd_ref, ye_ref, xbuf, acc, sem):
    e, f = pl.program_id(1), pl.program_id(2)
    n_exp, nf = pl.num_programs(1), pl.num_programs(2)
    last = pl.num_programs(0) * n_exp - 1
    lin = pl.program_id(0) * n_exp + e
    slot = lin % 2
    per_step = cap // nf
    rpt = wg_ref.shape[0] // LANES

    def row_copy(lin_t, part, i, slot_t):
        tok = idx_ref[(lin_t * nf + part) * per_step + i]
        return pltpu.make_async_copy(h2_hbm.at[pl.ds(pl.multiple_of(tok * rpt, rpt), rpt), :],
                                     xbuf.at[slot_t, part, pl.ds(i * rpt, rpt), :],
                                     sem.at[slot_t])

    def wait_rows(slot_t):
        pltpu.make_async_copy(xbuf.at[slot_t], xbuf.at[slot_t], sem.at[slot_t]).wait()

    @pl.when((lin == 0) & (f == 0))
    def _():
        acc[...] = jnp.zeros_like(acc)
        for part in range(nf):
            def first(i, carry):
                row_copy(lin, part, i, slot).start()
                return carry
            lax.fori_loop(0, per_step, first, 0)

    @pl.when(f == 0)
    def _():
        wait_rows(slot)

    nxt = jnp.minimum(lin + 1, last)
    for i in range(per_step):
        row_copy(nxt, f, i, 1 - slot).start()

    @pl.when((lin == last) & (f == nf - 1))
    def _():
        wait_rows(1 - slot)

    xv = jnp.concatenate(
        [jnp.concatenate([xbuf[slot, part, pl.ds(a, per_step, stride=rpt), :].astype(BF16)
                          for a in range(rpt)], axis=1) for part in range(nf)], axis=0)
    hg = _dot(xv, wg_ref[...].astype(BF16))
    hu = _dot(xv, wu_ref[...].astype(BF16))
    hid = (hg * _sigmoid(hg) * hu).astype(BF16)
    acc[...] = jnp.where(f > 0, acc[...], 0.0) + _dot(hid, wd_ref[...].astype(BF16))

    @pl.when(f == nf - 1)
    def _():
        ye_ref[0:cap, :] = acc[...].astype(BF16)
        ye_ref[cap:, :] = jnp.zeros((ye_ref.shape[0] - cap, ye_ref.shape[1]), BF16)


def _experts(h2, idx, w_gate, w_up, w_down, cap, rows, tf):
    n_exp, d, df = w_gate.shape
    bsz = h2.shape[0]
    rpt = d // LANES
    n = h2.shape[1] // rpt
    nf = df // tf
    assert cap % nf == 0
    rows_global = (idx + (jnp.arange(bsz, dtype=I32) * n)[:, None, None]).reshape(-1)
    return pl.pallas_call(
        functools.partial(_expert_kernel, cap),
        grid_spec=pltpu.PrefetchScalarGridSpec(
            num_scalar_prefetch=1,
            grid=(bsz, n_exp, nf),
            in_specs=[pl.BlockSpec(memory_space=pl.ANY),
                      pl.BlockSpec((None, d, tf), lambda b, e, f, s: (e, 0, f)),
                      pl.BlockSpec((None, d, tf), lambda b, e, f, s: (e, 0, f)),
                      pl.BlockSpec((None, tf, d), lambda b, e, f, s: (e, f, 0))],
            out_specs=pl.BlockSpec((None, None, rows, d), lambda b, e, f, s: (b, e, 0, 0)),
            scratch_shapes=[pltpu.VMEM((2, nf, (cap // nf) * rpt, LANES), F32),
                            pltpu.VMEM((cap, d), F32), pltpu.SemaphoreType.DMA((2,))]),
        out_shape=jax.ShapeDtypeStruct((bsz, n_exp, rows, d), BF16),
        compiler_params=_cparams("arbitrary", "arbitrary", "arbitrary"),
        name="experts",
    )(rows_global, h2.reshape(bsz * n * rpt, LANES), w_gate, w_up, w_down)


def _combine_windows(tm, n_exp):
    mean = tm * EC_CAPACITY / n_exp
    usual = mean + 5.5 * (mean * (1 - EC_CAPACITY / n_exp)) ** 0.5 + BF16_ROWS - 1
    usual = min(-(-int(usual) // BF16_ROWS) * BF16_ROWS, tm + BF16_ROWS)
    return usual, tm + BF16_ROWS


def _combine_kernel(n_exp, nsub, base_ref, ovf_ref, pos_ref, aff_ref, x1_ref, gt2_ref, fg_ref,
                    ye_hbm, out_ref, buf, fbuf, ffn_scr, sem, fsem):
    b, j = pl.program_id(0), pl.program_id(1)
    nt = pl.num_programs(1)
    step = b * nt + j
    slot = step % 2
    wp, wfull = buf.shape[2], fbuf.shape[0]
    d = buf.shape[-1]

    def start_of(tile, ex):
        return _slot_window(base_ref[tile * nsub * n_exp + ex])

    def window_copy(tile, ex, dst_slot):
        return pltpu.make_async_copy(ye_hbm.at[tile // nt, ex, pl.ds(start_of(tile, ex), wp), :],
                                     buf.at[dst_slot, ex], sem.at[dst_slot])

    def full_copy(ex):
        return pltpu.make_async_copy(ye_hbm.at[b, ex, pl.ds(start_of(step, ex), wfull), :],
                                     fbuf, fsem)

    @pl.when(step == 0)
    def _():
        for ex in range(n_exp):
            window_copy(step, ex, slot).start()

    @pl.when(step + 1 < pl.num_programs(0) * nt)
    def _():
        for ex in range(n_exp):
            window_copy(step + 1, ex, 1 - slot).start()

    def gates_t(rows, ex):
        srow = lax.broadcasted_iota(I32, (rows, LANES), 0)
        s0 = start_of(step, ex)
        parts = []
        for u in range(nsub):
            rel = pos_ref[pl.ds(u * n_exp + ex, 1), :] - s0
            gate = aff_ref[pl.ds(ex, 1), u * LANES:(u + 1) * LANES]
            parts.append(jnp.where(srow == rel, gate, 0.0).astype(BF16))
        return jnp.concatenate(parts, axis=1)

    for ex in range(n_exp):
        window_copy(step, ex, slot).wait()

    @pl.when(ovf_ref[step] == 0)
    def _():
        g_all = jnp.concatenate([gates_t(wp, ex) for ex in range(n_exp)], axis=0)
        ffn_scr[...] = _dot_tn(g_all, buf[slot].reshape(n_exp * wp, d))

    @pl.when(ovf_ref[step] != 0)
    def _():
        ffn_scr[...] = jnp.zeros_like(ffn_scr)
        for ex in range(n_exp):
            full_copy(ex).start()
            full_copy(ex).wait()
            ffn_scr[...] += _dot_tn(gates_t(wfull, ex), fbuf[...])

    x2 = x1_ref[...] + gt2_ref[...] * ffn_scr[...]
    ms = jnp.mean(x2 * x2, axis=-1, keepdims=True)
    out_ref[...] = x2 * lax.rsqrt(ms + EPS) * fg_ref[...]


def _combine(ye, pos, aff_t, before, x1, gt2, final_g, cap, tm):
    bsz, n, d = x1.shape
    n_exp = aff_t.shape[1]
    nsub = tm // LANES
    nt = n // tm
    wp, wfull = _combine_windows(tm, n_exp)
    starts = before[:, :, 0].reshape(bsz, n // LANES, n_exp)
    tile_starts = starts[:, ::nsub]
    tile_ends = jnp.concatenate([tile_starts[:, 1:], jnp.full((bsz, 1, n_exp), cap, I32)], axis=1)
    overflow = (tile_starts % BF16_ROWS) + (tile_ends - tile_starts) > wp
    ovf = jnp.any(overflow, axis=-1).astype(I32).reshape(-1)
    return pl.pallas_call(
        functools.partial(_combine_kernel, n_exp, nsub),
        grid_spec=pltpu.PrefetchScalarGridSpec(
            num_scalar_prefetch=2,
            grid=(bsz, nt),
            in_specs=[pl.BlockSpec((None, nsub * n_exp, LANES), lambda b, j, s, o: (b, j, 0)),
                      pl.BlockSpec((None, n_exp, tm), lambda b, j, s, o: (b, 0, j)),
                      pl.BlockSpec((None, tm, d), lambda b, j, s, o: (b, j, 0)),
                      pl.BlockSpec((None, 1, d), lambda b, j, s, o: (b, 0, 0)),
                      pl.BlockSpec((1, d), lambda b, j, s, o: (0, 0)),
                      pl.BlockSpec(memory_space=pl.ANY)],
            out_specs=pl.BlockSpec((None, tm, d), lambda b, j, s, o: (b, j, 0)),
            scratch_shapes=[pltpu.VMEM((2, n_exp, wp, d), BF16),
                            pltpu.VMEM((wfull, d), BF16),
                            pltpu.VMEM((tm, d), F32),
                            pltpu.SemaphoreType.DMA((2,)), pltpu.SemaphoreType.DMA(())]),
        out_shape=jax.ShapeDtypeStruct((bsz, n, d), F32),
        compiler_params=_cparams("arbitrary", "arbitrary"),
        name="combine",
    )(starts.reshape(-1), ovf, pos, aff_t, x1, gt2, final_g, ye)


def _pick(n, pref):
    t = min(n, pref)
    while n % t:
        t //= 2
    return t


def _tiles(n, d_expert):
    return dict(inproj=_pick(n, 1024), gla=_pick(n, 512), mixer=_pick(n, 512),
                combine=_pick(n, 512), expert_f=_pick(d_expert, 512))


def kernel(x, c, ctx, c_ctx, w_ada, b_ada, norm1_g, norm2_g, w_in, gla_w_a_up, gla_b_a,
           gla_norm_g, gla_w_o, conv_w, conv_w_out, merge_w_out, router_w,
           exp_w_gate, exp_w_up, exp_w_down, final_g):
    bsz, n, d = x.shape
    depth = w_ada.shape[0]
    assert depth == 1, "single-layer trunk"
    rank, dk = gla_w_a_up.shape[-2:]
    dv = gla_w_o.shape[1]
    hv = gla_norm_g.shape[-1]
    heads = dv // hv
    cch = conv_w.shape[-1]
    n_exp = router_w.shape[-1]
    cap = EC_CAPACITY * n // n_exp
    assert 2 * dk == d and dv == d and cch == d and 2 * rank <= LANES
    assert n % LANES == 0 and n % GRID_W == 0 and bsz + 1 <= 8 and n_exp % 8 == 0

    cond = jnp.zeros((8, d), F32).at[:bsz].set(c).at[bsz].set(c_ctx)
    mod = _adaln(cond, w_ada[0], b_ada[0]).reshape(8, N_MOD, d)
    sh1, sc1, gt1, sh2, sc2, gt2 = [mod[:bsz, k][:, None, :] for k in range(N_MOD)]
    csh1, csc1 = mod[bsz:bsz + 1, 0], mod[bsz:bsz + 1, 1]

    lr0 = 2 * dk + 2 * dv
    u0 = lr0 + 2 * rank
    groups = ((0, lr0), (u0 + cch, cch), (u0, cch), (u0 + 2 * cch, cch + 2 * d), (lr0, LANES))
    tiles = tuple((r0 + o, min(d, cnt - o)) for r0, cnt in groups for o in range(0, cnt, d))
    wt = _wprep(jnp.swapaxes(w_in[0], 0, 1), tiles)
    wup = gla_w_a_up[0].astype(BF16)
    wupf = jnp.zeros((LANES, dk), BF16).at[:rank].set(wup[0])
    wupb = jnp.zeros((LANES, dk), BF16).at[rank:2 * rank].set(wup[1])
    baf, bab = gla_b_a[0, 0:1], gla_b_a[0, 1:2]
    n1g, n2g = norm1_g[0:1], norm2_g[0:1]

    t = _tiles(n, exp_w_gate.shape[-1])
    s0f, s0b = _ctx_states(ctx, csh1, csc1, n1g, wt, wupf, wupb, baf, bab, heads, dk, dv)
    z, lr = _inproj(x, sh1, sc1, n1g, wt, t["inproj"])
    o_f, o_b = _gla(z, lr, wupf, wupb, baf, bab, s0f, s0b, heads, dk, dv, t["gla"])
    x1, h2, aff_t = _mixer(
        o_f, o_b, z, x, jnp.tile(gla_norm_g[0:1], (1, heads)), conv_w[0],
        gla_w_o[0].astype(BF16), conv_w_out[0].astype(BF16), merge_w_out[0].astype(BF16),
        gt1, sh2, sc2, n2g, router_w[0].T.astype(BF16), heads, dk, t["mixer"])

    pos, before, idx = _route(aff_t, cap)
    ye = _experts(h2, idx, exp_w_gate[0], exp_w_up[0], exp_w_down[0], cap,
                  cap + _combine_windows(t["combine"], n_exp)[1], t["expert_f"])
    return _combine(ye, pos, aff_t, before, x1, gt2, final_g.reshape(1, d), cap, t["combine"])
```

```python
import functools

import numpy as np
import jax
import jax.numpy as jnp
from jax import lax
from jax.experimental import pallas as pl
from jax.experimental.pallas import tpu as pltpu

F32 = jnp.float32
BF16 = jnp.bfloat16
I32 = jnp.int32

EPS = 1e-6
N_MOD = 6
GRID_W = 64
GLA_CHUNK = 64
GLA_GATE_NORM = 16.0
EC_CAPACITY = 2

LANES = 128
MXU_DEPTH = 256
BF16_ROWS = 16
VMEM_LIMIT = 56 * 1024 * 1024


def _cparams(*sem):
    return pltpu.CompilerParams(dimension_semantics=sem, vmem_limit_bytes=VMEM_LIMIT)


def _dot(a, b):
    return jnp.dot(a, b, preferred_element_type=F32)


def _dot_nt(a, b):
    return lax.dot_general(a, b, (((1,), (1,)), ((), ())), preferred_element_type=F32)


def _dot_tn(a, b):
    return lax.dot_general(a, b, (((0,), (0,)), ((), ())), preferred_element_type=F32)


def _sigmoid(v):
    return 1.0 / (1.0 + jnp.exp(-v))


def _log_sigmoid(v):
    return jnp.minimum(v, 0.0) - jnp.log(1.0 + jnp.exp(-jnp.abs(v)))


def _rms_mod(xv, g, shift, scale):
    ms = jnp.mean(xv * xv, axis=-1, keepdims=True)
    y = xv * lax.rsqrt(ms + EPS) * g
    return y * (1.0 + scale) + shift


def _split_bf16(v):
    hi = v.astype(BF16)
    lo = (v - hi.astype(F32)).astype(BF16)
    return hi, lo


def _adaln_kernel(c_ref, w_ref, b_ref, o_ref):
    cv = c_ref[...]
    s = (cv * _sigmoid(cv)).astype(BF16)
    o_ref[...] = _dot(s, w_ref[...].astype(BF16)) + b_ref[...]


def _adaln(cond, w, b):
    rows, d = cond.shape
    nout = w.shape[1]
    tn = _pick(nout, 2 * d)
    return pl.pallas_call(
        _adaln_kernel,
        grid=(nout // tn,),
        in_specs=[pl.BlockSpec((rows, d), lambda j: (0, 0)),
                  pl.BlockSpec((d, tn), lambda j: (0, j)),
                  pl.BlockSpec((1, tn), lambda j: (0, j))],
        out_specs=pl.BlockSpec((rows, tn), lambda j: (0, j)),
        out_shape=jax.ShapeDtypeStruct((rows, nout), F32),
        compiler_params=_cparams("arbitrary"),
        name="adaln",
    )(cond, w, b.reshape(1, nout))


def _wprep_kernel(tiles, wt_hbm, o_ref, buf, sem):
    def copy(t):
        row, cnt = tiles[t]
        return pltpu.make_async_copy(wt_hbm.at[pl.ds(row, cnt), :],
                                     buf.at[t % 2, pl.ds(0, cnt), :], sem.at[t % 2])

    copy(0).start()
    out_row = 0
    for t, (_, cnt) in enumerate(tiles):
        if t + 1 < len(tiles):
            copy(t + 1).start()
        copy(t).wait()
        o_ref[out_row:out_row + cnt, :] = buf[t % 2, 0:cnt, :].astype(BF16)
        out_row += cnt


def _wprep(wt, tiles):
    d = wt.shape[1]
    rows = sum(cnt for _, cnt in tiles)
    return pl.pallas_call(
        functools.partial(_wprep_kernel, tiles),
        in_specs=[pl.BlockSpec(memory_space=pl.ANY)],
        out_specs=pl.BlockSpec((rows, d), lambda: (0, 0)),
        out_shape=jax.ShapeDtypeStruct((rows, d), BF16),
        scratch_shapes=[pltpu.VMEM((2, max(cnt for _, cnt in tiles), d), F32),
                        pltpu.SemaphoreType.DMA((2,))],
        compiler_params=pltpu.CompilerParams(vmem_limit_bytes=VMEM_LIMIT),
        name="wprep",
    )(wt)


def _ctx_kernel(heads, ctx_ref, sh_ref, sc_ref, g_ref, wk_ref, wv_ref, wlr_ref,
                wupf_ref, wupb_ref, baf_ref, bab_ref, sf_ref, sb_ref):
    n = ctx_ref.shape[0]
    hc = _rms_mod(ctx_ref[...], g_ref[...], sh_ref[...], sc_ref[...]).astype(BF16)
    k = _dot_nt(hc, wk_ref[...])
    v = _dot_nt(hc, wv_ref[...]).astype(BF16)
    lr = _dot_nt(hc, wlr_ref[...]).astype(BF16)
    hk = k.shape[1] // heads
    hv = v.shape[1] // heads
    row = lax.broadcasted_iota(I32, (n, n), 0)
    col = lax.broadcasted_iota(I32, (n, n), 1)
    for wup_ref, ba_ref, s_ref, tri, last in (
            (wupf_ref, baf_ref, sf_ref, col <= row, n - 1),
            (wupb_ref, bab_ref, sb_ref, col >= row, 0)):
        la = _log_sigmoid(_dot(lr, wup_ref[...]) + ba_ref[...]) * (1.0 / GLA_GATE_NORM)
        hi, lo = _split_bf16(la)
        t = jnp.where(tri, 1.0, 0.0).astype(BF16)
        b = _dot(t, hi) + _dot(t, lo)
        kd = (k * jnp.exp(b[last:last + 1, :] - b)).astype(BF16)
        for h in range(heads):
            s_ref[h] = _dot_tn(v[:, h * hv:(h + 1) * hv], kd[:, h * hk:(h + 1) * hk])


def _ctx_states(ctx, sh, sc, g, wt, wupf, wupb, baf, bab, heads, dk, dv):
    bsz, n, d = ctx.shape
    lr_blk = (wt.shape[0] - LANES) // LANES
    hk, hv = dk // heads, dv // heads
    full = lambda *shape: pl.BlockSpec(shape, lambda b: (0,) * len(shape))
    st = jax.ShapeDtypeStruct((bsz, heads, hv, hk), F32)
    sspec = pl.BlockSpec((None, heads, hv, hk), lambda b: (b, 0, 0, 0))
    return pl.pallas_call(
        functools.partial(_ctx_kernel, heads),
        grid=(bsz,),
        in_specs=[pl.BlockSpec((None, n, d), lambda b: (b, 0, 0)),
                  full(1, d), full(1, d), full(1, d),
                  pl.BlockSpec((dk, d), lambda b: (1, 0)),
                  pl.BlockSpec((dv, d), lambda b: (2 * dk // dv, 0)),
                  pl.BlockSpec((LANES, d), lambda b: (lr_blk, 0)),
                  full(LANES, dk), full(LANES, dk), full(1, dk), full(1, dk)],
        out_specs=(sspec, sspec),
        out_shape=(st, st),
        compiler_params=_cparams("arbitrary"),
        name="ctx_state",
    )(ctx, sh, sc, g, wt, wt, wt, wupf, wupb, baf, bab)


def _inproj_kernel(x_ref, sh_ref, sc_ref, g_ref, w_ref, wlr_ref, z_ref, lr_ref, h_scr):
    j = pl.program_id(2)
    d = x_ref.shape[1]

    @pl.when(j == 0)
    def _():
        hb = _rms_mod(x_ref[...], g_ref[...], sh_ref[...], sc_ref[...]).astype(BF16)
        h_scr[...] = hb
        lr_ref[...] = _dot_nt(hb, wlr_ref[...])
        z_ref[...] = _dot_nt(hb, w_ref[...]).astype(BF16)

    @pl.when(j == 1)
    def _():
        a = _dot_nt(h_scr[...], w_ref[...])
        gv = a[:, :d]
        z_ref[:, :d] = (gv * _sigmoid(gv)).astype(BF16)
        z_ref[:, d:] = a[:, d:].astype(BF16)

    @pl.when(j == 2)
    def _():
        a = _dot_nt(h_scr[...], w_ref[...])
        z_ref[:, :d] = (a[:, d:] * a[:, :d]).astype(BF16)
        z_ref[:, d:] = jnp.zeros((z_ref.shape[0], d), BF16)

    @pl.when(j == 3)
    def _():
        z_ref[...] = _sigmoid(_dot_nt(h_scr[...], w_ref[...])).astype(BF16)


def _inproj(x, sh, sc, g, wt, tm):
    bsz, n, d = x.shape
    p = wt.shape[0] - LANES
    tn = 2 * d
    assert p == 4 * tn
    return pl.pallas_call(
        _inproj_kernel,
        grid=(bsz, n // tm, p // tn),
        in_specs=[pl.BlockSpec((None, tm, d), lambda b, i, j: (b, i, 0)),
                  pl.BlockSpec((None, 1, d), lambda b, i, j: (b, 0, 0)),
                  pl.BlockSpec((None, 1, d), lambda b, i, j: (b, 0, 0)),
                  pl.BlockSpec((1, d), lambda b, i, j: (0, 0)),
                  pl.BlockSpec((tn, d), lambda b, i, j: (j, 0)),
                  pl.BlockSpec((LANES, d), lambda b, i, j: (p // LANES, 0))],
        out_specs=(pl.BlockSpec((None, tm, tn), lambda b, i, j: (b, i, j)),
                   pl.BlockSpec((None, tm, LANES), lambda b, i, j: (b, i, 0))),
        out_shape=(jax.ShapeDtypeStruct((bsz, n, p), BF16),
                   jax.ShapeDtypeStruct((bsz, n, LANES), F32)),
        scratch_shapes=[pltpu.VMEM((tm, d), BF16)],
        compiler_params=_cparams("arbitrary", "arbitrary", "arbitrary"),
        name="inproj",
    )(x, sh, sc, g, wt, wt)


def _gla_kernel(heads, dk, qkvf_ref, lrf_ref, qkvb_ref, lrb_ref,
                wupf_ref, wupb_ref, baf_ref, bab_ref, trif_ref, trib_ref, s0f_ref, s0b_ref,
                of_ref, ob_ref, sf_scr, sb_scr):
    tb = qkvf_ref.shape[0]
    dv = qkvf_ref.shape[1] - 2 * dk
    hk, hv = dk // heads, dv // heads
    ck = GLA_CHUNK
    nck = tb // ck
    q_scale = hk ** -0.5

    @pl.when(pl.program_id(1) == 0)
    def _():
        sf_scr[...] = s0f_ref[...]
        sb_scr[...] = s0b_ref[...]

    crow = lax.broadcasted_iota(I32, (ck, ck), 0)
    ccol = lax.broadcasted_iota(I32, (ck, ck), 1)

    dirs = (
        (qkvf_ref, lrf_ref, wupf_ref, baf_ref, of_ref, sf_scr,
         trif_ref, ccol <= crow, ck - 1, range(nck)),
        (qkvb_ref, lrb_ref, wupb_ref, bab_ref, ob_ref, sb_scr,
         trib_ref, ccol >= crow, 0, range(nck - 1, -1, -1)),
    )
    b_alls = []
    for _, lr_ref, wup_ref, ba_ref, _, _, tri, _, _, _ in dirs:
        la = _log_sigmoid(_dot(lr_ref[...].astype(BF16), wup_ref[...]) + ba_ref[...])
        la = la * (1.0 / GLA_GATE_NORM)
        hi, lo = _split_bf16(la)
        t = tri[...]
        tr = t.shape[0]
        b_alls.append(jnp.concatenate(
            [_dot(t, hi[r:r + tr]) + _dot(t, lo[r:r + tr]) for r in range(0, tb, tr)], axis=0))
    for k in range(nck):
        work = []
        for di, (qkv_ref, _, _, _, _, _, _, cmask, last, order) in enumerate(dirs):
            r0 = order[k] * ck
            b = b_alls[di][r0:r0 + ck, :]
            b_last = b[last:last + 1, :]
            qc = qkv_ref[r0:r0 + ck, 0:dk].astype(F32) * q_scale
            kc = qkv_ref[r0:r0 + ck, dk:2 * dk].astype(F32)
            q_in = (qc * jnp.exp(b)).astype(BF16)
            k_in = (kc * jnp.exp(-b)).astype(BF16)
            k_dec = (kc * jnp.exp(b_last - b)).astype(BF16)
            decay = jnp.exp(b_last)
            units = []
            for h in range(heads):
                qh = q_in[:, h * hk:(h + 1) * hk]
                vh = qkv_ref[r0:r0 + ck, 2 * dk + h * hv:2 * dk + (h + 1) * hv]
                scores = jnp.where(cmask, _dot_nt(qh, k_in[:, h * hk:(h + 1) * hk]), 0.0)
                o_intra = _dot(scores.astype(BF16), vh)
                ut = _dot_tn(vh, k_dec[:, h * hk:(h + 1) * hk])
                units.append((qh, o_intra, ut, decay[:, h * hk:(h + 1) * hk]))
            work.append((r0, units))
        for (r0, units), (_, _, _, _, o_ref, s_scr, _, _, _, _) in zip(work, dirs):
            for h, (qh, o_intra, ut, dec) in enumerate(units):
                st = s_scr[h]
                o = o_intra + _dot_nt(qh, st.astype(BF16))
                o_ref[r0:r0 + ck, h * hv:(h + 1) * hv] = o.astype(BF16)
                s_scr[h] = st * dec + ut


def _gla(z, lr, wupf, wupb, baf, bab, s0f, s0b, heads, dk, dv, tb):
    bsz, n, _ = z.shape
    nb = n // tb
    hk, hv = dk // heads, dv // heads
    fwd = lambda cb: (lambda b, i: (b, i, cb))
    bwd = lambda cb: (lambda b, i: (b, nb - 1 - i, cb))
    full = lambda *shape: pl.BlockSpec(shape, lambda b, i: (0,) * len(shape))
    sspec = pl.BlockSpec((None, heads, hv, hk), lambda b, i: (b, 0, 0, 0))
    ost = jax.ShapeDtypeStruct((bsz, n, dv), BF16)
    tr = min(tb, MXU_DEPTH)
    assert tb % tr == 0 and tr % GLA_CHUNK == 0
    ri = np.arange(tr)
    same_chunk = (ri[:, None] // GLA_CHUNK) == (ri[None, :] // GLA_CHUNK)
    trif = jnp.asarray(same_chunk & (ri[None, :] <= ri[:, None]), BF16)
    trib = jnp.asarray(same_chunk & (ri[None, :] >= ri[:, None]), BF16)
    qkv = 2 * dk + dv
    return pl.pallas_call(
        functools.partial(_gla_kernel, heads, dk),
        grid=(bsz, nb),
        in_specs=[pl.BlockSpec((None, tb, qkv), fwd(0)), pl.BlockSpec((None, tb, LANES), fwd(0)),
                  pl.BlockSpec((None, tb, qkv), bwd(0)), pl.BlockSpec((None, tb, LANES), bwd(0)),
                  full(LANES, dk), full(LANES, dk), full(1, dk), full(1, dk),
                  full(tr, tr), full(tr, tr), sspec, sspec],
        out_specs=(pl.BlockSpec((None, tb, dv), fwd(0)), pl.BlockSpec((None, tb, dv), bwd(0))),
        out_shape=(ost, ost),
        scratch_shapes=[pltpu.VMEM((heads, hv, hk), F32), pltpu.VMEM((heads, hv, hk), F32)],
        compiler_params=_cparams("arbitrary", "arbitrary"),
        name="gla",
    )(z, lr, z, lr, wupf, wupb, baf, bab, trif, trib, s0f, s0b)


def _mixer_kernel(heads, of_ref, ob_ref, sg_ref, bg_ref, cu_ref, srg_ref, src_ref, x_ref,
                  gng_ref, cw_ref, wo_ref, wco_ref, wm_ref, gt1_ref, sh2_ref, sc2_ref, n2g_ref,
                  rwt_ref, x1_ref, h2_ref, aff_ref):
    tm, dv = of_ref.shape
    hv = dv // heads
    o = of_ref[...].astype(F32) + ob_ref[...].astype(F32)
    parts = []
    for h in range(heads):
        oh = o[:, h * hv:(h + 1) * hv]
        ms = jnp.mean(oh * oh, axis=-1, keepdims=True)
        parts.append(oh * lax.rsqrt(ms + EPS))
    on = jnp.concatenate(parts, axis=1) * gng_ref[...]
    y_gla = _dot((on * sg_ref[...].astype(F32)).astype(BF16), wo_ref[...])

    cu = cu_ref[...].astype(F32)
    gcol = lax.broadcasted_iota(I32, cu.shape, 0) & (GRID_W - 1)
    left = jnp.where(gcol == 0, 0.0, pltpu.roll(cu, 1, 0))
    right = jnp.where(gcol == GRID_W - 1, 0.0, pltpu.roll(cu, tm - 1, 0))
    cw = cw_ref[...]
    conv = left * cw[0:1, :] + cu * cw[1:2, :] + right * cw[2:3, :]
    y_conv = _dot((bg_ref[...].astype(F32) * conv).astype(BF16), wco_ref[...])

    merged = srg_ref[...].astype(F32) * y_gla + src_ref[...].astype(F32) * y_conv
    y = _dot(merged.astype(BF16), wm_ref[...])
    x1 = x_ref[...] + gt1_ref[...] * y
    x1_ref[...] = x1
    h2 = _rms_mod(x1, n2g_ref[...], sh2_ref[...], sc2_ref[...])
    rpt = h2.shape[1] // LANES
    for a in range(rpt):
        h2_ref[pl.ds(a, tm, stride=rpt), :] = h2[:, a * LANES:(a + 1) * LANES]
    logits = _dot_nt(rwt_ref[...], h2.astype(BF16))
    ex = jnp.exp(logits - jnp.max(logits, axis=0, keepdims=True))
    aff_ref[...] = ex / jnp.sum(ex, axis=0, keepdims=True)


def _mixer(o_f, o_b, z, x, gng, conv_w, w_o, w_co, w_m, gt1, sh2, sc2, n2g, rwt, heads, dk, tm):
    bsz, n, d = x.shape
    dv = o_f.shape[2]
    c = conv_w.shape[1]
    e = rwt.shape[0]
    zb = lambda cb: pl.BlockSpec((None, tm, d), lambda b, i: (b, i, cb))
    base = (2 * dk + dv) // d
    tok = lambda w: pl.BlockSpec((None, tm, w), lambda b, i: (b, i, 0))
    full = lambda *shape: pl.BlockSpec(shape, lambda b, i: (0,) * len(shape))
    perb = pl.BlockSpec((None, 1, d), lambda b, i: (b, 0, 0))
    return pl.pallas_call(
        functools.partial(_mixer_kernel, heads),
        grid=(bsz, n // tm),
        in_specs=[tok(dv), tok(dv), zb(base), zb(base + 1), zb(base + 2),
                  zb(base + 4), zb(base + 5), tok(d),
                  full(1, dv), full(3, c), full(dv, d), full(c, d), full(d, d),
                  perb, perb, perb, full(1, d), full(e, d)],
        out_specs=(tok(d), pl.BlockSpec((None, tm * (d // LANES), LANES), lambda b, i: (b, i, 0)),
                   pl.BlockSpec((None, e, tm), lambda b, i: (b, 0, i))),
        out_shape=(jax.ShapeDtypeStruct((bsz, n, d), F32),
                   jax.ShapeDtypeStruct((bsz, n * (d // LANES), LANES), F32),
                   jax.ShapeDtypeStruct((bsz, e, n), F32)),
        compiler_params=_cparams("arbitrary", "arbitrary"),
        name="mixer",
    )(o_f, o_b, z, z, z, z, z, x, gng, conv_w, w_o, w_co, w_m, gt1, sh2, sc2, n2g, rwt)


def _route_kernel(cap, aff_ref, lmat_ref, pos_ref, base_ref, idx_ref, loc_scr, bef_scr):
    e, n = aff_ref.shape
    nt = n // LANES
    aff = aff_ref[...]

    def count(mask):
        return jnp.sum(jnp.where(mask, 1.0, 0.0), axis=1, keepdims=True)

    def search(k, tbits):
        cand = tbits | jnp.left_shift(jnp.int32(1), 30 - k)
        ok = count(aff >= lax.bitcast_convert_type(cand, F32)) >= cap
        return jnp.where(ok, cand, tbits)

    tbits = lax.fori_loop(0, 31, search, jnp.zeros((e, 1), I32))
    thr = lax.bitcast_convert_type(tbits, F32)
    gt = aff > thr
    eq = aff == thr
    need = cap - count(gt)

    def stack(mask):
        m = jnp.where(mask, 1.0, 0.0)
        return jnp.concatenate([m[:, j * LANES:(j + 1) * LANES] for j in range(nt)], axis=0)

    r = lax.broadcasted_iota(I32, (LANES, LANES), 0)
    cl = lax.broadcasted_iota(I32, (LANES, LANES), 1)
    upper = jnp.where(r <= cl, 1.0, 0.0).astype(BF16)
    ones = jnp.ones((LANES, LANES), BF16)
    lmat = lmat_ref[...]

    def cumsum(ms):
        msb = ms.astype(BF16)
        before = _dot(_dot(lmat, msb).astype(BF16), ones)
        return _dot(msb, upper), before

    eq_s = stack(eq)
    loc_eq, before_eq = cumsum(eq_s)
    need_s = jnp.concatenate([need] * nt, axis=0)
    sel = jnp.maximum(stack(gt), jnp.where(loc_eq + before_eq <= need_s, eq_s, 0.0))
    loc, before = cumsum(sel)
    pos_ref[...] = jnp.where(sel > 0.0, loc + before - 1.0, -1.0).astype(I32)
    base_ref[...] = before.astype(I32)

    loc_scr[...] = loc
    bef_scr[...] = before
    slot = lax.broadcasted_iota(I32, (1, cap), 1).astype(F32)
    for ex in range(e):
        loc_e = loc_scr[pl.ds(ex, nt, stride=e), :]
        tprev = bef_scr[pl.ds(ex, nt, stride=e), :][:, 0:1]
        tincl = tprev + loc_e[:, LANES - 1:LANES]
        in_tile = jnp.where((tprev <= slot) & (slot < tincl), 1.0, 0.0)
        tile = jnp.sum(jnp.where(tincl <= slot, 1.0, 0.0), axis=0, keepdims=True)
        s_loc = slot - jnp.sum(in_tile * tprev, axis=0, keepdims=True)
        counts = _dot_tn(loc_e.astype(BF16), in_tile.astype(BF16))
        lane = jnp.sum(jnp.where(counts <= s_loc, 1.0, 0.0), axis=0, keepdims=True)
        idx_ref[pl.ds(ex, 1), :] = (tile * LANES + lane).astype(I32)


def _route(aff_t, cap):
    bsz, e, n = aff_t.shape
    nt = n // LANES
    rows = nt * e
    ri = np.arange(rows)
    lmat = ((ri[None, :] % e == ri[:, None] % e) & (ri[None, :] // e < ri[:, None] // e))
    lmat = jnp.asarray(lmat, BF16)
    st = jax.ShapeDtypeStruct((bsz, rows, LANES), I32)
    ospec = pl.BlockSpec((None, rows, LANES), lambda b: (b, 0, 0))
    return pl.pallas_call(
        functools.partial(_route_kernel, cap),
        grid=(bsz,),
        in_specs=[pl.BlockSpec((None, e, n), lambda b: (b, 0, 0)),
                  pl.BlockSpec((rows, rows), lambda b: (0, 0))],
        out_specs=(ospec, ospec, pl.BlockSpec((None, e, cap), lambda b: (b, 0, 0))),
        out_shape=(st, st, jax.ShapeDtypeStruct((bsz, e, cap), I32)),
        scratch_shapes=[pltpu.VMEM((rows, LANES), F32), pltpu.VMEM((rows, LANES), F32)],
        compiler_params=_cparams("arbitrary"),
        name="route",
    )(aff_t, lmat)


def _slot_window(base):
    return pl.multiple_of((base // BF16_ROWS) * BF16_ROWS, BF16_ROWS)


def _expert_kernel(cap, idx_ref, h2_hbm, wg_ref, wu_ref, wd_ref, ye_ref, xbuf, acc, sem):
    e, f = pl.program_id(1), pl.program_id(2)
    n_exp, nf = pl.num_programs(1), pl.num_programs(2)
    last = pl.num_programs(0) * n_exp - 1
    lin = pl.program_id(0) * n_exp + e
    slot = lin % 2
    per_step = cap // nf
    rpt = wg_ref.shape[0] // LANES

    def row_copy(lin_t, part, i, slot_t):
        tok = idx_ref[(lin_t * nf + part) * per_step + i]
        return pltpu.make_async_copy(h2_hbm.at[pl.ds(pl.multiple_of(tok * rpt, rpt), rpt), :],
                                     xbuf.at[slot_t, part, pl.ds(i * rpt, rpt), :],
                                     sem.at[slot_t])

    def wait_rows(slot_t):
        pltpu.make_async_copy(xbuf.at[slot_t], xbuf.at[slot_t], sem.at[slot_t]).wait()

    @pl.when((lin == 0) & (f == 0))
    def _():
        acc[...] = jnp.zeros_like(acc)
        for part in range(nf):
            def first(i, carry):
                row_copy(lin, part, i, slot).start()
                return carry
            lax.fori_loop(0, per_step, first, 0)

    @pl.when(f == 0)
    def _():
        wait_rows(slot)

    xv = jnp.concatenate(
        [jnp.concatenate([xbuf[slot, part, pl.ds(a, per_step, stride=rpt), :].astype(BF16)
                          for a in range(rpt)], axis=1) for part in range(nf)], axis=0)
    hg = _dot(xv, wg_ref[...].astype(BF16))
    hu = _dot(xv, wu_ref[...].astype(BF16))
    hid = (hg * _sigmoid(hg) * hu).astype(BF16)
    acc[...] = jnp.where(f > 0, acc[...], 0.0) + _dot(hid, wd_ref[...].astype(BF16))

    nxt = jnp.minimum(lin + 1, last)
    for i in range(per_step):
        row_copy(nxt, f, i, 1 - slot).start()

    @pl.when((lin == last) & (f == nf - 1))
    def _():
        wait_rows(1 - slot)

    @pl.when(f == nf - 1)
    def _():
        ye_ref[0:cap, :] = acc[...].astype(BF16)
        ye_ref[cap:, :] = jnp.zeros((ye_ref.shape[0] - cap, ye_ref.shape[1]), BF16)


def _experts(h2, idx, w_gate, w_up, w_down, cap, rows, tf):
    n_exp, d, df = w_gate.shape
    bsz = h2.shape[0]
    rpt = d // LANES
    n = h2.shape[1] // rpt
    nf = df // tf
    assert cap % nf == 0
    rows_global = (idx + (jnp.arange(bsz, dtype=I32) * n)[:, None, None]).reshape(-1)
    return pl.pallas_call(
        functools.partial(_expert_kernel, cap),
        grid_spec=pltpu.PrefetchScalarGridSpec(
            num_scalar_prefetch=1,
            grid=(bsz, n_exp, nf),
            in_specs=[pl.BlockSpec(memory_space=pl.ANY),
                      pl.BlockSpec((None, d, tf), lambda b, e, f, s: (e, 0, f)),
                      pl.BlockSpec((None, d, tf), lambda b, e, f, s: (e, 0, f)),
                      pl.BlockSpec((None, tf, d), lambda b, e, f, s: (e, f, 0))],
            out_specs=pl.BlockSpec((None, None, rows, d), lambda b, e, f, s: (b, e, 0, 0)),
            scratch_shapes=[pltpu.VMEM((2, nf, (cap // nf) * rpt, LANES), F32),
                            pltpu.VMEM((cap, d), F32), pltpu.SemaphoreType.DMA((2,))]),
        out_shape=jax.ShapeDtypeStruct((bsz, n_exp, rows, d), BF16),
        compiler_params=_cparams("arbitrary", "arbitrary", "arbitrary"),
        name="experts",
    )(rows_global, h2.reshape(bsz * n * rpt, LANES), w_gate, w_up, w_down)


def _combine_windows(tm, n_exp):
    mean = tm * EC_CAPACITY / n_exp
    usual = mean + 5.5 * (mean * (1 - EC_CAPACITY / n_exp)) ** 0.5 + BF16_ROWS - 1
    usual = min(-(-int(usual) // BF16_ROWS) * BF16_ROWS, tm + BF16_ROWS)
    return usual, tm + BF16_ROWS


def _combine_kernel(n_exp, nsub, base_ref, ovf_ref, pos_ref, aff_ref, x1_ref, gt2_ref, fg_ref,
                    ye_hbm, out_ref, buf, fbuf, ffn_scr, sem, fsem):
    b, j = pl.program_id(0), pl.program_id(1)
    nt = pl.num_programs(1)
    step = b * nt + j
    slot = step % 2
    wp, wfull = buf.shape[2], fbuf.shape[0]
    d = buf.shape[-1]

    def start_of(tile, ex):
        return _slot_window(base_ref[tile * nsub * n_exp + ex])

    def window_copy(tile, ex, dst_slot):
        return pltpu.make_async_copy(ye_hbm.at[tile // nt, ex, pl.ds(start_of(tile, ex), wp), :],
                                     buf.at[dst_slot, ex], sem.at[dst_slot])

    def full_copy(ex):
        return pltpu.make_async_copy(ye_hbm.at[b, ex, pl.ds(start_of(step, ex), wfull), :],
                                     fbuf, fsem)

    @pl.when(step == 0)
    def _():
        for ex in range(n_exp):
            window_copy(step, ex, slot).start()

    @pl.when(step + 1 < pl.num_programs(0) * nt)
    def _():
        for ex in range(n_exp):
            window_copy(step + 1, ex, 1 - slot).start()

    def gates_t(rows, ex):
        srow = lax.broadcasted_iota(I32, (rows, LANES), 0)
        s0 = start_of(step, ex)
        parts = []
        for u in range(nsub):
            rel = pos_ref[pl.ds(u * n_exp + ex, 1), :] - s0
            gate = aff_ref[pl.ds(ex, 1), u * LANES:(u + 1) * LANES]
            parts.append(jnp.where(srow == rel, gate, 0.0).astype(BF16))
        return jnp.concatenate(parts, axis=1)

    for ex in range(n_exp):
        window_copy(step, ex, slot).wait()

    @pl.when(ovf_ref[step] == 0)
    def _():
        g_all = jnp.concatenate([gates_t(wp, ex) for ex in range(n_exp)], axis=0)
        ffn_scr[...] = _dot_tn(g_all, buf[slot].reshape(n_exp * wp, d))

    @pl.when(ovf_ref[step] != 0)
    def _():
        ffn_scr[...] = jnp.zeros_like(ffn_scr)
        for ex in range(n_exp):
            full_copy(ex).start()
            full_copy(ex).wait()
            ffn_scr[...] += _dot_tn(gates_t(wfull, ex), fbuf[...])

    x2 = x1_ref[...] + gt2_ref[...] * ffn_scr[...]
    ms = jnp.mean(x2 * x2, axis=-1, keepdims=True)
    out_ref[...] = x2 * lax.rsqrt(ms + EPS) * fg_ref[...]


def _combine(ye, pos, aff_t, before, x1, gt2, final_g, cap, tm):
    bsz, n, d = x1.shape
    n_exp = aff_t.shape[1]
    nsub = tm // LANES
    nt = n // tm
    wp, wfull = _combine_windows(tm, n_exp)
    starts = before[:, :, 0].reshape(bsz, n // LANES, n_exp)
    tile_starts = starts[:, ::nsub]
    tile_ends = jnp.concatenate([tile_starts[:, 1:], jnp.full((bsz, 1, n_exp), cap, I32)], axis=1)
    overflow = (tile_starts % BF16_ROWS) + (tile_ends - tile_starts) > wp
    ovf = jnp.any(overflow, axis=-1).astype(I32).reshape(-1)
    return pl.pallas_call(
        functools.partial(_combine_kernel, n_exp, nsub),
        grid_spec=pltpu.PrefetchScalarGridSpec(
            num_scalar_prefetch=2,
            grid=(bsz, nt),
            in_specs=[pl.BlockSpec((None, nsub * n_exp, LANES), lambda b, j, s, o: (b, j, 0)),
                      pl.BlockSpec((None, n_exp, tm), lambda b, j, s, o: (b, 0, j)),
                      pl.BlockSpec((None, tm, d), lambda b, j, s, o: (b, j, 0)),
                      pl.BlockSpec((None, 1, d), lambda b, j, s, o: (b, 0, 0)),
                      pl.BlockSpec((1, d), lambda b, j, s, o: (0, 0)),
                      pl.BlockSpec(memory_space=pl.ANY)],
            out_specs=pl.BlockSpec((None, tm, d), lambda b, j, s, o: (b, j, 0)),
            scratch_shapes=[pltpu.VMEM((2, n_exp, wp, d), BF16),
                            pltpu.VMEM((wfull, d), BF16),
                            pltpu.VMEM((tm, d), F32),
                            pltpu.SemaphoreType.DMA((2,)), pltpu.SemaphoreType.DMA(())]),
        out_shape=jax.ShapeDtypeStruct((bsz, n, d), F32),
        compiler_params=_cparams("arbitrary", "arbitrary"),
        name="combine",
    )(starts.reshape(-1), ovf, pos, aff_t, x1, gt2, final_g, ye)


def _pick(n, pref):
    t = min(n, pref)
    while n % t:
        t //= 2
    return t


def _tiles(n, d_expert):
    return dict(inproj=_pick(n, 1024), gla=_pick(n, 512), mixer=_pick(n, 512),
                combine=_pick(n, 512), expert_f=_pick(d_expert, 512))


def kernel(x, c, ctx, c_ctx, w_ada, b_ada, norm1_g, norm2_g, w_in, gla_w_a_up, gla_b_a,
           gla_norm_g, gla_w_o, conv_w, conv_w_out, merge_w_out, router_w,
           exp_w_gate, exp_w_up, exp_w_down, final_g):
    bsz, n, d = x.shape
    depth = w_ada.shape[0]
    assert depth == 1, "single-layer trunk"
    rank, dk = gla_w_a_up.shape[-2:]
    dv = gla_w_o.shape[1]
    hv = gla_norm_g.shape[-1]
    heads = dv // hv
    cch = conv_w.shape[-1]
    n_exp = router_w.shape[-1]
    cap = EC_CAPACITY * n // n_exp
    assert 2 * dk == d and dv == d and cch == d and 2 * rank <= LANES
    assert n % LANES == 0 and n % GRID_W == 0 and bsz + 1 <= 8 and n_exp % 8 == 0

    cond = jnp.zeros((8, d), F32).at[:bsz].set(c).at[bsz].set(c_ctx)
    mod = _adaln(cond, w_ada[0], b_ada[0]).reshape(8, N_MOD, d)
    sh1, sc1, gt1, sh2, sc2, gt2 = [mod[:bsz, k][:, None, :] for k in range(N_MOD)]
    csh1, csc1 = mod[bsz:bsz + 1, 0], mod[bsz:bsz + 1, 1]

    lr0 = 2 * dk + 2 * dv
    u0 = lr0 + 2 * rank
    groups = ((0, lr0), (u0 + cch, cch), (u0, cch), (u0 + 2 * cch, cch + 2 * d), (lr0, LANES))
    tiles = tuple((r0 + o, min(d, cnt - o)) for r0, cnt in groups for o in range(0, cnt, d))
    wt = _wprep(jnp.swapaxes(w_in[0], 0, 1), tiles)
    wup = gla_w_a_up[0].astype(BF16)
    wupf = jnp.zeros((LANES, dk), BF16).at[:rank].set(wup[0])
    wupb = jnp.zeros((LANES, dk), BF16).at[rank:2 * rank].set(wup[1])
    baf, bab = gla_b_a[0, 0:1], gla_b_a[0, 1:2]
    n1g, n2g = norm1_g[0:1], norm2_g[0:1]

    t = _tiles(n, exp_w_gate.shape[-1])
    s0f, s0b = _ctx_states(ctx, csh1, csc1, n1g, wt, wupf, wupb, baf, bab, heads, dk, dv)
    z, lr = _inproj(x, sh1, sc1, n1g, wt, t["inproj"])
    o_f, o_b = _gla(z, lr, wupf, wupb, baf, bab, s0f, s0b, heads, dk, dv, t["gla"])
    x1, h2, aff_t = _mixer(
        o_f, o_b, z, x, jnp.tile(gla_norm_g[0:1], (1, heads)), conv_w[0],
        gla_w_o[0].astype(BF16), conv_w_out[0].astype(BF16), merge_w_out[0].astype(BF16),
        gt1, sh2, sc2, n2g, router_w[0].T.astype(BF16), heads, dk, t["mixer"])

    pos, before, idx = _route(aff_t, cap)
    ye = _experts(h2, idx, exp_w_gate[0], exp_w_up[0], exp_w_down[0], cap,
                  cap + _combine_windows(t["combine"], n_exp)[1], t["expert_f"])
    return _combine(ye, pos, aff_t, before, x1, gt2, final_g.reshape(1, d), cap, t["combine"])
```

```python
import functools

import numpy as np
import jax
import jax.numpy as jnp
from jax import lax
from jax.experimental import pallas as pl
from jax.experimental.pallas import tpu as pltpu

F32 = jnp.float32
BF16 = jnp.bfloat16
I32 = jnp.int32

EPS = 1e-6
N_MOD = 6
GRID_W = 64
GLA_CHUNK = 64
GLA_GATE_NORM = 16.0
EC_CAPACITY = 2

LANES = 128
MXU_DEPTH = 256
BF16_ROWS = 16
VMEM_LIMIT = 56 * 1024 * 1024


def _cparams(*sem):
    return pltpu.CompilerParams(dimension_semantics=sem, vmem_limit_bytes=VMEM_LIMIT)


def _dot(a, b):
    return jnp.dot(a, b, preferred_element_type=F32)


def _dot_nt(a, b):
    return lax.dot_general(a, b, (((1,), (1,)), ((), ())), preferred_element_type=F32)


def _dot_tn(a, b):
    return lax.dot_general(a, b, (((0,), (0,)), ((), ())), preferred_element_type=F32)


def _sigmoid(v):
    return 1.0 / (1.0 + jnp.exp(-v))


def _log_sigmoid(v):
    return jnp.minimum(v, 0.0) - jnp.log(1.0 + jnp.exp(-jnp.abs(v)))


def _rms_mod(xv, g, shift, scale):
    ms = jnp.mean(xv * xv, axis=-1, keepdims=True)
    y = xv * lax.rsqrt(ms + EPS) * g
    return y * (1.0 + scale) + shift


def _split_bf16(v):
    hi = v.astype(BF16)
    lo = (v - hi.astype(F32)).astype(BF16)
    return hi, lo


def _adaln_kernel(c_ref, w_ref, b_ref, o_ref):
    cv = c_ref[...]
    s = (cv * _sigmoid(cv)).astype(BF16)
    o_ref[...] = _dot(s, w_ref[...].astype(BF16)) + b_ref[...]


def _adaln(cond, w, b):
    rows, d = cond.shape
    nout = w.shape[1]
    tn = _pick(nout, 2 * d)
    return pl.pallas_call(
        _adaln_kernel,
        grid=(nout // tn,),
        in_specs=[pl.BlockSpec((rows, d), lambda j: (0, 0)),
                  pl.BlockSpec((d, tn), lambda j: (0, j)),
                  pl.BlockSpec((1, tn), lambda j: (0, j))],
        out_specs=pl.BlockSpec((rows, tn), lambda j: (0, j)),
        out_shape=jax.ShapeDtypeStruct((rows, nout), F32),
        compiler_params=_cparams("arbitrary"),
        name="adaln",
    )(cond, w, b.reshape(1, nout))


def _wprep_kernel(tiles, wt_hbm, o_ref, buf, sem):
    def copy(t):
        row, cnt = tiles[t]
        return pltpu.make_async_copy(wt_hbm.at[pl.ds(row, cnt), :],
                                     buf.at[t % 2, pl.ds(0, cnt), :], sem.at[t % 2])

    copy(0).start()
    out_row = 0
    for t, (_, cnt) in enumerate(tiles):
        if t + 1 < len(tiles):
            copy(t + 1).start()
        copy(t).wait()
        o_ref[out_row:out_row + cnt, :] = buf[t % 2, 0:cnt, :].astype(BF16)
        out_row += cnt


def _wprep(wt, tiles):
    d = wt.shape[1]
    rows = sum(cnt for _, cnt in tiles)
    return pl.pallas_call(
        functools.partial(_wprep_kernel, tiles),
        in_specs=[pl.BlockSpec(memory_space=pl.ANY)],
        out_specs=pl.BlockSpec((rows, d), lambda: (0, 0)),
        out_shape=jax.ShapeDtypeStruct((rows, d), BF16),
        scratch_shapes=[pltpu.VMEM((2, max(cnt for _, cnt in tiles), d), F32),
                        pltpu.SemaphoreType.DMA((2,))],
        compiler_params=pltpu.CompilerParams(vmem_limit_bytes=VMEM_LIMIT),
        name="wprep",
    )(wt)


def _ctx_kernel(heads, ctx_ref, sh_ref, sc_ref, g_ref, wk_ref, wv_ref, wlr_ref,
                wupf_ref, wupb_ref, baf_ref, bab_ref, sf_ref, sb_ref):
    n = ctx_ref.shape[0]
    hc = _rms_mod(ctx_ref[...], g_ref[...], sh_ref[...], sc_ref[...]).astype(BF16)
    k = _dot_nt(hc, wk_ref[...])
    v = _dot_nt(hc, wv_ref[...]).astype(BF16)
    lr = _dot_nt(hc, wlr_ref[...]).astype(BF16)
    hk = k.shape[1] // heads
    hv = v.shape[1] // heads
    row = lax.broadcasted_iota(I32, (n, n), 0)
    col = lax.broadcasted_iota(I32, (n, n), 1)
    for wup_ref, ba_ref, s_ref, tri, last in (
            (wupf_ref, baf_ref, sf_ref, col <= row, n - 1),
            (wupb_ref, bab_ref, sb_ref, col >= row, 0)):
        la = _log_sigmoid(_dot(lr, wup_ref[...]) + ba_ref[...]) * (1.0 / GLA_GATE_NORM)
        hi, lo = _split_bf16(la)
        t = jnp.where(tri, 1.0, 0.0).astype(BF16)
        b = _dot(t, hi) + _dot(t, lo)
        kd = (k * jnp.exp(b[last:last + 1, :] - b)).astype(BF16)
        for h in range(heads):
            s_ref[h] = _dot_tn(v[:, h * hv:(h + 1) * hv], kd[:, h * hk:(h + 1) * hk])


def _ctx_states(ctx, sh, sc, g, wt, wupf, wupb, baf, bab, heads, dk, dv):
    bsz, n, d = ctx.shape
    lr_blk = (wt.shape[0] - LANES) // LANES
    hk, hv = dk // heads, dv // heads
    full = lambda *shape: pl.BlockSpec(shape, lambda b: (0,) * len(shape))
    st = jax.ShapeDtypeStruct((bsz, heads, hv, hk), F32)
    sspec = pl.BlockSpec((None, heads, hv, hk), lambda b: (b, 0, 0, 0))
    return pl.pallas_call(
        functools.partial(_ctx_kernel, heads),
        grid=(bsz,),
        in_specs=[pl.BlockSpec((None, n, d), lambda b: (b, 0, 0)),
                  full(1, d), full(1, d), full(1, d),
                  pl.BlockSpec((dk, d), lambda b: (1, 0)),
                  pl.BlockSpec((dv, d), lambda b: (2 * dk // dv, 0)),
                  pl.BlockSpec((LANES, d), lambda b: (lr_blk, 0)),
                  full(LANES, dk), full(LANES, dk), full(1, dk), full(1, dk)],
        out_specs=(sspec, sspec),
        out_shape=(st, st),
        compiler_params=_cparams("arbitrary"),
        name="ctx_state",
    )(ctx, sh, sc, g, wt, wt, wt, wupf, wupb, baf, bab)


def _inproj_kernel(x_ref, sh_ref, sc_ref, g_ref, w_ref, wlr_ref, z_ref, lr_ref, h_scr):
    j = pl.program_id(2)
    d = x_ref.shape[1]

    @pl.when(j == 0)
    def _():
        hb = _rms_mod(x_ref[...], g_ref[...], sh_ref[...], sc_ref[...]).astype(BF16)
        h_scr[...] = hb
        lr_ref[...] = _dot_nt(hb, wlr_ref[...])
        z_ref[...] = _dot_nt(hb, w_ref[...]).astype(BF16)

    @pl.when(j == 1)
    def _():
        a = _dot_nt(h_scr[...], w_ref[...])
        gv = a[:, :d]
        z_ref[:, :d] = (gv * _sigmoid(gv)).astype(BF16)
        z_ref[:, d:] = a[:, d:].astype(BF16)

    @pl.when(j == 2)
    def _():
        a = _dot_nt(h_scr[...], w_ref[...])
        z_ref[:, :d] = (a[:, d:] * a[:, :d]).astype(BF16)
        z_ref[:, d:] = jnp.zeros((z_ref.shape[0], d), BF16)

    @pl.when(j == 3)
    def _():
        z_ref[...] = _sigmoid(_dot_nt(h_scr[...], w_ref[...])).astype(BF16)


def _inproj(x, sh, sc, g, wt, tm):
    bsz, n, d = x.shape
    p = wt.shape[0] - LANES
    tn = 2 * d
    assert p == 4 * tn
    return pl.pallas_call(
        _inproj_kernel,
        grid=(bsz, n // tm, p // tn),
        in_specs=[pl.BlockSpec((None, tm, d), lambda b, i, j: (b, i, 0)),
                  pl.BlockSpec((None, 1, d), lambda b, i, j: (b, 0, 0)),
                  pl.BlockSpec((None, 1, d), lambda b, i, j: (b, 0, 0)),
                  pl.BlockSpec((1, d), lambda b, i, j: (0, 0)),
                  pl.BlockSpec((tn, d), lambda b, i, j: (j, 0)),
                  pl.BlockSpec((LANES, d), lambda b, i, j: (p // LANES, 0))],
        out_specs=(pl.BlockSpec((None, tm, tn), lambda b, i, j: (b, i, j)),
                   pl.BlockSpec((None, tm, LANES), lambda b, i, j: (b, i, 0))),
        out_shape=(jax.ShapeDtypeStruct((bsz, n, p), BF16),
                   jax.ShapeDtypeStruct((bsz, n, LANES), F32)),
        scratch_shapes=[pltpu.VMEM((tm, d), BF16)],
        compiler_params=_cparams("arbitrary", "arbitrary", "arbitrary"),
        name="inproj",
    )(x, sh, sc, g, wt, wt)


def _gla_kernel(heads, dk, qkvf_ref, lrf_ref, qkvb_ref, lrb_ref,
                wupf_ref, wupb_ref, baf_ref, bab_ref, trif_ref, trib_ref, s0f_ref, s0b_ref,
                of_ref, ob_ref, sf_scr, sb_scr):
    tb = qkvf_ref.shape[0]
    dv = qkvf_ref.shape[1] - 2 * dk
    hk, hv = dk // heads, dv // heads
    ck = GLA_CHUNK
    nck = tb // ck
    q_scale = hk ** -0.5

    @pl.when(pl.program_id(1) == 0)
    def _():
        sf_scr[...] = s0f_ref[...]
        sb_scr[...] = s0b_ref[...]

    crow = lax.broadcasted_iota(I32, (ck, ck), 0)
    ccol = lax.broadcasted_iota(I32, (ck, ck), 1)

    dirs = (
        (qkvf_ref, lrf_ref, wupf_ref, baf_ref, of_ref, sf_scr,
         trif_ref, ccol <= crow, ck - 1, range(nck)),
        (qkvb_ref, lrb_ref, wupb_ref, bab_ref, ob_ref, sb_scr,
         trib_ref, ccol >= crow, 0, range(nck - 1, -1, -1)),
    )
    b_alls = []
    for _, lr_ref, wup_ref, ba_ref, _, _, tri, _, _, _ in dirs:
        la = _log_sigmoid(_dot(lr_ref[...].astype(BF16), wup_ref[...]) + ba_ref[...])
        la = la * (1.0 / GLA_GATE_NORM)
        hi, lo = _split_bf16(la)
        t = tri[...]
        tr = t.shape[0]
        b_alls.append(jnp.concatenate(
            [_dot(t, hi[r:r + tr]) + _dot(t, lo[r:r + tr]) for r in range(0, tb, tr)], axis=0))
    for k in range(nck):
        work = []
        for di, (qkv_ref, _, _, _, _, _, _, cmask, last, order) in enumerate(dirs):
            r0 = order[k] * ck
            b = b_alls[di][r0:r0 + ck, :]
            b_last = b[last:last + 1, :]
            qc = qkv_ref[r0:r0 + ck, 0:dk].astype(F32) * q_scale
            kc = qkv_ref[r0:r0 + ck, dk:2 * dk].astype(F32)
            q_in = (qc * jnp.exp(b)).astype(BF16)
            k_in = (kc * jnp.exp(-b)).astype(BF16)
            k_dec = (kc * jnp.exp(b_last - b)).astype(BF16)
            decay = jnp.exp(b_last)
            units = []
            for h in range(heads):
                qh = q_in[:, h * hk:(h + 1) * hk]
                vh = qkv_ref[r0:r0 + ck, 2 * dk + h * hv:2 * dk + (h + 1) * hv]
                scores = jnp.where(cmask, _dot_nt(qh, k_in[:, h * hk:(h + 1) * hk]), 0.0)
                o_intra = _dot(scores.astype(BF16), vh)
                ut = _dot_tn(vh, k_dec[:, h * hk:(h + 1) * hk])
                units.append((qh, o_intra, ut, decay[:, h * hk:(h + 1) * hk]))
            work.append((r0, units))
        for (r0, units), (_, _, _, _, o_ref, s_scr, _, _, _, _) in zip(work, dirs):
            for h, (qh, o_intra, ut, dec) in enumerate(units):
                st = s_scr[h]
                o = o_intra + _dot_nt(qh, st.astype(BF16))
                o_ref[r0:r0 + ck, h * hv:(h + 1) * hv] = o.astype(BF16)
                s_scr[h] = st * dec + ut


def _gla(z, lr, wupf, wupb, baf, bab, s0f, s0b, heads, dk, dv, tb):
    bsz, n, _ = z.shape
    nb = n // tb
    hk, hv = dk // heads, dv // heads
    fwd = lambda cb: (lambda b, i: (b, i, cb))
    bwd = lambda cb: (lambda b, i: (b, nb - 1 - i, cb))
    full = lambda *shape: pl.BlockSpec(shape, lambda b, i: (0,) * len(shape))
    sspec = pl.BlockSpec((None, heads, hv, hk), lambda b, i: (b, 0, 0, 0))
    ost = jax.ShapeDtypeStruct((bsz, n, dv), BF16)
    tr = min(tb, MXU_DEPTH)
    assert tb % tr == 0 and tr % GLA_CHUNK == 0
    ri = np.arange(tr)
    same_chunk = (ri[:, None] // GLA_CHUNK) == (ri[None, :] // GLA_CHUNK)
    trif = jnp.asarray(same_chunk & (ri[None, :] <= ri[:, None]), BF16)
    trib = jnp.asarray(same_chunk & (ri[None, :] >= ri[:, None]), BF16)
    qkv = 2 * dk + dv
    return pl.pallas_call(
        functools.partial(_gla_kernel, heads, dk),
        grid=(bsz, nb),
        in_specs=[pl.BlockSpec((None, tb, qkv), fwd(0)), pl.BlockSpec((None, tb, LANES), fwd(0)),
                  pl.BlockSpec((None, tb, qkv), bwd(0)), pl.BlockSpec((None, tb, LANES), bwd(0)),
                  full(LANES, dk), full(LANES, dk), full(1, dk), full(1, dk),
                  full(tr, tr), full(tr, tr), sspec, sspec],
        out_specs=(pl.BlockSpec((None, tb, dv), fwd(0)), pl.BlockSpec((None, tb, dv), bwd(0))),
        out_shape=(ost, ost),
        scratch_shapes=[pltpu.VMEM((heads, hv, hk), F32), pltpu.VMEM((heads, hv, hk), F32)],
        compiler_params=_cparams("arbitrary", "arbitrary"),
        name="gla",
    )(z, lr, z, lr, wupf, wupb, baf, bab, trif, trib, s0f, s0b)


def _mixer_kernel(heads, of_ref, ob_ref, sg_ref, bg_ref, cu_ref, srg_ref, src_ref, x_ref,
                  gng_ref, cw_ref, wo_ref, wco_ref, wm_ref, gt1_ref, sh2_ref, sc2_ref, n2g_ref,
                  rwt_ref, x1_ref, h2_ref, aff_ref):
    tm, dv = of_ref.shape
    hv = dv // heads
    o = of_ref[...].astype(F32) + ob_ref[...].astype(F32)
    parts = []
    for h in range(heads):
        oh = o[:, h * hv:(h + 1) * hv]
        ms = jnp.mean(oh * oh, axis=-1, keepdims=True)
        parts.append(oh * lax.rsqrt(ms + EPS))
    on = jnp.concatenate(parts, axis=1) * gng_ref[...]
    y_gla = _dot((on * sg_ref[...].astype(F32)).astype(BF16), wo_ref[...])

    cu = cu_ref[...].astype(F32)
    gcol = lax.broadcasted_iota(I32, cu.shape, 0) & (GRID_W - 1)
    left = jnp.where(gcol == 0, 0.0, pltpu.roll(cu, 1, 0))
    right = jnp.where(gcol == GRID_W - 1, 0.0, pltpu.roll(cu, tm - 1, 0))
    cw = cw_ref[...]
    conv = left * cw[0:1, :] + cu * cw[1:2, :] + right * cw[2:3, :]
    y_conv = _dot((bg_ref[...].astype(F32) * conv).astype(BF16), wco_ref[...])

    merged = srg_ref[...].astype(F32) * y_gla + src_ref[...].astype(F32) * y_conv
    y = _dot(merged.astype(BF16), wm_ref[...])
    x1 = x_ref[...] + gt1_ref[...] * y
    x1_ref[...] = x1
    h2 = _rms_mod(x1, n2g_ref[...], sh2_ref[...], sc2_ref[...])
    rpt = h2.shape[1] // LANES
    for a in range(rpt):
        h2_ref[pl.ds(a, tm, stride=rpt), :] = h2[:, a * LANES:(a + 1) * LANES]
    logits = _dot_nt(rwt_ref[...], h2.astype(BF16))
    ex = jnp.exp(logits - jnp.max(logits, axis=0, keepdims=True))
    aff_ref[...] = ex / jnp.sum(ex, axis=0, keepdims=True)


def _mixer(o_f, o_b, z, x, gng, conv_w, w_o, w_co, w_m, gt1, sh2, sc2, n2g, rwt, heads, dk, tm):
    bsz, n, d = x.shape
    dv = o_f.shape[2]
    c = conv_w.shape[1]
    e = rwt.shape[0]
    zb = lambda cb: pl.BlockSpec((None, tm, d), lambda b, i: (b, i, cb))
    base = (2 * dk + dv) // d
    tok = lambda w: pl.BlockSpec((None, tm, w), lambda b, i: (b, i, 0))
    full = lambda *shape: pl.BlockSpec(shape, lambda b, i: (0,) * len(shape))
    perb = pl.BlockSpec((None, 1, d), lambda b, i: (b, 0, 0))
    return pl.pallas_call(
        functools.partial(_mixer_kernel, heads),
        grid=(bsz, n // tm),
        in_specs=[tok(dv), tok(dv), zb(base), zb(base + 1), zb(base + 2),
                  zb(base + 4), zb(base + 5), tok(d),
                  full(1, dv), full(3, c), full(dv, d), full(c, d), full(d, d),
                  perb, perb, perb, full(1, d), full(e, d)],
        out_specs=(tok(d), pl.BlockSpec((None, tm * (d // LANES), LANES), lambda b, i: (b, i, 0)),
                   pl.BlockSpec((None, e, tm), lambda b, i: (b, 0, i))),
        out_shape=(jax.ShapeDtypeStruct((bsz, n, d), F32),
                   jax.ShapeDtypeStruct((bsz, n * (d // LANES), LANES), F32),
                   jax.ShapeDtypeStruct((bsz, e, n), F32)),
        compiler_params=_cparams("arbitrary", "arbitrary"),
        name="mixer",
    )(o_f, o_b, z, z, z, z, z, x, gng, conv_w, w_o, w_co, w_m, gt1, sh2, sc2, n2g, rwt)


def _route_kernel(cap, aff_ref, lmat_ref, pos_ref, base_ref, idx_ref, loc_scr, bef_scr):
    e, n = aff_ref.shape
    nt = n // LANES
    aff = aff_ref[...]

    def count(mask):
        return jnp.sum(jnp.where(mask, 1.0, 0.0), axis=1, keepdims=True)

    def search(k, tbits):
        cand = tbits | jnp.left_shift(jnp.int32(1), 30 - k)
        ok = count(aff >= lax.bitcast_convert_type(cand, F32)) >= cap
        return jnp.where(ok, cand, tbits)

    tbits = lax.fori_loop(0, 31, search, jnp.zeros((e, 1), I32))
    thr = lax.bitcast_convert_type(tbits, F32)
    gt = aff > thr
    eq = aff == thr
    need = cap - count(gt)

    def stack(mask):
        m = jnp.where(mask, 1.0, 0.0)
        return jnp.concatenate([m[:, j * LANES:(j + 1) * LANES] for j in range(nt)], axis=0)

    r = lax.broadcasted_iota(I32, (LANES, LANES), 0)
    cl = lax.broadcasted_iota(I32, (LANES, LANES), 1)
    upper = jnp.where(r <= cl, 1.0, 0.0).astype(BF16)
    ones = jnp.ones((LANES, LANES), BF16)
    lmat = lmat_ref[...]

    def cumsum(ms):
        msb = ms.astype(BF16)
        before = _dot(_dot(lmat, msb).astype(BF16), ones)
        return _dot(msb, upper), before

    eq_s = stack(eq)
    loc_eq, before_eq = cumsum(eq_s)
    need_s = jnp.concatenate([need] * nt, axis=0)
    sel = jnp.maximum(stack(gt), jnp.where(loc_eq + before_eq <= need_s, eq_s, 0.0))
    loc, before = cumsum(sel)
    pos_ref[...] = jnp.where(sel > 0.0, loc + before - 1.0, -1.0).astype(I32)
    base_ref[...] = before.astype(I32)

    loc_scr[...] = loc
    bef_scr[...] = before
    slot = lax.broadcasted_iota(I32, (1, cap), 1).astype(F32)
    for ex in range(e):
        loc_e = loc_scr[pl.ds(ex, nt, stride=e), :]
        tprev = bef_scr[pl.ds(ex, nt, stride=e), :][:, 0:1]
        tincl = tprev + loc_e[:, LANES - 1:LANES]
        in_tile = jnp.where((tprev <= slot) & (slot < tincl), 1.0, 0.0)
        tile = jnp.sum(jnp.where(tincl <= slot, 1.0, 0.0), axis=0, keepdims=True)
        s_loc = slot - jnp.sum(in_tile * tprev, axis=0, keepdims=True)
        counts = _dot_tn(loc_e.astype(BF16), in_tile.astype(BF16))
        lane = jnp.sum(jnp.where(counts <= s_loc, 1.0, 0.0), axis=0, keepdims=True)
        idx_ref[pl.ds(ex, 1), :] = (tile * LANES + lane).astype(I32)


def _route(aff_t, cap):
    bsz, e, n = aff_t.shape
    nt = n // LANES
    rows = nt * e
    ri = np.arange(rows)
    lmat = ((ri[None, :] % e == ri[:, None] % e) & (ri[None, :] // e < ri[:, None] // e))
    lmat = jnp.asarray(lmat, BF16)
    st = jax.ShapeDtypeStruct((bsz, rows, LANES), I32)
    ospec = pl.BlockSpec((None, rows, LANES), lambda b: (b, 0, 0))
    return pl.pallas_call(
        functools.partial(_route_kernel, cap),
        grid=(bsz,),
        in_specs=[pl.BlockSpec((None, e, n), lambda b: (b, 0, 0)),
                  pl.BlockSpec((rows, rows), lambda b: (0, 0))],
        out_specs=(ospec, ospec, pl.BlockSpec((None, e, cap), lambda b: (b, 0, 0))),
        out_shape=(st, st, jax.ShapeDtypeStruct((bsz, e, cap), I32)),
        scratch_shapes=[pltpu.VMEM((rows, LANES), F32), pltpu.VMEM((rows, LANES), F32)],
        compiler_params=_cparams("arbitrary"),
        name="route",
    )(aff_t, lmat)


def _slot_window(base):
    return pl.multiple_of((base // BF16_ROWS) * BF16_ROWS, BF16_ROWS)


def _expert_kernel(cap, idx_ref, h2_hbm, wg_ref, wu_ref, wd_ref, ye_ref, xbuf, acc, sem):
    e, f = pl.program_id(1), pl.program_id(2)
    n_exp, nf = pl.num_programs(1), pl.num_programs(2)
    last = pl.num_programs(0) * n_exp - 1
    lin = pl.program_id(0) * n_exp + e
    slot = lin % 2
    per_step = cap // nf
    rpt = wg_ref.shape[0] // LANES

    def row_copy(lin_t, part, i, slot_t):
        tok = idx_ref[(lin_t * nf + part) * per_step + i]
        return pltpu.make_async_copy(h2_hbm.at[pl.ds(pl.multiple_of(tok * rpt, rpt), rpt), :],
                                     xbuf.at[slot_t, part, pl.ds(i * rpt, rpt), :],
                                     sem.at[slot_t])

    def wait_rows(slot_t):
        pltpu.make_async_copy(xbuf.at[slot_t], xbuf.at[slot_t], sem.at[slot_t]).wait()

    @pl.when((lin == 0) & (f == 0))
    def _():
        acc[...] = jnp.zeros_like(acc)
        for part in range(nf):
            def first(i, carry):
                row_copy(lin, part, i, slot).start()
                return carry
            lax.fori_loop(0, per_step, first, 0)

    @pl.when(f == 0)
    def _():
        wait_rows(slot)

    xv = jnp.concatenate(
        [jnp.concatenate([xbuf[slot, part, pl.ds(a, per_step, stride=rpt), :].astype(BF16)
                          for a in range(rpt)], axis=1) for part in range(nf)], axis=0)
    hg = _dot(xv, wg_ref[...].astype(BF16))
    hu = _dot(xv, wu_ref[...].astype(BF16))
    hid = (hg * _sigmoid(hg) * hu).astype(BF16)
    acc[...] = jnp.where(f > 0, acc[...], 0.0) + _dot(hid, wd_ref[...].astype(BF16))

    nxt = jnp.minimum(lin + 1, last)
    for i in range(per_step):
        row_copy(nxt, f, i, 1 - slot).start(priority=1)

    @pl.when((lin == last) & (f == nf - 1))
    def _():
        wait_rows(1 - slot)

    @pl.when(f == nf - 1)
    def _():
        ye_ref[0:cap, :] = acc[...].astype(BF16)
        ye_ref[cap:, :] = jnp.zeros((ye_ref.shape[0] - cap, ye_ref.shape[1]), BF16)


def _experts(h2, idx, w_gate, w_up, w_down, cap, rows, tf):
    n_exp, d, df = w_gate.shape
    bsz = h2.shape[0]
    rpt = d // LANES
    n = h2.shape[1] // rpt
    nf = df // tf
    assert cap % nf == 0
    rows_global = (idx + (jnp.arange(bsz, dtype=I32) * n)[:, None, None]).reshape(-1)
    return pl.pallas_call(
        functools.partial(_expert_kernel, cap),
        grid_spec=pltpu.PrefetchScalarGridSpec(
            num_scalar_prefetch=1,
            grid=(bsz, n_exp, nf),
            in_specs=[pl.BlockSpec(memory_space=pl.ANY),
                      pl.BlockSpec((None, d, tf), lambda b, e, f, s: (e, 0, f)),
                      pl.BlockSpec((None, d, tf), lambda b, e, f, s: (e, 0, f)),
                      pl.BlockSpec((None, tf, d), lambda b, e, f, s: (e, f, 0))],
            out_specs=pl.BlockSpec((None, None, rows, d), lambda b, e, f, s: (b, e, 0, 0)),
            scratch_shapes=[pltpu.VMEM((2, nf, (cap // nf) * rpt, LANES), F32),
                            pltpu.VMEM((cap, d), F32), pltpu.SemaphoreType.DMA((2,))]),
        out_shape=jax.ShapeDtypeStruct((bsz, n_exp, rows, d), BF16),
        compiler_params=_cparams("arbitrary", "arbitrary", "arbitrary"),
        name="experts",
    )(rows_global, h2.reshape(bsz * n * rpt, LANES), w_gate, w_up, w_down)


def _combine_windows(tm, n_exp):
    mean = tm * EC_CAPACITY / n_exp
    usual = mean + 5.5 * (mean * (1 - EC_CAPACITY / n_exp)) ** 0.5 + BF16_ROWS - 1
    usual = min(-(-int(usual) // BF16_ROWS) * BF16_ROWS, tm + BF16_ROWS)
    return usual, tm + BF16_ROWS


def _combine_kernel(n_exp, nsub, base_ref, ovf_ref, pos_ref, aff_ref, x1_ref, gt2_ref, fg_ref,
                    ye_hbm, out_ref, buf, fbuf, ffn_scr, sem, fsem):
    b, j = pl.program_id(0), pl.program_id(1)
    nt = pl.num_programs(1)
    step = b * nt + j
    slot = step % 2
    wp, wfull = buf.shape[2], fbuf.shape[0]
    d = buf.shape[-1]

    def start_of(tile, ex):
        return _slot_window(base_ref[tile * nsub * n_exp + ex])

    def window_copy(tile, ex, dst_slot):
        return pltpu.make_async_copy(ye_hbm.at[tile // nt, ex, pl.ds(start_of(tile, ex), wp), :],
                                     buf.at[dst_slot, ex], sem.at[dst_slot])

    def full_copy(ex):
        return pltpu.make_async_copy(ye_hbm.at[b, ex, pl.ds(start_of(step, ex), wfull), :],
                                     fbuf, fsem)

    @pl.when(step == 0)
    def _():
        for ex in range(n_exp):
            window_copy(step, ex, slot).start()

    @pl.when(step + 1 < pl.num_programs(0) * nt)
    def _():
        for ex in range(n_exp):
            window_copy(step + 1, ex, 1 - slot).start()

    def gates_t(rows, ex):
        srow = lax.broadcasted_iota(I32, (rows, LANES), 0)
        s0 = start_of(step, ex)
        parts = []
        for u in range(nsub):
            rel = pos_ref[pl.ds(u * n_exp + ex, 1), :] - s0
            gate = aff_ref[pl.ds(ex, 1), u * LANES:(u + 1) * LANES]
            parts.append(jnp.where(srow == rel, gate, 0.0).astype(BF16))
        return jnp.concatenate(parts, axis=1)

    for ex in range(n_exp):
        window_copy(step, ex, slot).wait()

    @pl.when(ovf_ref[step] == 0)
    def _():
        g_all = jnp.concatenate([gates_t(wp, ex) for ex in range(n_exp)], axis=0)
        ffn_scr[...] = _dot_tn(g_all, buf[slot].reshape(n_exp * wp, d))

    @pl.when(ovf_ref[step] != 0)
    def _():
        ffn_scr[...] = jnp.zeros_like(ffn_scr)
        for ex in range(n_exp):
            full_copy(ex).start()
            full_copy(ex).wait()
            ffn_scr[...] += _dot_tn(gates_t(wfull, ex), fbuf[...])

    x2 = x1_ref[...] + gt2_ref[...] * ffn_scr[...]
    ms = jnp.mean(x2 * x2, axis=-1, keepdims=True)
    out_ref[...] = x2 * lax.rsqrt(ms + EPS) * fg_ref[...]


def _combine(ye, pos, aff_t, before, x1, gt2, final_g, cap, tm):
    bsz, n, d = x1.shape
    n_exp = aff_t.shape[1]
    nsub = tm // LANES
    nt = n // tm
    wp, wfull = _combine_windows(tm, n_exp)
    starts = before[:, :, 0].reshape(bsz, n // LANES, n_exp)
    tile_starts = starts[:, ::nsub]
    tile_ends = jnp.concatenate([tile_starts[:, 1:], jnp.full((bsz, 1, n_exp), cap, I32)], axis=1)
    overflow = (tile_starts % BF16_ROWS) + (tile_ends - tile_starts) > wp
    ovf = jnp.any(overflow, axis=-1).astype(I32).reshape(-1)
    return pl.pallas_call(
        functools.partial(_combine_kernel, n_exp, nsub),
        grid_spec=pltpu.PrefetchScalarGridSpec(
            num_scalar_prefetch=2,
            grid=(bsz, nt),
            in_specs=[pl.BlockSpec((None, nsub * n_exp, LANES), lambda b, j, s, o: (b, j, 0)),
                      pl.BlockSpec((None, n_exp, tm), lambda b, j, s, o: (b, 0, j)),
                      pl.BlockSpec((None, tm, d), lambda b, j, s, o: (b, j, 0)),
                      pl.BlockSpec((None, 1, d), lambda b, j, s, o: (b, 0, 0)),
                      pl.BlockSpec((1, d), lambda b, j, s, o: (0, 0)),
                      pl.BlockSpec(memory_space=pl.ANY)],
            out_specs=pl.BlockSpec((None, tm, d), lambda b, j, s, o: (b, j, 0)),
            scratch_shapes=[pltpu.VMEM((2, n_exp, wp, d), BF16),
                            pltpu.VMEM((wfull, d), BF16),
                            pltpu.VMEM((tm, d), F32),
                            pltpu.SemaphoreType.DMA((2,)), pltpu.SemaphoreType.DMA(())]),
        out_shape=jax.ShapeDtypeStruct((bsz, n, d), F32),
        compiler_params=_cparams("arbitrary", "arbitrary"),
        name="combine",
    )(starts.reshape(-1), ovf, pos, aff_t, x1, gt2, final_g, ye)


def _pick(n, pref):
    t = min(n, pref)
    while n % t:
        t //= 2
    return t


def _tiles(n, d_expert):
    return dict(inproj=_pick(n, 1024), gla=_pick(n, 512), mixer=_pick(n, 512),
                combine=_pick(n, 512), expert_f=_pick(d_expert, 512))


def kernel(x, c, ctx, c_ctx, w_ada, b_ada, norm1_g, norm2_g, w_in, gla_w_a_up, gla_b_a,
           gla_norm_g, gla_w_o, conv_w, conv_w_out, merge_w_out, router_w,
           exp_w_gate, exp_w_up, exp_w_down, final_g):
    bsz, n, d = x.shape
    depth = w_ada.shape[0]
    assert depth == 1, "single-layer trunk"
    rank, dk = gla_w_a_up.shape[-2:]
    dv = gla_w_o.shape[1]
    hv = gla_norm_g.shape[-1]
    heads = dv // hv
    cch = conv_w.shape[-1]
    n_exp = router_w.shape[-1]
    cap = EC_CAPACITY * n // n_exp
    assert 2 * dk == d and dv == d and cch == d and 2 * rank <= LANES
    assert n % LANES == 0 and n % GRID_W == 0 and bsz + 1 <= 8 and n_exp % 8 == 0

    cond = jnp.zeros((8, d), F32).at[:bsz].set(c).at[bsz].set(c_ctx)
    mod = _adaln(cond, w_ada[0], b_ada[0]).reshape(8, N_MOD, d)
    sh1, sc1, gt1, sh2, sc2, gt2 = [mod[:bsz, k][:, None, :] for k in range(N_MOD)]
    csh1, csc1 = mod[bsz:bsz + 1, 0], mod[bsz:bsz + 1, 1]

    lr0 = 2 * dk + 2 * dv
    u0 = lr0 + 2 * rank
    groups = ((0, lr0), (u0 + cch, cch), (u0, cch), (u0 + 2 * cch, cch + 2 * d), (lr0, LANES))
    tiles = tuple((r0 + o, min(d, cnt - o)) for r0, cnt in groups for o in range(0, cnt, d))
    wt = _wprep(jnp.swapaxes(w_in[0], 0, 1), tiles)
    wup = gla_w_a_up[0].astype(BF16)
    wupf = jnp.zeros((LANES, dk), BF16).at[:rank].set(wup[0])
    wupb = jnp.zeros((LANES, dk), BF16).at[rank:2 * rank].set(wup[1])
    baf, bab = gla_b_a[0, 0:1], gla_b_a[0, 1:2]
    n1g, n2g = norm1_g[0:1], norm2_g[0:1]

    t = _tiles(n, exp_w_gate.shape[-1])
    s0f, s0b = _ctx_states(ctx, csh1, csc1, n1g, wt, wupf, wupb, baf, bab, heads, dk, dv)
    z, lr = _inproj(x, sh1, sc1, n1g, wt, t["inproj"])
    o_f, o_b = _gla(z, lr, wupf, wupb, baf, bab, s0f, s0b, heads, dk, dv, t["gla"])
    x1, h2, aff_t = _mixer(
        o_f, o_b, z, x, jnp.tile(gla_norm_g[0:1], (1, heads)), conv_w[0],
        gla_w_o[0].astype(BF16), conv_w_out[0].astype(BF16), merge_w_out[0].astype(BF16),
        gt1, sh2, sc2, n2g, router_w[0].T.astype(BF16), heads, dk, t["mixer"])

    pos, before, idx = _route(aff_t, cap)
    ye = _experts(h2, idx, exp_w_gate[0], exp_w_up[0], exp_w_down[0], cap,
                  cap + _combine_windows(t["combine"], n_exp)[1], t["expert_f"])
    return _combine(ye, pos, aff_t, before, x1, gt2, final_g.reshape(1, d), cap, t["combine"])
```

```python
import functools

import numpy as np
import jax
import jax.numpy as jnp
from jax import lax
from jax.experimental import pallas as pl
from jax.experimental.pallas import tpu as pltpu

F32 = jnp.float32
BF16 = jnp.bfloat16
I32 = jnp.int32

EPS = 1e-6
N_MOD = 6
GRID_W = 64
GLA_CHUNK = 64
GLA_GATE_NORM = 16.0
EC_CAPACITY = 2

LANES = 128
MXU_DEPTH = 256
BF16_ROWS = 16
VMEM_LIMIT = 56 * 1024 * 1024


def _cparams(*sem):
    return pltpu.CompilerParams(dimension_semantics=sem, vmem_limit_bytes=VMEM_LIMIT)


def _dot(a, b):
    return jnp.dot(a, b, preferred_element_type=F32)


def _dot_nt(a, b):
    return lax.dot_general(a, b, (((1,), (1,)), ((), ())), preferred_element_type=F32)


def _dot_tn(a, b):
    return lax.dot_general(a, b, (((0,), (0,)), ((), ())), preferred_element_type=F32)


def _sigmoid(v):
    return 1.0 / (1.0 + jnp.exp(-v))


def _log_sigmoid(v):
    return jnp.minimum(v, 0.0) - jnp.log(1.0 + jnp.exp(-jnp.abs(v)))


def _rms_mod(xv, g, shift, scale):
    ms = jnp.mean(xv * xv, axis=-1, keepdims=True)
    y = xv * lax.rsqrt(ms + EPS) * g
    return y * (1.0 + scale) + shift


def _split_bf16(v):
    hi = v.astype(BF16)
    lo = (v - hi.astype(F32)).astype(BF16)
    return hi, lo


def _adaln_kernel(c_ref, w_ref, b_ref, o_ref):
    cv = c_ref[...]
    s = (cv * _sigmoid(cv)).astype(BF16)
    o_ref[...] = _dot(s, w_ref[...].astype(BF16)) + b_ref[...]


def _adaln(cond, w, b):
    rows, d = cond.shape
    nout = w.shape[1]
    tn = _pick(nout, 2 * d)
    return pl.pallas_call(
        _adaln_kernel,
        grid=(nout // tn,),
        in_specs=[pl.BlockSpec((rows, d), lambda j: (0, 0)),
                  pl.BlockSpec((d, tn), lambda j: (0, j)),
                  pl.BlockSpec((1, tn), lambda j: (0, j))],
        out_specs=pl.BlockSpec((rows, tn), lambda j: (0, j)),
        out_shape=jax.ShapeDtypeStruct((rows, nout), F32),
        compiler_params=_cparams("arbitrary"),
        name="adaln",
    )(cond, w, b.reshape(1, nout))


def _wprep_kernel(tiles, wt_hbm, o_ref, buf, sem):
    def copy(t):
        row, cnt = tiles[t]
        return pltpu.make_async_copy(wt_hbm.at[pl.ds(row, cnt), :],
                                     buf.at[t % 2, pl.ds(0, cnt), :], sem.at[t % 2])

    copy(0).start()
    out_row = 0
    for t, (_, cnt) in enumerate(tiles):
        if t + 1 < len(tiles):
            copy(t + 1).start()
        copy(t).wait()
        o_ref[out_row:out_row + cnt, :] = buf[t % 2, 0:cnt, :].astype(BF16)
        out_row += cnt


def _wprep(wt, tiles):
    d = wt.shape[1]
    rows = sum(cnt for _, cnt in tiles)
    return pl.pallas_call(
        functools.partial(_wprep_kernel, tiles),
        in_specs=[pl.BlockSpec(memory_space=pl.ANY)],
        out_specs=pl.BlockSpec((rows, d), lambda: (0, 0)),
        out_shape=jax.ShapeDtypeStruct((rows, d), BF16),
        scratch_shapes=[pltpu.VMEM((2, max(cnt for _, cnt in tiles), d), F32),
                        pltpu.SemaphoreType.DMA((2,))],
        compiler_params=pltpu.CompilerParams(vmem_limit_bytes=VMEM_LIMIT),
        name="wprep",
    )(wt)


def _ctx_kernel(heads, ctx_ref, sh_ref, sc_ref, g_ref, wk_ref, wv_ref, wlr_ref,
                wupf_ref, wupb_ref, baf_ref, bab_ref, sf_ref, sb_ref):
    n = ctx_ref.shape[0]
    hc = _rms_mod(ctx_ref[...], g_ref[...], sh_ref[...], sc_ref[...]).astype(BF16)
    k = _dot_nt(hc, wk_ref[...])
    v = _dot_nt(hc, wv_ref[...]).astype(BF16)
    lr = _dot_nt(hc, wlr_ref[...]).astype(BF16)
    hk = k.shape[1] // heads
    hv = v.shape[1] // heads
    row = lax.broadcasted_iota(I32, (n, n), 0)
    col = lax.broadcasted_iota(I32, (n, n), 1)
    for wup_ref, ba_ref, s_ref, tri, last in (
            (wupf_ref, baf_ref, sf_ref, col <= row, n - 1),
            (wupb_ref, bab_ref, sb_ref, col >= row, 0)):
        la = _log_sigmoid(_dot(lr, wup_ref[...]) + ba_ref[...]) * (1.0 / GLA_GATE_NORM)
        hi, lo = _split_bf16(la)
        t = jnp.where(tri, 1.0, 0.0).astype(BF16)
        b = _dot(t, hi) + _dot(t, lo)
        kd = (k * jnp.exp(b[last:last + 1, :] - b)).astype(BF16)
        for h in range(heads):
            s_ref[h] = _dot_tn(v[:, h * hv:(h + 1) * hv], kd[:, h * hk:(h + 1) * hk])


def _ctx_states(ctx, sh, sc, g, wt, wupf, wupb, baf, bab, heads, dk, dv):
    bsz, n, d = ctx.shape
    lr_blk = (wt.shape[0] - LANES) // LANES
    hk, hv = dk // heads, dv // heads
    full = lambda *shape: pl.BlockSpec(shape, lambda b: (0,) * len(shape))
    st = jax.ShapeDtypeStruct((bsz, heads, hv, hk), F32)
    sspec = pl.BlockSpec((None, heads, hv, hk), lambda b: (b, 0, 0, 0))
    return pl.pallas_call(
        functools.partial(_ctx_kernel, heads),
        grid=(bsz,),
        in_specs=[pl.BlockSpec((None, n, d), lambda b: (b, 0, 0)),
                  full(1, d), full(1, d), full(1, d),
                  pl.BlockSpec((dk, d), lambda b: (1, 0)),
                  pl.BlockSpec((dv, d), lambda b: (2 * dk // dv, 0)),
                  pl.BlockSpec((LANES, d), lambda b: (lr_blk, 0)),
                  full(LANES, dk), full(LANES, dk), full(1, dk), full(1, dk)],
        out_specs=(sspec, sspec),
        out_shape=(st, st),
        compiler_params=_cparams("arbitrary"),
        name="ctx_state",
    )(ctx, sh, sc, g, wt, wt, wt, wupf, wupb, baf, bab)


def _inproj_kernel(x_ref, sh_ref, sc_ref, g_ref, w_ref, wlr_ref, z_ref, lr_ref, h_scr):
    j = pl.program_id(2)
    d = x_ref.shape[1]

    @pl.when(j == 0)
    def _():
        hb = _rms_mod(x_ref[...], g_ref[...], sh_ref[...], sc_ref[...]).astype(BF16)
        h_scr[...] = hb
        lr_ref[...] = _dot_nt(hb, wlr_ref[...])
        z_ref[...] = _dot_nt(hb, w_ref[...]).astype(BF16)

    @pl.when(j == 1)
    def _():
        a = _dot_nt(h_scr[...], w_ref[...])
        gv = a[:, :d]
        z_ref[:, :d] = (gv * _sigmoid(gv)).astype(BF16)
        z_ref[:, d:] = a[:, d:].astype(BF16)

    @pl.when(j == 2)
    def _():
        a = _dot_nt(h_scr[...], w_ref[...])
        z_ref[:, :d] = (a[:, d:] * a[:, :d]).astype(BF16)
        z_ref[:, d:] = jnp.zeros((z_ref.shape[0], d), BF16)

    @pl.when(j == 3)
    def _():
        z_ref[...] = _sigmoid(_dot_nt(h_scr[...], w_ref[...])).astype(BF16)


def _inproj(x, sh, sc, g, wt, tm):
    bsz, n, d = x.shape
    p = wt.shape[0] - LANES
    tn = 2 * d
    assert p == 4 * tn
    return pl.pallas_call(
        _inproj_kernel,
        grid=(bsz, n // tm, p // tn),
        in_specs=[pl.BlockSpec((None, tm, d), lambda b, i, j: (b, i, 0)),
                  pl.BlockSpec((None, 1, d), lambda b, i, j: (b, 0, 0)),
                  pl.BlockSpec((None, 1, d), lambda b, i, j: (b, 0, 0)),
                  pl.BlockSpec((1, d), lambda b, i, j: (0, 0)),
                  pl.BlockSpec((tn, d), lambda b, i, j: (j, 0)),
                  pl.BlockSpec((LANES, d), lambda b, i, j: (p // LANES, 0))],
        out_specs=(pl.BlockSpec((None, tm, tn), lambda b, i, j: (b, i, j)),
                   pl.BlockSpec((None, tm, LANES), lambda b, i, j: (b, i, 0))),
        out_shape=(jax.ShapeDtypeStruct((bsz, n, p), BF16),
                   jax.ShapeDtypeStruct((bsz, n, LANES), F32)),
        scratch_shapes=[pltpu.VMEM((tm, d), BF16)],
        compiler_params=_cparams("arbitrary", "arbitrary", "arbitrary"),
        name="inproj",
    )(x, sh, sc, g, wt, wt)


def _gla_kernel(heads, dk, qkvf_ref, lrf_ref, qkvb_ref, lrb_ref,
                wupf_ref, wupb_ref, baf_ref, bab_ref, trif_ref, trib_ref, s0f_ref, s0b_ref,
                of_ref, ob_ref, sf_scr, sb_scr):
    tb = qkvf_ref.shape[0]
    dv = qkvf_ref.shape[1] - 2 * dk
    hk, hv = dk // heads, dv // heads
    ck = GLA_CHUNK
    nck = tb // ck
    q_scale = hk ** -0.5

    @pl.when(pl.program_id(1) == 0)
    def _():
        sf_scr[...] = s0f_ref[...]
        sb_scr[...] = s0b_ref[...]

    crow = lax.broadcasted_iota(I32, (ck, ck), 0)
    ccol = lax.broadcasted_iota(I32, (ck, ck), 1)

    dirs = (
        (qkvf_ref, lrf_ref, wupf_ref, baf_ref, of_ref, sf_scr,
         trif_ref, ccol <= crow, ck - 1, range(nck)),
        (qkvb_ref, lrb_ref, wupb_ref, bab_ref, ob_ref, sb_scr,
         trib_ref, ccol >= crow, 0, range(nck - 1, -1, -1)),
    )
    b_alls = []
    for _, lr_ref, wup_ref, ba_ref, _, _, tri, _, _, _ in dirs:
        la = _log_sigmoid(_dot(lr_ref[...].astype(BF16), wup_ref[...]) + ba_ref[...])
        la = la * (1.0 / GLA_GATE_NORM)
        hi, lo = _split_bf16(la)
        t = tri[...]
        tr = t.shape[0]
        b_alls.append(jnp.concatenate(
            [_dot(t, hi[r:r + tr]) + _dot(t, lo[r:r + tr]) for r in range(0, tb, tr)], axis=0))
    for k in range(nck):
        work = []
        for di, (qkv_ref, _, _, _, _, _, _, cmask, last, order) in enumerate(dirs):
            r0 = order[k] * ck
            b = b_alls[di][r0:r0 + ck, :]
            b_last = b[last:last + 1, :]
            qc = qkv_ref[r0:r0 + ck, 0:dk].astype(F32) * q_scale
            kc = qkv_ref[r0:r0 + ck, dk:2 * dk].astype(F32)
            q_in = (qc * jnp.exp(b)).astype(BF16)
            k_in = (kc * jnp.exp(-b)).astype(BF16)
            k_dec = (kc * jnp.exp(b_last - b)).astype(BF16)
            decay = jnp.exp(b_last)
            units = []
            for h in range(heads):
                qh = q_in[:, h * hk:(h + 1) * hk]
                vh = qkv_ref[r0:r0 + ck, 2 * dk + h * hv:2 * dk + (h + 1) * hv]
                scores = jnp.where(cmask, _dot_nt(qh, k_in[:, h * hk:(h + 1) * hk]), 0.0)
                o_intra = _dot(scores.astype(BF16), vh)
                ut = _dot_tn(vh, k_dec[:, h * hk:(h + 1) * hk])
                units.append((qh, o_intra, ut, decay[:, h * hk:(h + 1) * hk]))
            work.append((r0, units))
        for (r0, units), (_, _, _, _, o_ref, s_scr, _, _, _, _) in zip(work, dirs):
            for h, (qh, o_intra, ut, dec) in enumerate(units):
                st = s_scr[h]
                o = o_intra + _dot_nt(qh, st.astype(BF16))
                o_ref[r0:r0 + ck, h * hv:(h + 1) * hv] = o.astype(BF16)
                s_scr[h] = st * dec + ut


def _gla(z, lr, wupf, wupb, baf, bab, s0f, s0b, heads, dk, dv, tb):
    bsz, n, _ = z.shape
    nb = n // tb
    hk, hv = dk // heads, dv // heads
    fwd = lambda cb: (lambda b, i: (b, i, cb))
    bwd = lambda cb: (lambda b, i: (b, nb - 1 - i, cb))
    full = lambda *shape: pl.BlockSpec(shape, lambda b, i: (0,) * len(shape))
    sspec = pl.BlockSpec((None, heads, hv, hk), lambda b, i: (b, 0, 0, 0))
    ost = jax.ShapeDtypeStruct((bsz, n, dv), BF16)
    tr = min(tb, MXU_DEPTH)
    assert tb % tr == 0 and tr % GLA_CHUNK == 0
    ri = np.arange(tr)
    same_chunk = (ri[:, None] // GLA_CHUNK) == (ri[None, :] // GLA_CHUNK)
    trif = jnp.asarray(same_chunk & (ri[None, :] <= ri[:, None]), BF16)
    trib = jnp.asarray(same_chunk & (ri[None, :] >= ri[:, None]), BF16)
    qkv = 2 * dk + dv
    return pl.pallas_call(
        functools.partial(_gla_kernel, heads, dk),
        grid=(bsz, nb),
        in_specs=[pl.BlockSpec((None, tb, qkv), fwd(0)), pl.BlockSpec((None, tb, LANES), fwd(0)),
                  pl.BlockSpec((None, tb, qkv), bwd(0)), pl.BlockSpec((None, tb, LANES), bwd(0)),
                  full(LANES, dk), full(LANES, dk), full(1, dk), full(1, dk),
                  full(tr, tr), full(tr, tr), sspec, sspec],
        out_specs=(pl.BlockSpec((None, tb, dv), fwd(0)), pl.BlockSpec((None, tb, dv), bwd(0))),
        out_shape=(ost, ost),
        scratch_shapes=[pltpu.VMEM((heads, hv, hk), F32), pltpu.VMEM((heads, hv, hk), F32)],
        compiler_params=_cparams("arbitrary", "arbitrary"),
        name="gla",
    )(z, lr, z, lr, wupf, wupb, baf, bab, trif, trib, s0f, s0b)


def _mixer_kernel(heads, of_ref, ob_ref, sg_ref, bg_ref, cu_ref, srg_ref, src_ref, x_ref,
                  gng_ref, cw_ref, wo_ref, wco_ref, wm_ref, gt1_ref, sh2_ref, sc2_ref, n2g_ref,
                  rwt_ref, x1_ref, h2_ref, aff_ref):
    tm, dv = of_ref.shape
    hv = dv // heads
    o = of_ref[...].astype(F32) + ob_ref[...].astype(F32)
    parts = []
    for h in range(heads):
        oh = o[:, h * hv:(h + 1) * hv]
        ms = jnp.mean(oh * oh, axis=-1, keepdims=True)
        parts.append(oh * lax.rsqrt(ms + EPS))
    on = jnp.concatenate(parts, axis=1) * gng_ref[...]
    a_gla = (on * sg_ref[...].astype(F32)).astype(BF16)

    cu = cu_ref[...].astype(F32)
    gcol = lax.broadcasted_iota(I32, cu.shape, 0) & (GRID_W - 1)
    left = jnp.where(gcol == 0, 0.0, pltpu.roll(cu, 1, 0))
    right = jnp.where(gcol == GRID_W - 1, 0.0, pltpu.roll(cu, tm - 1, 0))
    cw = cw_ref[...]
    conv = left * cw[0:1, :] + cu * cw[1:2, :] + right * cw[2:3, :]
    y_conv = _dot((bg_ref[...].astype(F32) * conv).astype(BF16), wco_ref[...])
    y_gla = _dot(a_gla, wo_ref[...])

    merged = srg_ref[...].astype(F32) * y_gla + src_ref[...].astype(F32) * y_conv
    y = _dot(merged.astype(BF16), wm_ref[...])
    x1 = x_ref[...] + gt1_ref[...] * y
    x1_ref[...] = x1
    h2 = _rms_mod(x1, n2g_ref[...], sh2_ref[...], sc2_ref[...])
    rpt = h2.shape[1] // LANES
    for a in range(rpt):
        h2_ref[pl.ds(a, tm, stride=rpt), :] = h2[:, a * LANES:(a + 1) * LANES]
    logits = _dot_nt(rwt_ref[...], h2.astype(BF16))
    ex = jnp.exp(logits - jnp.max(logits, axis=0, keepdims=True))
    aff_ref[...] = ex / jnp.sum(ex, axis=0, keepdims=True)


def _mixer(o_f, o_b, z, x, gng, conv_w, w_o, w_co, w_m, gt1, sh2, sc2, n2g, rwt, heads, dk, tm):
    bsz, n, d = x.shape
    dv = o_f.shape[2]
    c = conv_w.shape[1]
    e = rwt.shape[0]
    zb = lambda cb: pl.BlockSpec((None, tm, d), lambda b, i: (b, i, cb))
    base = (2 * dk + dv) // d
    tok = lambda w: pl.BlockSpec((None, tm, w), lambda b, i: (b, i, 0))
    full = lambda *shape: pl.BlockSpec(shape, lambda b, i: (0,) * len(shape))
    perb = pl.BlockSpec((None, 1, d), lambda b, i: (b, 0, 0))
    return pl.pallas_call(
        functools.partial(_mixer_kernel, heads),
        grid=(bsz, n // tm),
        in_specs=[tok(dv), tok(dv), zb(base), zb(base + 1), zb(base + 2),
                  zb(base + 4), zb(base + 5), tok(d),
                  full(1, dv), full(3, c), full(dv, d), full(c, d), full(d, d),
                  perb, perb, perb, full(1, d), full(e, d)],
        out_specs=(tok(d), pl.BlockSpec((None, tm * (d // LANES), LANES), lambda b, i: (b, i, 0)),
                   pl.BlockSpec((None, e, tm), lambda b, i: (b, 0, i))),
        out_shape=(jax.ShapeDtypeStruct((bsz, n, d), F32),
                   jax.ShapeDtypeStruct((bsz, n * (d // LANES), LANES), F32),
                   jax.ShapeDtypeStruct((bsz, e, n), F32)),
        compiler_params=_cparams("arbitrary", "arbitrary"),
        name="mixer",
    )(o_f, o_b, z, z, z, z, z, x, gng, conv_w, w_o, w_co, w_m, gt1, sh2, sc2, n2g, rwt)


def _route_kernel(cap, aff_ref, lmat_ref, pos_ref, base_ref, idx_ref, loc_scr, bef_scr):
    e, n = aff_ref.shape
    nt = n // LANES
    aff = aff_ref[...]

    def count(mask):
        return jnp.sum(jnp.where(mask, 1.0, 0.0), axis=1, keepdims=True)

    def search(k, tbits):
        cand = tbits | jnp.left_shift(jnp.int32(1), 30 - k)
        ok = count(aff >= lax.bitcast_convert_type(cand, F32)) >= cap
        return jnp.where(ok, cand, tbits)

    tbits = lax.fori_loop(0, 31, search, jnp.zeros((e, 1), I32))
    thr = lax.bitcast_convert_type(tbits, F32)
    gt = aff > thr
    eq = aff == thr
    need = cap - count(gt)

    def stack(mask):
        m = jnp.where(mask, 1.0, 0.0)
        return jnp.concatenate([m[:, j * LANES:(j + 1) * LANES] for j in range(nt)], axis=0)

    r = lax.broadcasted_iota(I32, (LANES, LANES), 0)
    cl = lax.broadcasted_iota(I32, (LANES, LANES), 1)
    upper = jnp.where(r <= cl, 1.0, 0.0).astype(BF16)
    ones = jnp.ones((LANES, LANES), BF16)
    lmat = lmat_ref[...]

    def cumsum(ms):
        msb = ms.astype(BF16)
        before = _dot(_dot(lmat, msb).astype(BF16), ones)
        return _dot(msb, upper), before

    eq_s = stack(eq)
    loc_eq, before_eq = cumsum(eq_s)
    need_s = jnp.concatenate([need] * nt, axis=0)
    sel = jnp.maximum(stack(gt), jnp.where(loc_eq + before_eq <= need_s, eq_s, 0.0))
    loc, before = cumsum(sel)
    pos_ref[...] = jnp.where(sel > 0.0, loc + before - 1.0, -1.0).astype(I32)
    base_ref[...] = before.astype(I32)

    loc_scr[...] = loc
    bef_scr[...] = before
    slot = lax.broadcasted_iota(I32, (1, cap), 1).astype(F32)
    for ex in range(e):
        loc_e = loc_scr[pl.ds(ex, nt, stride=e), :]
        tprev = bef_scr[pl.ds(ex, nt, stride=e), :][:, 0:1]
        tincl = tprev + loc_e[:, LANES - 1:LANES]
        in_tile = jnp.where((tprev <= slot) & (slot < tincl), 1.0, 0.0)
        tile = jnp.sum(jnp.where(tincl <= slot, 1.0, 0.0), axis=0, keepdims=True)
        s_loc = slot - jnp.sum(in_tile * tprev, axis=0, keepdims=True)
        counts = _dot_tn(loc_e.astype(BF16), in_tile.astype(BF16))
        lane = jnp.sum(jnp.where(counts <= s_loc, 1.0, 0.0), axis=0, keepdims=True)
        idx_ref[pl.ds(ex, 1), :] = (tile * LANES + lane).astype(I32)


def _route(aff_t, cap):
    bsz, e, n = aff_t.shape
    nt = n // LANES
    rows = nt * e
    ri = np.arange(rows)
    lmat = ((ri[None, :] % e == ri[:, None] % e) & (ri[None, :] // e < ri[:, None] // e))
    lmat = jnp.asarray(lmat, BF16)
    st = jax.ShapeDtypeStruct((bsz, rows, LANES), I32)
    ospec = pl.BlockSpec((None, rows, LANES), lambda b: (b, 0, 0))
    return pl.pallas_call(
        functools.partial(_route_kernel, cap),
        grid=(bsz,),
        in_specs=[pl.BlockSpec((None, e, n), lambda b: (b, 0, 0)),
                  pl.BlockSpec((rows, rows), lambda b: (0, 0))],
        out_specs=(ospec, ospec, pl.BlockSpec((None, e, cap), lambda b: (b, 0, 0))),
        out_shape=(st, st, jax.ShapeDtypeStruct((bsz, e, cap), I32)),
        scratch_shapes=[pltpu.VMEM((rows, LANES), F32), pltpu.VMEM((rows, LANES), F32)],
        compiler_params=_cparams("arbitrary"),
        name="route",
    )(aff_t, lmat)


def _slot_window(base):
    return pl.multiple_of((base // BF16_ROWS) * BF16_ROWS, BF16_ROWS)


def _expert_kernel(cap, idx_ref, h2_hbm, wg_ref, wu_ref, wd_ref, ye_ref, xbuf, acc, sem):
    e, f = pl.program_id(1), pl.program_id(2)
    n_exp, nf = pl.num_programs(1), pl.num_programs(2)
    last = pl.num_programs(0) * n_exp - 1
    lin = pl.program_id(0) * n_exp + e
    slot = lin % 2
    per_step = cap // nf
    rpt = wg_ref.shape[0] // LANES

    def row_copy(lin_t, part, i, slot_t):
        tok = idx_ref[(lin_t * nf + part) * per_step + i]
        return pltpu.make_async_copy(h2_hbm.at[pl.ds(pl.multiple_of(tok * rpt, rpt), rpt), :],
                                     xbuf.at[slot_t, part, pl.ds(i * rpt, rpt), :],
                                     sem.at[slot_t])

    def wait_rows(slot_t):
        pltpu.make_async_copy(xbuf.at[slot_t], xbuf.at[slot_t], sem.at[slot_t]).wait()

    @pl.when((lin == 0) & (f == 0))
    def _():
        acc[...] = jnp.zeros_like(acc)
        for part in range(nf):
            def first(i, carry):
                row_copy(lin, part, i, slot).start()
                return carry
            lax.fori_loop(0, per_step, first, 0)

    @pl.when(f == 0)
    def _():
        wait_rows(slot)

    xv = jnp.concatenate(
        [jnp.concatenate([xbuf[slot, part, pl.ds(a, per_step, stride=rpt), :].astype(BF16)
                          for a in range(rpt)], axis=1) for part in range(nf)], axis=0)
    hg = _dot(xv, wg_ref[...].astype(BF16))
    hu = _dot(xv, wu_ref[...].astype(BF16))
    hid = (hg * _sigmoid(hg) * hu).astype(BF16)
    acc[...] = jnp.where(f > 0, acc[...], 0.0) + _dot(hid, wd_ref[...].astype(BF16))

    nxt = jnp.minimum(lin + 1, last)
    for i in range(per_step):
        row_copy(nxt, f, i, 1 - slot).start()

    @pl.when((lin == last) & (f == nf - 1))
    def _():
        wait_rows(1 - slot)

    @pl.when(f == nf - 1)
    def _():
        ye_ref[0:cap, :] = acc[...].astype(BF16)
        ye_ref[cap:, :] = jnp.zeros((ye_ref.shape[0] - cap, ye_ref.shape[1]), BF16)


def _experts(h2, idx, w_gate, w_up, w_down, cap, rows, tf):
    n_exp, d, df = w_gate.shape
    bsz = h2.shape[0]
    rpt = d // LANES
    n = h2.shape[1] // rpt
    nf = df // tf
    assert cap % nf == 0
    rows_global = (idx + (jnp.arange(bsz, dtype=I32) * n)[:, None, None]).reshape(-1)
    return pl.pallas_call(
        functools.partial(_expert_kernel, cap),
        grid_spec=pltpu.PrefetchScalarGridSpec(
            num_scalar_prefetch=1,
            grid=(bsz, n_exp, nf),
            in_specs=[pl.BlockSpec(memory_space=pl.ANY),
                      pl.BlockSpec((None, d, tf), lambda b, e, f, s: (e, 0, f)),
                      pl.BlockSpec((None, d, tf), lambda b, e, f, s: (e, 0, f)),
                      pl.BlockSpec((None, tf, d), lambda b, e, f, s: (e, f, 0))],
            out_specs=pl.BlockSpec((None, None, rows, d), lambda b, e, f, s: (b, e, 0, 0)),
            scratch_shapes=[pltpu.VMEM((2, nf, (cap // nf) * rpt, LANES), F32),
                            pltpu.VMEM((cap, d), F32), pltpu.SemaphoreType.DMA((2,))]),
        out_shape=jax.ShapeDtypeStruct((bsz, n_exp, rows, d), BF16),
        compiler_params=_cparams("arbitrary", "arbitrary", "arbitrary"),
        name="experts",
    )(rows_global, h2.reshape(bsz * n * rpt, LANES), w_gate, w_up, w_down)


def _combine_windows(tm, n_exp):
    mean = tm * EC_CAPACITY / n_exp
    usual = mean + 5.5 * (mean * (1 - EC_CAPACITY / n_exp)) ** 0.5 + BF16_ROWS - 1
    usual = min(-(-int(usual) // BF16_ROWS) * BF16_ROWS, tm + BF16_ROWS)
    return usual, tm + BF16_ROWS


def _combine_kernel(n_exp, nsub, base_ref, ovf_ref, pos_ref, aff_ref, x1_ref, gt2_ref, fg_ref,
                    ye_hbm, out_ref, buf, fbuf, ffn_scr, sem, fsem):
    b, j = pl.program_id(0), pl.program_id(1)
    nt = pl.num_programs(1)
    step = b * nt + j
    slot = step % 2
    wp, wfull = buf.shape[2], fbuf.shape[0]
    d = buf.shape[-1]

    def start_of(tile, ex):
        return _slot_window(base_ref[tile * nsub * n_exp + ex])

    def window_copy(tile, ex, dst_slot):
        return pltpu.make_async_copy(ye_hbm.at[tile // nt, ex, pl.ds(start_of(tile, ex), wp), :],
                                     buf.at[dst_slot, ex], sem.at[dst_slot])

    def full_copy(ex):
        return pltpu.make_async_copy(ye_hbm.at[b, ex, pl.ds(start_of(step, ex), wfull), :],
                                     fbuf, fsem)

    @pl.when(step == 0)
    def _():
        for ex in range(n_exp):
            window_copy(step, ex, slot).start()

    @pl.when(step + 1 < pl.num_programs(0) * nt)
    def _():
        for ex in range(n_exp):
            window_copy(step + 1, ex, 1 - slot).start()

    def gates_t(rows, ex):
        srow = lax.broadcasted_iota(I32, (rows, LANES), 0)
        s0 = start_of(step, ex)
        parts = []
        for u in range(nsub):
            rel = pos_ref[pl.ds(u * n_exp + ex, 1), :] - s0
            gate = aff_ref[pl.ds(ex, 1), u * LANES:(u + 1) * LANES]
            parts.append(jnp.where(srow == rel, gate, 0.0).astype(BF16))
        return jnp.concatenate(parts, axis=1)

    for ex in range(n_exp):
        window_copy(step, ex, slot).wait()

    @pl.when(ovf_ref[step] == 0)
    def _():
        g_all = jnp.concatenate([gates_t(wp, ex) for ex in range(n_exp)], axis=0)
        ffn_scr[...] = _dot_tn(g_all, buf[slot].reshape(n_exp * wp, d))

    @pl.when(ovf_ref[step] != 0)
    def _():
        ffn_scr[...] = jnp.zeros_like(ffn_scr)
        for ex in range(n_exp):
            full_copy(ex).start()
            full_copy(ex).wait()
            ffn_scr[...] += _dot_tn(gates_t(wfull, ex), fbuf[...])

    x2 = x1_ref[...] + gt2_ref[...] * ffn_scr[...]
    ms = jnp.mean(x2 * x2, axis=-1, keepdims=True)
    out_ref[...] = x2 * lax.rsqrt(ms + EPS) * fg_ref[...]


def _combine(ye, pos, aff_t, before, x1, gt2, final_g, cap, tm):
    bsz, n, d = x1.shape
    n_exp = aff_t.shape[1]
    nsub = tm // LANES
    nt = n // tm
    wp, wfull = _combine_windows(tm, n_exp)
    starts = before[:, :, 0].reshape(bsz, n // LANES, n_exp)
    tile_starts = starts[:, ::nsub]
    tile_ends = jnp.concatenate([tile_starts[:, 1:], jnp.full((bsz, 1, n_exp), cap, I32)], axis=1)
    overflow = (tile_starts % BF16_ROWS) + (tile_ends - tile_starts) > wp
    ovf = jnp.any(overflow, axis=-1).astype(I32).reshape(-1)
    return pl.pallas_call(
        functools.partial(_combine_kernel, n_exp, nsub),
        grid_spec=pltpu.PrefetchScalarGridSpec(
            num_scalar_prefetch=2,
            grid=(bsz, nt),
            in_specs=[pl.BlockSpec((None, nsub * n_exp, LANES), lambda b, j, s, o: (b, j, 0)),
                      pl.BlockSpec((None, n_exp, tm), lambda b, j, s, o: (b, 0, j)),
                      pl.BlockSpec((None, tm, d), lambda b, j, s, o: (b, j, 0)),
                      pl.BlockSpec((None, 1, d), lambda b, j, s, o: (b, 0, 0)),
                      pl.BlockSpec((1, d), lambda b, j, s, o: (0, 0)),
                      pl.BlockSpec(memory_space=pl.ANY)],
            out_specs=pl.BlockSpec((None, tm, d), lambda b, j, s, o: (b, j, 0)),
            scratch_shapes=[pltpu.VMEM((2, n_exp, wp, d), BF16),
                            pltpu.VMEM((wfull, d), BF16),
                            pltpu.VMEM((tm, d), F32),
                            pltpu.SemaphoreType.DMA((2,)), pltpu.SemaphoreType.DMA(())]),
        out_shape=jax.ShapeDtypeStruct((bsz, n, d), F32),
        compiler_params=_cparams("arbitrary", "arbitrary"),
        name="combine",
    )(starts.reshape(-1), ovf, pos, aff_t, x1, gt2, final_g, ye)


def _pick(n, pref):
    t = min(n, pref)
    while n % t:
        t //= 2
    return t


def _tiles(n, d_expert):
    return dict(inproj=_pick(n, 1024), gla=_pick(n, 512), mixer=_pick(n, 512),
                combine=_pick(n, 512), expert_f=_pick(d_expert, 512))


def kernel(x, c, ctx, c_ctx, w_ada, b_ada, norm1_g, norm2_g, w_in, gla_w_a_up, gla_b_a,
           gla_norm_g, gla_w_o, conv_w, conv_w_out, merge_w_out, router_w,
           exp_w_gate, exp_w_up, exp_w_down, final_g):
    bsz, n, d = x.shape
    depth = w_ada.shape[0]
    assert depth == 1, "single-layer trunk"
    rank, dk = gla_w_a_up.shape[-2:]
    dv = gla_w_o.shape[1]
    hv = gla_norm_g.shape[-1]
    heads = dv // hv
    cch = conv_w.shape[-1]
    n_exp = router_w.shape[-1]
    cap = EC_CAPACITY * n // n_exp
    assert 2 * dk == d and dv == d and cch == d and 2 * rank <= LANES
    assert n % LANES == 0 and n % GRID_W == 0 and bsz + 1 <= 8 and n_exp % 8 == 0

    cond = jnp.zeros((8, d), F32).at[:bsz].set(c).at[bsz].set(c_ctx)
    mod = _adaln(cond, w_ada[0], b_ada[0]).reshape(8, N_MOD, d)
    sh1, sc1, gt1, sh2, sc2, gt2 = [mod[:bsz, k][:, None, :] for k in range(N_MOD)]
    csh1, csc1 = mod[bsz:bsz + 1, 0], mod[bsz:bsz + 1, 1]

    lr0 = 2 * dk + 2 * dv
    u0 = lr0 + 2 * rank
    groups = ((0, lr0), (u0 + cch, cch), (u0, cch), (u0 + 2 * cch, cch + 2 * d), (lr0, LANES))
    tiles = tuple((r0 + o, min(d, cnt - o)) for r0, cnt in groups for o in range(0, cnt, d))
    wt = _wprep(jnp.swapaxes(w_in[0], 0, 1), tiles)
    wup = gla_w_a_up[0].astype(BF16)
    wupf = jnp.zeros((LANES, dk), BF16).at[:rank].set(wup[0])
    wupb = jnp.zeros((LANES, dk), BF16).at[rank:2 * rank].set(wup[1])
    baf, bab = gla_b_a[0, 0:1], gla_b_a[0, 1:2]
    n1g, n2g = norm1_g[0:1], norm2_g[0:1]

    t = _tiles(n, exp_w_gate.shape[-1])
    s0f, s0b = _ctx_states(ctx, csh1, csc1, n1g, wt, wupf, wupb, baf, bab, heads, dk, dv)
    z, lr = _inproj(x, sh1, sc1, n1g, wt, t["inproj"])
    o_f, o_b = _gla(z, lr, wupf, wupb, baf, bab, s0f, s0b, heads, dk, dv, t["gla"])
    x1, h2, aff_t = _mixer(
        o_f, o_b, z, x, jnp.tile(gla_norm_g[0:1], (1, heads)), conv_w[0],
        gla_w_o[0].astype(BF16), conv_w_out[0].astype(BF16), merge_w_out[0].astype(BF16),
        gt1, sh2, sc2, n2g, router_w[0].T.astype(BF16), heads, dk, t["mixer"])

    pos, before, idx = _route(aff_t, cap)
    ye = _experts(h2, idx, exp_w_gate[0], exp_w_up[0], exp_w_down[0], cap,
                  cap + _combine_windows(t["combine"], n_exp)[1], t["expert_f"])
    return _combine(ye, pos, aff_t, before, x1, gt2, final_g.reshape(1, d), cap, t["combine"])
```
